```python
import math
import jax, jax.numpy as jnp
from jax import lax
import numpy as np

D_MODEL = 1024
BATCH = 32
SEQ = 2048
DEPTH = 1

HEAD_DIM = 64
NSA_HEADS = 8
NSA_KV_GROUPS = 2
NSA_Q_PER_GROUP = NSA_HEADS // NSA_KV_GROUPS
NSA_WIDTH = NSA_HEADS * HEAD_DIM
KV_WIDTH = NSA_KV_GROUPS * HEAD_DIM
CMP_BLOCK = 32
CMP_STRIDE = 16
SEL_BLOCK = 64
SEL_TOPN = 8
FORCE_BONUS = 1000.0
WINDOW = 512
Q_BLOCK = 128
SEL_Q_BLOCK = 64
POOL_WINDOWS = (2, 4, 8, 16)
POOL_GROUP = 64
POOL_WIDTH = POOL_GROUP * len(POOL_WINDOWS)
MEM_HEADS = 4
MEM_WIDTH = MEM_HEADS * HEAD_DIM
N_BRANCH = 3
IN_WIDTHS = (NSA_WIDTH, 6 * KV_WIDTH, 3 * NSA_HEADS, POOL_WIDTH, MEM_WIDTH, N_BRANCH * D_MODEL)
IN_WIDTH = NSA_WIDTH + 6 * KV_WIDTH + 3 * NSA_HEADS + POOL_WIDTH + MEM_WIDTH + N_BRANCH * D_MODEL
N_EXPERTS = 32
TOP_K = 4
D_FF = 1024
SWIGLU_LIMIT = 7.0
SWIGLU_ALPHA = 1.702
ROPE_THETA = 10000.0
EPS = 1e-5
NEG_INF = -1e30
TINY = 1e-30

kernel_name = "hybrid_nsa_pool_memxattn_moe"


def rms_norm(x, g):
    xf = x.astype(jnp.float32)
    y = xf * lax.rsqrt(jnp.mean(xf * xf, axis=-1, keepdims=True) + EPS)
    return (y * g.astype(jnp.float32)).astype(x.dtype)


def rope(x, pos):
    half = HEAD_DIM // 2
    inv = ROPE_THETA ** (-jnp.arange(half, dtype=jnp.float32) / half)
    ang = pos.astype(jnp.float32)[:, None] * inv[None, :]
    cos = jnp.cos(ang)[None, :, None, :]
    sin = jnp.sin(ang)[None, :, None, :]
    xf = x.astype(jnp.float32)
    x1, x2 = xf[..., :half], xf[..., half:]
    return jnp.concatenate([x1 * cos - x2 * sin, x2 * cos + x1 * sin], axis=-1).astype(x.dtype)


def masked_softmax(s, mask):
    s = jnp.where(mask, s.astype(jnp.float32), NEG_INF)
    m = jnp.max(s, axis=-1, keepdims=True)
    e = jnp.where(mask, jnp.exp(s - m), 0.0)
    return e / jnp.maximum(jnp.sum(e, axis=-1, keepdims=True), TINY)


def nsa_compressed(q, k, v, cmp_pos, cmp_w1, cmp_w2):
    B, S = q.shape[0], q.shape[1]
    nc = (S - CMP_BLOCK) // CMP_STRIDE + 1
    tok = jnp.arange(nc)[:, None] * CMP_STRIDE + jnp.arange(CMP_BLOCK)[None, :]

    def compress(t, i):
        blk = t[:, tok] + cmp_pos[i][None, None, :, None, :]
        hid = jax.nn.silu(jnp.einsum('bnlgd,lde->bnge', blk, cmp_w1[i]))
        return jnp.einsum('bnge,ef->bngf', hid, cmp_w2[i])

    kc = compress(k, 0)
    vc = compress(v, 1)
    qg = q.reshape(B, S, NSA_KV_GROUPS, NSA_Q_PER_GROUP, HEAD_DIM)
    s = jnp.einsum('bsghd,bngd->bghsn', qg, kc) * (HEAD_DIM ** -0.5)
    t = jnp.arange(S)
    blk_end = jnp.arange(nc) * CMP_STRIDE + CMP_BLOCK - 1
    mask = blk_end[None, :] <= t[:, None]
    p = masked_softmax(s, mask)
    o = jnp.einsum('bghsn,bngd->bsghd', p.astype(vc.dtype), vc)
    return o, jnp.sum(p, axis=2)


def nsa_select_indices(p_grp, S):
    nc = p_grp.shape[-1]
    nsb = S // SEL_BLOCK
    i = jnp.arange(nc)
    j = jnp.arange(nsb)
    overlap = ((i[:, None] * CMP_STRIDE <= j[None, :] * SEL_BLOCK + SEL_BLOCK - 1)
               & (i[:, None] * CMP_STRIDE + CMP_BLOCK - 1 >= j[None, :] * SEL_BLOCK)).astype(jnp.float32)
    score = jnp.einsum('bgsn,nj->bgsj', p_grp, overlap)
    t = jnp.arange(S)
    cur = t // SEL_BLOCK
    valid = j[None, :] * SEL_BLOCK <= t[:, None]
    forced = (j[None, :] == 0) | (j[None, :] == cur[:, None]) | (j[None, :] == cur[:, None] - 1)
    score = jnp.where(valid, score + jnp.where(forced, FORCE_BONUS, 0.0), -1.0)
    _, idx = lax.top_k(score, min(SEL_TOPN, nsb))
    return idx


def nsa_selected(q, k, v, idx):
    B, S = q.shape[0], q.shape[1]
    n = idx.shape[-1]
    nsb = S // SEL_BLOCK
    kb = k.reshape(B, nsb, SEL_BLOCK, NSA_KV_GROUPS, HEAD_DIM).transpose(0, 3, 1, 2, 4)
    vb = v.reshape(B, nsb, SEL_BLOCK, NSA_KV_GROUPS, HEAD_DIM).transpose(0, 3, 1, 2, 4)
    gather = jax.vmap(jax.vmap(lambda tbl, ids: tbl[ids]))
    m_len = n * SEL_BLOCK

    def chunk(c):
        start = c * SEL_Q_BLOCK
        qc = lax.dynamic_slice_in_dim(q, start, SEL_Q_BLOCK, axis=1).reshape(
            B, SEL_Q_BLOCK, NSA_KV_GROUPS, NSA_Q_PER_GROUP, HEAD_DIM)
        ic = lax.dynamic_slice_in_dim(idx, start, SEL_Q_BLOCK, axis=2)
        flat = ic.reshape(B, NSA_KV_GROUPS, SEL_Q_BLOCK * n)
        kg = gather(kb, flat).reshape(B, NSA_KV_GROUPS, SEL_Q_BLOCK, m_len, HEAD_DIM)
        vg = gather(vb, flat).reshape(B, NSA_KV_GROUPS, SEL_Q_BLOCK, m_len, HEAD_DIM)
        kpos = (ic[..., None] * SEL_BLOCK + jnp.arange(SEL_BLOCK)).reshape(B, NSA_KV_GROUPS, SEL_Q_BLOCK, m_len)
        qpos = start + jnp.arange(SEL_Q_BLOCK)
        mask = (kpos <= qpos[:, None])[:, :, None]
        s = jnp.einsum('bqghd,bgqmd->bghqm', qc, kg) * (HEAD_DIM ** -0.5)
        p = masked_softmax(s, mask)
        return jnp.einsum('bghqm,bgqmd->bqghd', p.astype(vg.dtype), vg)

    o = lax.map(chunk, jnp.arange(S // SEL_Q_BLOCK))
    return o.transpose(1, 0, 2, 3, 4, 5).reshape(B, S, NSA_KV_GROUPS, NSA_Q_PER_GROUP, HEAD_DIM)


def nsa_window(q, k, v):
    B, S = q.shape[0], q.shape[1]
    kp = jnp.pad(k, ((0, 0), (WINDOW, 0), (0, 0), (0, 0)))
    vp = jnp.pad(v, ((0, 0), (WINDOW, 0), (0, 0), (0, 0)))
    span = WINDOW + Q_BLOCK

    def chunk(c):
        start = c * Q_BLOCK
        qc = lax.dynamic_slice_in_dim(q, start, Q_BLOCK, axis=1).reshape(
            B, Q_BLOCK, NSA_KV_GROUPS, NSA_Q_PER_GROUP, HEAD_DIM)
        kc = lax.dynamic_slice_in_dim(kp, start, span, axis=1)
        vc = lax.dynamic_slice_in_dim(vp, start, span, axis=1)
        qpos = start + jnp.arange(Q_BLOCK)
        kpos = start - WINDOW + jnp.arange(span)
        diff = qpos[:, None] - kpos[None, :]
        mask = (diff >= 0) & (diff < WINDOW) & (kpos[None, :] >= 0)
        s = jnp.einsum('bqghd,bkgd->bghqk', qc, kc) * (HEAD_DIM ** -0.5)
        p = masked_softmax(s, mask)
        return jnp.einsum('bghqk,bkgd->bqghd', p.astype(vc.dtype), vc)

    o = lax.map(chunk, jnp.arange(S // Q_BLOCK))
    return o.transpose(1, 0, 2, 3, 4, 5).reshape(B, S, NSA_KV_GROUPS, NSA_Q_PER_GROUP, HEAD_DIM)


def pool_mixer(u, w_pool, pool_scale):
    B, S, _ = u.shape
    uf = u.astype(jnp.float32)
    c = jnp.pad(jnp.cumsum(uf, axis=1), ((0, 0), (1, 0), (0, 0)))
    t = jnp.arange(S)
    groups = []
    for gi, w in enumerate(POOL_WINDOWS):
        sl = slice(gi * POOL_GROUP, (gi + 1) * POOL_GROUP)
        lo = jnp.maximum(t + 1 - w, 0)
        cg = c[..., sl]
        cnt = (t + 1 - lo).astype(jnp.float32)
        groups.append((cg[:, t + 1] - cg[:, lo]) / cnt[None, :, None] - uf[..., sl])
    z = jnp.stack(groups, axis=2)
    z = jnp.einsum('bsgc,gce->bsge', z, w_pool.astype(jnp.float32)).reshape(B, S, POOL_WIDTH)
    return (z * pool_scale.astype(jnp.float32)).astype(u.dtype)


def memory_xattn(q, mem_n, w_mem_kv):
    B, S, _ = q.shape
    M = mem_n.shape[1]
    kv = (mem_n @ w_mem_kv).reshape(B, M, 2, MEM_HEADS, HEAD_DIM)
    qh = q.reshape(B, S, MEM_HEADS, HEAD_DIM)
    s = jnp.einsum('bshd,bmhd->bhsm', qh, kv[:, :, 0]) * (HEAD_DIM ** -0.5)
    p = jax.nn.softmax(s.astype(jnp.float32), axis=-1)
    o = jnp.einsum('bhsm,bmhd->bshd', p.astype(kv.dtype), kv[:, :, 1])
    return o.reshape(B, S, MEM_WIDTH)


def clamped_swiglu(gu):
    gate, up = gu[..., :D_FF], gu[..., D_FF:]
    gate = jnp.minimum(gate, SWIGLU_LIMIT)
    up = jnp.clip(up, -SWIGLU_LIMIT, SWIGLU_LIMIT)
    return (up + 1.0) * (gate * jax.nn.sigmoid(SWIGLU_ALPHA * gate))


def moe_ffn(h, w_router, b_router, w_gate_up, b_gate_up, w_down, b_down):
    B, S, D = h.shape
    hf = h.reshape(B * S, D)
    logits = (hf @ w_router + b_router).astype(jnp.float32)
    top_val, top_idx = lax.top_k(logits, TOP_K)
    weights = jax.nn.softmax(top_val, axis=-1)
    combine = jnp.sum(jax.nn.one_hot(top_idx, N_EXPERTS, dtype=jnp.float32) * weights[..., None], axis=1)
    out = jnp.zeros_like(hf)
    for e in range(N_EXPERTS):
        y = clamped_swiglu(hf @ w_gate_up[e] + b_gate_up[e]) @ w_down[e] + b_down[e]
        out = out + combine[:, e:e + 1].astype(hf.dtype) * y
    return out.reshape(B, S, D)


def setup_inputs(seed: int = 0) -> dict:
    key = jax.random.key(seed)
    ks = jax.random.split(key, 24)
    f32 = jnp.float32

    def nrm(k, shape, scale):
        return jax.random.normal(k, shape, f32) * scale

    L = DEPTH
    return {
        "x": nrm(ks[0], (BATCH, SEQ, D_MODEL), 1.0),
        "mem": nrm(ks[1], (BATCH, 256, D_MODEL), 1.0),
        "norm_mix": 1.0 + nrm(ks[2], (L, D_MODEL), 0.02),
        "norm_mem": 1.0 + nrm(ks[3], (L, D_MODEL), 0.02),
        "w_in": nrm(ks[4], (L, D_MODEL, IN_WIDTH), D_MODEL ** -0.5),
        "cmp_pos": nrm(ks[5], (L, 2, CMP_BLOCK, HEAD_DIM), 0.1),
        "cmp_w1": nrm(ks[6], (L, 2, CMP_BLOCK, HEAD_DIM, HEAD_DIM), (CMP_BLOCK * HEAD_DIM) ** -0.5),
        "cmp_w2": nrm(ks[7], (L, 2, HEAD_DIM, HEAD_DIM), HEAD_DIM ** -0.5),
        "w_pool": nrm(ks[8], (L, len(POOL_WINDOWS), POOL_GROUP, POOL_GROUP), POOL_GROUP ** -0.5),
        "pool_scale": 1.0 + nrm(ks[9], (L, POOL_WIDTH), 0.02),
        "w_mem_kv": nrm(ks[10], (L, D_MODEL, 2 * MEM_WIDTH), D_MODEL ** -0.5),
        "w_up_nsa": nrm(ks[11], (L, NSA_WIDTH, D_MODEL), NSA_WIDTH ** -0.5),
        "w_up_pool": nrm(ks[12], (L, POOL_WIDTH, D_MODEL), POOL_WIDTH ** -0.5),
        "w_up_mem": nrm(ks[13], (L, MEM_WIDTH, D_MODEL), MEM_WIDTH ** -0.5),
        "w_out": nrm(ks[14], (L, D_MODEL, D_MODEL), D_MODEL ** -0.5),
        "norm_ffn": 1.0 + nrm(ks[15], (L, D_MODEL), 0.02),
        "w_router": nrm(ks[16], (L, D_MODEL, N_EXPERTS), D_MODEL ** -0.5),
        "b_router": nrm(ks[17], (L, N_EXPERTS), 0.01),
        "w_gate_up": nrm(ks[18], (L, N_EXPERTS, D_MODEL, 2 * D_FF), D_MODEL ** -0.5),
        "b_gate_up": nrm(ks[19], (L, N_EXPERTS, 2 * D_FF), 0.01),
        "w_down": nrm(ks[20], (L, N_EXPERTS, D_FF, D_MODEL), D_FF ** -0.5),
        "b_down": nrm(ks[21], (L, N_EXPERTS, D_MODEL), 0.01),
        "norm_final": 1.0 + nrm(ks[22], (D_MODEL,), 0.02),
    }


def reference(x, mem, norm_mix, norm_mem, w_in, cmp_pos, cmp_w1, cmp_w2, w_pool, pool_scale,
              w_mem_kv, w_up_nsa, w_up_pool, w_up_mem, w_out, norm_ffn, w_router, b_router,
              w_gate_up, b_gate_up, w_down, b_down, norm_final):
    B, S, D = x.shape
    pos = jnp.arange(S)
    split_points = []
    acc = 0
    for w in IN_WIDTHS[:-1]:
        acc += w
        split_points.append(acc)

    for l in range(DEPTH):
        h = rms_norm(x, norm_mix[l])
        proj = h @ w_in[l]
        q, kv, nsa_gate, pool_in, q_mem, merge_gate = jnp.split(proj, split_points, axis=-1)
        q = q.reshape(B, S, NSA_HEADS, HEAD_DIM)
        kv = kv.reshape(B, S, 6, NSA_KV_GROUPS, HEAD_DIM)
        k_cmp, v_cmp = kv[:, :, 0], kv[:, :, 1]
        k_sel, v_sel = rope(kv[:, :, 2], pos), kv[:, :, 3]
        k_win, v_win = rope(kv[:, :, 4], pos), kv[:, :, 5]
        q_rot = rope(q, pos)

        o_cmp, p_grp = nsa_compressed(q, k_cmp, v_cmp, cmp_pos[l], cmp_w1[l], cmp_w2[l])
        sel_idx = nsa_select_indices(p_grp, S)
        o_sel = nsa_selected(q_rot, k_sel, v_sel, sel_idx)
        o_win = nsa_window(q_rot, k_win, v_win)
        g = jax.nn.sigmoid(nsa_gate.astype(jnp.float32)).astype(x.dtype).reshape(
            B, S, NSA_KV_GROUPS, NSA_Q_PER_GROUP, 3)
        o_nsa = (g[..., 0:1] * o_cmp + g[..., 1:2] * o_sel + g[..., 2:3] * o_win).reshape(B, S, NSA_WIDTH)

        o_pool = pool_mixer(pool_in, w_pool[l], pool_scale[l])

        o_mem = memory_xattn(q_mem, rms_norm(mem, norm_mem[l]), w_mem_kv[l])

        mg = jax.nn.sigmoid(merge_gate.astype(jnp.float32)).astype(x.dtype).reshape(B, S, N_BRANCH, D)
        merged = (mg[:, :, 0] * (o_nsa @ w_up_nsa[l])
                  + mg[:, :, 1] * (o_pool @ w_up_pool[l])
                  + mg[:, :, 2] * (o_mem @ w_up_mem[l]))
        x = x + merged @ w_out[l]

        x = x + moe_ffn(rms_norm(x, norm_ffn[l]), w_router[l], b_router[l],
                        w_gate_up[l], b_gate_up[l], w_down[l], b_down[l])

    return rms_norm(x, norm_final)
```

```python
import functools
import math

import jax
import jax.numpy as jnp
from jax import lax
from jax.experimental import pallas as pl
from jax.experimental.pallas import tpu as pltpu

F32 = jnp.float32
BF16 = jnp.bfloat16
U32 = jnp.uint32

D_MODEL = 1024
HEAD_DIM = 64
NSA_HEADS = 8
NSA_GROUPS = 2
HEADS_PER_GROUP = NSA_HEADS // NSA_GROUPS
NSA_WIDTH = NSA_HEADS * HEAD_DIM
KV_WIDTH = NSA_GROUPS * HEAD_DIM
CMP_BLOCK = 32
CMP_STRIDE = 16
SEL_BLOCK = 64
SEL_TOPN = 8
FORCE_BONUS = 1000.0
WINDOW = 512
POOL_WINDOWS = (2, 4, 8, 16)
POOL_GROUP = 64
POOL_WIDTH = POOL_GROUP * len(POOL_WINDOWS)
POOL_HALO = 16
MEM_HEADS = 4
MEM_WIDTH = MEM_HEADS * HEAD_DIM
N_EXPERTS = 32
TOP_K = 4
D_FF = 1024
SWIGLU_LIMIT = 7.0
SWIGLU_ALPHA = 1.702
ROPE_THETA = 10000.0
EPS = 1e-5
NEG_INF = -1e30
TINY = 1e-30
QK_SCALE = HEAD_DIM ** -0.5

LANES = 128
GATE_PAD = LANES
PACKED = D_MODEL // 2

TM_IN = 512
TQ = 128
CK = 512
TM_MERGE = 256
TM_E = 512
R_GATHER = 1024
TM_FINAL = 512
VMEM_LIMIT = 56 * 1024 * 1024


def _rms(x, g):
    return x * lax.rsqrt(jnp.mean(x * x, axis=-1, keepdims=True) + EPS) * g


def _dot(a, b):
    return jnp.dot(a, b, preferred_element_type=F32)


def _dot_nt(a, b):
    return lax.dot_general(a, b, (((1,), (1,)), ((), ())), preferred_element_type=F32)


def _pack_bf16_pairs(v):
    n = v.shape[1] // 2
    r = v.astype(BF16).astype(F32)
    lo = pltpu.bitcast(r[:, :n], U32) >> 16
    hi = pltpu.bitcast(r[:, n:], U32) & jnp.uint32(0xFFFF0000)
    return lo | hi


def _unpack_bf16_pairs(w):
    lo = pltpu.bitcast(w << 16, F32)
    hi = pltpu.bitcast(w & jnp.uint32(0xFFFF0000), F32)
    return jnp.concatenate([lo, hi], axis=1)


def _rank_desc(score, period, n_rolls):
    lane = lax.broadcasted_iota(jnp.int32, score.shape, 1) % period
    rank = jnp.zeros(score.shape, F32)
    for r in range(1, n_rolls + 1):
        other = pltpu.roll(score, r, 1)
        beats = (other > score) | ((other == score) & (lane >= r))
        rank = rank + beats.astype(F32)
    return rank


def _memkv_kernel(mem_ref, g_ref, w_ref, k_ref, v_ref):
    m = _rms(mem_ref[0], g_ref[...]).astype(BF16)
    kv = _dot(m, w_ref[...])
    for h in range(MEM_HEADS):
        k_ref[0, h] = kv[:, h * HEAD_DIM:(h + 1) * HEAD_DIM].astype(BF16)
        v_ref[0, h] = kv[:, MEM_WIDTH + h * HEAD_DIM:MEM_WIDTH + (h + 1) * HEAD_DIM].astype(BF16)


def _memkv(mem, g, w):
    B, M, D = mem.shape
    return pl.pallas_call(
        _memkv_kernel,
        grid=(B,),
        in_specs=[pl.BlockSpec((1, M, D), lambda b: (b, 0, 0)),
                  pl.BlockSpec((1, D), lambda b: (0, 0)),
                  pl.BlockSpec((D, 2 * MEM_WIDTH), lambda b: (0, 0))],
        out_specs=[pl.BlockSpec((1, MEM_HEADS, M, HEAD_DIM), lambda b: (b, 0, 0, 0)),
                   pl.BlockSpec((1, MEM_HEADS, M, HEAD_DIM), lambda b: (b, 0, 0, 0))],
        out_shape=[jax.ShapeDtypeStruct((B, MEM_HEADS, M, HEAD_DIM), BF16),
                   jax.ShapeDtypeStruct((B, MEM_HEADS, M, HEAD_DIM), BF16)],
        compiler_params=pltpu.CompilerParams(dimension_semantics=("arbitrary",), vmem_limit_bytes=VMEM_LIMIT),
        name="memkv",
    )(mem, g, w)


IN_COLS = NSA_WIDTH + 6 * KV_WIDTH + POOL_WIDTH + MEM_WIDTH + GATE_PAD


def _inproj_kernel(x_ref, g_ref, w_ref, cos_ref, sin_ref,
                   qc_ref, qr_ref, kvc_ref, ksel_ref, vsel_ref, kwin_ref, vwin_ref, pool_ref, qm_ref, gate_ref):
    h = _rms(x_ref[...], g_ref[...]).astype(BF16)
    p = _dot(h, w_ref[...])
    cos = cos_ref[...]
    sin = sin_ref[...]
    lane = lax.broadcasted_iota(jnp.int32, cos.shape, 1)
    first_half = (lane % HEAD_DIM) < (HEAD_DIM // 2)

    def rope(c):
        partner = jnp.where(first_half, pltpu.roll(c, LANES - HEAD_DIM // 2, 1), pltpu.roll(c, HEAD_DIM // 2, 1))
        return c * cos + partner * sin

    def halves(c):
        return c[:, :HEAD_DIM], c[:, HEAD_DIM:]

    for j in range(NSA_WIDTH // LANES):
        c = p[:, j * LANES:(j + 1) * LANES]
        r = rope(c)
        for hh, (cc, rr) in enumerate(zip(halves(c), halves(r))):
            qc_ref[2 * j + hh] = (cc * QK_SCALE).astype(BF16)
            qr_ref[2 * j + hh] = (rr * QK_SCALE).astype(BF16)
    o = NSA_WIDTH
    kvc_ref[...] = p[:, o:o + 2 * KV_WIDTH]
    o += 2 * KV_WIDTH
    for ref, rot in ((ksel_ref, True), (vsel_ref, False), (kwin_ref, True), (vwin_ref, False)):
        c = p[:, o:o + KV_WIDTH]
        if rot:
            c = rope(c)
        for g, cc in enumerate(halves(c)):
            ref[g] = cc.astype(BF16)
        o += KV_WIDTH
    pool_ref[...] = p[:, o:o + POOL_WIDTH]
    o += POOL_WIDTH
    for hh in range(MEM_HEADS):
        qm_ref[hh] = (p[:, o + hh * HEAD_DIM:o + (hh + 1) * HEAD_DIM] * QK_SCALE).astype(BF16)
    o += MEM_WIDTH
    gate_ref[...] = jax.nn.sigmoid(p[:, o:o + GATE_PAD])


def _inproj(x2, g, w_a, cos_t, sin_t, S):
    T = x2.shape[0]
    tm = TM_IN
    n_s = S // tm
    head_spec = lambda n: pl.BlockSpec((n, tm, HEAD_DIM), lambda i: (0, i, 0))
    row_spec = lambda w: pl.BlockSpec((tm, w), lambda i: (i, 0))
    return pl.pallas_call(
        _inproj_kernel,
        grid=(T // tm,),
        in_specs=[row_spec(D_MODEL),
                  pl.BlockSpec((1, D_MODEL), lambda i: (0, 0)),
                  pl.BlockSpec((D_MODEL, IN_COLS), lambda i: (0, 0)),
                  pl.BlockSpec((tm, LANES), lambda i: (i % n_s, 0)),
                  pl.BlockSpec((tm, LANES), lambda i: (i % n_s, 0))],
        out_specs=[head_spec(NSA_HEADS), head_spec(NSA_HEADS), row_spec(2 * KV_WIDTH),
                   head_spec(NSA_GROUPS), head_spec(NSA_GROUPS), head_spec(NSA_GROUPS), head_spec(NSA_GROUPS),
                   row_spec(POOL_WIDTH), head_spec(MEM_HEADS), row_spec(GATE_PAD)],
        out_shape=[jax.ShapeDtypeStruct((NSA_HEADS, T, HEAD_DIM), BF16),
                   jax.ShapeDtypeStruct((NSA_HEADS, T, HEAD_DIM), BF16),
                   jax.ShapeDtypeStruct((T, 2 * KV_WIDTH), F32),
                   jax.ShapeDtypeStruct((NSA_GROUPS, T, HEAD_DIM), BF16),
                   jax.ShapeDtypeStruct((NSA_GROUPS, T, HEAD_DIM), BF16),
                   jax.ShapeDtypeStruct((NSA_GROUPS, T, HEAD_DIM), BF16),
                   jax.ShapeDtypeStruct((NSA_GROUPS, T, HEAD_DIM), BF16),
                   jax.ShapeDtypeStruct((T, POOL_WIDTH), F32),
                   jax.ShapeDtypeStruct((MEM_HEADS, T, HEAD_DIM), BF16),
                   jax.ShapeDtypeStruct((T, GATE_PAD), F32)],
        compiler_params=pltpu.CompilerParams(dimension_semantics=("arbitrary",), vmem_limit_bytes=VMEM_LIMIT),
        name="inproj",
    )(x2, g, w_a, cos_t, sin_t)


N_CMP_PAD = 128
CMP_FLAT = CMP_BLOCK * HEAD_DIM
CMP_HALF = CMP_STRIDE * HEAD_DIM


def _nsa_kernel(qc_ref, qr_ref, kvc_ref, ksel_ref, vsel_ref, kwin_ref, vwin_ref, gate_ref,
                w1_ref, w2_ref, pos_ref, ovl_ref, exp_ref, o_ref, kc_s, vc_s, mask_s, *, S):
    i = pl.program_id(1)
    tq = TQ
    n_sel = S // SEL_BLOCK
    n_chunks_total = S // CK

    @pl.when(i == 0)
    def _compress():
        for kv, dst in ((0, kc_s), (1, vc_s)):
            w1 = w1_ref[kv]
            posterm = _dot(pos_ref[kv], w1)[0:1]
            for g in range(NSA_GROUPS):
                a = kvc_ref[0, kv * NSA_GROUPS + g].astype(BF16)
                p1 = _dot(a, w1[:CMP_HALF])
                p2 = _dot(a, w1[CMP_HALF:])
                hid = p1 + pltpu.roll(p2, N_CMP_PAD - 1, 0) + posterm
                hid = hid * jax.nn.sigmoid(hid)
                dst[g] = _dot(hid.astype(BF16), w2_ref[kv]).astype(BF16)

    q0 = i * tq
    tpos = q0 + lax.broadcasted_iota(jnp.int32, (tq, 1), 0)
    lane = lax.broadcasted_iota(jnp.int32, (tq, LANES), 1)
    cmp_valid = (lane * CMP_STRIDE + CMP_BLOCK - 1) <= tpos
    jblk = lane % n_sel
    cur = tpos // SEL_BLOCK
    sel_valid = jblk * SEL_BLOCK <= tpos
    forced = (jblk == 0) | (jblk == cur) | (jblk == cur - 1)

    win_span = WINDOW + tq
    win_start = pl.multiple_of(jnp.maximum(q0 - WINDOW, 0), tq)
    win_kpos = win_start + lax.broadcasted_iota(jnp.int32, (1, win_span), 1)
    win_diff = tpos - win_kpos
    win_mask = ((win_diff >= 0) & (win_diff < WINDOW))[None]

    n_chunks = (q0 + tq + CK - 1) // CK

    for g in range(NSA_GROUPS):
        p_grp = jnp.zeros((tq, N_CMP_PAD), F32)
        o_cmp = []
        for hh in range(HEADS_PER_GROUP):
            s = _dot_nt(qc_ref[g * HEADS_PER_GROUP + hh], kc_s[g])
            s = jnp.where(cmp_valid, s, NEG_INF)
            m = jnp.max(s, axis=-1, keepdims=True)
            e = jnp.where(cmp_valid, jnp.exp(s - m), 0.0)
            p = e / jnp.maximum(jnp.sum(e, axis=-1, keepdims=True), TINY)
            p_grp = p_grp + p
            o_cmp.append(_dot(p.astype(BF16), vc_s[g]))

        ovl = ovl_ref[...]
        p_hi = p_grp.astype(BF16)
        r1 = p_grp - p_hi.astype(F32)
        p_mid = r1.astype(BF16)
        p_lo = (r1 - p_mid.astype(F32)).astype(BF16)
        score = _dot(p_hi, ovl) + _dot(p_mid, ovl) + _dot(p_lo, ovl)
        score = jnp.where(sel_valid, score + jnp.where(forced, FORCE_BONUS, 0.0), -1.0)
        rank = _rank_desc(score, n_sel, n_sel - 1)
        sel = (rank < SEL_TOPN).astype(BF16)
        for c in range(n_chunks_total):
            mask_s[c] = _dot(sel, exp_ref[c])

        q_rot = qr_ref[g * HEADS_PER_GROUP:(g + 1) * HEADS_PER_GROUP].reshape(HEADS_PER_GROUP * tq, HEAD_DIM)

        def sel_body(c, carry):
            m_i, l_i, acc = carry
            k0 = pl.multiple_of(c * CK, CK)
            k = ksel_ref[g, pl.ds(k0, CK), :]
            v = vsel_ref[g, pl.ds(k0, CK), :]
            s = _dot_nt(q_rot, k).reshape(HEADS_PER_GROUP, tq, CK)
            kpos = k0 + lax.broadcasted_iota(jnp.int32, (1, CK), 1)
            mk = ((mask_s[c] > 0.5) & (kpos <= tpos))[None]
            s = jnp.where(mk, s, NEG_INF)
            m_new = jnp.maximum(m_i, jnp.max(s, axis=-1, keepdims=True))
            alpha = jnp.exp(m_i - m_new)
            p = jnp.where(mk, jnp.exp(s - m_new), 0.0)
            l_new = alpha * l_i + jnp.sum(p, axis=-1, keepdims=True)
            pv = _dot(p.reshape(HEADS_PER_GROUP * tq, CK).astype(BF16), v)
            acc = alpha * acc + pv.reshape(HEADS_PER_GROUP, tq, HEAD_DIM)
            return m_new, l_new, acc

        init = (jnp.full((HEADS_PER_GROUP, tq, 1), NEG_INF, F32),
                jnp.zeros((HEADS_PER_GROUP, tq, 1), F32),
                jnp.zeros((HEADS_PER_GROUP, tq, HEAD_DIM), F32))
        _, l_f, acc_f = lax.fori_loop(0, n_chunks, sel_body, init)
        o_sel = acc_f / jnp.maximum(l_f, TINY)

        k = kwin_ref[g, pl.ds(win_start, win_span), :]
        v = vwin_ref[g, pl.ds(win_start, win_span), :]
        s = _dot_nt(q_rot, k).reshape(HEADS_PER_GROUP, tq, win_span)
        s = jnp.where(win_mask, s, NEG_INF)
        m = jnp.max(s, axis=-1, keepdims=True)
        e = jnp.where(win_mask, jnp.exp(s - m), 0.0)
        p = e / jnp.maximum(jnp.sum(e, axis=-1, keepdims=True), TINY)
        o_win = _dot(p.reshape(HEADS_PER_GROUP * tq, win_span).astype(BF16), v).reshape(HEADS_PER_GROUP, tq, HEAD_DIM)

        for hh in range(HEADS_PER_GROUP):
            h = g * HEADS_PER_GROUP + hh
            g_cmp = gate_ref[:, 3 * h:3 * h + 1]
            g_sel = gate_ref[:, 3 * h + 1:3 * h + 2]
            g_win = gate_ref[:, 3 * h + 2:3 * h + 3]
            o_h = g_cmp * o_cmp[hh] + g_sel * o_sel[hh] + g_win * o_win[hh]
            o_ref[:, h * HEAD_DIM:(h + 1) * HEAD_DIM] = o_h.astype(BF16)


def _nsa(qc, qr, kvc, ksel, vsel, kwin, vwin, gates, w1, w2, pos, ovl, expand, B, S):
    T = B * S
    tq = TQ
    nq = S // tq
    q_spec = pl.BlockSpec((NSA_HEADS, tq, HEAD_DIM), lambda b, i: (0, b * nq + i, 0))
    kv_spec = pl.BlockSpec((NSA_GROUPS, S, HEAD_DIM), lambda b, i: (0, b, 0))
    full = lambda a: pl.BlockSpec(a.shape, lambda b, i: (0,) * a.ndim)
    return pl.pallas_call(
        functools.partial(_nsa_kernel, S=S),
        grid=(B, nq),
        in_specs=[q_spec, q_spec,
                  pl.BlockSpec((1, 2 * NSA_GROUPS, N_CMP_PAD, CMP_HALF), lambda b, i: (b, 0, 0, 0)),
                  kv_spec, kv_spec, kv_spec, kv_spec,
                  pl.BlockSpec((tq, GATE_PAD), lambda b, i: (b * nq + i, 0)),
                  full(w1), full(w2), full(pos), full(ovl), full(expand)],
        out_specs=pl.BlockSpec((tq, NSA_WIDTH), lambda b, i: (b * nq + i, 0)),
        out_shape=jax.ShapeDtypeStruct((T, NSA_WIDTH), BF16),
        scratch_shapes=[pltpu.VMEM((NSA_GROUPS, N_CMP_PAD, HEAD_DIM), BF16),
                        pltpu.VMEM((NSA_GROUPS, N_CMP_PAD, HEAD_DIM), BF16),
                        pltpu.VMEM((S // CK, tq, CK), F32)],
        compiler_params=pltpu.CompilerParams(dimension_semantics=("arbitrary", "arbitrary"),
                                             vmem_limit_bytes=VMEM_LIMIT),
        name="nsa",
    )(qc, qr, kvc, ksel, vsel, kwin, vwin, gates, w1, w2, pos, ovl, expand)


ROUTE_E, ROUTE_R, ROUTE_W = 0, TOP_K, 2 * TOP_K


def _merge_kernel(x_ref, onsa_ref, pool_ref, prev_ref, qm_ref, km_ref, vm_ref,
                  gmix_ref, wmg_ref, wpool_ref, pscale_ref, wun_ref, wup_ref, wum_ref, wout_ref,
                  gffn_ref, wrh_ref, wrl_ref, br_ref, tri_ref, upper_ref,
                  x1_ref, hp_ref, route_ref, cnt_ref, omem_s, carry_s, *, S):
    i = pl.program_id(0)
    tm = TM_MERGE
    n_s = S // tm

    @pl.when(i == 0)
    def _init():
        carry_s[...] = jnp.zeros_like(carry_s)

    x = x_ref[...]
    h = _rms(x, gmix_ref[...]).astype(BF16)
    mg = jax.nn.sigmoid(_dot(h, wmg_ref[...]))

    u = pool_ref[...]
    seq_tile = i % n_s
    prev = jnp.where(seq_tile == 0, 0.0, prev_ref[...])
    ext = jnp.concatenate([prev, u], axis=0)
    b2 = ext[1:] + ext[:-1]
    b4 = b2[2:] + b2[:-2]
    b8 = b4[4:] + b4[:-4]
    b16 = b8[8:] + b8[:-8]
    sums = (b2[POOL_HALO - 1:POOL_HALO - 1 + tm], b4[POOL_HALO - 3:POOL_HALO - 3 + tm],
            b8[POOL_HALO - 7:POOL_HALO - 7 + tm], b16[POOL_HALO - 15:POOL_HALO - 15 + tm])
    t_seq = seq_tile * tm + lax.broadcasted_iota(jnp.int32, (tm, 1), 0)
    lane_p = lax.broadcasted_iota(jnp.int32, (tm, POOL_WIDTH), 1)
    z = jnp.zeros((tm, POOL_WIDTH), F32)
    for gi, w in enumerate(POOL_WINDOWS):
        cnt = jnp.minimum(t_seq + 1, w).astype(F32)
        z = jnp.where(lane_p // POOL_GROUP == gi, sums[gi] / cnt, z)
    z = z - u
    o_pool = (_dot(z.astype(BF16), wpool_ref[...]) * pscale_ref[...]).astype(BF16)

    for hh in range(MEM_HEADS):
        s = _dot_nt(qm_ref[hh], km_ref[0, hh])
        m = jnp.max(s, axis=-1, keepdims=True)
        e = jnp.exp(s - m)
        p = e / jnp.sum(e, axis=-1, keepdims=True)
        omem_s[:, hh * HEAD_DIM:(hh + 1) * HEAD_DIM] = _dot(p.astype(BF16), vm_ref[0, hh]).astype(BF16)

    merged = (mg[:, :D_MODEL] * _dot(onsa_ref[...], wun_ref[...])
              + mg[:, D_MODEL:2 * D_MODEL] * _dot(o_pool, wup_ref[...])
              + mg[:, 2 * D_MODEL:] * _dot(omem_s[...], wum_ref[...]))
    x1 = x + _dot(merged.astype(BF16), wout_ref[...])
    x1_ref[...] = x1
    hf = _rms(x1, gffn_ref[...])
    hp_ref[...] = _pack_bf16_pairs(hf)

    hf_hi = hf.astype(BF16)
    hf_lo = (hf - hf_hi.astype(F32)).astype(BF16)
    logits = (_dot(hf_hi, wrh_ref[...]) + _dot(hf_hi, wrl_ref[...]) + _dot(hf_lo, wrh_ref[...])) + br_ref[...]
    lane = lax.broadcasted_iota(jnp.int32, (tm, LANES), 1)
    rank = _rank_desc(logits, N_EXPERTS, N_EXPERTS - 1)
    chosen = (rank < TOP_K) & (lane < N_EXPERTS)
    m = jnp.max(logits, axis=-1, keepdims=True)
    e = jnp.where(chosen, jnp.exp(logits - m), 0.0)
    comb = e / jnp.sum(e, axis=-1, keepdims=True)

    chosen_b = chosen.astype(BF16)
    chosen_f = chosen.astype(F32)
    in_expert = _dot(tri_ref[...], chosen_b) + carry_s[0:1]
    carry_new = carry_s[0:1] + jnp.sum(chosen_f, axis=0, keepdims=True)
    carry_s[...] = jnp.broadcast_to(carry_new, carry_s.shape)
    cnt_ref[...] = jnp.broadcast_to(carry_new, cnt_ref.shape)

    before = _dot(chosen_b, upper_ref[...])
    lane_f = lane.astype(F32)
    route = jnp.zeros((tm, LANES), F32)
    for k in range(TOP_K):
        pick = chosen & (before == k)
        e_k = jnp.sum(jnp.where(pick, lane_f, 0.0), axis=-1, keepdims=True)
        r_k = jnp.sum(jnp.where(pick, in_expert, 0.0), axis=-1, keepdims=True)
        w_k = jnp.sum(jnp.where(pick, comb, 0.0), axis=-1, keepdims=True)
        route = (route + jnp.where(lane == ROUTE_E + k, e_k, 0.0) + jnp.where(lane == ROUTE_R + k, r_k, 0.0)
                 + jnp.where(lane == ROUTE_W + k, w_k, 0.0))
    route_ref[...] = route


def _merge(x2, onsa, pool_in, qm, km, vm, consts, B, S):
    T = B * S
    tm = TM_MERGE
    n_s = S // tm
    M = km.shape[2]
    halo_per_tile = tm // POOL_HALO
    row = lambda w: pl.BlockSpec((tm, w), lambda i: (i, 0))
    full = lambda a: pl.BlockSpec(a.shape, lambda i: (0,) * a.ndim)
    mem_spec = pl.BlockSpec((1, MEM_HEADS, M, HEAD_DIM), lambda i: (i // n_s, 0, 0, 0))
    return pl.pallas_call(
        functools.partial(_merge_kernel, S=S),
        grid=(T // tm,),
        in_specs=[row(D_MODEL), row(NSA_WIDTH), row(POOL_WIDTH),
                  pl.BlockSpec((POOL_HALO, POOL_WIDTH), lambda i: (jnp.maximum(i * halo_per_tile - 1, 0), 0)),
                  pl.BlockSpec((MEM_HEADS, tm, HEAD_DIM), lambda i: (0, i, 0)),
                  mem_spec, mem_spec] + [full(c) for c in consts],
        out_specs=[row(D_MODEL), row(PACKED), row(LANES), pl.BlockSpec((8, LANES), lambda i: (0, 0))],
        out_shape=[jax.ShapeDtypeStruct((T, D_MODEL), F32),
                   jax.ShapeDtypeStruct((T, PACKED), U32),
                   jax.ShapeDtypeStruct((T, LANES), F32),
                   jax.ShapeDtypeStruct((8, LANES), F32)],
        scratch_shapes=[pltpu.VMEM((tm, MEM_WIDTH), BF16), pltpu.VMEM((8, LANES), F32)],
        compiler_params=pltpu.CompilerParams(dimension_semantics=("arbitrary",), vmem_limit_bytes=VMEM_LIMIT),
        name="merge",
    )(x2, onsa, pool_in, pool_in, qm, km, vm, *consts)


def _gather_kernel(idx_ref, table_ref, out_ref, sem):
    rows = out_ref.shape[0]

    def row_copy(src_row, dst_row):
        return pltpu.make_async_copy(table_ref.at[pl.ds(src_row, 1)], out_ref.at[pl.ds(dst_row, 1)], sem)

    def issue(r, c):
        row_copy(idx_ref[0, 0, r], r).start()
        return c

    lax.fori_loop(0, rows, issue, 0)
    pltpu.make_async_copy(table_ref.at[pl.ds(0, rows)], out_ref, sem).wait()


def _gather_rows(table, idx):
    n = idx.shape[0]
    rows = R_GATHER
    width = table.shape[1]
    idx3 = idx.reshape(n // rows, 1, rows)
    return pl.pallas_call(
        _gather_kernel,
        grid=(n // rows,),
        in_specs=[pl.BlockSpec((1, 1, rows), lambda i: (i, 0, 0), memory_space=pltpu.SMEM),
                  pl.BlockSpec(memory_space=pl.ANY)],
        out_specs=pl.BlockSpec((rows, width), lambda i: (i, 0)),
        out_shape=jax.ShapeDtypeStruct((n, width), table.dtype),
        scratch_shapes=[pltpu.SemaphoreType.DMA],
        compiler_params=pltpu.CompilerParams(dimension_semantics=("arbitrary",), vmem_limit_bytes=VMEM_LIMIT),
        name="gather_rows",
    )(idx3, table)


def _moe_kernel(te_ref, nu_ref, xs_ref, wgu_ref, bgu_ref, wd_ref, bd_ref, ys_ref):
    j = pl.program_id(0)

    @pl.when(j < nu_ref[0])
    def _compute():
        xb = _unpack_bf16_pairs(xs_ref[...]).astype(BF16)
        gu = _dot(xb, wgu_ref[0]) + bgu_ref[0]
        gate = jnp.minimum(gu[:, :D_FF], SWIGLU_LIMIT)
        up = jnp.clip(gu[:, D_FF:], -SWIGLU_LIMIT, SWIGLU_LIMIT)
        act = (up + 1.0) * (gate * jax.nn.sigmoid(SWIGLU_ALPHA * gate))
        y = _dot(act.astype(BF16), wd_ref[0]) + bd_ref[0]
        ys_ref[...] = _pack_bf16_pairs(y)

    @pl.when(j >= nu_ref[0])
    def _unused():
        ys_ref[...] = jnp.zeros_like(ys_ref)


def _moe(tile_expert, n_used, xs, wgu, bgu, wd, bd):
    P = xs.shape[0]
    tm = TM_E
    grid_spec = pltpu.PrefetchScalarGridSpec(
        num_scalar_prefetch=2,
        grid=(P // tm,),
        in_specs=[pl.BlockSpec((tm, PACKED), lambda j, te, nu: (j, 0)),
                  pl.BlockSpec((1, D_MODEL, 2 * D_FF), lambda j, te, nu: (te[j], 0, 0)),
                  pl.BlockSpec((1, 1, 2 * D_FF), lambda j, te, nu: (te[j], 0, 0)),
                  pl.BlockSpec((1, D_FF, D_MODEL), lambda j, te, nu: (te[j], 0, 0)),
                  pl.BlockSpec((1, 1, D_MODEL), lambda j, te, nu: (te[j], 0, 0))],
        out_specs=pl.BlockSpec((tm, PACKED), lambda j, te, nu: (j, 0)),
    )
    return pl.pallas_call(
        _moe_kernel,
        grid_spec=grid_spec,
        out_shape=jax.ShapeDtypeStruct((P, PACKED), U32),
        compiler_params=pltpu.CompilerParams(dimension_semantics=("arbitrary",), vmem_limit_bytes=VMEM_LIMIT),
        name="moe",
    )(tile_expert, n_used, xs, wgu, bgu, wd, bd)


def _final_kernel(x1_ref, yg_ref, route_ref, g_ref, o_ref):
    acc = x1_ref[...]
    for k in range(TOP_K):
        acc = acc + route_ref[:, ROUTE_W + k:ROUTE_W + k + 1] * _unpack_bf16_pairs(yg_ref[k])
    o_ref[...] = _rms(acc, g_ref[...])


def _final(x1, yg, route, g):
    T = x1.shape[0]
    tm = TM_FINAL
    return pl.pallas_call(
        _final_kernel,
        grid=(T // tm,),
        in_specs=[pl.BlockSpec((tm, D_MODEL), lambda i: (i, 0)),
                  pl.BlockSpec((TOP_K, tm, PACKED), lambda i: (0, i, 0)),
                  pl.BlockSpec((tm, LANES), lambda i: (i, 0)),
                  pl.BlockSpec((1, D_MODEL), lambda i: (0, 0))],
        out_specs=pl.BlockSpec((tm, D_MODEL), lambda i: (i, 0)),
        out_shape=jax.ShapeDtypeStruct((T, D_MODEL), F32),
        compiler_params=pltpu.CompilerParams(dimension_semantics=("arbitrary",), vmem_limit_bytes=VMEM_LIMIT),
        name="final",
    )(x1, yg, route, g)


def _rope_tables(S):
    half = HEAD_DIM // 2
    inv = ROPE_THETA ** (-jnp.arange(half, dtype=F32) / half)
    ang = jnp.arange(S, dtype=F32)[:, None] * inv[None, :]
    cos = jnp.tile(jnp.cos(ang), (1, LANES // half))
    sin = jnp.tile(jnp.concatenate([-jnp.sin(ang), jnp.sin(ang)], axis=1), (1, LANES // HEAD_DIM))
    return cos, sin


def _selection_constants(S):
    nc = (S - CMP_BLOCK) // CMP_STRIDE + 1
    n_sel = S // SEL_BLOCK
    i = jnp.arange(N_CMP_PAD)[:, None]
    j = jnp.arange(LANES)[None, :] % n_sel
    overlap = ((i * CMP_STRIDE <= j * SEL_BLOCK + SEL_BLOCK - 1)
               & (i * CMP_STRIDE + CMP_BLOCK - 1 >= j * SEL_BLOCK) & (i < nc)).astype(BF16)
    row = jnp.arange(LANES)[:, None]
    key = jnp.arange(S)[None, :]
    expand = ((key // SEL_BLOCK == row) & (row < n_sel)).astype(BF16)
    expand = expand.reshape(LANES, S // CK, CK).transpose(1, 0, 2)
    return overlap, expand


def kernel(x, mem, norm_mix, norm_mem, w_in, cmp_pos, cmp_w1, cmp_w2, w_pool, pool_scale, w_mem_kv, w_up_nsa,
           w_up_pool, w_up_mem, w_out, norm_ffn, w_router, b_router, w_gate_up, b_gate_up, w_down, b_down,
           norm_final):
    B, S, D = x.shape
    T = B * S
    assert D == D_MODEL and S % CK == 0 and S // SEL_BLOCK == 32 and T % R_GATHER == 0
    l = 0
    x2 = x.reshape(T, D)

    w = w_in[l]
    o_gate = NSA_WIDTH + 6 * KV_WIDTH
    n_gate = 3 * NSA_HEADS
    o_pool = o_gate + n_gate
    o_qm = o_pool + POOL_WIDTH
    o_mg = o_qm + MEM_WIDTH
    w_a = jnp.concatenate([w[:, :o_gate], w[:, o_pool:o_mg], w[:, o_gate:o_pool],
                           jnp.zeros((D, GATE_PAD - n_gate), F32)], axis=1).astype(BF16)
    w_mg = w[:, o_mg:].astype(BF16)
    cos_t, sin_t = _rope_tables(S)
    overlap, expand = _selection_constants(S)
    w1 = cmp_w1[l].reshape(2, CMP_FLAT, HEAD_DIM).astype(BF16)
    w2 = cmp_w2[l].astype(BF16)
    pos = jnp.broadcast_to(cmp_pos[l].reshape(2, 1, CMP_FLAT), (2, 8, CMP_FLAT)).astype(BF16)
    wpool_bd = jnp.zeros((POOL_WIDTH, POOL_WIDTH), F32)
    for gi in range(len(POOL_WINDOWS)):
        wpool_bd = wpool_bd.at[gi * POOL_GROUP:(gi + 1) * POOL_GROUP, gi * POOL_GROUP:(gi + 1) * POOL_GROUP].set(w_pool[l, gi])
    wr = jnp.tile(w_router[l], (1, LANES // N_EXPERTS))
    wr_hi = wr.astype(BF16)
    wr_lo = (wr - wr_hi.astype(F32)).astype(BF16)
    br = jnp.tile(b_router[l], LANES // N_EXPERTS).reshape(1, LANES)
    tri = (jnp.arange(TM_MERGE)[None, :] < jnp.arange(TM_MERGE)[:, None]).astype(BF16)
    e_row = jnp.arange(LANES)[:, None]
    upper = ((e_row < jnp.arange(LANES)[None, :]) & (e_row < N_EXPERTS)).astype(BF16)

    km, vm = _memkv(mem, norm_mem[l].reshape(1, D), w_mem_kv[l].astype(BF16))
    qc, qr, kvcmp, ksel, vsel, kwin, vwin, pool_in, qm, gates = _inproj(
        x2, norm_mix[l].reshape(1, D), w_a, cos_t, sin_t, S)
    kvc = kvcmp.reshape(B, S // CMP_STRIDE, CMP_STRIDE, 2 * NSA_GROUPS, HEAD_DIM).transpose(0, 3, 1, 2, 4)
    kvc = kvc.reshape(B, 2 * NSA_GROUPS, S // CMP_STRIDE, CMP_HALF)
    o_nsa = _nsa(qc, qr, kvc, ksel, vsel, kwin, vwin, gates, w1, w2, pos, overlap, expand, B, S)
    consts = [norm_mix[l].reshape(1, D), w_mg, wpool_bd.astype(BF16), pool_scale[l].reshape(1, POOL_WIDTH),
              w_up_nsa[l].astype(BF16), w_up_pool[l].astype(BF16), w_up_mem[l].astype(BF16), w_out[l].astype(BF16),
              norm_ffn[l].reshape(1, D), wr_hi, wr_lo, br, tri, upper]
    x1, h_packed, route, counts = _merge(x2, o_nsa, pool_in, qm, km, vm, consts, B, S)

    counts = counts[0, :N_EXPERTS].astype(jnp.int32)
    padded = ((counts + TM_E - 1) // TM_E) * TM_E
    ends = jnp.cumsum(padded)
    starts = ends - padded
    n_tiles = (T * TOP_K) // TM_E + N_EXPERTS
    P = n_tiles * TM_E
    e_k = route[:, ROUTE_E:ROUTE_E + TOP_K].astype(jnp.int32)
    r_k = route[:, ROUTE_R:ROUTE_R + TOP_K].astype(jnp.int32)
    dest = starts[e_k] + r_k
    tok = jnp.broadcast_to(jnp.arange(T, dtype=jnp.int32)[:, None], (T, TOP_K))
    src = jnp.zeros((P,), jnp.int32).at[dest.reshape(-1)].set(tok.reshape(-1))
    tile_start = jnp.arange(n_tiles, dtype=jnp.int32) * TM_E
    tile_expert = jnp.minimum(jnp.sum(tile_start[:, None] >= ends[None, :], axis=1), N_EXPERTS - 1).astype(jnp.int32)
    n_used = (ends[-1] // TM_E).astype(jnp.int32).reshape(1)

    xs = _gather_rows(h_packed, src)
    ys = _moe(tile_expert, n_used, xs, w_gate_up[l].astype(BF16), b_gate_up[l].reshape(N_EXPERTS, 1, 2 * D_FF),
              w_down[l].astype(BF16), b_down[l].reshape(N_EXPERTS, 1, D_MODEL))
    yg = _gather_rows(ys, dest.T.reshape(-1)).reshape(TOP_K, T, PACKED)
    out = _final(x1, yg, route, norm_final.reshape(1, D))
    return out.reshape(B, S, D)
```

```python
import functools

import jax
import jax.numpy as jnp
from jax import lax
from jax.experimental import pallas as pl
from jax.experimental.pallas import tpu as pltpu

F32 = jnp.float32
BF16 = jnp.bfloat16

D_MODEL = 1024
HEAD_DIM = 64
NSA_HEADS = 8
NSA_GROUPS = 2
HEADS_PER_GROUP = NSA_HEADS // NSA_GROUPS
NSA_WIDTH = NSA_HEADS * HEAD_DIM
KV_WIDTH = NSA_GROUPS * HEAD_DIM
CMP_BLOCK = 32
CMP_STRIDE = 16
SEL_BLOCK = 64
SEL_TOPN = 8
FORCE_BONUS = 1000.0
WINDOW = 512
POOL_WINDOWS = (2, 4, 8, 16)
POOL_GROUP = 64
POOL_WIDTH = POOL_GROUP * len(POOL_WINDOWS)
POOL_HALO = 16
MEM_HEADS = 4
MEM_WIDTH = MEM_HEADS * HEAD_DIM
N_EXPERTS = 32
TOP_K = 4
D_FF = 1024
SWIGLU_LIMIT = 7.0
SWIGLU_ALPHA = 1.702
ROPE_THETA = 10000.0
EPS = 1e-5
NEG_INF = -1e30
TINY = 1e-30
QK_SCALE = HEAD_DIM ** -0.5

LANES = 128
GATE_PAD = LANES

TM_IN = 512
TQ = 128
CKS = 512
WIN_CHUNKS = (256, 256, 128)
TM_MERGE = 256
TM_E = 512
TM_DISPATCH = 512
R_GATHER = 512
TM_FINAL = 512
VMEM_LIMIT = 56 * 1024 * 1024


def _rms(x, g):
    return x * lax.rsqrt(jnp.mean(x * x, axis=-1, keepdims=True) + EPS) * g


def _dot(a, b):
    return jnp.dot(a, b, preferred_element_type=F32)


def _dot_nt(a, b):
    return lax.dot_general(a, b, (((1,), (1,)), ((), ())), preferred_element_type=F32)


def _rank_desc(score, period, n_rolls):
    lane = lax.broadcasted_iota(jnp.int32, score.shape, 1) % period
    rank = jnp.zeros(score.shape, F32)
    for r in range(1, n_rolls + 1):
        other = pltpu.roll(score, r, 1)
        beats = (other > score) | ((other == score) & (lane >= r))
        rank = rank + beats.astype(F32)
    return rank


def _memkv_kernel(mem_ref, g_ref, w_ref, k_ref, v_ref):
    m = _rms(mem_ref[0], g_ref[...]).astype(BF16)
    kv = _dot(m, w_ref[...])
    for h in range(MEM_HEADS):
        k_ref[0, h] = kv[:, h * HEAD_DIM:(h + 1) * HEAD_DIM].astype(BF16)
        v_ref[0, h] = kv[:, MEM_WIDTH + h * HEAD_DIM:MEM_WIDTH + (h + 1) * HEAD_DIM].astype(BF16)


def _memkv(mem, g, w):
    B, M, D = mem.shape
    return pl.pallas_call(
        _memkv_kernel,
        grid=(B,),
        in_specs=[pl.BlockSpec((1, M, D), lambda b: (b, 0, 0)),
                  pl.BlockSpec((1, D), lambda b: (0, 0)),
                  pl.BlockSpec((D, 2 * MEM_WIDTH), lambda b: (0, 0))],
        out_specs=[pl.BlockSpec((1, MEM_HEADS, M, HEAD_DIM), lambda b: (b, 0, 0, 0)),
                   pl.BlockSpec((1, MEM_HEADS, M, HEAD_DIM), lambda b: (b, 0, 0, 0))],
        out_shape=[jax.ShapeDtypeStruct((B, MEM_HEADS, M, HEAD_DIM), BF16),
                   jax.ShapeDtypeStruct((B, MEM_HEADS, M, HEAD_DIM), BF16)],
        compiler_params=pltpu.CompilerParams(dimension_semantics=("arbitrary",), vmem_limit_bytes=VMEM_LIMIT),
        name="memkv",
    )(mem, g, w)


IN_COLS = NSA_WIDTH + 6 * KV_WIDTH + POOL_WIDTH + MEM_WIDTH + GATE_PAD


def _inproj_kernel(x_ref, g_ref, w_ref, cos_ref, sin_ref,
                   qc_ref, qr_ref, kvc_ref, ksel_ref, vsel_ref, kwin_ref, vwin_ref, pool_ref, qm_ref, gate_ref):
    h = _rms(x_ref[...], g_ref[...]).astype(BF16)
    p = _dot(h, w_ref[...])
    cos = cos_ref[...]
    sin = sin_ref[...]
    lane = lax.broadcasted_iota(jnp.int32, cos.shape, 1)
    first_half = (lane % HEAD_DIM) < (HEAD_DIM // 2)

    def rope(c):
        partner = jnp.where(first_half, pltpu.roll(c, LANES - HEAD_DIM // 2, 1), pltpu.roll(c, HEAD_DIM // 2, 1))
        return c * cos + partner * sin

    def halves(c):
        return c[:, :HEAD_DIM], c[:, HEAD_DIM:]

    for j in range(NSA_WIDTH // LANES):
        c = p[:, j * LANES:(j + 1) * LANES]
        r = rope(c)
        for hh, (cc, rr) in enumerate(zip(halves(c), halves(r))):
            qc_ref[2 * j + hh] = (cc * QK_SCALE).astype(BF16)
            qr_ref[2 * j + hh] = (rr * QK_SCALE).astype(BF16)
    o = NSA_WIDTH
    kvc_ref[...] = p[:, o:o + 2 * KV_WIDTH]
    o += 2 * KV_WIDTH
    for ref, rot in ((ksel_ref, True), (vsel_ref, False), (kwin_ref, True), (vwin_ref, False)):
        c = p[:, o:o + KV_WIDTH]
        if rot:
            c = rope(c)
        for g, cc in enumerate(halves(c)):
            ref[g] = cc.astype(BF16)
        o += KV_WIDTH
    pool_ref[...] = p[:, o:o + POOL_WIDTH]
    o += POOL_WIDTH
    for hh in range(MEM_HEADS):
        qm_ref[hh] = (p[:, o + hh * HEAD_DIM:o + (hh + 1) * HEAD_DIM] * QK_SCALE).astype(BF16)
    o += MEM_WIDTH
    gate_ref[...] = jax.nn.sigmoid(p[:, o:o + GATE_PAD])


def _inproj(x2, g, w_a, cos_t, sin_t, S):
    T = x2.shape[0]
    tm = TM_IN
    n_s = S // tm
    head_spec = lambda n: pl.BlockSpec((n, tm, HEAD_DIM), lambda i: (0, i, 0))
    row_spec = lambda w: pl.BlockSpec((tm, w), lambda i: (i, 0))
    return pl.pallas_call(
        _inproj_kernel,
        grid=(T // tm,),
        in_specs=[row_spec(D_MODEL),
                  pl.BlockSpec((1, D_MODEL), lambda i: (0, 0)),
                  pl.BlockSpec((D_MODEL, IN_COLS), lambda i: (0, 0)),
                  pl.BlockSpec((tm, LANES), lambda i: (i % n_s, 0)),
                  pl.BlockSpec((tm, LANES), lambda i: (i % n_s, 0))],
        out_specs=[head_spec(NSA_HEADS), head_spec(NSA_HEADS), row_spec(2 * KV_WIDTH),
                   head_spec(NSA_GROUPS), head_spec(NSA_GROUPS), head_spec(NSA_GROUPS), head_spec(NSA_GROUPS),
                   row_spec(POOL_WIDTH), head_spec(MEM_HEADS), row_spec(GATE_PAD)],
        out_shape=[jax.ShapeDtypeStruct((NSA_HEADS, T, HEAD_DIM), BF16),
                   jax.ShapeDtypeStruct((NSA_HEADS, T, HEAD_DIM), BF16),
                   jax.ShapeDtypeStruct((T, 2 * KV_WIDTH), F32),
                   jax.ShapeDtypeStruct((NSA_GROUPS, T, HEAD_DIM), BF16),
                   jax.ShapeDtypeStruct((NSA_GROUPS, T, HEAD_DIM), BF16),
                   jax.ShapeDtypeStruct((NSA_GROUPS, T, HEAD_DIM), BF16),
                   jax.ShapeDtypeStruct((NSA_GROUPS, T, HEAD_DIM), BF16),
                   jax.ShapeDtypeStruct((T, POOL_WIDTH), F32),
                   jax.ShapeDtypeStruct((MEM_HEADS, T, HEAD_DIM), BF16),
                   jax.ShapeDtypeStruct((T, GATE_PAD), F32)],
        compiler_params=pltpu.CompilerParams(dimension_semantics=("arbitrary",), vmem_limit_bytes=VMEM_LIMIT),
        name="inproj",
    )(x2, g, w_a, cos_t, sin_t)


N_CMP_PAD = 128
CMP_FLAT = CMP_BLOCK * HEAD_DIM
CMP_HALF = CMP_STRIDE * HEAD_DIM


KV_LANE_CHUNK = LANES


def _nsa_kernel(qc_ref, qr_ref, kvc_ref, ksel_ref, vsel_ref, kwin_ref, vwin_ref, gate_ref,
                w1_ref, w2_ref, w2t_ref, pos_ref, ovl_ref, drop_ref, o_ref, kc_s, vct_s, *, S):
    i = pl.program_id(1)
    tq = TQ
    hpg = HEADS_PER_GROUP
    n_sel = S // SEL_BLOCK
    hq = hpg * tq

    @pl.when(i == 0)
    def _compress():
        for kv in range(2):
            w1 = w1_ref[kv]
            posterm = _dot(pos_ref[kv], w1)[0:1]
            for g in range(NSA_GROUPS):
                a = kvc_ref[0, kv * NSA_GROUPS + g].astype(BF16)
                p1 = _dot(a, w1[:CMP_HALF])
                p2 = _dot(a, w1[CMP_HALF:])
                hid = p1 + pltpu.roll(p2, N_CMP_PAD - 1, 0) + posterm
                hid = (hid * jax.nn.sigmoid(hid)).astype(BF16)
                if kv == 0:
                    kc_s[g] = _dot(hid, w2_ref[kv]).astype(BF16)
                else:
                    vct_s[g] = _dot_nt(w2t_ref[kv], hid).astype(BF16)

    q0 = i * tq
    t_lane = q0 + lax.broadcasted_iota(jnp.int32, (1, tq), 1)

    def key_pos(start, n):
        return start + lax.broadcasted_iota(jnp.int32, (n, 1), 0)

    cmp_valid = (key_pos(0, N_CMP_PAD) * CMP_STRIDE + CMP_BLOCK - 1) <= t_lane
    jrow = lax.broadcasted_iota(jnp.int32, (n_sel, tq), 0)
    cur = t_lane // SEL_BLOCK
    sel_valid = jrow * SEL_BLOCK <= t_lane
    forced = (jrow == 0) | (jrow == cur) | (jrow == cur - 1)

    cd = q0 // CKS
    causal_bias = jnp.where(key_pos(cd * CKS, CKS) <= t_lane, 0.0, NEG_INF)
    win_chunks = []
    hi = q0 + tq
    for n in WIN_CHUNKS:
        lo = hi - n
        start = pl.multiple_of(jnp.maximum(lo, 0), LANES)
        kp = key_pos(start, n)
        diff = t_lane - kp
        win_chunks.append((start, n, jnp.where((diff >= 0) & (diff < WINDOW) & (kp < hi), 0.0, NEG_INF)))
        hi = lo

    def values_t(ref, g, start, n):
        c0 = start // KV_LANE_CHUNK
        return jnp.concatenate([ref[g, 0, c0 + j] for j in range(n // KV_LANE_CHUNK)], axis=1)

    def attend(q_all, k, v_t, bias, state):
        m_i, l_i, acc = state
        n = k.shape[0]
        s_all = _dot_nt(k, q_all)
        ms, ls, alphas, ps = [], [], [], []
        for hh in range(hpg):
            sl = slice(hh * tq, (hh + 1) * tq)
            s = s_all[:, sl] + bias
            m_new = jnp.maximum(m_i[:, sl], jnp.max(s, axis=0, keepdims=True))
            alpha = jnp.exp(m_i[:, sl] - m_new)
            p = jnp.exp(s - m_new)
            ls.append(alpha * l_i[:, sl] + jnp.sum(p, axis=0, keepdims=True))
            ps.append(p.astype(BF16))
            ms.append(m_new)
            alphas.append(alpha)
        pv = jnp.concatenate([_dot(v_t, jnp.concatenate(ps[a:a + 2], axis=1)) for a in range(0, hpg, 2)], axis=1)
        return (jnp.concatenate(ms, axis=1), jnp.concatenate(ls, axis=1),
                jnp.concatenate(alphas, axis=1) * acc + pv)

    def fresh_state():
        return jnp.full((1, hq), NEG_INF, F32), jnp.zeros((1, hq), F32), jnp.zeros((HEAD_DIM, hq), F32)

    def select_blocks(g):
        q_cmp = qc_ref[g * hpg:(g + 1) * hpg].reshape(hq, HEAD_DIM)
        s_all = _dot_nt(kc_s[g], q_cmp)
        p_grp = jnp.zeros((N_CMP_PAD, tq), F32)
        ps = []
        for hh in range(hpg):
            sl = slice(hh * tq, (hh + 1) * tq)
            s = jnp.where(cmp_valid, s_all[:, sl], NEG_INF)
            m = jnp.max(s, axis=0, keepdims=True)
            e = jnp.where(cmp_valid, jnp.exp(s - m), 0.0)
            p = e * (1.0 / jnp.maximum(jnp.sum(e, axis=0, keepdims=True), TINY))
            p_grp = p_grp + p
            ps.append(p.astype(BF16))
        o_cmp = _dot(vct_s[g], jnp.concatenate(ps, axis=1))

        ovl = ovl_ref[...]
        p_hi = p_grp.astype(BF16)
        r1 = p_grp - p_hi.astype(F32)
        p_mid = r1.astype(BF16)
        p_lo = (r1 - p_mid.astype(F32)).astype(BF16)
        score = _dot(ovl, p_hi) + _dot(ovl, p_mid) + _dot(ovl, p_lo)
        score = jnp.where(sel_valid, score + jnp.where(forced, FORCE_BONUS, 0.0), -1.0)
        rank = jnp.zeros((n_sel, tq), F32)
        for jp in range(n_sel):
            other = score[jp:jp + 1, :]
            beats = (other > score) | ((other == score) & (jrow > jp))
            rank = rank + beats.astype(F32)
        return o_cmp, (rank >= SEL_TOPN).astype(BF16)

    groups = range(NSA_GROUPS)
    cmp_out = [select_blocks(g) for g in groups]
    q_rot = [qr_ref[g * hpg:(g + 1) * hpg].reshape(hq, HEAD_DIM) for g in groups]

    def attend_selected(g, c, extra_bias, state):
        k0 = pl.multiple_of(c * CKS, CKS)
        bias = _dot(drop_ref[c], cmp_out[g][1])
        if extra_bias is not None:
            bias = bias + extra_bias
        return attend(q_rot[g], ksel_ref[g, pl.ds(k0, CKS), :], values_t(vsel_ref, g, k0, CKS), bias, state)

    sel_state = tuple(attend_selected(g, cd, causal_bias, fresh_state()) for g in groups)

    win_state = [fresh_state() for g in groups]
    for start, n, bias in win_chunks:
        for g in groups:
            win_state[g] = attend(q_rot[g], kwin_ref[g, pl.ds(start, n), :], values_t(vwin_ref, g, start, n),
                                  bias, win_state[g])

    def sel_body(it, states):
        return tuple(attend_selected(g, cd - 1 - it, None, states[g]) for g in groups)

    sel_state = lax.fori_loop(0, cd, sel_body, sel_state)

    gates_t = gate_ref[...].T
    out_rows = []
    for g in groups:
        o_cmp = cmp_out[g][0]
        o_sel = sel_state[g][2] * (1.0 / sel_state[g][1])
        o_win = win_state[g][2] * (1.0 / win_state[g][1])
        for hh in range(hpg):
            h = g * hpg + hh
            sl = slice(hh * tq, (hh + 1) * tq)
            out_rows.append(gates_t[3 * h:3 * h + 1] * o_cmp[:, sl] + gates_t[3 * h + 1:3 * h + 2] * o_sel[:, sl]
                            + gates_t[3 * h + 2:3 * h + 3] * o_win[:, sl])
    o_ref[...] = jnp.concatenate(out_rows, axis=0).T.astype(BF16)


def _nsa(qc, qr, kvc, ksel, vsel_t, kwin, vwin_t, gates, w1, w2, w2t, pos, ovl, drop_bias, B, S):
    T = B * S
    tq = TQ
    nq = S // tq
    q_spec = pl.BlockSpec((NSA_HEADS, tq, HEAD_DIM), lambda b, i: (0, b * nq + i, 0))
    k_spec = pl.BlockSpec((NSA_GROUPS, S, HEAD_DIM), lambda b, i: (0, b, 0))
    v_spec = pl.BlockSpec((NSA_GROUPS, 1, S // KV_LANE_CHUNK, HEAD_DIM, KV_LANE_CHUNK), lambda b, i: (0, b, 0, 0, 0))
    full = lambda a: pl.BlockSpec(a.shape, lambda b, i: (0,) * a.ndim)
    return pl.pallas_call(
        functools.partial(_nsa_kernel, S=S),
        grid=(B, nq),
        in_specs=[q_spec, q_spec,
                  pl.BlockSpec((1, 2 * NSA_GROUPS, N_CMP_PAD, CMP_HALF), lambda b, i: (b, 0, 0, 0)),
                  k_spec, v_spec, k_spec, v_spec,
                  pl.BlockSpec((tq, GATE_PAD), lambda b, i: (b * nq + i, 0)),
                  full(w1), full(w2), full(w2t), full(pos), full(ovl), full(drop_bias)],
        out_specs=pl.BlockSpec((tq, NSA_WIDTH), lambda b, i: (b * nq + i, 0)),
        out_shape=jax.ShapeDtypeStruct((T, NSA_WIDTH), BF16),
        scratch_shapes=[pltpu.VMEM((NSA_GROUPS, N_CMP_PAD, HEAD_DIM), BF16),
                        pltpu.VMEM((NSA_GROUPS, HEAD_DIM, N_CMP_PAD), BF16)],
        compiler_params=pltpu.CompilerParams(dimension_semantics=("arbitrary", "arbitrary"),
                                             vmem_limit_bytes=VMEM_LIMIT),
        name="nsa",
    )(qc, qr, kvc, ksel, vsel_t, kwin, vwin_t, gates, w1, w2, w2t, pos, ovl, drop_bias)


ROUTE_E, ROUTE_R, ROUTE_W = 0, TOP_K, 2 * TOP_K


def _merge_kernel(x_ref, onsa_ref, pool_ref, prev_ref, qm_ref, km_ref, vm_ref,
                  gmix_ref, wmg_ref, wpool_ref, pscale_ref, wun_ref, wup_ref, wum_ref, wout_ref,
                  gffn_ref, wrh_ref, wrl_ref, br_ref, tri_ref, upper_ref,
                  x1_ref, hf_ref, route_ref, cnt_ref, omem_s, carry_s, *, S):
    i = pl.program_id(0)
    tm = TM_MERGE
    n_s = S // tm

    @pl.when(i == 0)
    def _init():
        carry_s[...] = jnp.zeros_like(carry_s)

    x = x_ref[...]
    h = _rms(x, gmix_ref[...]).astype(BF16)
    mg = jax.nn.sigmoid(_dot(h, wmg_ref[...]))

    u = pool_ref[...]
    seq_tile = i % n_s
    prev = jnp.where(seq_tile == 0, 0.0, prev_ref[...])
    ext = jnp.concatenate([prev, u], axis=0)
    b2 = ext[1:] + ext[:-1]
    b4 = b2[2:] + b2[:-2]
    b8 = b4[4:] + b4[:-4]
    b16 = b8[8:] + b8[:-8]
    sums = (b2[POOL_HALO - 1:POOL_HALO - 1 + tm], b4[POOL_HALO - 3:POOL_HALO - 3 + tm],
            b8[POOL_HALO - 7:POOL_HALO - 7 + tm], b16[POOL_HALO - 15:POOL_HALO - 15 + tm])
    t_seq = seq_tile * tm + lax.broadcasted_iota(jnp.int32, (tm, 1), 0)
    lane_p = lax.broadcasted_iota(jnp.int32, (tm, POOL_WIDTH), 1)
    z = jnp.zeros((tm, POOL_WIDTH), F32)
    for gi, w in enumerate(POOL_WINDOWS):
        cnt = jnp.minimum(t_seq + 1, w).astype(F32)
        z = jnp.where(lane_p // POOL_GROUP == gi, sums[gi] / cnt, z)
    z = z - u
    o_pool = (_dot(z.astype(BF16), wpool_ref[...]) * pscale_ref[...]).astype(BF16)

    for hh in range(MEM_HEADS):
        s = _dot_nt(qm_ref[hh], km_ref[0, hh])
        m = jnp.max(s, axis=-1, keepdims=True)
        e = jnp.exp(s - m)
        p = e / jnp.sum(e, axis=-1, keepdims=True)
        omem_s[:, hh * HEAD_DIM:(hh + 1) * HEAD_DIM] = _dot(p.astype(BF16), vm_ref[0, hh]).astype(BF16)

    merged = (mg[:, :D_MODEL] * _dot(onsa_ref[...], wun_ref[...])
              + mg[:, D_MODEL:2 * D_MODEL] * _dot(o_pool, wup_ref[...])
              + mg[:, 2 * D_MODEL:] * _dot(omem_s[...], wum_ref[...]))
    x1 = x + _dot(merged.astype(BF16), wout_ref[...])
    x1_ref[...] = x1
    hf = _rms(x1, gffn_ref[...])
    hf_ref[...] = hf

    hf_hi = hf.astype(BF16)
    hf_lo = (hf - hf_hi.astype(F32)).astype(BF16)
    logits = (_dot(hf_hi, wrh_ref[...]) + _dot(hf_hi, wrl_ref[...]) + _dot(hf_lo, wrh_ref[...])) + br_ref[...]
    lane = lax.broadcasted_iota(jnp.int32, (tm, LANES), 1)
    rank = _rank_desc(logits, N_EXPERTS, N_EXPERTS - 1)
    chosen = (rank < TOP_K) & (lane < N_EXPERTS)
    m = jnp.max(logits, axis=-1, keepdims=True)
    e = jnp.where(chosen, jnp.exp(logits - m), 0.0)
    comb = e / jnp.sum(e, axis=-1, keepdims=True)

    chosen_b = chosen.astype(BF16)
    chosen_f = chosen.astype(F32)
    in_expert = _dot(tri_ref[...], chosen_b) + carry_s[0:1]
    carry_new = carry_s[0:1] + jnp.sum(chosen_f, axis=0, keepdims=True)
    carry_s[...] = jnp.broadcast_to(carry_new, carry_s.shape)
    cnt_ref[...] = jnp.broadcast_to(carry_new, cnt_ref.shape)

    before = _dot(chosen_b, upper_ref[...])
    lane_f = lane.astype(F32)
    route = jnp.zeros((tm, LANES), F32)
    for k in range(TOP_K):
        pick = chosen & (before == k)
        e_k = jnp.sum(jnp.where(pick, lane_f, 0.0), axis=-1, keepdims=True)
        r_k = jnp.sum(jnp.where(pick, in_expert, 0.0), axis=-1, keepdims=True)
        w_k = jnp.sum(jnp.where(pick, comb, 0.0), axis=-1, keepdims=True)
        route = (route + jnp.where(lane == ROUTE_E + k, e_k, 0.0) + jnp.where(lane == ROUTE_R + k, r_k, 0.0)
                 + jnp.where(lane == ROUTE_W + k, w_k, 0.0))
    route_ref[...] = route


def _merge(x2, onsa, pool_in, qm, km, vm, consts, B, S):
    T = B * S
    tm = TM_MERGE
    n_s = S // tm
    M = km.shape[2]
    halo_per_tile = tm // POOL_HALO
    row = lambda w: pl.BlockSpec((tm, w), lambda i: (i, 0))
    full = lambda a: pl.BlockSpec(a.shape, lambda i: (0,) * a.ndim)
    mem_spec = pl.BlockSpec((1, MEM_HEADS, M, HEAD_DIM), lambda i: (i // n_s, 0, 0, 0))
    return pl.pallas_call(
        functools.partial(_merge_kernel, S=S),
        grid=(T // tm,),
        in_specs=[row(D_MODEL), row(NSA_WIDTH), row(POOL_WIDTH),
                  pl.BlockSpec((POOL_HALO, POOL_WIDTH), lambda i: (jnp.maximum(i * halo_per_tile - 1, 0), 0)),
                  pl.BlockSpec((MEM_HEADS, tm, HEAD_DIM), lambda i: (0, i, 0)),
                  mem_spec, mem_spec] + [full(c) for c in consts],
        out_specs=[row(D_MODEL), row(D_MODEL), row(LANES), pl.BlockSpec((8, LANES), lambda i: (0, 0))],
        out_shape=[jax.ShapeDtypeStruct((T, D_MODEL), F32),
                   jax.ShapeDtypeStruct((T, D_MODEL), F32),
                   jax.ShapeDtypeStruct((T, LANES), F32),
                   jax.ShapeDtypeStruct((8, LANES), F32)],
        scratch_shapes=[pltpu.VMEM((tm, MEM_WIDTH), BF16), pltpu.VMEM((8, LANES), F32)],
        compiler_params=pltpu.CompilerParams(dimension_semantics=("arbitrary",), vmem_limit_bytes=VMEM_LIMIT),
        name="merge",
    )(x2, onsa, pool_in, pool_in, qm, km, vm, *consts)


def _dispatch_kernel(ends_ref, padded_ref, dest_ref, h_ref, xs_ref, zero_s, sem, zsem):
    i = pl.program_id(0)
    tm = h_ref.shape[0]

    @pl.when(i == 0)
    def _clear_pad_tiles():
        zero_s[...] = jnp.zeros_like(zero_s)

        def tile_copy(e):
            start = pl.multiple_of(ends_ref[e] - TM_E, TM_E)
            return pltpu.make_async_copy(zero_s, xs_ref.at[pl.ds(start, TM_E)], zsem)

        for e in range(N_EXPERTS):
            @pl.when(padded_ref[e] > 0)
            def _():
                tile_copy(e).start()
        for e in range(N_EXPERTS):
            @pl.when(padded_ref[e] > 0)
            def _():
                tile_copy(e).wait()

        def tail_copy(j):
            return pltpu.make_async_copy(zero_s, xs_ref.at[pl.ds(pl.multiple_of(j * TM_E, TM_E), TM_E)], zsem)

        def tail_start(j, c):
            tail_copy(j).start()
            return c

        def tail_wait(j, c):
            tail_copy(j).wait()
            return c

        n_used = ends_ref[N_EXPERTS - 1] // TM_E
        lax.fori_loop(n_used, xs_ref.shape[0] // TM_E, tail_start, 0)
        lax.fori_loop(n_used, xs_ref.shape[0] // TM_E, tail_wait, 0)

    def issue(r, c):
        for k in range(TOP_K):
            d = dest_ref[0, 0, r * TOP_K + k]
            pltpu.make_async_copy(h_ref.at[pl.ds(r, 1)], xs_ref.at[pl.ds(d, 1)], sem).start()
        return c

    lax.fori_loop(0, tm, issue, 0)
    for k in range(TOP_K):
        pltpu.make_async_copy(h_ref, xs_ref.at[pl.ds(0, tm)], sem).wait()


def _dispatch(dest_flat, ends, padded, hf, P):
    T = hf.shape[0]
    tm = TM_DISPATCH
    dest3 = dest_flat.reshape(T // tm, 1, tm * TOP_K)
    grid_spec = pltpu.PrefetchScalarGridSpec(
        num_scalar_prefetch=2,
        grid=(T // tm,),
        in_specs=[pl.BlockSpec((1, 1, tm * TOP_K), lambda i, en, pd: (i, 0, 0), memory_space=pltpu.SMEM),
                  pl.BlockSpec((tm, D_MODEL), lambda i, en, pd: (i, 0))],
        out_specs=pl.BlockSpec(memory_space=pl.ANY),
        scratch_shapes=[pltpu.VMEM((TM_E, D_MODEL), F32), pltpu.SemaphoreType.DMA, pltpu.SemaphoreType.DMA],
    )
    return pl.pallas_call(
        _dispatch_kernel,
        grid_spec=grid_spec,
        out_shape=jax.ShapeDtypeStruct((P, D_MODEL), F32),
        compiler_params=pltpu.CompilerParams(dimension_semantics=("arbitrary",), vmem_limit_bytes=VMEM_LIMIT),
        name="dispatch",
    )(ends, padded, dest3, hf)


def _gather_kernel(idx_ref, table_ref, out_ref, sem):
    rows = out_ref.shape[0]

    def row_copy(src_row, dst_row):
        return pltpu.make_async_copy(table_ref.at[pl.ds(src_row, 1)], out_ref.at[pl.ds(dst_row, 1)], sem)

    def issue(r, c):
        row_copy(idx_ref[0, 0, r], r).start()
        return c

    lax.fori_loop(0, rows, issue, 0)
    pltpu.make_async_copy(table_ref.at[pl.ds(0, rows)], out_ref, sem).wait()


def _gather_rows(table, idx):
    n = idx.shape[0]
    rows = R_GATHER
    width = table.shape[1]
    idx3 = idx.reshape(n // rows, 1, rows)
    return pl.pallas_call(
        _gather_kernel,
        grid=(n // rows,),
        in_specs=[pl.BlockSpec((1, 1, rows), lambda i: (i, 0, 0), memory_space=pltpu.SMEM),
                  pl.BlockSpec(memory_space=pl.ANY)],
        out_specs=pl.BlockSpec((rows, width), lambda i: (i, 0)),
        out_shape=jax.ShapeDtypeStruct((n, width), table.dtype),
        scratch_shapes=[pltpu.SemaphoreType.DMA],
        compiler_params=pltpu.CompilerParams(dimension_semantics=("arbitrary",), vmem_limit_bytes=VMEM_LIMIT),
        name="gather_rows",
    )(idx3, table)


def _moe_kernel(te_ref, nu_ref, xs_ref, wgu_ref, bgu_ref, wd_ref, bd_ref, ys_ref):
    j = pl.program_id(0)

    @pl.when(j < nu_ref[0])
    def _compute():
        xb = xs_ref[...].astype(BF16)
        gu = _dot(xb, wgu_ref[0]) + bgu_ref[0]
        gate = jnp.minimum(gu[:, :D_FF], SWIGLU_LIMIT)
        up = jnp.clip(gu[:, D_FF:], -SWIGLU_LIMIT, SWIGLU_LIMIT)
        act = (up + 1.0) * (gate * jax.nn.sigmoid(SWIGLU_ALPHA * gate))
        ys_ref[...] = _dot(act.astype(BF16), wd_ref[0]) + bd_ref[0]

    @pl.when(j >= nu_ref[0])
    def _unused():
        ys_ref[...] = jnp.zeros_like(ys_ref)


def _moe(tile_expert, n_used, xs, wgu, bgu, wd, bd):
    P = xs.shape[0]
    tm = TM_E
    grid_spec = pltpu.PrefetchScalarGridSpec(
        num_scalar_prefetch=2,
        grid=(P // tm,),
        in_specs=[pl.BlockSpec((tm, D_MODEL), lambda j, te, nu: (j, 0)),
                  pl.BlockSpec((1, D_MODEL, 2 * D_FF), lambda j, te, nu: (te[j], 0, 0)),
                  pl.BlockSpec((1, 1, 2 * D_FF), lambda j, te, nu: (te[j], 0, 0)),
                  pl.BlockSpec((1, D_FF, D_MODEL), lambda j, te, nu: (te[j], 0, 0)),
                  pl.BlockSpec((1, 1, D_MODEL), lambda j, te, nu: (te[j], 0, 0))],
        out_specs=pl.BlockSpec((tm, D_MODEL), lambda j, te, nu: (j, 0)),
    )
    return pl.pallas_call(
        _moe_kernel,
        grid_spec=grid_spec,
        out_shape=jax.ShapeDtypeStruct((P, D_MODEL), F32),
        compiler_params=pltpu.CompilerParams(dimension_semantics=("arbitrary",), vmem_limit_bytes=VMEM_LIMIT),
        name="moe",
    )(tile_expert, n_used, xs, wgu, bgu, wd, bd)


def _final_kernel(x1_ref, yg_ref, route_ref, g_ref, o_ref):
    acc = x1_ref[...]
    for k in range(TOP_K):
        acc = acc + route_ref[:, ROUTE_W + k:ROUTE_W + k + 1] * yg_ref[k]
    o_ref[...] = _rms(acc, g_ref[...])


def _final(x1, yg, route, g):
    T = x1.shape[0]
    tm = TM_FINAL
    return pl.pallas_call(
        _final_kernel,
        grid=(T // tm,),
        in_specs=[pl.BlockSpec((tm, D_MODEL), lambda i: (i, 0)),
                  pl.BlockSpec((TOP_K, tm, D_MODEL), lambda i: (0, i, 0)),
                  pl.BlockSpec((tm, LANES), lambda i: (i, 0)),
                  pl.BlockSpec((1, D_MODEL), lambda i: (0, 0))],
        out_specs=pl.BlockSpec((tm, D_MODEL), lambda i: (i, 0)),
        out_shape=jax.ShapeDtypeStruct((T, D_MODEL), F32),
        compiler_params=pltpu.CompilerParams(dimension_semantics=("arbitrary",), vmem_limit_bytes=VMEM_LIMIT),
        name="final",
    )(x1, yg, route, g)


def _rope_tables(S):
    half = HEAD_DIM // 2
    inv = ROPE_THETA ** (-jnp.arange(half, dtype=F32) / half)
    ang = jnp.arange(S, dtype=F32)[:, None] * inv[None, :]
    cos = jnp.tile(jnp.cos(ang), (1, LANES // half))
    sin = jnp.tile(jnp.concatenate([-jnp.sin(ang), jnp.sin(ang)], axis=1), (1, LANES // HEAD_DIM))
    return cos, sin


def _selection_constants(S):
    nc = (S - CMP_BLOCK) // CMP_STRIDE + 1
    n_sel = S // SEL_BLOCK
    j = jnp.arange(n_sel)[:, None]
    i = jnp.arange(N_CMP_PAD)[None, :]
    overlap_t = ((i * CMP_STRIDE <= j * SEL_BLOCK + SEL_BLOCK - 1)
                 & (i * CMP_STRIDE + CMP_BLOCK - 1 >= j * SEL_BLOCK) & (i < nc)).astype(BF16)
    row = jnp.arange(LANES)[:, None]
    key = jnp.arange(S)[None, :]
    drop_bias = jnp.where(key // SEL_BLOCK == row, NEG_INF, 0.0).astype(BF16)[:n_sel]
    drop_bias = drop_bias.T.reshape(S // CKS, CKS, n_sel)
    return overlap_t, drop_bias


def kernel(x, mem, norm_mix, norm_mem, w_in, cmp_pos, cmp_w1, cmp_w2, w_pool, pool_scale, w_mem_kv, w_up_nsa,
           w_up_pool, w_up_mem, w_out, norm_ffn, w_router, b_router, w_gate_up, b_gate_up, w_down, b_down,
           norm_final):
    B, S, D = x.shape
    T = B * S
    assert D == D_MODEL and S % CKS == 0 and S // SEL_BLOCK == 32 and T % TM_DISPATCH == 0
    l = 0
    x2 = x.reshape(T, D)

    w = w_in[l]
    o_gate = NSA_WIDTH + 6 * KV_WIDTH
    n_gate = 3 * NSA_HEADS
    o_pool = o_gate + n_gate
    o_qm = o_pool + POOL_WIDTH
    o_mg = o_qm + MEM_WIDTH
    w_a = jnp.concatenate([w[:, :o_gate], w[:, o_pool:o_mg], w[:, o_gate:o_pool],
                           jnp.zeros((D, GATE_PAD - n_gate), F32)], axis=1).astype(BF16)
    w_mg = w[:, o_mg:].astype(BF16)
    cos_t, sin_t = _rope_tables(S)
    overlap_t, drop_bias = _selection_constants(S)
    w1 = cmp_w1[l].reshape(2, CMP_FLAT, HEAD_DIM).astype(BF16)
    w2 = cmp_w2[l].astype(BF16)
    w2t = jnp.swapaxes(w2, 1, 2)
    pos = jnp.broadcast_to(cmp_pos[l].reshape(2, 1, CMP_FLAT), (2, 8, CMP_FLAT)).astype(BF16)
    wpool_bd = jnp.zeros((POOL_WIDTH, POOL_WIDTH), F32)
    for gi in range(len(POOL_WINDOWS)):
        wpool_bd = wpool_bd.at[gi * POOL_GROUP:(gi + 1) * POOL_GROUP, gi * POOL_GROUP:(gi + 1) * POOL_GROUP].set(w_pool[l, gi])
    wr = jnp.tile(w_router[l], (1, LANES // N_EXPERTS))
    wr_hi = wr.astype(BF16)
    wr_lo = (wr - wr_hi.astype(F32)).astype(BF16)
    br = jnp.tile(b_router[l], LANES // N_EXPERTS).reshape(1, LANES)
    tri = (jnp.arange(TM_MERGE)[None, :] < jnp.arange(TM_MERGE)[:, None]).astype(BF16)
    e_row = jnp.arange(LANES)[:, None]
    upper = ((e_row < jnp.arange(LANES)[None, :]) & (e_row < N_EXPERTS)).astype(BF16)

    km, vm = _memkv(mem, norm_mem[l].reshape(1, D), w_mem_kv[l].astype(BF16))
    qc, qr, kvcmp, ksel, vsel, kwin, vwin, pool_in, qm, gates = _inproj(
        x2, norm_mix[l].reshape(1, D), w_a, cos_t, sin_t, S)
    kvc = kvcmp.reshape(B, S // CMP_STRIDE, CMP_STRIDE, 2 * NSA_GROUPS, HEAD_DIM).transpose(0, 3, 1, 2, 4)
    kvc = kvc.reshape(B, 2 * NSA_GROUPS, S // CMP_STRIDE, CMP_HALF)
    to_slabs = lambda v: jnp.swapaxes(v.reshape(NSA_GROUPS, B, S // KV_LANE_CHUNK, KV_LANE_CHUNK, HEAD_DIM), 3, 4)
    o_nsa = _nsa(qc, qr, kvc, ksel, to_slabs(vsel), kwin, to_slabs(vwin), gates, w1, w2, w2t, pos, overlap_t,
                 drop_bias, B, S)
    consts = [norm_mix[l].reshape(1, D), w_mg, wpool_bd.astype(BF16), pool_scale[l].reshape(1, POOL_WIDTH),
              w_up_nsa[l].astype(BF16), w_up_pool[l].astype(BF16), w_up_mem[l].astype(BF16), w_out[l].astype(BF16),
              norm_ffn[l].reshape(1, D), wr_hi, wr_lo, br, tri, upper]
    x1, hf, route, counts = _merge(x2, o_nsa, pool_in, qm, km, vm, consts, B, S)

    counts = counts[0, :N_EXPERTS].astype(jnp.int32)
    padded = ((counts + TM_E - 1) // TM_E) * TM_E
    ends = jnp.cumsum(padded)
    starts = ends - padded
    n_tiles = (T * TOP_K) // TM_E + N_EXPERTS
    P = n_tiles * TM_E
    e_k = route[:, ROUTE_E:ROUTE_E + TOP_K].astype(jnp.int32)
    r_k = route[:, ROUTE_R:ROUTE_R + TOP_K].astype(jnp.int32)
    dest = starts[e_k] + r_k
    tile_start = jnp.arange(n_tiles, dtype=jnp.int32) * TM_E
    tile_expert = jnp.minimum(jnp.sum(tile_start[:, None] >= ends[None, :], axis=1), N_EXPERTS - 1).astype(jnp.int32)
    n_used = (ends[-1] // TM_E).astype(jnp.int32).reshape(1)

    xs = _dispatch(dest.reshape(-1), ends, padded, hf, P)
    ys = _moe(tile_expert, n_used, xs, w_gate_up[l].astype(BF16), b_gate_up[l].reshape(N_EXPERTS, 1, 2 * D_FF),
              w_down[l].astype(BF16), b_down[l].reshape(N_EXPERTS, 1, D_MODEL))
    yg = _gather_rows(ys, dest.T.reshape(-1)).reshape(TOP_K, T, D_MODEL)
    out = _final(x1, yg, route, norm_final.reshape(1, D))
    return out.reshape(B, S, D)
```

```python
import functools

import jax
import jax.numpy as jnp
from jax import lax
from jax.experimental import pallas as pl
from jax.experimental.pallas import tpu as pltpu

F32 = jnp.float32
BF16 = jnp.bfloat16

D_MODEL = 1024
HEAD_DIM = 64
NSA_HEADS = 8
NSA_GROUPS = 2
HEADS_PER_GROUP = NSA_HEADS // NSA_GROUPS
NSA_WIDTH = NSA_HEADS * HEAD_DIM
KV_WIDTH = NSA_GROUPS * HEAD_DIM
CMP_BLOCK = 32
CMP_STRIDE = 16
SEL_BLOCK = 64
SEL_TOPN = 8
FORCE_BONUS = 1000.0
WINDOW = 512
POOL_WINDOWS = (2, 4, 8, 16)
POOL_GROUP = 64
POOL_WIDTH = POOL_GROUP * len(POOL_WINDOWS)
POOL_HALO = 16
MEM_HEADS = 4
MEM_WIDTH = MEM_HEADS * HEAD_DIM
N_EXPERTS = 32
TOP_K = 4
D_FF = 1024
SWIGLU_LIMIT = 7.0
SWIGLU_ALPHA = 1.702
ROPE_THETA = 10000.0
EPS = 1e-5
NEG_INF = -1e30
TINY = 1e-30
QK_SCALE = HEAD_DIM ** -0.5
LOG2_E = 1.4426950408889634
NSA_Q_SCALE = QK_SCALE * LOG2_E

LANES = 128
GATE_PAD = LANES

TM_IN = 512
TQ = 128
CKS = 512
WIN_CHUNKS = (256, 256, 128)
TM_MERGE = 512
TM_E = 512
TM_DISPATCH = 512
TM_FINAL = 256
VMEM_LIMIT = 56 * 1024 * 1024


def _rms(x, g):
    return x * lax.rsqrt(jnp.mean(x * x, axis=-1, keepdims=True) + EPS) * g


def _dot(a, b):
    return jnp.dot(a, b, preferred_element_type=F32)


def _dot_nt(a, b):
    return lax.dot_general(a, b, (((1,), (1,)), ((), ())), preferred_element_type=F32)


def _memkv_kernel(mem_ref, g_ref, w_ref, k_ref, v_ref):
    m = _rms(mem_ref[0], g_ref[...]).astype(BF16)
    kv = _dot(m, w_ref[...])
    for h in range(MEM_HEADS):
        k_ref[0, h] = kv[:, h * HEAD_DIM:(h + 1) * HEAD_DIM].astype(BF16)
        v_ref[0, h] = kv[:, MEM_WIDTH + h * HEAD_DIM:MEM_WIDTH + (h + 1) * HEAD_DIM].astype(BF16)


def _memkv(mem, g, w):
    B, M, D = mem.shape
    return pl.pallas_call(
        _memkv_kernel,
        grid=(B,),
        in_specs=[pl.BlockSpec((1, M, D), lambda b: (b, 0, 0)),
                  pl.BlockSpec((1, D), lambda b: (0, 0)),
                  pl.BlockSpec((D, 2 * MEM_WIDTH), lambda b: (0, 0))],
        out_specs=[pl.BlockSpec((1, MEM_HEADS, M, HEAD_DIM), lambda b: (b, 0, 0, 0)),
                   pl.BlockSpec((1, MEM_HEADS, M, HEAD_DIM), lambda b: (b, 0, 0, 0))],
        out_shape=[jax.ShapeDtypeStruct((B, MEM_HEADS, M, HEAD_DIM), BF16),
                   jax.ShapeDtypeStruct((B, MEM_HEADS, M, HEAD_DIM), BF16)],
        compiler_params=pltpu.CompilerParams(dimension_semantics=("arbitrary",), vmem_limit_bytes=VMEM_LIMIT),
        name="memkv",
    )(mem, g, w)


IN_COLS = NSA_WIDTH + 6 * KV_WIDTH + POOL_WIDTH + MEM_WIDTH + GATE_PAD


def _inproj_kernel(x_ref, g_ref, w_ref, cos_ref, sin_ref,
                   qc_ref, qr_ref, kvc_ref, ksel_ref, vsel_ref, kwin_ref, vwin_ref, pool_ref, qm_ref, gate_ref):
    h = _rms(x_ref[...], g_ref[...]).astype(BF16)
    p = _dot(h, w_ref[...])
    cos = cos_ref[...]
    sin = sin_ref[...]
    lane = lax.broadcasted_iota(jnp.int32, cos.shape, 1)
    first_half = (lane % HEAD_DIM) < (HEAD_DIM // 2)

    def rope(c):
        partner = jnp.where(first_half, pltpu.roll(c, LANES - HEAD_DIM // 2, 1), pltpu.roll(c, HEAD_DIM // 2, 1))
        return c * cos + partner * sin

    def halves(c):
        return c[:, :HEAD_DIM], c[:, HEAD_DIM:]

    for j in range(NSA_WIDTH // LANES):
        c = p[:, j * LANES:(j + 1) * LANES]
        r = rope(c)
        for hh, (cc, rr) in enumerate(zip(halves(c), halves(r))):
            qc_ref[2 * j + hh] = (cc * NSA_Q_SCALE).astype(BF16)
            qr_ref[2 * j + hh] = (rr * NSA_Q_SCALE).astype(BF16)
    o = NSA_WIDTH
    kvc_ref[...] = p[:, o:o + 2 * KV_WIDTH]
    o += 2 * KV_WIDTH
    for ref, rot in ((ksel_ref, True), (vsel_ref, False), (kwin_ref, True), (vwin_ref, False)):
        c = p[:, o:o + KV_WIDTH]
        if rot:
            c = rope(c)
        for g, cc in enumerate(halves(c)):
            ref[g] = cc.astype(BF16)
        o += KV_WIDTH
    pool_ref[...] = p[:, o:o + POOL_WIDTH]
    o += POOL_WIDTH
    for hh in range(MEM_HEADS):
        qm_ref[hh] = (p[:, o + hh * HEAD_DIM:o + (hh + 1) * HEAD_DIM] * QK_SCALE).astype(BF16)
    o += MEM_WIDTH
    gate_ref[...] = jax.nn.sigmoid(p[:, o:o + GATE_PAD])


def _inproj(x2, g, w_a, cos_t, sin_t, S):
    T = x2.shape[0]
    tm = TM_IN
    n_s = S // tm
    head_spec = lambda n: pl.BlockSpec((n, tm, HEAD_DIM), lambda i: (0, i, 0))
    row_spec = lambda w: pl.BlockSpec((tm, w), lambda i: (i, 0))
    return pl.pallas_call(
        _inproj_kernel,
        grid=(T // tm,),
        in_specs=[row_spec(D_MODEL),
                  pl.BlockSpec((1, D_MODEL), lambda i: (0, 0)),
                  pl.BlockSpec((D_MODEL, IN_COLS), lambda i: (0, 0)),
                  pl.BlockSpec((tm, LANES), lambda i: (i % n_s, 0)),
                  pl.BlockSpec((tm, LANES), lambda i: (i % n_s, 0))],
        out_specs=[head_spec(NSA_HEADS), head_spec(NSA_HEADS), row_spec(2 * KV_WIDTH),
                   head_spec(NSA_GROUPS), head_spec(NSA_GROUPS), head_spec(NSA_GROUPS), head_spec(NSA_GROUPS),
                   row_spec(POOL_WIDTH), head_spec(MEM_HEADS), row_spec(GATE_PAD)],
        out_shape=[jax.ShapeDtypeStruct((NSA_HEADS, T, HEAD_DIM), BF16),
                   jax.ShapeDtypeStruct((NSA_HEADS, T, HEAD_DIM), BF16),
                   jax.ShapeDtypeStruct((T, 2 * KV_WIDTH), F32),
                   jax.ShapeDtypeStruct((NSA_GROUPS, T, HEAD_DIM), BF16),
                   jax.ShapeDtypeStruct((NSA_GROUPS, T, HEAD_DIM), BF16),
                   jax.ShapeDtypeStruct((NSA_GROUPS, T, HEAD_DIM), BF16),
                   jax.ShapeDtypeStruct((NSA_GROUPS, T, HEAD_DIM), BF16),
                   jax.ShapeDtypeStruct((T, POOL_WIDTH), F32),
                   jax.ShapeDtypeStruct((MEM_HEADS, T, HEAD_DIM), BF16),
                   jax.ShapeDtypeStruct((T, GATE_PAD), F32)],
        compiler_params=pltpu.CompilerParams(dimension_semantics=("arbitrary",), vmem_limit_bytes=VMEM_LIMIT),
        name="inproj",
    )(x2, g, w_a, cos_t, sin_t)


N_CMP_PAD = 128
CMP_FLAT = CMP_BLOCK * HEAD_DIM
CMP_HALF = CMP_STRIDE * HEAD_DIM


KV_LANE_CHUNK = LANES


def _nsa_kernel(qc_ref, qr_ref, kvc_ref, ksel_ref, vsel_ref, kwin_ref, vwin_ref, gate_ref,
                w1_ref, w2_ref, w2t_ref, pos_ref, ovl_ref, drop_ref, o_ref, kc_s, vct_s, *, S):
    i = pl.program_id(1)
    tq = TQ
    hpg = HEADS_PER_GROUP
    n_sel = S // SEL_BLOCK
    hq = hpg * tq

    @pl.when(i == 0)
    def _compress():
        for kv in range(2):
            w1 = w1_ref[kv]
            posterm = _dot(pos_ref[kv], w1)[0:1]
            for g in range(NSA_GROUPS):
                a = kvc_ref[0, kv * NSA_GROUPS + g].astype(BF16)
                p1 = _dot(a, w1[:CMP_HALF])
                p2 = _dot(a, w1[CMP_HALF:])
                hid = p1 + pltpu.roll(p2, N_CMP_PAD - 1, 0) + posterm
                hid = (hid * jax.nn.sigmoid(hid)).astype(BF16)
                if kv == 0:
                    kc_s[g] = _dot(hid, w2_ref[kv]).astype(BF16)
                else:
                    vct_s[g] = _dot_nt(w2t_ref[kv], hid).astype(BF16)

    q0 = i * tq
    t_lane = q0 + lax.broadcasted_iota(jnp.int32, (1, tq), 1)

    def key_pos(start, n):
        return start + lax.broadcasted_iota(jnp.int32, (n, 1), 0)

    cmp_valid = (key_pos(0, N_CMP_PAD) * CMP_STRIDE + CMP_BLOCK - 1) <= t_lane
    jrow = lax.broadcasted_iota(jnp.int32, (n_sel, tq), 0)
    cur = t_lane // SEL_BLOCK
    sel_valid = jrow * SEL_BLOCK <= t_lane
    forced = (jrow == 0) | (jrow == cur) | (jrow == cur - 1)

    cd = q0 // CKS
    causal_bias = jnp.where(key_pos(cd * CKS, CKS) <= t_lane, 0.0, NEG_INF)
    win_chunks = []
    hi = q0 + tq
    for n in WIN_CHUNKS:
        lo = hi - n
        start = pl.multiple_of(jnp.maximum(lo, 0), LANES)
        kp = key_pos(start, n)
        diff = t_lane - kp
        win_chunks.append((start, n, jnp.where((diff >= 0) & (diff < WINDOW) & (kp < hi), 0.0, NEG_INF)))
        hi = lo

    def values_t(ref, g, start, n):
        c0 = start // KV_LANE_CHUNK
        return jnp.concatenate([ref[g, 0, c0 + j] for j in range(n // KV_LANE_CHUNK)], axis=1)

    def attend(q_all, k, v_t, bias, state):
        m_i, l_i, acc = state
        n = k.shape[0]
        s_all = _dot_nt(k, q_all)
        ms, ls, alphas, ps = [], [], [], []
        for hh in range(hpg):
            sl = slice(hh * tq, (hh + 1) * tq)
            s = s_all[:, sl] + bias
            m_new = jnp.maximum(m_i[:, sl], jnp.max(s, axis=0, keepdims=True))
            alpha = jnp.exp2(m_i[:, sl] - m_new)
            p = jnp.exp2(s - m_new)
            ls.append(alpha * l_i[:, sl] + jnp.sum(p, axis=0, keepdims=True))
            ps.append(p.astype(BF16))
            ms.append(m_new)
            alphas.append(alpha)
        pv = jnp.concatenate([_dot(v_t, jnp.concatenate(ps[a:a + 2], axis=1)) for a in range(0, hpg, 2)], axis=1)
        return (jnp.concatenate(ms, axis=1), jnp.concatenate(ls, axis=1),
                jnp.concatenate(alphas, axis=1) * acc + pv)

    def fresh_state():
        return jnp.full((1, hq), NEG_INF, F32), jnp.zeros((1, hq), F32), jnp.zeros((HEAD_DIM, hq), F32)

    def select_blocks(g):
        q_cmp = qc_ref[g * hpg:(g + 1) * hpg].reshape(hq, HEAD_DIM)
        s_all = _dot_nt(kc_s[g], q_cmp)
        p_grp = jnp.zeros((N_CMP_PAD, tq), F32)
        ps = []
        for hh in range(hpg):
            sl = slice(hh * tq, (hh + 1) * tq)
            s = jnp.where(cmp_valid, s_all[:, sl], NEG_INF)
            m = jnp.max(s, axis=0, keepdims=True)
            e = jnp.where(cmp_valid, jnp.exp2(s - m), 0.0)
            p = e * (1.0 / jnp.maximum(jnp.sum(e, axis=0, keepdims=True), TINY))
            p_grp = p_grp + p
            ps.append(p.astype(BF16))
        o_cmp = _dot(vct_s[g], jnp.concatenate(ps, axis=1))

        ovl = ovl_ref[...]
        p_hi = p_grp.astype(BF16)
        r1 = p_grp - p_hi.astype(F32)
        p_mid = r1.astype(BF16)
        p_lo = (r1 - p_mid.astype(F32)).astype(BF16)
        score = _dot(ovl, p_hi) + _dot(ovl, p_mid) + _dot(ovl, p_lo)
        score = jnp.where(sel_valid, score + jnp.where(forced, FORCE_BONUS, 0.0), -1.0)
        rank = jnp.zeros((n_sel, tq), F32)
        for jp in range(n_sel):
            other = score[jp:jp + 1, :]
            beats = (other > score) | ((other == score) & (jrow > jp))
            rank = rank + beats.astype(F32)
        return o_cmp, (rank >= SEL_TOPN).astype(BF16)

    groups = range(NSA_GROUPS)
    cmp_out = [select_blocks(g) for g in groups]
    q_rot = [qr_ref[g * hpg:(g + 1) * hpg].reshape(hq, HEAD_DIM) for g in groups]

    def attend_selected(g, c, extra_bias, state):
        k0 = pl.multiple_of(c * CKS, CKS)
        bias = _dot(drop_ref[c], cmp_out[g][1])
        if extra_bias is not None:
            bias = bias + extra_bias
        return attend(q_rot[g], ksel_ref[g, pl.ds(k0, CKS), :], values_t(vsel_ref, g, k0, CKS), bias, state)

    sel_state = tuple(attend_selected(g, cd, causal_bias, fresh_state()) for g in groups)

    win_state = [fresh_state() for g in groups]
    for start, n, bias in win_chunks:
        for g in groups:
            win_state[g] = attend(q_rot[g], kwin_ref[g, pl.ds(start, n), :], values_t(vwin_ref, g, start, n),
                                  bias, win_state[g])

    def sel_body(it, states):
        return tuple(attend_selected(g, cd - 1 - it, None, states[g]) for g in groups)

    sel_state = lax.fori_loop(0, cd, sel_body, sel_state)

    gates_t = gate_ref[...].T
    out_rows = []
    for g in groups:
        o_cmp = cmp_out[g][0]
        o_sel = sel_state[g][2] * (1.0 / sel_state[g][1])
        o_win = win_state[g][2] * (1.0 / win_state[g][1])
        for hh in range(hpg):
            h = g * hpg + hh
            sl = slice(hh * tq, (hh + 1) * tq)
            out_rows.append(gates_t[3 * h:3 * h + 1] * o_cmp[:, sl] + gates_t[3 * h + 1:3 * h + 2] * o_sel[:, sl]
                            + gates_t[3 * h + 2:3 * h + 3] * o_win[:, sl])
    o_ref[...] = jnp.concatenate(out_rows, axis=0).T.astype(BF16)


def _nsa(qc, qr, kvc, ksel, vsel_t, kwin, vwin_t, gates, w1, w2, w2t, pos, ovl, drop_bias, B, S):
    T = B * S
    tq = TQ
    nq = S // tq
    q_spec = pl.BlockSpec((NSA_HEADS, tq, HEAD_DIM), lambda b, i: (0, b * nq + i, 0))
    k_spec = pl.BlockSpec((NSA_GROUPS, S, HEAD_DIM), lambda b, i: (0, b, 0))
    v_spec = pl.BlockSpec((NSA_GROUPS, 1, S // KV_LANE_CHUNK, HEAD_DIM, KV_LANE_CHUNK), lambda b, i: (0, b, 0, 0, 0))
    full = lambda a: pl.BlockSpec(a.shape, lambda b, i: (0,) * a.ndim)
    return pl.pallas_call(
        functools.partial(_nsa_kernel, S=S),
        grid=(B, nq),
        in_specs=[q_spec, q_spec,
                  pl.BlockSpec((1, 2 * NSA_GROUPS, N_CMP_PAD, CMP_HALF), lambda b, i: (b, 0, 0, 0)),
                  k_spec, v_spec, k_spec, v_spec,
                  pl.BlockSpec((tq, GATE_PAD), lambda b, i: (b * nq + i, 0)),
                  full(w1), full(w2), full(w2t), full(pos), full(ovl), full(drop_bias)],
        out_specs=pl.BlockSpec((tq, NSA_WIDTH), lambda b, i: (b * nq + i, 0)),
        out_shape=jax.ShapeDtypeStruct((T, NSA_WIDTH), BF16),
        scratch_shapes=[pltpu.VMEM((NSA_GROUPS, N_CMP_PAD, HEAD_DIM), BF16),
                        pltpu.VMEM((NSA_GROUPS, HEAD_DIM, N_CMP_PAD), BF16)],
        compiler_params=pltpu.CompilerParams(dimension_semantics=("arbitrary", "arbitrary"),
                                             vmem_limit_bytes=VMEM_LIMIT),
        name="nsa",
    )(qc, qr, kvc, ksel, vsel_t, kwin, vwin_t, gates, w1, w2, w2t, pos, ovl, drop_bias)


ROUTE_E, ROUTE_R, ROUTE_W = 0, TOP_K, 2 * TOP_K
ROUTE_ROWS = 16


def _merge_kernel(x_ref, onsa_ref, pool_ref, prev_ref, qm_ref, km_ref, vm_ref,
                  gmix_ref, wmg_ref, wpool_ref, pscale_ref, wun_ref, wup_ref, wum_ref, wout_ref,
                  gffn_ref, wrh_ref, wrl_ref, br_ref, tri_ref, lower_ref,
                  x1_ref, hf_ref, route_ref, cnt_ref, omem_s, carry_s, *, S):
    i = pl.program_id(0)
    tm = TM_MERGE
    n_s = S // tm

    @pl.when(i == 0)
    def _init():
        carry_s[...] = jnp.zeros_like(carry_s)

    x = x_ref[...]
    h = _rms(x, gmix_ref[...]).astype(BF16)
    mg = jax.nn.sigmoid(_dot(h, wmg_ref[...]))

    u = pool_ref[...]
    seq_tile = i % n_s
    prev = jnp.where(seq_tile == 0, 0.0, prev_ref[...])
    ext = jnp.concatenate([prev, u], axis=0)
    b2 = ext[1:] + ext[:-1]
    b4 = b2[2:] + b2[:-2]
    b8 = b4[4:] + b4[:-4]
    b16 = b8[8:] + b8[:-8]
    sums = (b2[POOL_HALO - 1:POOL_HALO - 1 + tm], b4[POOL_HALO - 3:POOL_HALO - 3 + tm],
            b8[POOL_HALO - 7:POOL_HALO - 7 + tm], b16[POOL_HALO - 15:POOL_HALO - 15 + tm])
    t_seq = seq_tile * tm + lax.broadcasted_iota(jnp.int32, (tm, 1), 0)
    lane_p = lax.broadcasted_iota(jnp.int32, (tm, POOL_WIDTH), 1)
    z = jnp.zeros((tm, POOL_WIDTH), F32)
    for gi, w in enumerate(POOL_WINDOWS):
        cnt = jnp.minimum(t_seq + 1, w).astype(F32)
        z = jnp.where(lane_p // POOL_GROUP == gi, sums[gi] / cnt, z)
    z = z - u
    o_pool = (_dot(z.astype(BF16), wpool_ref[...]) * pscale_ref[...]).astype(BF16)

    for hh in range(MEM_HEADS):
        s = _dot_nt(qm_ref[hh], km_ref[0, hh])
        m = jnp.max(s, axis=-1, keepdims=True)
        e = jnp.exp(s - m)
        p = e / jnp.sum(e, axis=-1, keepdims=True)
        omem_s[:, hh * HEAD_DIM:(hh + 1) * HEAD_DIM] = _dot(p.astype(BF16), vm_ref[0, hh]).astype(BF16)

    merged = (mg[:, :D_MODEL] * _dot(onsa_ref[...], wun_ref[...])
              + mg[:, D_MODEL:2 * D_MODEL] * _dot(o_pool, wup_ref[...])
              + mg[:, 2 * D_MODEL:] * _dot(omem_s[...], wum_ref[...]))
    x1 = x + _dot(merged.astype(BF16), wout_ref[...])
    x1_ref[...] = x1
    hf = _rms(x1, gffn_ref[...])
    hf_ref[...] = hf

    hf_hi = hf.astype(BF16)
    hf_lo = (hf - hf_hi.astype(F32)).astype(BF16)
    logits = (_dot_nt(wrh_ref[...], hf_hi) + _dot_nt(wrl_ref[...], hf_hi) + _dot_nt(wrh_ref[...], hf_lo)
              + br_ref[...])
    erow = lax.broadcasted_iota(jnp.int32, (N_EXPERTS, tm), 0)
    rank = jnp.zeros((N_EXPERTS, tm), F32)
    for jp in range(N_EXPERTS):
        other = logits[jp:jp + 1, :]
        beats = (other > logits) | ((other == logits) & (erow > jp))
        rank = rank + beats.astype(F32)
    chosen = rank < TOP_K
    m = jnp.max(logits, axis=0, keepdims=True)
    e = jnp.where(chosen, jnp.exp(logits - m), 0.0)
    comb = e * (1.0 / jnp.sum(e, axis=0, keepdims=True))

    chosen_b = chosen.astype(BF16)
    carry = carry_s[:, 0:1]
    in_expert = _dot(chosen_b, tri_ref[...]) + carry
    carry_new = carry + jnp.sum(chosen.astype(F32), axis=1, keepdims=True)
    carry_s[...] = jnp.broadcast_to(carry_new, carry_s.shape)
    cnt_ref[...] = jnp.broadcast_to(carry_new, cnt_ref.shape)

    before = _dot(lower_ref[...], chosen_b)
    erow_f = erow.astype(F32)
    fields = {ROUTE_E: erow_f, ROUTE_R: in_expert, ROUTE_W: comb}
    rows = [None] * ROUTE_ROWS
    for k in range(TOP_K):
        pick = chosen & (before == k)
        for base, val in fields.items():
            rows[base + k] = jnp.sum(jnp.where(pick, val, 0.0), axis=0, keepdims=True)
    zero_row = jnp.zeros((1, tm), F32)
    route_ref[...] = jnp.concatenate([zero_row if r is None else r for r in rows], axis=0)


def _merge(x2, onsa, pool_in, qm, km, vm, consts, B, S):
    T = B * S
    tm = TM_MERGE
    n_s = S // tm
    M = km.shape[2]
    halo_per_tile = tm // POOL_HALO
    row = lambda w: pl.BlockSpec((tm, w), lambda i: (i, 0))
    full = lambda a: pl.BlockSpec(a.shape, lambda i: (0,) * a.ndim)
    mem_spec = pl.BlockSpec((1, MEM_HEADS, M, HEAD_DIM), lambda i: (i // n_s, 0, 0, 0))
    return pl.pallas_call(
        functools.partial(_merge_kernel, S=S),
        grid=(T // tm,),
        in_specs=[row(D_MODEL), row(NSA_WIDTH), row(POOL_WIDTH),
                  pl.BlockSpec((POOL_HALO, POOL_WIDTH), lambda i: (jnp.maximum(i * halo_per_tile - 1, 0), 0)),
                  pl.BlockSpec((MEM_HEADS, tm, HEAD_DIM), lambda i: (0, i, 0)),
                  mem_spec, mem_spec] + [full(c) for c in consts],
        out_specs=[row(D_MODEL), row(D_MODEL), pl.BlockSpec((ROUTE_ROWS, tm), lambda i: (0, i)),
                   pl.BlockSpec((N_EXPERTS, LANES), lambda i: (0, 0))],
        out_shape=[jax.ShapeDtypeStruct((T, D_MODEL), F32),
                   jax.ShapeDtypeStruct((T, D_MODEL), F32),
                   jax.ShapeDtypeStruct((ROUTE_ROWS, T), F32),
                   jax.ShapeDtypeStruct((N_EXPERTS, LANES), F32)],
        scratch_shapes=[pltpu.VMEM((tm, MEM_WIDTH), BF16), pltpu.VMEM((N_EXPERTS, LANES), F32)],
        compiler_params=pltpu.CompilerParams(dimension_semantics=("arbitrary",), vmem_limit_bytes=VMEM_LIMIT),
        name="merge",
    )(x2, onsa, pool_in, pool_in, qm, km, vm, *consts)


def _dispatch_kernel(ends_ref, padded_ref, dest_ref, h_ref, xs_ref, zero_s, sem, zsem):
    i = pl.program_id(0)
    tm = h_ref.shape[0]

    @pl.when(i == 0)
    def _clear_pad_tiles():
        zero_s[...] = jnp.zeros_like(zero_s)

        def tile_copy(e):
            start = pl.multiple_of(ends_ref[e] - TM_E, TM_E)
            return pltpu.make_async_copy(zero_s, xs_ref.at[pl.ds(start, TM_E)], zsem)

        for e in range(N_EXPERTS):
            @pl.when(padded_ref[e] > 0)
            def _():
                tile_copy(e).start()
        for e in range(N_EXPERTS):
            @pl.when(padded_ref[e] > 0)
            def _():
                tile_copy(e).wait()

        def tail_copy(j):
            return pltpu.make_async_copy(zero_s, xs_ref.at[pl.ds(pl.multiple_of(j * TM_E, TM_E), TM_E)], zsem)

        def tail_start(j, c):
            tail_copy(j).start()
            return c

        def tail_wait(j, c):
            tail_copy(j).wait()
            return c

        n_used = ends_ref[N_EXPERTS - 1] // TM_E
        lax.fori_loop(n_used, xs_ref.shape[0] // TM_E, tail_start, 0)
        lax.fori_loop(n_used, xs_ref.shape[0] // TM_E, tail_wait, 0)

    def issue(r, c):
        for k in range(TOP_K):
            d = dest_ref[0, 0, k * tm + r]
            pltpu.make_async_copy(h_ref.at[pl.ds(r, 1)], xs_ref.at[pl.ds(d, 1)], sem).start()
        return c

    lax.fori_loop(0, tm, issue, 0)
    for k in range(TOP_K):
        pltpu.make_async_copy(h_ref, xs_ref.at[pl.ds(0, tm)], sem).wait()


def _tile_major(dest, tm):
    T = dest.shape[1]
    return dest.reshape(TOP_K, T // tm, tm).transpose(1, 0, 2).reshape(T // tm, 1, TOP_K * tm)


def _dispatch(dest, ends, padded, hf, P):
    T = hf.shape[0]
    tm = TM_DISPATCH
    dest3 = _tile_major(dest, tm)
    grid_spec = pltpu.PrefetchScalarGridSpec(
        num_scalar_prefetch=2,
        grid=(T // tm,),
        in_specs=[pl.BlockSpec((1, 1, tm * TOP_K), lambda i, en, pd: (i, 0, 0), memory_space=pltpu.SMEM),
                  pl.BlockSpec((tm, D_MODEL), lambda i, en, pd: (i, 0))],
        out_specs=pl.BlockSpec(memory_space=pl.ANY),
        scratch_shapes=[pltpu.VMEM((TM_E, D_MODEL), F32), pltpu.SemaphoreType.DMA, pltpu.SemaphoreType.DMA],
    )
    return pl.pallas_call(
        _dispatch_kernel,
        grid_spec=grid_spec,
        out_shape=jax.ShapeDtypeStruct((P, D_MODEL), F32),
        compiler_params=pltpu.CompilerParams(dimension_semantics=("arbitrary",), vmem_limit_bytes=VMEM_LIMIT),
        name="dispatch",
    )(ends, padded, dest3, hf)


def _moe_kernel(te_ref, nu_ref, xs_ref, wgu_ref, bgu_ref, wd_ref, bd_ref, ys_ref):
    j = pl.program_id(0)

    @pl.when(j < nu_ref[0])
    def _compute():
        xb = xs_ref[...].astype(BF16)
        gu = _dot(xb, wgu_ref[0]) + bgu_ref[0]
        gate = jnp.minimum(gu[:, :D_FF], SWIGLU_LIMIT)
        up = jnp.clip(gu[:, D_FF:], -SWIGLU_LIMIT, SWIGLU_LIMIT)
        act = (up + 1.0) * (gate * jax.nn.sigmoid(SWIGLU_ALPHA * gate))
        ys_ref[...] = _dot(act.astype(BF16), wd_ref[0]) + bd_ref[0]

    @pl.when(j >= nu_ref[0])
    def _unused():
        ys_ref[...] = jnp.zeros_like(ys_ref)


def _moe(tile_expert, n_used, xs, wgu, bgu, wd, bd):
    P = xs.shape[0]
    tm = TM_E
    grid_spec = pltpu.PrefetchScalarGridSpec(
        num_scalar_prefetch=2,
        grid=(P // tm,),
        in_specs=[pl.BlockSpec((tm, D_MODEL), lambda j, te, nu: (j, 0)),
                  pl.BlockSpec((1, D_MODEL, 2 * D_FF), lambda j, te, nu: (te[j], 0, 0)),
                  pl.BlockSpec((1, 1, 2 * D_FF), lambda j, te, nu: (te[j], 0, 0)),
                  pl.BlockSpec((1, D_FF, D_MODEL), lambda j, te, nu: (te[j], 0, 0)),
                  pl.BlockSpec((1, 1, D_MODEL), lambda j, te, nu: (te[j], 0, 0))],
        out_specs=pl.BlockSpec((tm, D_MODEL), lambda j, te, nu: (j, 0)),
    )
    return pl.pallas_call(
        _moe_kernel,
        grid_spec=grid_spec,
        out_shape=jax.ShapeDtypeStruct((P, D_MODEL), F32),
        compiler_params=pltpu.CompilerParams(dimension_semantics=("arbitrary",), vmem_limit_bytes=VMEM_LIMIT),
        name="moe",
    )(tile_expert, n_used, xs, wgu, bgu, wd, bd)


def _final_kernel(idx_ref, nxt_ref, x1_ref, route_ref, g_ref, ys_ref, o_ref, buf, sems):
    i = pl.program_id(0)
    n = pl.num_programs(0)
    tm = x1_ref.shape[0]

    def row_copy(src_row, slot, k, r):
        return pltpu.make_async_copy(ys_ref.at[pl.ds(src_row, 1)], buf.at[slot, k, pl.ds(r, 1)], sems.at[slot])

    def fetch(ref, slot):
        def issue(r, c):
            for k in range(TOP_K):
                row_copy(ref[0, 0, k * tm + r], slot, k, r).start()
            return c
        lax.fori_loop(0, tm, issue, 0)

    @pl.when(i == 0)
    def _first():
        fetch(idx_ref, 0)

    @pl.when(i + 1 < n)
    def _prefetch():
        fetch(nxt_ref, (i + 1) % 2)

    slot = i % 2
    for k in range(TOP_K):
        pltpu.make_async_copy(ys_ref.at[pl.ds(0, tm)], buf.at[slot, k], sems.at[slot]).wait()

    route_t = jnp.concatenate([route_ref[...], jnp.zeros((LANES - ROUTE_ROWS, tm), F32)], axis=0).T
    acc = x1_ref[...]
    for k in range(TOP_K):
        acc = acc + route_t[:, ROUTE_W + k:ROUTE_W + k + 1] * buf[slot, k]
    o_ref[...] = _rms(acc, g_ref[...])


def _final(x1, ys, dest, route, g):
    T = x1.shape[0]
    tm = TM_FINAL
    n = T // tm
    idx3 = _tile_major(dest, tm)
    smem_spec = lambda f: pl.BlockSpec((1, 1, TOP_K * tm), f, memory_space=pltpu.SMEM)
    return pl.pallas_call(
        _final_kernel,
        grid=(n,),
        in_specs=[smem_spec(lambda i: (i, 0, 0)),
                  smem_spec(lambda i: (jnp.minimum(i + 1, n - 1), 0, 0)),
                  pl.BlockSpec((tm, D_MODEL), lambda i: (i, 0)),
                  pl.BlockSpec((ROUTE_ROWS, tm), lambda i: (0, i)),
                  pl.BlockSpec((1, D_MODEL), lambda i: (0, 0)),
                  pl.BlockSpec(memory_space=pl.ANY)],
        out_specs=pl.BlockSpec((tm, D_MODEL), lambda i: (i, 0)),
        out_shape=jax.ShapeDtypeStruct((T, D_MODEL), F32),
        scratch_shapes=[pltpu.VMEM((2, TOP_K, tm, D_MODEL), F32), pltpu.SemaphoreType.DMA((2,))],
        compiler_params=pltpu.CompilerParams(dimension_semantics=("arbitrary",), vmem_limit_bytes=VMEM_LIMIT),
        name="final",
    )(idx3, idx3, x1, route, g, ys)


def _rope_tables(S):
    half = HEAD_DIM // 2
    inv = ROPE_THETA ** (-jnp.arange(half, dtype=F32) / half)
    ang = jnp.arange(S, dtype=F32)[:, None] * inv[None, :]
    cos = jnp.tile(jnp.cos(ang), (1, LANES // half))
    sin = jnp.tile(jnp.concatenate([-jnp.sin(ang), jnp.sin(ang)], axis=1), (1, LANES // HEAD_DIM))
    return cos, sin


def _selection_constants(S):
    nc = (S - CMP_BLOCK) // CMP_STRIDE + 1
    n_sel = S // SEL_BLOCK
    j = jnp.arange(n_sel)[:, None]
    i = jnp.arange(N_CMP_PAD)[None, :]
    overlap_t = ((i * CMP_STRIDE <= j * SEL_BLOCK + SEL_BLOCK - 1)
                 & (i * CMP_STRIDE + CMP_BLOCK - 1 >= j * SEL_BLOCK) & (i < nc)).astype(BF16)
    row = jnp.arange(LANES)[:, None]
    key = jnp.arange(S)[None, :]
    drop_bias = jnp.where(key // SEL_BLOCK == row, NEG_INF, 0.0).astype(BF16)[:n_sel]
    drop_bias = drop_bias.T.reshape(S // CKS, CKS, n_sel)
    return overlap_t, drop_bias


def kernel(x, mem, norm_mix, norm_mem, w_in, cmp_pos, cmp_w1, cmp_w2, w_pool, pool_scale, w_mem_kv, w_up_nsa,
           w_up_pool, w_up_mem, w_out, norm_ffn, w_router, b_router, w_gate_up, b_gate_up, w_down, b_down,
           norm_final):
    B, S, D = x.shape
    T = B * S
    assert D == D_MODEL and S % CKS == 0 and S // SEL_BLOCK == 32 and T % TM_DISPATCH == 0
    l = 0
    x2 = x.reshape(T, D)

    w = w_in[l]
    o_gate = NSA_WIDTH + 6 * KV_WIDTH
    n_gate = 3 * NSA_HEADS
    o_pool = o_gate + n_gate
    o_qm = o_pool + POOL_WIDTH
    o_mg = o_qm + MEM_WIDTH
    w_a = jnp.concatenate([w[:, :o_gate], w[:, o_pool:o_mg], w[:, o_gate:o_pool],
                           jnp.zeros((D, GATE_PAD - n_gate), F32)], axis=1).astype(BF16)
    w_mg = w[:, o_mg:].astype(BF16)
    cos_t, sin_t = _rope_tables(S)
    overlap_t, drop_bias = _selection_constants(S)
    w1 = cmp_w1[l].reshape(2, CMP_FLAT, HEAD_DIM).astype(BF16)
    w2 = cmp_w2[l].astype(BF16)
    w2t = jnp.swapaxes(w2, 1, 2)
    pos = jnp.broadcast_to(cmp_pos[l].reshape(2, 1, CMP_FLAT), (2, 8, CMP_FLAT)).astype(BF16)
    wpool_bd = jnp.zeros((POOL_WIDTH, POOL_WIDTH), F32)
    for gi in range(len(POOL_WINDOWS)):
        wpool_bd = wpool_bd.at[gi * POOL_GROUP:(gi + 1) * POOL_GROUP, gi * POOL_GROUP:(gi + 1) * POOL_GROUP].set(w_pool[l, gi])
    wr = w_router[l].T
    wr_hi = wr.astype(BF16)
    wr_lo = (wr - wr_hi.astype(F32)).astype(BF16)
    br = b_router[l].reshape(N_EXPERTS, 1)
    tri = (jnp.arange(TM_MERGE)[:, None] < jnp.arange(TM_MERGE)[None, :]).astype(BF16)
    lower = (jnp.arange(N_EXPERTS)[None, :] < jnp.arange(N_EXPERTS)[:, None]).astype(BF16)

    km, vm = _memkv(mem, norm_mem[l].reshape(1, D), w_mem_kv[l].astype(BF16))
    qc, qr, kvcmp, ksel, vsel, kwin, vwin, pool_in, qm, gates = _inproj(
        x2, norm_mix[l].reshape(1, D), w_a, cos_t, sin_t, S)
    kvc = kvcmp.reshape(B, S // CMP_STRIDE, CMP_STRIDE, 2 * NSA_GROUPS, HEAD_DIM).transpose(0, 3, 1, 2, 4)
    kvc = kvc.reshape(B, 2 * NSA_GROUPS, S // CMP_STRIDE, CMP_HALF)
    to_slabs = lambda v: jnp.swapaxes(v.reshape(NSA_GROUPS, B, S // KV_LANE_CHUNK, KV_LANE_CHUNK, HEAD_DIM), 3, 4)
    o_nsa = _nsa(qc, qr, kvc, ksel, to_slabs(vsel), kwin, to_slabs(vwin), gates, w1, w2, w2t, pos, overlap_t,
                 drop_bias, B, S)
    consts = [norm_mix[l].reshape(1, D), w_mg, wpool_bd.astype(BF16), pool_scale[l].reshape(1, POOL_WIDTH),
              w_up_nsa[l].astype(BF16), w_up_pool[l].astype(BF16), w_up_mem[l].astype(BF16), w_out[l].astype(BF16),
              norm_ffn[l].reshape(1, D), wr_hi, wr_lo, br, tri, lower]
    x1, hf, route, counts = _merge(x2, o_nsa, pool_in, qm, km, vm, consts, B, S)

    counts = counts[:, 0].astype(jnp.int32)
    padded = ((counts + TM_E - 1) // TM_E) * TM_E
    ends = jnp.cumsum(padded)
    starts = ends - padded
    n_tiles = (T * TOP_K) // TM_E + N_EXPERTS
    P = n_tiles * TM_E
    e_k = route[ROUTE_E:ROUTE_E + TOP_K].astype(jnp.int32)
    r_k = route[ROUTE_R:ROUTE_R + TOP_K].astype(jnp.int32)
    dest = starts[e_k] + r_k
    tile_start = jnp.arange(n_tiles, dtype=jnp.int32) * TM_E
    tile_expert = jnp.minimum(jnp.sum(tile_start[:, None] >= ends[None, :], axis=1), N_EXPERTS - 1).astype(jnp.int32)
    n_used = (ends[-1] // TM_E).astype(jnp.int32).reshape(1)

    xs = _dispatch(dest, ends, padded, hf, P)
    ys = _moe(tile_expert, n_used, xs, w_gate_up[l].astype(BF16), b_gate_up[l].reshape(N_EXPERTS, 1, 2 * D_FF),
              w_down[l].astype(BF16), b_down[l].reshape(N_EXPERTS, 1, D_MODEL))
    out = _final(x1, ys, dest, route, norm_final.reshape(1, D))
    return out.reshape(B, S, D)
```

```python
import functools

import jax
import jax.numpy as jnp
from jax import lax
from jax.experimental import pallas as pl
from jax.experimental.pallas import tpu as pltpu

F32 = jnp.float32
BF16 = jnp.bfloat16

D_MODEL = 1024
HEAD_DIM = 64
NSA_HEADS = 8
NSA_GROUPS = 2
HEADS_PER_GROUP = NSA_HEADS // NSA_GROUPS
NSA_WIDTH = NSA_HEADS * HEAD_DIM
KV_WIDTH = NSA_GROUPS * HEAD_DIM
CMP_BLOCK = 32
CMP_STRIDE = 16
SEL_BLOCK = 64
SEL_TOPN = 8
FORCE_BONUS = 1000.0
WINDOW = 512
POOL_WINDOWS = (2, 4, 8, 16)
POOL_GROUP = 64
POOL_WIDTH = POOL_GROUP * len(POOL_WINDOWS)
POOL_HALO = 16
MEM_HEADS = 4
MEM_WIDTH = MEM_HEADS * HEAD_DIM
N_EXPERTS = 32
TOP_K = 4
D_FF = 1024
SWIGLU_LIMIT = 7.0
SWIGLU_ALPHA = 1.702
ROPE_THETA = 10000.0
EPS = 1e-5
NEG_INF = -1e30
TINY = 1e-30
QK_SCALE = HEAD_DIM ** -0.5
LOG2_E = 1.4426950408889634
NSA_Q_SCALE = QK_SCALE * LOG2_E

LANES = 128
GATE_PAD = LANES

TM_IN = 512
TQ = 128
CKS = 512
WIN_CHUNKS = (256, 256, 128)
TM_MERGE = 512
TM_E = 512
TM_DISPATCH = 512
TM_FINAL = 256
VMEM_LIMIT = 56 * 1024 * 1024


def _rms(x, g):
    return x * lax.rsqrt(jnp.mean(x * x, axis=-1, keepdims=True) + EPS) * g


def _dot(a, b):
    return jnp.dot(a, b, preferred_element_type=F32)


def _dot_nt(a, b):
    return lax.dot_general(a, b, (((1,), (1,)), ((), ())), preferred_element_type=F32)


def _memkv_kernel(mem_ref, g_ref, w_ref, k_ref, v_ref):
    m = _rms(mem_ref[0], g_ref[...]).astype(BF16)
    kv = _dot(m, w_ref[...])
    for h in range(MEM_HEADS):
        k_ref[0, h] = kv[:, h * HEAD_DIM:(h + 1) * HEAD_DIM].astype(BF16)
        v_ref[0, h] = kv[:, MEM_WIDTH + h * HEAD_DIM:MEM_WIDTH + (h + 1) * HEAD_DIM].astype(BF16)


def _memkv(mem, g, w):
    B, M, D = mem.shape
    return pl.pallas_call(
        _memkv_kernel,
        grid=(B,),
        in_specs=[pl.BlockSpec((1, M, D), lambda b: (b, 0, 0)),
                  pl.BlockSpec((1, D), lambda b: (0, 0)),
                  pl.BlockSpec((D, 2 * MEM_WIDTH), lambda b: (0, 0))],
        out_specs=[pl.BlockSpec((1, MEM_HEADS, M, HEAD_DIM), lambda b: (b, 0, 0, 0)),
                   pl.BlockSpec((1, MEM_HEADS, M, HEAD_DIM), lambda b: (b, 0, 0, 0))],
        out_shape=[jax.ShapeDtypeStruct((B, MEM_HEADS, M, HEAD_DIM), BF16),
                   jax.ShapeDtypeStruct((B, MEM_HEADS, M, HEAD_DIM), BF16)],
        compiler_params=pltpu.CompilerParams(dimension_semantics=("arbitrary",), vmem_limit_bytes=VMEM_LIMIT),
        name="memkv",
    )(mem, g, w)


IN_COLS = NSA_WIDTH + 6 * KV_WIDTH + POOL_WIDTH + MEM_WIDTH + GATE_PAD


def _inproj_kernel(x_ref, g_ref, w_ref, cos_ref, sin_ref,
                   qc_ref, qr_ref, kvc_ref, ksel_ref, vsel_ref, kwin_ref, vwin_ref, pool_ref, qm_ref, gate_ref):
    h = _rms(x_ref[...], g_ref[...]).astype(BF16)
    p = _dot(h, w_ref[...])
    cos = cos_ref[...]
    sin = sin_ref[...]
    lane = lax.broadcasted_iota(jnp.int32, cos.shape, 1)
    first_half = (lane % HEAD_DIM) < (HEAD_DIM // 2)

    def rope(c):
        partner = jnp.where(first_half, pltpu.roll(c, LANES - HEAD_DIM // 2, 1), pltpu.roll(c, HEAD_DIM // 2, 1))
        return c * cos + partner * sin

    def halves(c):
        return c[:, :HEAD_DIM], c[:, HEAD_DIM:]

    for j in range(NSA_WIDTH // LANES):
        c = p[:, j * LANES:(j + 1) * LANES]
        r = rope(c)
        for hh, (cc, rr) in enumerate(zip(halves(c), halves(r))):
            qc_ref[2 * j + hh] = (cc * NSA_Q_SCALE).astype(BF16)
            qr_ref[2 * j + hh] = (rr * NSA_Q_SCALE).astype(BF16)
    o = NSA_WIDTH
    kvc_ref[...] = p[:, o:o + 2 * KV_WIDTH]
    o += 2 * KV_WIDTH
    for ref, rot in ((ksel_ref, True), (vsel_ref, False), (kwin_ref, True), (vwin_ref, False)):
        c = p[:, o:o + KV_WIDTH]
        if rot:
            c = rope(c)
        for g, cc in enumerate(halves(c)):
            ref[g] = cc.astype(BF16)
        o += KV_WIDTH
    pool_ref[...] = p[:, o:o + POOL_WIDTH]
    o += POOL_WIDTH
    for hh in range(MEM_HEADS):
        qm_ref[hh] = (p[:, o + hh * HEAD_DIM:o + (hh + 1) * HEAD_DIM] * QK_SCALE).astype(BF16)
    o += MEM_WIDTH
    gate_ref[...] = jax.nn.sigmoid(p[:, o:o + GATE_PAD])


def _inproj(x2, g, w_a, cos_t, sin_t, S):
    T = x2.shape[0]
    tm = TM_IN
    n_s = S // tm
    head_spec = lambda n: pl.BlockSpec((n, tm, HEAD_DIM), lambda i: (0, i, 0))
    row_spec = lambda w: pl.BlockSpec((tm, w), lambda i: (i, 0))
    return pl.pallas_call(
        _inproj_kernel,
        grid=(T // tm,),
        in_specs=[row_spec(D_MODEL),
                  pl.BlockSpec((1, D_MODEL), lambda i: (0, 0)),
                  pl.BlockSpec((D_MODEL, IN_COLS), lambda i: (0, 0)),
                  pl.BlockSpec((tm, LANES), lambda i: (i % n_s, 0)),
                  pl.BlockSpec((tm, LANES), lambda i: (i % n_s, 0))],
        out_specs=[head_spec(NSA_HEADS), head_spec(NSA_HEADS), row_spec(2 * KV_WIDTH),
                   head_spec(NSA_GROUPS), head_spec(NSA_GROUPS), head_spec(NSA_GROUPS), head_spec(NSA_GROUPS),
                   row_spec(POOL_WIDTH), head_spec(MEM_HEADS), row_spec(GATE_PAD)],
        out_shape=[jax.ShapeDtypeStruct((NSA_HEADS, T, HEAD_DIM), BF16),
                   jax.ShapeDtypeStruct((NSA_HEADS, T, HEAD_DIM), BF16),
                   jax.ShapeDtypeStruct((T, 2 * KV_WIDTH), F32),
                   jax.ShapeDtypeStruct((NSA_GROUPS, T, HEAD_DIM), BF16),
                   jax.ShapeDtypeStruct((NSA_GROUPS, T, HEAD_DIM), BF16),
                   jax.ShapeDtypeStruct((NSA_GROUPS, T, HEAD_DIM), BF16),
                   jax.ShapeDtypeStruct((NSA_GROUPS, T, HEAD_DIM), BF16),
                   jax.ShapeDtypeStruct((T, POOL_WIDTH), F32),
                   jax.ShapeDtypeStruct((MEM_HEADS, T, HEAD_DIM), BF16),
                   jax.ShapeDtypeStruct((T, GATE_PAD), F32)],
        compiler_params=pltpu.CompilerParams(dimension_semantics=("arbitrary",), vmem_limit_bytes=VMEM_LIMIT),
        name="inproj",
    )(x2, g, w_a, cos_t, sin_t)


N_CMP_PAD = 128
CMP_FLAT = CMP_BLOCK * HEAD_DIM
CMP_HALF = CMP_STRIDE * HEAD_DIM


KV_LANE_CHUNK = LANES


def _nsa_kernel(qc_ref, qr_ref, kvc_ref, ksel_ref, vsel_ref, kwin_ref, vwin_ref, gate_ref,
                w1_ref, w2_ref, w2t_ref, pos_ref, ovl_ref, drop_ref, o_ref, kc_s, vct_s, *, S):
    i = pl.program_id(1)
    tq = TQ
    hpg = HEADS_PER_GROUP
    n_sel = S // SEL_BLOCK
    hq = hpg * tq

    @pl.when(i == 0)
    def _compress():
        for kv in range(2):
            w1 = w1_ref[kv]
            posterm = _dot(pos_ref[kv], w1)[0:1]
            for g in range(NSA_GROUPS):
                a = kvc_ref[0, kv * NSA_GROUPS + g].astype(BF16)
                p1 = _dot(a, w1[:CMP_HALF])
                p2 = _dot(a, w1[CMP_HALF:])
                hid = p1 + pltpu.roll(p2, N_CMP_PAD - 1, 0) + posterm
                hid = (hid * jax.nn.sigmoid(hid)).astype(BF16)
                if kv == 0:
                    kc_s[g] = _dot(hid, w2_ref[kv]).astype(BF16)
                else:
                    vct_s[g] = _dot_nt(w2t_ref[kv], hid).astype(BF16)

    q0 = i * tq
    t_lane = q0 + lax.broadcasted_iota(jnp.int32, (1, tq), 1)

    def key_pos(start, n):
        return start + lax.broadcasted_iota(jnp.int32, (n, 1), 0)

    cmp_valid = (key_pos(0, N_CMP_PAD) * CMP_STRIDE + CMP_BLOCK - 1) <= t_lane
    jrow = lax.broadcasted_iota(jnp.int32, (n_sel, tq), 0)
    cur = t_lane // SEL_BLOCK
    sel_valid = jrow * SEL_BLOCK <= t_lane
    forced = (jrow == 0) | (jrow == cur) | (jrow == cur - 1)

    cd = q0 // CKS
    causal_bias = jnp.where(key_pos(cd * CKS, CKS) <= t_lane, 0.0, NEG_INF)
    win_chunks = []
    hi = q0 + tq
    for n in WIN_CHUNKS:
        lo = hi - n
        start = pl.multiple_of(jnp.maximum(lo, 0), LANES)
        kp = key_pos(start, n)
        diff = t_lane - kp
        win_chunks.append((start, n, jnp.where((diff >= 0) & (diff < WINDOW) & (kp < hi), 0.0, NEG_INF)))
        hi = lo

    def values_t(ref, g, start, n):
        c0 = start // KV_LANE_CHUNK
        return jnp.concatenate([ref[g, 0, c0 + j] for j in range(n // KV_LANE_CHUNK)], axis=1)

    def attend(jobs):
        scores = [_dot_nt(k, q_all) for q_all, k, _, _, _ in jobs]
        heads = [slice(hh * tq, (hh + 1) * tq) for hh in range(hpg)]
        m_new = [jnp.concatenate([jnp.maximum(m_i[:, sl], jnp.max(s_all[:, sl] + bias, axis=0, keepdims=True))
                                  for sl in heads], axis=1)
                 for (_, _, _, bias, (m_i, _, _)), s_all in zip(jobs, scores)]
        soft = []
        for (_, _, _, bias, (m_i, l_i, _)), s_all, m in zip(jobs, scores, m_new):
            ps = [jnp.exp2(s_all[:, sl] + bias - m[:, sl]) for sl in heads]
            alpha = jnp.exp2(m_i - m)
            l = alpha * l_i + jnp.concatenate([jnp.sum(p, axis=0, keepdims=True) for p in ps], axis=1)
            soft.append((m, l, alpha, jnp.concatenate([p.astype(BF16) for p in ps], axis=1)))
        return [(m, l, alpha * acc + _dot(v_t, p))
                for (_, _, v_t, _, (_, _, acc)), (m, l, alpha, p) in zip(jobs, soft)]

    def fresh_state():
        return jnp.full((1, hq), NEG_INF, F32), jnp.zeros((1, hq), F32), jnp.zeros((HEAD_DIM, hq), F32)

    def select_blocks(g):
        q_cmp = qc_ref[g * hpg:(g + 1) * hpg].reshape(hq, HEAD_DIM)
        s_all = _dot_nt(kc_s[g], q_cmp)
        p_grp = jnp.zeros((N_CMP_PAD, tq), F32)
        ps = []
        for hh in range(hpg):
            sl = slice(hh * tq, (hh + 1) * tq)
            s = jnp.where(cmp_valid, s_all[:, sl], NEG_INF)
            m = jnp.max(s, axis=0, keepdims=True)
            e = jnp.where(cmp_valid, jnp.exp2(s - m), 0.0)
            p = e * (1.0 / jnp.maximum(jnp.sum(e, axis=0, keepdims=True), TINY))
            p_grp = p_grp + p
            ps.append(p.astype(BF16))
        o_cmp = _dot(vct_s[g], jnp.concatenate(ps, axis=1))

        ovl = ovl_ref[...]
        p_hi = p_grp.astype(BF16)
        r1 = p_grp - p_hi.astype(F32)
        p_mid = r1.astype(BF16)
        p_lo = (r1 - p_mid.astype(F32)).astype(BF16)
        score = _dot(ovl, p_hi) + _dot(ovl, p_mid) + _dot(ovl, p_lo)
        score = jnp.where(sel_valid, score + jnp.where(forced, FORCE_BONUS, 0.0), -1.0)
        rank = jnp.zeros((n_sel, tq), F32)
        for jp in range(n_sel):
            other = score[jp:jp + 1, :]
            beats = (other > score) | ((other == score) & (jrow > jp))
            rank = rank + beats.astype(F32)
        return o_cmp, (rank >= SEL_TOPN).astype(BF16)

    groups = range(NSA_GROUPS)
    cmp_out = [select_blocks(g) for g in groups]
    q_rot = [qr_ref[g * hpg:(g + 1) * hpg].reshape(hq, HEAD_DIM) for g in groups]

    def selected_job(g, c, extra_bias, state):
        k0 = pl.multiple_of(c * CKS, CKS)
        bias = _dot(drop_ref[c], cmp_out[g][1])
        if extra_bias is not None:
            bias = bias + extra_bias
        return q_rot[g], ksel_ref[g, pl.ds(k0, CKS), :], values_t(vsel_ref, g, k0, CKS), bias, state

    def window_job(g, chunk, state):
        start, n, bias = chunk
        return q_rot[g], kwin_ref[g, pl.ds(start, n), :], values_t(vwin_ref, g, start, n), bias, state

    first = attend([selected_job(g, cd, causal_bias, fresh_state()) for g in groups]
                   + [window_job(g, win_chunks[0], fresh_state()) for g in groups])
    sel_state, win_state = tuple(first[:NSA_GROUPS]), first[NSA_GROUPS:]
    for chunk in win_chunks[1:]:
        win_state = attend([window_job(g, chunk, win_state[g]) for g in groups])

    def sel_body(it, states):
        return tuple(attend([selected_job(g, cd - 1 - it, None, states[g]) for g in groups]))

    sel_state = lax.fori_loop(0, cd, sel_body, sel_state)

    gates_t = gate_ref[...].T
    out_rows = []
    for g in groups:
        o_cmp = cmp_out[g][0]
        o_sel = sel_state[g][2] * (1.0 / sel_state[g][1])
        o_win = win_state[g][2] * (1.0 / win_state[g][1])
        for hh in range(hpg):
            h = g * hpg + hh
            sl = slice(hh * tq, (hh + 1) * tq)
            out_rows.append(gates_t[3 * h:3 * h + 1] * o_cmp[:, sl] + gates_t[3 * h + 1:3 * h + 2] * o_sel[:, sl]
                            + gates_t[3 * h + 2:3 * h + 3] * o_win[:, sl])
    o_ref[...] = jnp.concatenate(out_rows, axis=0).T.astype(BF16)


def _nsa(qc, qr, kvc, ksel, vsel_t, kwin, vwin_t, gates, w1, w2, w2t, pos, ovl, drop_bias, B, S):
    T = B * S
    tq = TQ
    nq = S // tq
    q_spec = pl.BlockSpec((NSA_HEADS, tq, HEAD_DIM), lambda b, i: (0, b * nq + i, 0))
    k_spec = pl.BlockSpec((NSA_GROUPS, S, HEAD_DIM), lambda b, i: (0, b, 0))
    v_spec = pl.BlockSpec((NSA_GROUPS, 1, S // KV_LANE_CHUNK, HEAD_DIM, KV_LANE_CHUNK), lambda b, i: (0, b, 0, 0, 0))
    full = lambda a: pl.BlockSpec(a.shape, lambda b, i: (0,) * a.ndim)
    return pl.pallas_call(
        functools.partial(_nsa_kernel, S=S),
        grid=(B, nq),
        in_specs=[q_spec, q_spec,
                  pl.BlockSpec((1, 2 * NSA_GROUPS, N_CMP_PAD, CMP_HALF), lambda b, i: (b, 0, 0, 0)),
                  k_spec, v_spec, k_spec, v_spec,
                  pl.BlockSpec((tq, GATE_PAD), lambda b, i: (b * nq + i, 0)),
                  full(w1), full(w2), full(w2t), full(pos), full(ovl), full(drop_bias)],
        out_specs=pl.BlockSpec((tq, NSA_WIDTH), lambda b, i: (b * nq + i, 0)),
        out_shape=jax.ShapeDtypeStruct((T, NSA_WIDTH), BF16),
        scratch_shapes=[pltpu.VMEM((NSA_GROUPS, N_CMP_PAD, HEAD_DIM), BF16),
                        pltpu.VMEM((NSA_GROUPS, HEAD_DIM, N_CMP_PAD), BF16)],
        compiler_params=pltpu.CompilerParams(dimension_semantics=("arbitrary", "arbitrary"),
                                             vmem_limit_bytes=VMEM_LIMIT),
        name="nsa",
    )(qc, qr, kvc, ksel, vsel_t, kwin, vwin_t, gates, w1, w2, w2t, pos, ovl, drop_bias)


ROUTE_E, ROUTE_R, ROUTE_W = 0, TOP_K, 2 * TOP_K
ROUTE_ROWS = 16


def _merge_kernel(x_ref, onsa_ref, pool_ref, prev_ref, qm_ref, km_ref, vm_ref,
                  gmix_ref, wmg_ref, wpool_ref, pscale_ref, wun_ref, wup_ref, wum_ref, wout_ref,
                  gffn_ref, wrh_ref, wrl_ref, br_ref, tri_ref, lower_ref,
                  x1_ref, hf_ref, route_ref, cnt_ref, omem_s, carry_s, *, S):
    i = pl.program_id(0)
    tm = TM_MERGE
    n_s = S // tm

    @pl.when(i == 0)
    def _init():
        carry_s[...] = jnp.zeros_like(carry_s)

    x = x_ref[...]
    h = _rms(x, gmix_ref[...]).astype(BF16)
    mg = jax.nn.sigmoid(_dot(h, wmg_ref[...]))

    u = pool_ref[...]
    seq_tile = i % n_s
    prev = jnp.where(seq_tile == 0, 0.0, prev_ref[...])
    ext = jnp.concatenate([prev, u], axis=0)
    b2 = ext[1:] + ext[:-1]
    b4 = b2[2:] + b2[:-2]
    b8 = b4[4:] + b4[:-4]
    b16 = b8[8:] + b8[:-8]
    sums = (b2[POOL_HALO - 1:POOL_HALO - 1 + tm], b4[POOL_HALO - 3:POOL_HALO - 3 + tm],
            b8[POOL_HALO - 7:POOL_HALO - 7 + tm], b16[POOL_HALO - 15:POOL_HALO - 15 + tm])
    t_seq = seq_tile * tm + lax.broadcasted_iota(jnp.int32, (tm, 1), 0)
    lane_p = lax.broadcasted_iota(jnp.int32, (tm, POOL_WIDTH), 1)
    z = jnp.zeros((tm, POOL_WIDTH), F32)
    for gi, w in enumerate(POOL_WINDOWS):
        cnt = jnp.minimum(t_seq + 1, w).astype(F32)
        z = jnp.where(lane_p // POOL_GROUP == gi, sums[gi] / cnt, z)
    z = z - u
    o_pool = (_dot(z.astype(BF16), wpool_ref[...]) * pscale_ref[...]).astype(BF16)

    for hh in range(MEM_HEADS):
        s = _dot_nt(qm_ref[hh], km_ref[0, hh])
        m = jnp.max(s, axis=-1, keepdims=True)
        e = jnp.exp(s - m)
        p = e / jnp.sum(e, axis=-1, keepdims=True)
        omem_s[:, hh * HEAD_DIM:(hh + 1) * HEAD_DIM] = _dot(p.astype(BF16), vm_ref[0, hh]).astype(BF16)

    merged = (mg[:, :D_MODEL] * _dot(onsa_ref[...], wun_ref[...])
              + mg[:, D_MODEL:2 * D_MODEL] * _dot(o_pool, wup_ref[...])
              + mg[:, 2 * D_MODEL:] * _dot(omem_s[...], wum_ref[...]))
    x1 = x + _dot(merged.astype(BF16), wout_ref[...])
    x1_ref[...] = x1
    hf = _rms(x1, gffn_ref[...])
    hf_ref[...] = hf

    hf_hi = hf.astype(BF16)
    hf_lo = (hf - hf_hi.astype(F32)).astype(BF16)
    logits = (_dot_nt(wrh_ref[...], hf_hi) + _dot_nt(wrl_ref[...], hf_hi) + _dot_nt(wrh_ref[...], hf_lo)
              + br_ref[...])
    erow = lax.broadcasted_iota(jnp.int32, (N_EXPERTS, tm), 0)
    rank = jnp.zeros((N_EXPERTS, tm), F32)
    for jp in range(N_EXPERTS):
        other = logits[jp:jp + 1, :]
        beats = (other > logits) | ((other == logits) & (erow > jp))
        rank = rank + beats.astype(F32)
    chosen = rank < TOP_K
    m = jnp.max(logits, axis=0, keepdims=True)
    e = jnp.where(chosen, jnp.exp(logits - m), 0.0)
    comb = e * (1.0 / jnp.sum(e, axis=0, keepdims=True))

    chosen_b = chosen.astype(BF16)
    carry = carry_s[:, 0:1]
    in_expert = _dot(chosen_b, tri_ref[...]) + carry
    carry_new = carry + jnp.sum(chosen.astype(F32), axis=1, keepdims=True)
    carry_s[...] = jnp.broadcast_to(carry_new, carry_s.shape)
    cnt_ref[...] = jnp.broadcast_to(carry_new, cnt_ref.shape)

    before = _dot(lower_ref[...], chosen_b)
    erow_f = erow.astype(F32)
    fields = {ROUTE_E: erow_f, ROUTE_R: in_expert, ROUTE_W: comb}
    rows = [None] * ROUTE_ROWS
    for k in range(TOP_K):
        pick = chosen & (before == k)
        for base, val in fields.items():
            rows[base + k] = jnp.sum(jnp.where(pick, val, 0.0), axis=0, keepdims=True)
    zero_row = jnp.zeros((1, tm), F32)
    route_ref[...] = jnp.concatenate([zero_row if r is None else r for r in rows], axis=0)


def _merge(x2, onsa, pool_in, qm, km, vm, consts, B, S):
    T = B * S
    tm = TM_MERGE
    n_s = S // tm
    M = km.shape[2]
    halo_per_tile = tm // POOL_HALO
    row = lambda w: pl.BlockSpec((tm, w), lambda i: (i, 0))
    full = lambda a: pl.BlockSpec(a.shape, lambda i: (0,) * a.ndim)
    mem_spec = pl.BlockSpec((1, MEM_HEADS, M, HEAD_DIM), lambda i: (i // n_s, 0, 0, 0))
    return pl.pallas_call(
        functools.partial(_merge_kernel, S=S),
        grid=(T // tm,),
        in_specs=[row(D_MODEL), row(NSA_WIDTH), row(POOL_WIDTH),
                  pl.BlockSpec((POOL_HALO, POOL_WIDTH), lambda i: (jnp.maximum(i * halo_per_tile - 1, 0), 0)),
                  pl.BlockSpec((MEM_HEADS, tm, HEAD_DIM), lambda i: (0, i, 0)),
                  mem_spec, mem_spec] + [full(c) for c in consts],
        out_specs=[row(D_MODEL), row(D_MODEL), pl.BlockSpec((ROUTE_ROWS, tm), lambda i: (0, i)),
                   pl.BlockSpec((N_EXPERTS, LANES), lambda i: (0, 0))],
        out_shape=[jax.ShapeDtypeStruct((T, D_MODEL), F32),
                   jax.ShapeDtypeStruct((T, D_MODEL), F32),
                   jax.ShapeDtypeStruct((ROUTE_ROWS, T), F32),
                   jax.ShapeDtypeStruct((N_EXPERTS, LANES), F32)],
        scratch_shapes=[pltpu.VMEM((tm, MEM_WIDTH), BF16), pltpu.VMEM((N_EXPERTS, LANES), F32)],
        compiler_params=pltpu.CompilerParams(dimension_semantics=("arbitrary",), vmem_limit_bytes=VMEM_LIMIT),
        name="merge",
    )(x2, onsa, pool_in, pool_in, qm, km, vm, *consts)


def _dispatch_kernel(ends_ref, padded_ref, dest_ref, h_ref, xs_ref, zero_s, sem, zsem):
    i = pl.program_id(0)
    tm = h_ref.shape[0]

    @pl.when(i == 0)
    def _clear_pad_tiles():
        zero_s[...] = jnp.zeros_like(zero_s)

        def tile_copy(e):
            start = pl.multiple_of(ends_ref[e] - TM_E, TM_E)
            return pltpu.make_async_copy(zero_s, xs_ref.at[pl.ds(start, TM_E)], zsem)

        for e in range(N_EXPERTS):
            @pl.when(padded_ref[e] > 0)
            def _():
                tile_copy(e).start()
        for e in range(N_EXPERTS):
            @pl.when(padded_ref[e] > 0)
            def _():
                tile_copy(e).wait()

        def tail_copy(j):
            return pltpu.make_async_copy(zero_s, xs_ref.at[pl.ds(pl.multiple_of(j * TM_E, TM_E), TM_E)], zsem)

        def tail_start(j, c):
            tail_copy(j).start()
            return c

        def tail_wait(j, c):
            tail_copy(j).wait()
            return c

        n_used = ends_ref[N_EXPERTS - 1] // TM_E
        lax.fori_loop(n_used, xs_ref.shape[0] // TM_E, tail_start, 0)
        lax.fori_loop(n_used, xs_ref.shape[0] // TM_E, tail_wait, 0)

    def issue(r, c):
        for k in range(TOP_K):
            d = dest_ref[0, 0, k * tm + r]
            pltpu.make_async_copy(h_ref.at[pl.ds(r, 1)], xs_ref.at[pl.ds(d, 1)], sem).start()
        return c

    lax.fori_loop(0, tm, issue, 0)
    for k in range(TOP_K):
        pltpu.make_async_copy(h_ref, xs_ref.at[pl.ds(0, tm)], sem).wait()


def _tile_major(dest, tm):
    T = dest.shape[1]
    return dest.reshape(TOP_K, T // tm, tm).transpose(1, 0, 2).reshape(T // tm, 1, TOP_K * tm)


def _dispatch(dest, ends, padded, hf, P):
    T = hf.shape[0]
    tm = TM_DISPATCH
    dest3 = _tile_major(dest, tm)
    grid_spec = pltpu.PrefetchScalarGridSpec(
        num_scalar_prefetch=2,
        grid=(T // tm,),
        in_specs=[pl.BlockSpec((1, 1, tm * TOP_K), lambda i, en, pd: (i, 0, 0), memory_space=pltpu.SMEM),
                  pl.BlockSpec((tm, D_MODEL), lambda i, en, pd: (i, 0))],
        out_specs=pl.BlockSpec(memory_space=pl.ANY),
        scratch_shapes=[pltpu.VMEM((TM_E, D_MODEL), F32), pltpu.SemaphoreType.DMA, pltpu.SemaphoreType.DMA],
    )
    return pl.pallas_call(
        _dispatch_kernel,
        grid_spec=grid_spec,
        out_shape=jax.ShapeDtypeStruct((P, D_MODEL), F32),
        compiler_params=pltpu.CompilerParams(dimension_semantics=("arbitrary",), vmem_limit_bytes=VMEM_LIMIT),
        name="dispatch",
    )(ends, padded, dest3, hf)


def _moe_kernel(te_ref, nu_ref, xs_ref, wgu_ref, bgu_ref, wd_ref, bd_ref, ys_ref):
    j = pl.program_id(0)

    @pl.when(j < nu_ref[0])
    def _compute():
        xb = xs_ref[...].astype(BF16)
        gu = _dot(xb, wgu_ref[0]) + bgu_ref[0]
        gate = jnp.minimum(gu[:, :D_FF], SWIGLU_LIMIT)
        up = jnp.clip(gu[:, D_FF:], -SWIGLU_LIMIT, SWIGLU_LIMIT)
        act = (up + 1.0) * (gate * jax.nn.sigmoid(SWIGLU_ALPHA * gate))
        ys_ref[...] = _dot(act.astype(BF16), wd_ref[0]) + bd_ref[0]

    @pl.when(j >= nu_ref[0])
    def _unused():
        ys_ref[...] = jnp.zeros_like(ys_ref)


def _moe(tile_expert, n_used, xs, wgu, bgu, wd, bd):
    P = xs.shape[0]
    tm = TM_E
    grid_spec = pltpu.PrefetchScalarGridSpec(
        num_scalar_prefetch=2,
        grid=(P // tm,),
        in_specs=[pl.BlockSpec((tm, D_MODEL), lambda j, te, nu: (j, 0)),
                  pl.BlockSpec((1, D_MODEL, 2 * D_FF), lambda j, te, nu: (te[j], 0, 0)),
                  pl.BlockSpec((1, 1, 2 * D_FF), lambda j, te, nu: (te[j], 0, 0)),
                  pl.BlockSpec((1, D_FF, D_MODEL), lambda j, te, nu: (te[j], 0, 0)),
                  pl.BlockSpec((1, 1, D_MODEL), lambda j, te, nu: (te[j], 0, 0))],
        out_specs=pl.BlockSpec((tm, D_MODEL), lambda j, te, nu: (j, 0)),
    )
    return pl.pallas_call(
        _moe_kernel,
        grid_spec=grid_spec,
        out_shape=jax.ShapeDtypeStruct((P, D_MODEL), F32),
        compiler_params=pltpu.CompilerParams(dimension_semantics=("arbitrary",), vmem_limit_bytes=VMEM_LIMIT),
        name="moe",
    )(tile_expert, n_used, xs, wgu, bgu, wd, bd)


def _final_kernel(idx_ref, nxt_ref, x1_ref, route_ref, g_ref, ys_ref, o_ref, buf, sems):
    i = pl.program_id(0)
    n = pl.num_programs(0)
    tm = x1_ref.shape[0]

    def row_copy(src_row, slot, k, r):
        return pltpu.make_async_copy(ys_ref.at[pl.ds(src_row, 1)], buf.at[slot, k, pl.ds(r, 1)], sems.at[slot])

    def fetch(ref, slot):
        def issue(r, c):
            for k in range(TOP_K):
                row_copy(ref[0, 0, k * tm + r], slot, k, r).start()
            return c
        lax.fori_loop(0, tm, issue, 0)

    @pl.when(i == 0)
    def _first():
        fetch(idx_ref, 0)

    @pl.when(i + 1 < n)
    def _prefetch():
        fetch(nxt_ref, (i + 1) % 2)

    slot = i % 2
    for k in range(TOP_K):
        pltpu.make_async_copy(ys_ref.at[pl.ds(0, tm)], buf.at[slot, k], sems.at[slot]).wait()

    route_t = jnp.concatenate([route_ref[...], jnp.zeros((LANES - ROUTE_ROWS, tm), F32)], axis=0).T
    acc = x1_ref[...]
    for k in range(TOP_K):
        acc = acc + route_t[:, ROUTE_W + k:ROUTE_W + k + 1] * buf[slot, k]
    o_ref[...] = _rms(acc, g_ref[...])


def _final(x1, ys, dest, route, g):
    T = x1.shape[0]
    tm = TM_FINAL
    n = T // tm
    idx3 = _tile_major(dest, tm)
    smem_spec = lambda f: pl.BlockSpec((1, 1, TOP_K * tm), f, memory_space=pltpu.SMEM)
    return pl.pallas_call(
        _final_kernel,
        grid=(n,),
        in_specs=[smem_spec(lambda i: (i, 0, 0)),
                  smem_spec(lambda i: (jnp.minimum(i + 1, n - 1), 0, 0)),
                  pl.BlockSpec((tm, D_MODEL), lambda i: (i, 0)),
                  pl.BlockSpec((ROUTE_ROWS, tm), lambda i: (0, i)),
                  pl.BlockSpec((1, D_MODEL), lambda i: (0, 0)),
                  pl.BlockSpec(memory_space=pl.ANY)],
        out_specs=pl.BlockSpec((tm, D_MODEL), lambda i: (i, 0)),
        out_shape=jax.ShapeDtypeStruct((T, D_MODEL), F32),
        scratch_shapes=[pltpu.VMEM((2, TOP_K, tm, D_MODEL), F32), pltpu.SemaphoreType.DMA((2,))],
        compiler_params=pltpu.CompilerParams(dimension_semantics=("arbitrary",), vmem_limit_bytes=VMEM_LIMIT),
        name="final",
    )(idx3, idx3, x1, route, g, ys)


def _rope_tables(S):
    half = HEAD_DIM // 2
    inv = ROPE_THETA ** (-jnp.arange(half, dtype=F32) / half)
    ang = jnp.arange(S, dtype=F32)[:, None] * inv[None, :]
    cos = jnp.tile(jnp.cos(ang), (1, LANES // half))
    sin = jnp.tile(jnp.concatenate([-jnp.sin(ang), jnp.sin(ang)], axis=1), (1, LANES // HEAD_DIM))
    return cos, sin


def _selection_constants(S):
    nc = (S - CMP_BLOCK) // CMP_STRIDE + 1
    n_sel = S // SEL_BLOCK
    j = jnp.arange(n_sel)[:, None]
    i = jnp.arange(N_CMP_PAD)[None, :]
    overlap_t = ((i * CMP_STRIDE <= j * SEL_BLOCK + SEL_BLOCK - 1)
                 & (i * CMP_STRIDE + CMP_BLOCK - 1 >= j * SEL_BLOCK) & (i < nc)).astype(BF16)
    row = jnp.arange(LANES)[:, None]
    key = jnp.arange(S)[None, :]
    drop_bias = jnp.where(key // SEL_BLOCK == row, NEG_INF, 0.0).astype(BF16)[:n_sel]
    drop_bias = drop_bias.T.reshape(S // CKS, CKS, n_sel)
    return overlap_t, drop_bias


def kernel(x, mem, norm_mix, norm_mem, w_in, cmp_pos, cmp_w1, cmp_w2, w_pool, pool_scale, w_mem_kv, w_up_nsa,
           w_up_pool, w_up_mem, w_out, norm_ffn, w_router, b_router, w_gate_up, b_gate_up, w_down, b_down,
           norm_final):
    B, S, D = x.shape
    T = B * S
    assert D == D_MODEL and S % CKS == 0 and S // SEL_BLOCK == 32 and T % TM_DISPATCH == 0
    l = 0
    x2 = x.reshape(T, D)

    w = w_in[l]
    o_gate = NSA_WIDTH + 6 * KV_WIDTH
    n_gate = 3 * NSA_HEADS
    o_pool = o_gate + n_gate
    o_qm = o_pool + POOL_WIDTH
    o_mg = o_qm + MEM_WIDTH
    w_a = jnp.concatenate([w[:, :o_gate], w[:, o_pool:o_mg], w[:, o_gate:o_pool],
                           jnp.zeros((D, GATE_PAD - n_gate), F32)], axis=1).astype(BF16)
    w_mg = w[:, o_mg:].astype(BF16)
    cos_t, sin_t = _rope_tables(S)
    overlap_t, drop_bias = _selection_constants(S)
    w1 = cmp_w1[l].reshape(2, CMP_FLAT, HEAD_DIM).astype(BF16)
    w2 = cmp_w2[l].astype(BF16)
    w2t = jnp.swapaxes(w2, 1, 2)
    pos = jnp.broadcast_to(cmp_pos[l].reshape(2, 1, CMP_FLAT), (2, 8, CMP_FLAT)).astype(BF16)
    wpool_bd = jnp.zeros((POOL_WIDTH, POOL_WIDTH), F32)
    for gi in range(len(POOL_WINDOWS)):
        wpool_bd = wpool_bd.at[gi * POOL_GROUP:(gi + 1) * POOL_GROUP, gi * POOL_GROUP:(gi + 1) * POOL_GROUP].set(w_pool[l, gi])
    wr = w_router[l].T
    wr_hi = wr.astype(BF16)
    wr_lo = (wr - wr_hi.astype(F32)).astype(BF16)
    br = b_router[l].reshape(N_EXPERTS, 1)
    tri = (jnp.arange(TM_MERGE)[:, None] < jnp.arange(TM_MERGE)[None, :]).astype(BF16)
    lower = (jnp.arange(N_EXPERTS)[None, :] < jnp.arange(N_EXPERTS)[:, None]).astype(BF16)

    km, vm = _memkv(mem, norm_mem[l].reshape(1, D), w_mem_kv[l].astype(BF16))
    qc, qr, kvcmp, ksel, vsel, kwin, vwin, pool_in, qm, gates = _inproj(
        x2, norm_mix[l].reshape(1, D), w_a, cos_t, sin_t, S)
    kvc = kvcmp.reshape(B, S // CMP_STRIDE, CMP_STRIDE, 2 * NSA_GROUPS, HEAD_DIM).transpose(0, 3, 1, 2, 4)
    kvc = kvc.reshape(B, 2 * NSA_GROUPS, S // CMP_STRIDE, CMP_HALF)
    to_slabs = lambda v: jnp.swapaxes(v.reshape(NSA_GROUPS, B, S // KV_LANE_CHUNK, KV_LANE_CHUNK, HEAD_DIM), 3, 4)
    o_nsa = _nsa(qc, qr, kvc, ksel, to_slabs(vsel), kwin, to_slabs(vwin), gates, w1, w2, w2t, pos, overlap_t,
                 drop_bias, B, S)
    consts = [norm_mix[l].reshape(1, D), w_mg, wpool_bd.astype(BF16), pool_scale[l].reshape(1, POOL_WIDTH),
              w_up_nsa[l].astype(BF16), w_up_pool[l].astype(BF16), w_up_mem[l].astype(BF16), w_out[l].astype(BF16),
              norm_ffn[l].reshape(1, D), wr_hi, wr_lo, br, tri, lower]
    x1, hf, route, counts = _merge(x2, o_nsa, pool_in, qm, km, vm, consts, B, S)

    counts = counts[:, 0].astype(jnp.int32)
    padded = ((counts + TM_E - 1) // TM_E) * TM_E
    ends = jnp.cumsum(padded)
    starts = ends - padded
    n_tiles = (T * TOP_K) // TM_E + N_EXPERTS
    P = n_tiles * TM_E
    e_k = route[ROUTE_E:ROUTE_E + TOP_K].astype(jnp.int32)
    r_k = route[ROUTE_R:ROUTE_R + TOP_K].astype(jnp.int32)
    group_start = jnp.zeros_like(e_k)
    for e in range(N_EXPERTS):
        group_start = jnp.where(e_k == e, starts[e], group_start)
    dest = group_start + r_k
    tile_start = jnp.arange(n_tiles, dtype=jnp.int32) * TM_E
    tile_expert = jnp.minimum(jnp.sum(tile_start[:, None] >= ends[None, :], axis=1), N_EXPERTS - 1).astype(jnp.int32)
    n_used = (ends[-1] // TM_E).astype(jnp.int32).reshape(1)

    xs = _dispatch(dest, ends, padded, hf, P)
    ys = _moe(tile_expert, n_used, xs, w_gate_up[l].astype(BF16), b_gate_up[l].reshape(N_EXPERTS, 1, 2 * D_FF),
              w_down[l].astype(BF16), b_down[l].reshape(N_EXPERTS, 1, D_MODEL))
    out = _final(x1, ys, dest, route, norm_final.reshape(1, D))
    return out.reshape(B, S, D)
```

```python
import functools

import jax
import jax.numpy as jnp
from jax import lax
from jax.experimental import pallas as pl
from jax.experimental.pallas import tpu as pltpu

F32 = jnp.float32
BF16 = jnp.bfloat16

D_MODEL = 1024
HEAD_DIM = 64
NSA_HEADS = 8
NSA_GROUPS = 2
HEADS_PER_GROUP = NSA_HEADS // NSA_GROUPS
NSA_WIDTH = NSA_HEADS * HEAD_DIM
KV_WIDTH = NSA_GROUPS * HEAD_DIM
CMP_BLOCK = 32
CMP_STRIDE = 16
SEL_BLOCK = 64
SEL_TOPN = 8
FORCE_BONUS = 1000.0
WINDOW = 512
POOL_WINDOWS = (2, 4, 8, 16)
POOL_GROUP = 64
POOL_WIDTH = POOL_GROUP * len(POOL_WINDOWS)
POOL_HALO = 16
MEM_HEADS = 4
MEM_WIDTH = MEM_HEADS * HEAD_DIM
N_EXPERTS = 32
TOP_K = 4
D_FF = 1024
SWIGLU_LIMIT = 7.0
SWIGLU_ALPHA = 1.702
ROPE_THETA = 10000.0
EPS = 1e-5
NEG_INF = -1e30
TINY = 1e-30
QK_SCALE = HEAD_DIM ** -0.5
LOG2_E = 1.4426950408889634
NSA_Q_SCALE = QK_SCALE * LOG2_E

LANES = 128
GATE_PAD = LANES

TM_IN = 512
TQ = 128
CKS = 512
WIN_CHUNKS = (256, 256, 128)
TM_MERGE = 512
TM_E = 512
TM_DISPATCH = 512
TM_FINAL = 256
VMEM_LIMIT = 56 * 1024 * 1024


def _rms(x, g):
    return x * lax.rsqrt(jnp.mean(x * x, axis=-1, keepdims=True) + EPS) * g


def _sigmoid(x):
    return 0.5 * jnp.tanh(0.5 * x) + 0.5


def _dot(a, b):
    return jnp.dot(a, b, preferred_element_type=F32)


def _dot_nt(a, b):
    return lax.dot_general(a, b, (((1,), (1,)), ((), ())), preferred_element_type=F32)


def _memkv_kernel(mem_ref, g_ref, w_ref, k_ref, v_ref):
    m = _rms(mem_ref[0], g_ref[...]).astype(BF16)
    kv = _dot(m, w_ref[...])
    for h in range(MEM_HEADS):
        k_ref[0, h] = kv[:, h * HEAD_DIM:(h + 1) * HEAD_DIM].astype(BF16)
        v_ref[0, h] = kv[:, MEM_WIDTH + h * HEAD_DIM:MEM_WIDTH + (h + 1) * HEAD_DIM].astype(BF16)


def _memkv(mem, g, w):
    B, M, D = mem.shape
    return pl.pallas_call(
        _memkv_kernel,
        grid=(B,),
        in_specs=[pl.BlockSpec((1, M, D), lambda b: (b, 0, 0)),
                  pl.BlockSpec((1, D), lambda b: (0, 0)),
                  pl.BlockSpec((D, 2 * MEM_WIDTH), lambda b: (0, 0))],
        out_specs=[pl.BlockSpec((1, MEM_HEADS, M, HEAD_DIM), lambda b: (b, 0, 0, 0)),
                   pl.BlockSpec((1, MEM_HEADS, M, HEAD_DIM), lambda b: (b, 0, 0, 0))],
        out_shape=[jax.ShapeDtypeStruct((B, MEM_HEADS, M, HEAD_DIM), BF16),
                   jax.ShapeDtypeStruct((B, MEM_HEADS, M, HEAD_DIM), BF16)],
        compiler_params=pltpu.CompilerParams(dimension_semantics=("arbitrary",), vmem_limit_bytes=VMEM_LIMIT),
        name="memkv",
    )(mem, g, w)


IN_COLS = NSA_WIDTH + 6 * KV_WIDTH + POOL_WIDTH + MEM_WIDTH + GATE_PAD


def _inproj_kernel(x_ref, g_ref, w_ref, cos_ref, sin_ref,
                   qc_ref, qr_ref, kvc_ref, ksel_ref, vsel_ref, kwin_ref, vwin_ref, pool_ref, qm_ref, gate_ref):
    h = _rms(x_ref[...], g_ref[...]).astype(BF16)
    p = _dot(h, w_ref[...])
    cos = cos_ref[...]
    sin = sin_ref[...]
    lane = lax.broadcasted_iota(jnp.int32, cos.shape, 1)
    first_half = (lane % HEAD_DIM) < (HEAD_DIM // 2)

    def rope(c):
        partner = jnp.where(first_half, pltpu.roll(c, LANES - HEAD_DIM // 2, 1), pltpu.roll(c, HEAD_DIM // 2, 1))
        return c * cos + partner * sin

    def halves(c):
        return c[:, :HEAD_DIM], c[:, HEAD_DIM:]

    for j in range(NSA_WIDTH // LANES):
        c = p[:, j * LANES:(j + 1) * LANES]
        r = rope(c)
        for hh, (cc, rr) in enumerate(zip(halves(c), halves(r))):
            qc_ref[2 * j + hh] = (cc * NSA_Q_SCALE).astype(BF16)
            qr_ref[2 * j + hh] = (rr * NSA_Q_SCALE).astype(BF16)
    o = NSA_WIDTH
    kvc_ref[...] = p[:, o:o + 2 * KV_WIDTH]
    o += 2 * KV_WIDTH
    for ref, rot in ((ksel_ref, True), (vsel_ref, False), (kwin_ref, True), (vwin_ref, False)):
        c = p[:, o:o + KV_WIDTH]
        if rot:
            c = rope(c)
        for g, cc in enumerate(halves(c)):
            ref[g] = cc.astype(BF16)
        o += KV_WIDTH
    pool_ref[...] = p[:, o:o + POOL_WIDTH]
    o += POOL_WIDTH
    for hh in range(MEM_HEADS):
        qm_ref[hh] = (p[:, o + hh * HEAD_DIM:o + (hh + 1) * HEAD_DIM] * QK_SCALE).astype(BF16)
    o += MEM_WIDTH
    gate_ref[...] = _sigmoid(p[:, o:o + GATE_PAD])


def _inproj(x2, g, w_a, cos_t, sin_t, S):
    T = x2.shape[0]
    tm = TM_IN
    n_s = S // tm
    head_spec = lambda n: pl.BlockSpec((n, tm, HEAD_DIM), lambda i: (0, i, 0))
    row_spec = lambda w: pl.BlockSpec((tm, w), lambda i: (i, 0))
    return pl.pallas_call(
        _inproj_kernel,
        grid=(T // tm,),
        in_specs=[row_spec(D_MODEL),
                  pl.BlockSpec((1, D_MODEL), lambda i: (0, 0)),
                  pl.BlockSpec((D_MODEL, IN_COLS), lambda i: (0, 0)),
                  pl.BlockSpec((tm, LANES), lambda i: (i % n_s, 0)),
                  pl.BlockSpec((tm, LANES), lambda i: (i % n_s, 0))],
        out_specs=[head_spec(NSA_HEADS), head_spec(NSA_HEADS), row_spec(2 * KV_WIDTH),
                   head_spec(NSA_GROUPS), head_spec(NSA_GROUPS), head_spec(NSA_GROUPS), head_spec(NSA_GROUPS),
                   row_spec(POOL_WIDTH), head_spec(MEM_HEADS), row_spec(GATE_PAD)],
        out_shape=[jax.ShapeDtypeStruct((NSA_HEADS, T, HEAD_DIM), BF16),
                   jax.ShapeDtypeStruct((NSA_HEADS, T, HEAD_DIM), BF16),
                   jax.ShapeDtypeStruct((T, 2 * KV_WIDTH), F32),
                   jax.ShapeDtypeStruct((NSA_GROUPS, T, HEAD_DIM), BF16),
                   jax.ShapeDtypeStruct((NSA_GROUPS, T, HEAD_DIM), BF16),
                   jax.ShapeDtypeStruct((NSA_GROUPS, T, HEAD_DIM), BF16),
                   jax.ShapeDtypeStruct((NSA_GROUPS, T, HEAD_DIM), BF16),
                   jax.ShapeDtypeStruct((T, POOL_WIDTH), F32),
                   jax.ShapeDtypeStruct((MEM_HEADS, T, HEAD_DIM), BF16),
                   jax.ShapeDtypeStruct((T, GATE_PAD), F32)],
        compiler_params=pltpu.CompilerParams(dimension_semantics=("arbitrary",), vmem_limit_bytes=VMEM_LIMIT),
        name="inproj",
    )(x2, g, w_a, cos_t, sin_t)


N_CMP_PAD = 128
CMP_FLAT = CMP_BLOCK * HEAD_DIM
CMP_HALF = CMP_STRIDE * HEAD_DIM


KV_LANE_CHUNK = LANES


def _nsa_kernel(qc_ref, qr_ref, kvc_ref, ksel_ref, vsel_ref, kwin_ref, vwin_ref, gate_ref,
                w1_ref, w2_ref, w2t_ref, pos_ref, ovl_ref, drop_ref, o_ref, kc_s, vct_s, *, S):
    i = pl.program_id(1)
    tq = TQ
    hpg = HEADS_PER_GROUP
    n_sel = S // SEL_BLOCK
    hq = hpg * tq

    @pl.when(i == 0)
    def _compress():
        for kv in range(2):
            w1 = w1_ref[kv]
            posterm = _dot(pos_ref[kv], w1)[0:1]
            for g in range(NSA_GROUPS):
                a = kvc_ref[0, kv * NSA_GROUPS + g].astype(BF16)
                p1 = _dot(a, w1[:CMP_HALF])
                p2 = _dot(a, w1[CMP_HALF:])
                hid = p1 + pltpu.roll(p2, N_CMP_PAD - 1, 0) + posterm
                hid = (hid * _sigmoid(hid)).astype(BF16)
                if kv == 0:
                    kc_s[g] = _dot(hid, w2_ref[kv]).astype(BF16)
                else:
                    vct_s[g] = _dot_nt(w2t_ref[kv], hid).astype(BF16)

    q0 = i * tq
    t_lane = q0 + lax.broadcasted_iota(jnp.int32, (1, tq), 1)

    def key_pos(start, n):
        return start + lax.broadcasted_iota(jnp.int32, (n, 1), 0)

    cmp_valid = (key_pos(0, N_CMP_PAD) * CMP_STRIDE + CMP_BLOCK - 1) <= t_lane
    jrow = lax.broadcasted_iota(jnp.int32, (n_sel, tq), 0)
    cur = t_lane // SEL_BLOCK
    sel_valid = jrow * SEL_BLOCK <= t_lane
    forced = (jrow == 0) | (jrow == cur) | (jrow == cur - 1)

    cd = q0 // CKS
    causal_bias = jnp.where(key_pos(cd * CKS, CKS) <= t_lane, 0.0, NEG_INF)
    win_chunks = []
    hi = q0 + tq
    for n in WIN_CHUNKS:
        lo = hi - n
        start = pl.multiple_of(jnp.maximum(lo, 0), LANES)
        kp = key_pos(start, n)
        diff = t_lane - kp
        win_chunks.append((start, n, jnp.where((diff >= 0) & (diff < WINDOW) & (kp < hi), 0.0, NEG_INF)))
        hi = lo

    def values_t(ref, g, start, n):
        c0 = start // KV_LANE_CHUNK
        return jnp.concatenate([ref[g, 0, c0 + j] for j in range(n // KV_LANE_CHUNK)], axis=1)

    def attend(jobs):
        scores = [_dot_nt(k, q_all) for q_all, k, _, _ in jobs]
        heads = [slice(hh * tq, (hh + 1) * tq) for hh in range(hpg)]
        maxes = [jnp.concatenate([jnp.max(s_all[:, sl] + bias, axis=0, keepdims=True) for sl in heads], axis=1)
                 for (_, _, _, bias), s_all in zip(jobs, scores)]
        soft = []
        for (_, _, _, bias), s_all, m in zip(jobs, scores, maxes):
            ps = [jnp.exp2(s_all[:, sl] + bias - m[:, sl]) for sl in heads]
            l = jnp.concatenate([jnp.sum(p, axis=0, keepdims=True) for p in ps], axis=1)
            soft.append((l, jnp.concatenate([p.astype(BF16) for p in ps], axis=1)))
        return [(m, l, _dot(v_t, p)) for (_, _, v_t, _), m, (l, p) in zip(jobs, maxes, soft)]

    def merge(a, b):
        m = jnp.maximum(a[0], b[0])
        wa = jnp.exp2(a[0] - m)
        wb = jnp.exp2(b[0] - m)
        return m, wa * a[1] + wb * b[1], wa * a[2] + wb * b[2]

    def select_blocks(g):
        q_cmp = qc_ref[g * hpg:(g + 1) * hpg].reshape(hq, HEAD_DIM)
        s_all = _dot_nt(kc_s[g], q_cmp)
        p_grp = jnp.zeros((N_CMP_PAD, tq), F32)
        ps = []
        for hh in range(hpg):
            sl = slice(hh * tq, (hh + 1) * tq)
            s = jnp.where(cmp_valid, s_all[:, sl], NEG_INF)
            m = jnp.max(s, axis=0, keepdims=True)
            e = jnp.where(cmp_valid, jnp.exp2(s - m), 0.0)
            p = e * (1.0 / jnp.maximum(jnp.sum(e, axis=0, keepdims=True), TINY))
            p_grp = p_grp + p
            ps.append(p.astype(BF16))
        o_cmp = _dot(vct_s[g], jnp.concatenate(ps, axis=1))

        ovl = ovl_ref[...]
        p_hi = p_grp.astype(BF16)
        r1 = p_grp - p_hi.astype(F32)
        p_mid = r1.astype(BF16)
        p_lo = (r1 - p_mid.astype(F32)).astype(BF16)
        score = _dot(ovl, p_hi) + _dot(ovl, p_mid) + _dot(ovl, p_lo)
        score = jnp.where(sel_valid, score + jnp.where(forced, FORCE_BONUS, 0.0), -1.0)
        rank = jnp.zeros((n_sel, tq), F32)
        for jp in range(n_sel):
            other = score[jp:jp + 1, :]
            beats = (other > score) | ((other == score) & (jrow > jp))
            rank = rank + beats.astype(F32)
        return o_cmp, (rank >= SEL_TOPN).astype(BF16)

    groups = range(NSA_GROUPS)
    cmp_out = [select_blocks(g) for g in groups]
    q_rot = [qr_ref[g * hpg:(g + 1) * hpg].reshape(hq, HEAD_DIM) for g in groups]

    def selected_job(g, c, extra_bias):
        k0 = pl.multiple_of(c * CKS, CKS)
        bias = _dot(drop_ref[c], cmp_out[g][1])
        if extra_bias is not None:
            bias = bias + extra_bias
        return q_rot[g], ksel_ref[g, pl.ds(k0, CKS), :], values_t(vsel_ref, g, k0, CKS), bias

    def window_job(g, chunk):
        start, n, bias = chunk
        return q_rot[g], kwin_ref[g, pl.ds(start, n), :], values_t(vwin_ref, g, start, n), bias

    stats = attend([selected_job(g, cd, causal_bias) for g in groups]
                   + [window_job(g, chunk) for chunk in win_chunks for g in groups])
    sel_state = tuple(stats[:NSA_GROUPS])
    win_state = list(stats[NSA_GROUPS:2 * NSA_GROUPS])
    for j in range(1, len(win_chunks)):
        for g in groups:
            win_state[g] = merge(win_state[g], stats[NSA_GROUPS * (1 + j) + g])

    def sel_body(it, states):
        new = attend([selected_job(g, cd - 1 - it, None) for g in groups])
        return tuple(merge(states[g], new[g]) for g in groups)

    sel_state = lax.fori_loop(0, cd, sel_body, sel_state)

    gates_t = gate_ref[...].T
    out_rows = []
    for g in groups:
        o_cmp = cmp_out[g][0]
        o_sel = sel_state[g][2] * (1.0 / sel_state[g][1])
        o_win = win_state[g][2] * (1.0 / win_state[g][1])
        for hh in range(hpg):
            h = g * hpg + hh
            sl = slice(hh * tq, (hh + 1) * tq)
            out_rows.append(gates_t[3 * h:3 * h + 1] * o_cmp[:, sl] + gates_t[3 * h + 1:3 * h + 2] * o_sel[:, sl]
                            + gates_t[3 * h + 2:3 * h + 3] * o_win[:, sl])
    o_ref[...] = jnp.concatenate(out_rows, axis=0).T.astype(BF16)


def _nsa(qc, qr, kvc, ksel, vsel_t, kwin, vwin_t, gates, w1, w2, w2t, pos, ovl, drop_bias, B, S):
    T = B * S
    tq = TQ
    nq = S // tq
    q_spec = pl.BlockSpec((NSA_HEADS, tq, HEAD_DIM), lambda b, i: (0, b * nq + i, 0))
    k_spec = pl.BlockSpec((NSA_GROUPS, S, HEAD_DIM), lambda b, i: (0, b, 0))
    v_spec = pl.BlockSpec((NSA_GROUPS, 1, S // KV_LANE_CHUNK, HEAD_DIM, KV_LANE_CHUNK), lambda b, i: (0, b, 0, 0, 0))
    full = lambda a: pl.BlockSpec(a.shape, lambda b, i: (0,) * a.ndim)
    return pl.pallas_call(
        functools.partial(_nsa_kernel, S=S),
        grid=(B, nq),
        in_specs=[q_spec, q_spec,
                  pl.BlockSpec((1, 2 * NSA_GROUPS, N_CMP_PAD, CMP_HALF), lambda b, i: (b, 0, 0, 0)),
                  k_spec, v_spec, k_spec, v_spec,
                  pl.BlockSpec((tq, GATE_PAD), lambda b, i: (b * nq + i, 0)),
                  full(w1), full(w2), full(w2t), full(pos), full(ovl), full(drop_bias)],
        out_specs=pl.BlockSpec((tq, NSA_WIDTH), lambda b, i: (b * nq + i, 0)),
        out_shape=jax.ShapeDtypeStruct((T, NSA_WIDTH), BF16),
        scratch_shapes=[pltpu.VMEM((NSA_GROUPS, N_CMP_PAD, HEAD_DIM), BF16),
                        pltpu.VMEM((NSA_GROUPS, HEAD_DIM, N_CMP_PAD), BF16)],
        compiler_params=pltpu.CompilerParams(dimension_semantics=("arbitrary", "arbitrary"),
                                             vmem_limit_bytes=VMEM_LIMIT),
        name="nsa",
    )(qc, qr, kvc, ksel, vsel_t, kwin, vwin_t, gates, w1, w2, w2t, pos, ovl, drop_bias)


ROUTE_E, ROUTE_R, ROUTE_W = 0, TOP_K, 2 * TOP_K
ROUTE_ROWS = 16


def _merge_kernel(x_ref, onsa_ref, pool_ref, prev_ref, qm_ref, km_ref, vm_ref,
                  gmix_ref, wmg_ref, wpool_ref, pscale_ref, wun_ref, wup_ref, wum_ref, wout_ref,
                  gffn_ref, wrh_ref, wrl_ref, br_ref, tri_ref, lower_ref,
                  x1_ref, hf_ref, route_ref, cnt_ref, omem_s, carry_s, *, S):
    i = pl.program_id(0)
    tm = TM_MERGE
    n_s = S // tm

    @pl.when(i == 0)
    def _init():
        carry_s[...] = jnp.zeros_like(carry_s)

    x = x_ref[...]
    h = _rms(x, gmix_ref[...]).astype(BF16)
    mg = _sigmoid(_dot(h, wmg_ref[...]))

    u = pool_ref[...]
    seq_tile = i % n_s
    prev = jnp.where(seq_tile == 0, 0.0, prev_ref[...])
    ext = jnp.concatenate([prev, u], axis=0)
    b2 = ext[1:] + ext[:-1]
    b4 = b2[2:] + b2[:-2]
    b8 = b4[4:] + b4[:-4]
    b16 = b8[8:] + b8[:-8]
    sums = (b2[POOL_HALO - 1:POOL_HALO - 1 + tm], b4[POOL_HALO - 3:POOL_HALO - 3 + tm],
            b8[POOL_HALO - 7:POOL_HALO - 7 + tm], b16[POOL_HALO - 15:POOL_HALO - 15 + tm])
    t_seq = seq_tile * tm + lax.broadcasted_iota(jnp.int32, (tm, 1), 0)
    lane_p = lax.broadcasted_iota(jnp.int32, (tm, POOL_WIDTH), 1)
    z = jnp.zeros((tm, POOL_WIDTH), F32)
    for gi, w in enumerate(POOL_WINDOWS):
        cnt = jnp.minimum(t_seq + 1, w).astype(F32)
        z = jnp.where(lane_p // POOL_GROUP == gi, sums[gi] / cnt, z)
    z = z - u
    o_pool = (_dot(z.astype(BF16), wpool_ref[...]) * pscale_ref[...]).astype(BF16)

    for hh in range(MEM_HEADS):
        s = _dot_nt(qm_ref[hh], km_ref[0, hh])
        m = jnp.max(s, axis=-1, keepdims=True)
        e = jnp.exp(s - m)
        p = e / jnp.sum(e, axis=-1, keepdims=True)
        omem_s[:, hh * HEAD_DIM:(hh + 1) * HEAD_DIM] = _dot(p.astype(BF16), vm_ref[0, hh]).astype(BF16)

    merged = (mg[:, :D_MODEL] * _dot(onsa_ref[...], wun_ref[...])
              + mg[:, D_MODEL:2 * D_MODEL] * _dot(o_pool, wup_ref[...])
              + mg[:, 2 * D_MODEL:] * _dot(omem_s[...], wum_ref[...]))
    x1 = x + _dot(merged.astype(BF16), wout_ref[...])
    x1_ref[...] = x1
    hf = _rms(x1, gffn_ref[...])
    hf_ref[...] = hf

    hf_hi = hf.astype(BF16)
    hf_lo = (hf - hf_hi.astype(F32)).astype(BF16)
    logits = (_dot_nt(wrh_ref[...], hf_hi) + _dot_nt(wrl_ref[...], hf_hi) + _dot_nt(wrh_ref[...], hf_lo)
              + br_ref[...])
    erow = lax.broadcasted_iota(jnp.int32, (N_EXPERTS, tm), 0)
    rank = jnp.zeros((N_EXPERTS, tm), F32)
    for jp in range(N_EXPERTS):
        other = logits[jp:jp + 1, :]
        beats = (other > logits) | ((other == logits) & (erow > jp))
        rank = rank + beats.astype(F32)
    chosen = rank < TOP_K
    m = jnp.max(logits, axis=0, keepdims=True)
    e = jnp.where(chosen, jnp.exp(logits - m), 0.0)
    comb = e * (1.0 / jnp.sum(e, axis=0, keepdims=True))

    chosen_b = chosen.astype(BF16)
    carry = carry_s[:, 0:1]
    in_expert = _dot(chosen_b, tri_ref[...]) + carry
    carry_new = carry + jnp.sum(chosen.astype(F32), axis=1, keepdims=True)
    carry_s[...] = jnp.broadcast_to(carry_new, carry_s.shape)
    cnt_ref[...] = jnp.broadcast_to(carry_new, cnt_ref.shape)

    before = _dot(lower_ref[...], chosen_b)
    erow_f = erow.astype(F32)
    fields = {ROUTE_E: erow_f, ROUTE_R: in_expert, ROUTE_W: comb}
    rows = [None] * ROUTE_ROWS
    for k in range(TOP_K):
        pick = chosen & (before == k)
        for base, val in fields.items():
            rows[base + k] = jnp.sum(jnp.where(pick, val, 0.0), axis=0, keepdims=True)
    zero_row = jnp.zeros((1, tm), F32)
    route_ref[...] = jnp.concatenate([zero_row if r is None else r for r in rows], axis=0)


def _merge(x2, onsa, pool_in, qm, km, vm, consts, B, S):
    T = B * S
    tm = TM_MERGE
    n_s = S // tm
    M = km.shape[2]
    halo_per_tile = tm // POOL_HALO
    row = lambda w: pl.BlockSpec((tm, w), lambda i: (i, 0))
    full = lambda a: pl.BlockSpec(a.shape, lambda i: (0,) * a.ndim)
    mem_spec = pl.BlockSpec((1, MEM_HEADS, M, HEAD_DIM), lambda i: (i // n_s, 0, 0, 0))
    return pl.pallas_call(
        functools.partial(_merge_kernel, S=S),
        grid=(T // tm,),
        in_specs=[row(D_MODEL), row(NSA_WIDTH), row(POOL_WIDTH),
                  pl.BlockSpec((POOL_HALO, POOL_WIDTH), lambda i: (jnp.maximum(i * halo_per_tile - 1, 0), 0)),
                  pl.BlockSpec((MEM_HEADS, tm, HEAD_DIM), lambda i: (0, i, 0)),
                  mem_spec, mem_spec] + [full(c) for c in consts],
        out_specs=[row(D_MODEL), row(D_MODEL), pl.BlockSpec((ROUTE_ROWS, tm), lambda i: (0, i)),
                   pl.BlockSpec((N_EXPERTS, LANES), lambda i: (0, 0))],
        out_shape=[jax.ShapeDtypeStruct((T, D_MODEL), F32),
                   jax.ShapeDtypeStruct((T, D_MODEL), F32),
                   jax.ShapeDtypeStruct((ROUTE_ROWS, T), F32),
                   jax.ShapeDtypeStruct((N_EXPERTS, LANES), F32)],
        scratch_shapes=[pltpu.VMEM((tm, MEM_WIDTH), BF16), pltpu.VMEM((N_EXPERTS, LANES), F32)],
        compiler_params=pltpu.CompilerParams(dimension_semantics=("arbitrary",), vmem_limit_bytes=VMEM_LIMIT),
        name="merge",
    )(x2, onsa, pool_in, pool_in, qm, km, vm, *consts)


def _dispatch_kernel(ends_ref, padded_ref, dest_ref, h_ref, xs_ref, zero_s, sem, zsem):
    i = pl.program_id(0)
    tm = h_ref.shape[0]

    @pl.when(i == 0)
    def _clear_pad_tiles():
        zero_s[...] = jnp.zeros_like(zero_s)

        def tile_copy(e):
            start = pl.multiple_of(ends_ref[e] - TM_E, TM_E)
            return pltpu.make_async_copy(zero_s, xs_ref.at[pl.ds(start, TM_E)], zsem)

        for e in range(N_EXPERTS):
            @pl.when(padded_ref[e] > 0)
            def _():
                tile_copy(e).start()
        for e in range(N_EXPERTS):
            @pl.when(padded_ref[e] > 0)
            def _():
                tile_copy(e).wait()

        def tail_copy(j):
            return pltpu.make_async_copy(zero_s, xs_ref.at[pl.ds(pl.multiple_of(j * TM_E, TM_E), TM_E)], zsem)

        def tail_start(j, c):
            tail_copy(j).start()
            return c

        def tail_wait(j, c):
            tail_copy(j).wait()
            return c

        n_used = ends_ref[N_EXPERTS - 1] // TM_E
        lax.fori_loop(n_used, xs_ref.shape[0] // TM_E, tail_start, 0)
        lax.fori_loop(n_used, xs_ref.shape[0] // TM_E, tail_wait, 0)

    def issue(r, c):
        for k in range(TOP_K):
            d = dest_ref[0, 0, k * tm + r]
            pltpu.make_async_copy(h_ref.at[pl.ds(r, 1)], xs_ref.at[pl.ds(d, 1)], sem).start()
        return c

    lax.fori_loop(0, tm, issue, 0)
    for k in range(TOP_K):
        pltpu.make_async_copy(h_ref, xs_ref.at[pl.ds(0, tm)], sem).wait()


def _tile_major(dest, tm):
    T = dest.shape[1]
    return dest.reshape(TOP_K, T // tm, tm).transpose(1, 0, 2).reshape(T // tm, 1, TOP_K * tm)


def _dispatch(dest, ends, padded, hf, P):
    T = hf.shape[0]
    tm = TM_DISPATCH
    dest3 = _tile_major(dest, tm)
    grid_spec = pltpu.PrefetchScalarGridSpec(
        num_scalar_prefetch=2,
        grid=(T // tm,),
        in_specs=[pl.BlockSpec((1, 1, tm * TOP_K), lambda i, en, pd: (i, 0, 0), memory_space=pltpu.SMEM),
                  pl.BlockSpec((tm, D_MODEL), lambda i, en, pd: (i, 0))],
        out_specs=pl.BlockSpec(memory_space=pl.ANY),
        scratch_shapes=[pltpu.VMEM((TM_E, D_MODEL), F32), pltpu.SemaphoreType.DMA, pltpu.SemaphoreType.DMA],
    )
    return pl.pallas_call(
        _dispatch_kernel,
        grid_spec=grid_spec,
        out_shape=jax.ShapeDtypeStruct((P, D_MODEL), F32),
        compiler_params=pltpu.CompilerParams(dimension_semantics=("arbitrary",), vmem_limit_bytes=VMEM_LIMIT),
        name="dispatch",
    )(ends, padded, dest3, hf)


def _moe_kernel(te_ref, nu_ref, xs_ref, wgu_ref, bgu_ref, wd_ref, bd_ref, ys_ref, wgu_s, wd_s):
    j = pl.program_id(0)
    used = j < nu_ref[0]
    new_expert = (j == 0) | (te_ref[j] != te_ref[jnp.maximum(j - 1, 0)])

    @pl.when(used & new_expert)
    def _cast_weights():
        wgu_s[...] = wgu_ref[0].astype(BF16)
        wd_s[...] = wd_ref[0].astype(BF16)

    @pl.when(used)
    def _compute():
        xb = xs_ref[...].astype(BF16)
        gu = _dot(xb, wgu_s[...]) + bgu_ref[0]
        gate = jnp.minimum(gu[:, :D_FF], SWIGLU_LIMIT)
        up = jnp.clip(gu[:, D_FF:], -SWIGLU_LIMIT, SWIGLU_LIMIT)
        act = (up + 1.0) * (gate * _sigmoid(SWIGLU_ALPHA * gate))
        ys_ref[...] = _dot(act.astype(BF16), wd_s[...]) + bd_ref[0]

    @pl.when(j >= nu_ref[0])
    def _unused():
        ys_ref[...] = jnp.zeros_like(ys_ref)


def _moe(tile_expert, n_used, xs, wgu, bgu, wd, bd):
    P = xs.shape[0]
    tm = TM_E
    grid_spec = pltpu.PrefetchScalarGridSpec(
        num_scalar_prefetch=2,
        grid=(P // tm,),
        in_specs=[pl.BlockSpec((tm, D_MODEL), lambda j, te, nu: (j, 0)),
                  pl.BlockSpec((1, D_MODEL, 2 * D_FF), lambda j, te, nu: (te[j], 0, 0)),
                  pl.BlockSpec((1, 1, 2 * D_FF), lambda j, te, nu: (te[j], 0, 0)),
                  pl.BlockSpec((1, D_FF, D_MODEL), lambda j, te, nu: (te[j], 0, 0)),
                  pl.BlockSpec((1, 1, D_MODEL), lambda j, te, nu: (te[j], 0, 0))],
        out_specs=pl.BlockSpec((tm, D_MODEL), lambda j, te, nu: (j, 0)),
        scratch_shapes=[pltpu.VMEM((D_MODEL, 2 * D_FF), BF16), pltpu.VMEM((D_FF, D_MODEL), BF16)],
    )
    return pl.pallas_call(
        _moe_kernel,
        grid_spec=grid_spec,
        out_shape=jax.ShapeDtypeStruct((P, D_MODEL), F32),
        compiler_params=pltpu.CompilerParams(dimension_semantics=("arbitrary",), vmem_limit_bytes=VMEM_LIMIT),
        name="moe",
    )(tile_expert, n_used, xs, wgu, bgu, wd, bd)


def _final_kernel(idx_ref, nxt_ref, x1_ref, route_ref, g_ref, ys_ref, o_ref, buf, sems):
    i = pl.program_id(0)
    n = pl.num_programs(0)
    tm = x1_ref.shape[0]

    def row_copy(src_row, slot, k, r):
        return pltpu.make_async_copy(ys_ref.at[pl.ds(src_row, 1)], buf.at[slot, k, pl.ds(r, 1)], sems.at[slot])

    def fetch(ref, slot):
        def issue(r, c):
            for k in range(TOP_K):
                row_copy(ref[0, 0, k * tm + r], slot, k, r).start()
            return c
        lax.fori_loop(0, tm, issue, 0)

    @pl.when(i == 0)
    def _first():
        fetch(idx_ref, 0)

    @pl.when(i + 1 < n)
    def _prefetch():
        fetch(nxt_ref, (i + 1) % 2)

    slot = i % 2
    for k in range(TOP_K):
        pltpu.make_async_copy(ys_ref.at[pl.ds(0, tm)], buf.at[slot, k], sems.at[slot]).wait()

    route_t = jnp.concatenate([route_ref[...], jnp.zeros((LANES - ROUTE_ROWS, tm), F32)], axis=0).T
    acc = x1_ref[...]
    for k in range(TOP_K):
        acc = acc + route_t[:, ROUTE_W + k:ROUTE_W + k + 1] * buf[slot, k]
    o_ref[...] = _rms(acc, g_ref[...])


def _final(x1, ys, dest, route, g):
    T = x1.shape[0]
    tm = TM_FINAL
    n = T // tm
    idx3 = _tile_major(dest, tm)
    smem_spec = lambda f: pl.BlockSpec((1, 1, TOP_K * tm), f, memory_space=pltpu.SMEM)
    return pl.pallas_call(
        _final_kernel,
        grid=(n,),
        in_specs=[smem_spec(lambda i: (i, 0, 0)),
                  smem_spec(lambda i: (jnp.minimum(i + 1, n - 1), 0, 0)),
                  pl.BlockSpec((tm, D_MODEL), lambda i: (i, 0)),
                  pl.BlockSpec((ROUTE_ROWS, tm), lambda i: (0, i)),
                  pl.BlockSpec((1, D_MODEL), lambda i: (0, 0)),
                  pl.BlockSpec(memory_space=pl.ANY)],
        out_specs=pl.BlockSpec((tm, D_MODEL), lambda i: (i, 0)),
        out_shape=jax.ShapeDtypeStruct((T, D_MODEL), F32),
        scratch_shapes=[pltpu.VMEM((2, TOP_K, tm, D_MODEL), F32), pltpu.SemaphoreType.DMA((2,))],
        compiler_params=pltpu.CompilerParams(dimension_semantics=("arbitrary",), vmem_limit_bytes=VMEM_LIMIT),
        name="final",
    )(idx3, idx3, x1, route, g, ys)


def _rope_tables(S):
    half = HEAD_DIM // 2
    inv = ROPE_THETA ** (-jnp.arange(half, dtype=F32) / half)
    ang = jnp.arange(S, dtype=F32)[:, None] * inv[None, :]
    cos = jnp.tile(jnp.cos(ang), (1, LANES // half))
    sin = jnp.tile(jnp.concatenate([-jnp.sin(ang), jnp.sin(ang)], axis=1), (1, LANES // HEAD_DIM))
    return cos, sin


def _selection_constants(S):
    nc = (S - CMP_BLOCK) // CMP_STRIDE + 1
    n_sel = S // SEL_BLOCK
    j = jnp.arange(n_sel)[:, None]
    i = jnp.arange(N_CMP_PAD)[None, :]
    overlap_t = ((i * CMP_STRIDE <= j * SEL_BLOCK + SEL_BLOCK - 1)
                 & (i * CMP_STRIDE + CMP_BLOCK - 1 >= j * SEL_BLOCK) & (i < nc)).astype(BF16)
    row = jnp.arange(LANES)[:, None]
    key = jnp.arange(S)[None, :]
    drop_bias = jnp.where(key // SEL_BLOCK == row, NEG_INF, 0.0).astype(BF16)[:n_sel]
    drop_bias = drop_bias.T.reshape(S // CKS, CKS, n_sel)
    return overlap_t, drop_bias


def kernel(x, mem, norm_mix, norm_mem, w_in, cmp_pos, cmp_w1, cmp_w2, w_pool, pool_scale, w_mem_kv, w_up_nsa,
           w_up_pool, w_up_mem, w_out, norm_ffn, w_router, b_router, w_gate_up, b_gate_up, w_down, b_down,
           norm_final):
    B, S, D = x.shape
    T = B * S
    assert D == D_MODEL and S % CKS == 0 and S // SEL_BLOCK == 32 and T % TM_DISPATCH == 0
    l = 0
    x2 = x.reshape(T, D)

    w = w_in[l]
    o_gate = NSA_WIDTH + 6 * KV_WIDTH
    n_gate = 3 * NSA_HEADS
    o_pool = o_gate + n_gate
    o_qm = o_pool + POOL_WIDTH
    o_mg = o_qm + MEM_WIDTH
    w_a = jnp.concatenate([w[:, :o_gate], w[:, o_pool:o_mg], w[:, o_gate:o_pool],
                           jnp.zeros((D, GATE_PAD - n_gate), F32)], axis=1).astype(BF16)
    w_mg = w[:, o_mg:].astype(BF16)
    cos_t, sin_t = _rope_tables(S)
    overlap_t, drop_bias = _selection_constants(S)
    w1 = cmp_w1[l].reshape(2, CMP_FLAT, HEAD_DIM).astype(BF16)
    w2 = cmp_w2[l].astype(BF16)
    w2t = jnp.swapaxes(w2, 1, 2)
    pos = jnp.broadcast_to(cmp_pos[l].reshape(2, 1, CMP_FLAT), (2, 8, CMP_FLAT)).astype(BF16)
    wpool_bd = jnp.zeros((POOL_WIDTH, POOL_WIDTH), F32)
    for gi in range(len(POOL_WINDOWS)):
        wpool_bd = wpool_bd.at[gi * POOL_GROUP:(gi + 1) * POOL_GROUP, gi * POOL_GROUP:(gi + 1) * POOL_GROUP].set(w_pool[l, gi])
    wr = w_router[l].T
    wr_hi = wr.astype(BF16)
    wr_lo = (wr - wr_hi.astype(F32)).astype(BF16)
    br = b_router[l].reshape(N_EXPERTS, 1)
    tri = (jnp.arange(TM_MERGE)[:, None] < jnp.arange(TM_MERGE)[None, :]).astype(BF16)
    lower = (jnp.arange(N_EXPERTS)[None, :] < jnp.arange(N_EXPERTS)[:, None]).astype(BF16)

    km, vm = _memkv(mem, norm_mem[l].reshape(1, D), w_mem_kv[l].astype(BF16))
    qc, qr, kvcmp, ksel, vsel, kwin, vwin, pool_in, qm, gates = _inproj(
        x2, norm_mix[l].reshape(1, D), w_a, cos_t, sin_t, S)
    kvc = kvcmp.reshape(B, S // CMP_STRIDE, CMP_STRIDE, 2 * NSA_GROUPS, HEAD_DIM).transpose(0, 3, 1, 2, 4)
    kvc = kvc.reshape(B, 2 * NSA_GROUPS, S // CMP_STRIDE, CMP_HALF)
    to_slabs = lambda v: jnp.swapaxes(v.reshape(NSA_GROUPS, B, S // KV_LANE_CHUNK, KV_LANE_CHUNK, HEAD_DIM), 3, 4)
    o_nsa = _nsa(qc, qr, kvc, ksel, to_slabs(vsel), kwin, to_slabs(vwin), gates, w1, w2, w2t, pos, overlap_t,
                 drop_bias, B, S)
    consts = [norm_mix[l].reshape(1, D), w_mg, wpool_bd.astype(BF16), pool_scale[l].reshape(1, POOL_WIDTH),
              w_up_nsa[l].astype(BF16), w_up_pool[l].astype(BF16), w_up_mem[l].astype(BF16), w_out[l].astype(BF16),
              norm_ffn[l].reshape(1, D), wr_hi, wr_lo, br, tri, lower]
    x1, hf, route, counts = _merge(x2, o_nsa, pool_in, qm, km, vm, consts, B, S)

    counts = counts[:, 0].astype(jnp.int32)
    padded = ((counts + TM_E - 1) // TM_E) * TM_E
    ends = jnp.cumsum(padded)
    starts = ends - padded
    n_tiles = (T * TOP_K) // TM_E + N_EXPERTS
    P = n_tiles * TM_E
    e_k = route[ROUTE_E:ROUTE_E + TOP_K].astype(jnp.int32)
    r_k = route[ROUTE_R:ROUTE_R + TOP_K].astype(jnp.int32)
    group_start = jnp.zeros_like(e_k)
    for e in range(N_EXPERTS):
        group_start = jnp.where(e_k == e, starts[e], group_start)
    dest = group_start + r_k
    tile_start = jnp.arange(n_tiles, dtype=jnp.int32) * TM_E
    tile_expert = jnp.minimum(jnp.sum(tile_start[:, None] >= ends[None, :], axis=1), N_EXPERTS - 1).astype(jnp.int32)
    n_used = (ends[-1] // TM_E).astype(jnp.int32).reshape(1)

    xs = _dispatch(dest, ends, padded, hf, P)
    ys = _moe(tile_expert, n_used, xs, w_gate_up[l], b_gate_up[l].reshape(N_EXPERTS, 1, 2 * D_FF),
              w_down[l], b_down[l].reshape(N_EXPERTS, 1, D_MODEL))
    out = _final(x1, ys, dest, route, norm_final.reshape(1, D))
    return out.reshape(B, S, D)
```

```python
import functools

import jax
import jax.numpy as jnp
from jax import lax
from jax.experimental import pallas as pl
from jax.experimental.pallas import tpu as pltpu
from jax.experimental.pallas import tpu_sc as plsc

F32 = jnp.float32
BF16 = jnp.bfloat16

D_MODEL = 1024
HEAD_DIM = 64
NSA_HEADS = 8
NSA_GROUPS = 2
HEADS_PER_GROUP = NSA_HEADS // NSA_GROUPS
NSA_WIDTH = NSA_HEADS * HEAD_DIM
KV_WIDTH = NSA_GROUPS * HEAD_DIM
CMP_BLOCK = 32
CMP_STRIDE = 16
SEL_BLOCK = 64
SEL_TOPN = 8
FORCE_BONUS = 1000.0
WINDOW = 512
POOL_WINDOWS = (2, 4, 8, 16)
POOL_GROUP = 64
POOL_WIDTH = POOL_GROUP * len(POOL_WINDOWS)
POOL_HALO = 16
MEM_HEADS = 4
MEM_WIDTH = MEM_HEADS * HEAD_DIM
N_EXPERTS = 32
TOP_K = 4
D_FF = 1024
SWIGLU_LIMIT = 7.0
SWIGLU_ALPHA = 1.702
ROPE_THETA = 10000.0
EPS = 1e-5
NEG_INF = -1e30
TINY = 1e-30
QK_SCALE = HEAD_DIM ** -0.5
LOG2_E = 1.4426950408889634
NSA_Q_SCALE = QK_SCALE * LOG2_E

LANES = 128
GATE_PAD = LANES

TM_IN = 512
TQ = 256
CKS = 512
WIN_CHUNKS = (256, 256, 256)
TM_MERGE = 512
TM_E = 512
TM_DISPATCH = 512
TM_FINAL = 512
SC_CORES = 2
SC_SUBCORES = 16
SC_ROWS = 32
VMEM_LIMIT = 56 * 1024 * 1024


def _rms(x, g):
    return x * lax.rsqrt(jnp.mean(x * x, axis=-1, keepdims=True) + EPS) * g


def _sigmoid(x):
    return 0.5 * jnp.tanh(0.5 * x) + 0.5


def _dot(a, b):
    return jnp.dot(a, b, preferred_element_type=F32)


def _dot_nt(a, b):
    return lax.dot_general(a, b, (((1,), (1,)), ((), ())), preferred_element_type=F32)


def _memkv_kernel(mem_ref, g_ref, w_ref, k_ref, v_ref):
    m = _rms(mem_ref[0], g_ref[...]).astype(BF16)
    kv = _dot(m, w_ref[...])
    for h in range(MEM_HEADS):
        k_ref[0, h] = kv[:, h * HEAD_DIM:(h + 1) * HEAD_DIM].astype(BF16)
        v_ref[0, h] = kv[:, MEM_WIDTH + h * HEAD_DIM:MEM_WIDTH + (h + 1) * HEAD_DIM].astype(BF16)


def _memkv(mem, g, w):
    B, M, D = mem.shape
    return pl.pallas_call(
        _memkv_kernel,
        grid=(B,),
        in_specs=[pl.BlockSpec((1, M, D), lambda b: (b, 0, 0)),
                  pl.BlockSpec((1, D), lambda b: (0, 0)),
                  pl.BlockSpec((D, 2 * MEM_WIDTH), lambda b: (0, 0))],
        out_specs=[pl.BlockSpec((1, MEM_HEADS, M, HEAD_DIM), lambda b: (b, 0, 0, 0)),
                   pl.BlockSpec((1, MEM_HEADS, M, HEAD_DIM), lambda b: (b, 0, 0, 0))],
        out_shape=[jax.ShapeDtypeStruct((B, MEM_HEADS, M, HEAD_DIM), BF16),
                   jax.ShapeDtypeStruct((B, MEM_HEADS, M, HEAD_DIM), BF16)],
        compiler_params=pltpu.CompilerParams(dimension_semantics=("arbitrary",), vmem_limit_bytes=VMEM_LIMIT),
        name="memkv",
    )(mem, g, w)


IN_COLS = NSA_WIDTH + 6 * KV_WIDTH + POOL_WIDTH + MEM_WIDTH + GATE_PAD


def _inproj_kernel(x_ref, g_ref, w_ref, cos_ref, sin_ref,
                   qc_ref, qr_ref, kvc_ref, ksel_ref, vsel_ref, kwin_ref, vwin_ref, pool_ref, qm_ref, gate_ref):
    h = _rms(x_ref[...], g_ref[...]).astype(BF16)
    p = _dot(h, w_ref[...])
    cos = cos_ref[...]
    sin = sin_ref[...]
    lane = lax.broadcasted_iota(jnp.int32, cos.shape, 1)
    first_half = (lane % HEAD_DIM) < (HEAD_DIM // 2)

    def rope(c):
        partner = jnp.where(first_half, pltpu.roll(c, LANES - HEAD_DIM // 2, 1), pltpu.roll(c, HEAD_DIM // 2, 1))
        return c * cos + partner * sin

    def halves(c):
        return c[:, :HEAD_DIM], c[:, HEAD_DIM:]

    for j in range(NSA_WIDTH // LANES):
        c = p[:, j * LANES:(j + 1) * LANES]
        r = rope(c)
        for hh, (cc, rr) in enumerate(zip(halves(c), halves(r))):
            qc_ref[2 * j + hh] = (cc * NSA_Q_SCALE).astype(BF16)
            qr_ref[2 * j + hh] = (rr * NSA_Q_SCALE).astype(BF16)
    o = NSA_WIDTH
    kvc_ref[...] = p[:, o:o + 2 * KV_WIDTH]
    o += 2 * KV_WIDTH
    for ref, rot in ((ksel_ref, True), (vsel_ref, False), (kwin_ref, True), (vwin_ref, False)):
        c = p[:, o:o + KV_WIDTH]
        if rot:
            c = rope(c)
        for g, cc in enumerate(halves(c)):
            ref[g] = cc.astype(BF16)
        o += KV_WIDTH
    pool_ref[...] = p[:, o:o + POOL_WIDTH]
    o += POOL_WIDTH
    for hh in range(MEM_HEADS):
        qm_ref[hh] = (p[:, o + hh * HEAD_DIM:o + (hh + 1) * HEAD_DIM] * QK_SCALE).astype(BF16)
    o += MEM_WIDTH
    gate_ref[...] = _sigmoid(p[:, o:o + GATE_PAD])


def _inproj(x2, g, w_a, cos_t, sin_t, S):
    T = x2.shape[0]
    tm = TM_IN
    n_s = S // tm
    head_spec = lambda n: pl.BlockSpec((n, tm, HEAD_DIM), lambda i: (0, i, 0))
    row_spec = lambda w: pl.BlockSpec((tm, w), lambda i: (i, 0))
    return pl.pallas_call(
        _inproj_kernel,
        grid=(T // tm,),
        in_specs=[row_spec(D_MODEL),
                  pl.BlockSpec((1, D_MODEL), lambda i: (0, 0)),
                  pl.BlockSpec((D_MODEL, IN_COLS), lambda i: (0, 0)),
                  pl.BlockSpec((tm, LANES), lambda i: (i % n_s, 0)),
                  pl.BlockSpec((tm, LANES), lambda i: (i % n_s, 0))],
        out_specs=[head_spec(NSA_HEADS), head_spec(NSA_HEADS), row_spec(2 * KV_WIDTH),
                   head_spec(NSA_GROUPS), head_spec(NSA_GROUPS), head_spec(NSA_GROUPS), head_spec(NSA_GROUPS),
                   row_spec(POOL_WIDTH), head_spec(MEM_HEADS), row_spec(GATE_PAD)],
        out_shape=[jax.ShapeDtypeStruct((NSA_HEADS, T, HEAD_DIM), BF16),
                   jax.ShapeDtypeStruct((NSA_HEADS, T, HEAD_DIM), BF16),
                   jax.ShapeDtypeStruct((T, 2 * KV_WIDTH), F32),
                   jax.ShapeDtypeStruct((NSA_GROUPS, T, HEAD_DIM), BF16),
                   jax.ShapeDtypeStruct((NSA_GROUPS, T, HEAD_DIM), BF16),
                   jax.ShapeDtypeStruct((NSA_GROUPS, T, HEAD_DIM), BF16),
                   jax.ShapeDtypeStruct((NSA_GROUPS, T, HEAD_DIM), BF16),
                   jax.ShapeDtypeStruct((T, POOL_WIDTH), F32),
                   jax.ShapeDtypeStruct((MEM_HEADS, T, HEAD_DIM), BF16),
                   jax.ShapeDtypeStruct((T, GATE_PAD), F32)],
        compiler_params=pltpu.CompilerParams(dimension_semantics=("arbitrary",), vmem_limit_bytes=VMEM_LIMIT),
        name="inproj",
    )(x2, g, w_a, cos_t, sin_t)


N_CMP_PAD = 128
CMP_FLAT = CMP_BLOCK * HEAD_DIM
CMP_HALF = CMP_STRIDE * HEAD_DIM


KV_LANE_CHUNK = LANES


def _nsa_kernel(qc_ref, qr_ref, kvc_ref, ksel_ref, vsel_ref, kwin_ref, vwin_ref, gate_ref,
                w1_ref, w2_ref, w2t_ref, pos_ref, ovl_ref, drop_ref, o_ref, kc_s, vct_s, *, S):
    i = pl.program_id(1)
    tq = TQ
    hpg = HEADS_PER_GROUP
    n_sel = S // SEL_BLOCK
    hq = hpg * tq

    @pl.when(i == 0)
    def _compress():
        for kv in range(2):
            w1 = w1_ref[kv]
            posterm = _dot(pos_ref[kv], w1)[0:1]
            for g in range(NSA_GROUPS):
                a = kvc_ref[0, kv * NSA_GROUPS + g].astype(BF16)
                p1 = _dot(a, w1[:CMP_HALF])
                p2 = _dot(a, w1[CMP_HALF:])
                hid = p1 + pltpu.roll(p2, N_CMP_PAD - 1, 0) + posterm
                hid = (hid * _sigmoid(hid)).astype(BF16)
                if kv == 0:
                    kc_s[g] = _dot(hid, w2_ref[kv]).astype(BF16)
                else:
                    vct_s[g] = _dot_nt(w2t_ref[kv], hid).astype(BF16)

    q0 = i * tq
    t_lane = q0 + lax.broadcasted_iota(jnp.int32, (1, tq), 1)

    def key_pos(start, n):
        return start + lax.broadcasted_iota(jnp.int32, (n, 1), 0)

    cmp_valid = (key_pos(0, N_CMP_PAD) * CMP_STRIDE + CMP_BLOCK - 1) <= t_lane
    jrow = lax.broadcasted_iota(jnp.int32, (n_sel, tq), 0)
    cur = t_lane // SEL_BLOCK
    sel_valid = jrow * SEL_BLOCK <= t_lane
    forced = (jrow == 0) | (jrow == cur) | (jrow == cur - 1)

    cd = q0 // CKS
    causal_bias = jnp.where(key_pos(cd * CKS, CKS) <= t_lane, 0.0, NEG_INF)
    win_chunks = []
    hi = q0 + tq
    for n in WIN_CHUNKS:
        lo = hi - n
        start = pl.multiple_of(jnp.maximum(lo, 0), LANES)
        kp = key_pos(start, n)
        diff = t_lane - kp
        win_chunks.append((start, n, jnp.where((diff >= 0) & (diff < WINDOW) & (kp < hi), 0.0, NEG_INF)))
        hi = lo

    def values_t(ref, g, start, n):
        c0 = start // KV_LANE_CHUNK
        return jnp.concatenate([ref[g, 0, c0 + j] for j in range(n // KV_LANE_CHUNK)], axis=1)

    def attend(jobs):
        scores = [_dot_nt(k, q_all) for q_all, k, _, _ in jobs]
        heads = [slice(hh * tq, (hh + 1) * tq) for hh in range(hpg)]
        maxes = [jnp.concatenate([jnp.max(s_all[:, sl] + bias, axis=0, keepdims=True) for sl in heads], axis=1)
                 for (_, _, _, bias), s_all in zip(jobs, scores)]
        soft = []
        for (_, _, _, bias), s_all, m in zip(jobs, scores, maxes):
            ps = [jnp.exp2(s_all[:, sl] + bias - m[:, sl]) for sl in heads]
            l = jnp.concatenate([jnp.sum(p, axis=0, keepdims=True) for p in ps], axis=1)
            soft.append((l, jnp.concatenate([p.astype(BF16) for p in ps], axis=1)))
        return [(m, l, _dot(v_t, p)) for (_, _, v_t, _), m, (l, p) in zip(jobs, maxes, soft)]

    def merge(a, b):
        m = jnp.maximum(a[0], b[0])
        wa = jnp.exp2(a[0] - m)
        wb = jnp.exp2(b[0] - m)
        return m, wa * a[1] + wb * b[1], wa * a[2] + wb * b[2]

    def select_blocks(g):
        q_cmp = qc_ref[g * hpg:(g + 1) * hpg].reshape(hq, HEAD_DIM)
        s_all = _dot_nt(kc_s[g], q_cmp)
        p_grp = jnp.zeros((N_CMP_PAD, tq), F32)
        ps = []
        for hh in range(hpg):
            sl = slice(hh * tq, (hh + 1) * tq)
            s = jnp.where(cmp_valid, s_all[:, sl], NEG_INF)
            m = jnp.max(s, axis=0, keepdims=True)
            e = jnp.where(cmp_valid, jnp.exp2(s - m), 0.0)
            p = e * (1.0 / jnp.maximum(jnp.sum(e, axis=0, keepdims=True), TINY))
            p_grp = p_grp + p
            ps.append(p.astype(BF16))
        o_cmp = _dot(vct_s[g], jnp.concatenate(ps, axis=1))

        ovl = ovl_ref[...]
        p_hi = p_grp.astype(BF16)
        r1 = p_grp - p_hi.astype(F32)
        p_mid = r1.astype(BF16)
        p_lo = (r1 - p_mid.astype(F32)).astype(BF16)
        score = _dot(ovl, p_hi) + _dot(ovl, p_mid) + _dot(ovl, p_lo)
        score = jnp.where(sel_valid, score + jnp.where(forced, FORCE_BONUS, 0.0), -1.0)
        rank = jnp.zeros((n_sel, tq), F32)
        for jp in range(n_sel):
            other = score[jp:jp + 1, :]
            beats = (other > score) | ((other == score) & (jrow > jp))
            rank = rank + beats.astype(F32)
        return o_cmp, (rank >= SEL_TOPN).astype(BF16)

    groups = range(NSA_GROUPS)
    cmp_out = [select_blocks(g) for g in groups]
    q_rot = [qr_ref[g * hpg:(g + 1) * hpg].reshape(hq, HEAD_DIM) for g in groups]

    def selected_job(g, c, extra_bias):
        k0 = pl.multiple_of(c * CKS, CKS)
        bias = _dot(drop_ref[c], cmp_out[g][1])
        if extra_bias is not None:
            bias = bias + extra_bias
        return q_rot[g], ksel_ref[g, pl.ds(k0, CKS), :], values_t(vsel_ref, g, k0, CKS), bias

    def window_job(g, chunk):
        start, n, bias = chunk
        return q_rot[g], kwin_ref[g, pl.ds(start, n), :], values_t(vwin_ref, g, start, n), bias

    stats = attend([selected_job(g, cd, causal_bias) for g in groups]
                   + [window_job(g, chunk) for chunk in win_chunks for g in groups])
    sel_state = tuple(stats[:NSA_GROUPS])
    win_state = list(stats[NSA_GROUPS:2 * NSA_GROUPS])
    for j in range(1, len(win_chunks)):
        for g in groups:
            win_state[g] = merge(win_state[g], stats[NSA_GROUPS * (1 + j) + g])

    def sel_body(it, states):
        new = attend([selected_job(g, cd - 1 - it, None) for g in groups])
        return tuple(merge(states[g], new[g]) for g in groups)

    sel_state = lax.fori_loop(0, cd, sel_body, sel_state)

    gates_t = gate_ref[...].T
    out_rows = []
    for g in groups:
        o_cmp = cmp_out[g][0]
        o_sel = sel_state[g][2] * (1.0 / sel_state[g][1])
        o_win = win_state[g][2] * (1.0 / win_state[g][1])
        for hh in range(hpg):
            h = g * hpg + hh
            sl = slice(hh * tq, (hh + 1) * tq)
            out_rows.append(gates_t[3 * h:3 * h + 1] * o_cmp[:, sl] + gates_t[3 * h + 1:3 * h + 2] * o_sel[:, sl]
                            + gates_t[3 * h + 2:3 * h + 3] * o_win[:, sl])
    o_ref[...] = jnp.concatenate(out_rows, axis=0).T.astype(BF16)


def _nsa(qc, qr, kvc, ksel, vsel_t, kwin, vwin_t, gates, w1, w2, w2t, pos, ovl, drop_bias, B, S):
    T = B * S
    tq = TQ
    nq = S // tq
    q_spec = pl.BlockSpec((NSA_HEADS, tq, HEAD_DIM), lambda b, i: (0, b * nq + i, 0))
    k_spec = pl.BlockSpec((NSA_GROUPS, S, HEAD_DIM), lambda b, i: (0, b, 0))
    v_spec = pl.BlockSpec((NSA_GROUPS, 1, S // KV_LANE_CHUNK, HEAD_DIM, KV_LANE_CHUNK), lambda b, i: (0, b, 0, 0, 0))
    full = lambda a: pl.BlockSpec(a.shape, lambda b, i: (0,) * a.ndim)
    return pl.pallas_call(
        functools.partial(_nsa_kernel, S=S),
        grid=(B, nq),
        in_specs=[q_spec, q_spec,
                  pl.BlockSpec((1, 2 * NSA_GROUPS, N_CMP_PAD, CMP_HALF), lambda b, i: (b, 0, 0, 0)),
                  k_spec, v_spec, k_spec, v_spec,
                  pl.BlockSpec((tq, GATE_PAD), lambda b, i: (b * nq + i, 0)),
                  full(w1), full(w2), full(w2t), full(pos), full(ovl), full(drop_bias)],
        out_specs=pl.BlockSpec((tq, NSA_WIDTH), lambda b, i: (b * nq + i, 0)),
        out_shape=jax.ShapeDtypeStruct((T, NSA_WIDTH), BF16),
        scratch_shapes=[pltpu.VMEM((NSA_GROUPS, N_CMP_PAD, HEAD_DIM), BF16),
                        pltpu.VMEM((NSA_GROUPS, HEAD_DIM, N_CMP_PAD), BF16)],
        compiler_params=pltpu.CompilerParams(dimension_semantics=("arbitrary", "arbitrary"),
                                             vmem_limit_bytes=VMEM_LIMIT),
        name="nsa",
    )(qc, qr, kvc, ksel, vsel_t, kwin, vwin_t, gates, w1, w2, w2t, pos, ovl, drop_bias)


ROUTE_E, ROUTE_R, ROUTE_W = 0, TOP_K, 2 * TOP_K
ROUTE_ROWS = 16


def _merge_kernel(x_ref, onsa_ref, pool_ref, prev_ref, qm_ref, km_ref, vm_ref,
                  gmix_ref, wmg_ref, wpool_ref, pscale_ref, wun_ref, wup_ref, wum_ref, wout_ref,
                  gffn_ref, wrh_ref, wrl_ref, br_ref, tri_ref, lower_ref,
                  x1_ref, hf_ref, route_ref, cnt_ref, omem_s, carry_s, *, S):
    i = pl.program_id(0)
    tm = TM_MERGE
    n_s = S // tm

    @pl.when(i == 0)
    def _init():
        carry_s[...] = jnp.zeros_like(carry_s)

    x = x_ref[...]
    h = _rms(x, gmix_ref[...]).astype(BF16)
    mg = _sigmoid(_dot(h, wmg_ref[...]))

    u = pool_ref[...]
    seq_tile = i % n_s
    prev = jnp.where(seq_tile == 0, 0.0, prev_ref[...])
    ext = jnp.concatenate([prev, u], axis=0)
    b2 = ext[1:] + ext[:-1]
    b4 = b2[2:] + b2[:-2]
    b8 = b4[4:] + b4[:-4]
    b16 = b8[8:] + b8[:-8]
    sums = (b2[POOL_HALO - 1:POOL_HALO - 1 + tm], b4[POOL_HALO - 3:POOL_HALO - 3 + tm],
            b8[POOL_HALO - 7:POOL_HALO - 7 + tm], b16[POOL_HALO - 15:POOL_HALO - 15 + tm])
    t_seq = seq_tile * tm + lax.broadcasted_iota(jnp.int32, (tm, 1), 0)
    lane_p = lax.broadcasted_iota(jnp.int32, (tm, POOL_WIDTH), 1)
    z = jnp.zeros((tm, POOL_WIDTH), F32)
    for gi, w in enumerate(POOL_WINDOWS):
        cnt = jnp.minimum(t_seq + 1, w).astype(F32)
        z = jnp.where(lane_p // POOL_GROUP == gi, sums[gi] / cnt, z)
    z = z - u
    o_pool = (_dot(z.astype(BF16), wpool_ref[...]) * pscale_ref[...]).astype(BF16)

    for hh in range(MEM_HEADS):
        s = _dot_nt(qm_ref[hh], km_ref[0, hh])
        m = jnp.max(s, axis=-1, keepdims=True)
        e = jnp.exp(s - m)
        p = e / jnp.sum(e, axis=-1, keepdims=True)
        omem_s[:, hh * HEAD_DIM:(hh + 1) * HEAD_DIM] = _dot(p.astype(BF16), vm_ref[0, hh]).astype(BF16)

    merged = (mg[:, :D_MODEL] * _dot(onsa_ref[...], wun_ref[...])
              + mg[:, D_MODEL:2 * D_MODEL] * _dot(o_pool, wup_ref[...])
              + mg[:, 2 * D_MODEL:] * _dot(omem_s[...], wum_ref[...]))
    x1 = x + _dot(merged.astype(BF16), wout_ref[...])
    x1_ref[...] = x1
    hf = _rms(x1, gffn_ref[...])
    hf_ref[...] = hf

    hf_hi = hf.astype(BF16)
    hf_lo = (hf - hf_hi.astype(F32)).astype(BF16)
    logits = (_dot_nt(wrh_ref[...], hf_hi) + _dot_nt(wrl_ref[...], hf_hi) + _dot_nt(wrh_ref[...], hf_lo)
              + br_ref[...])
    erow = lax.broadcasted_iota(jnp.int32, (N_EXPERTS, tm), 0)
    rank = jnp.zeros((N_EXPERTS, tm), F32)
    for jp in range(N_EXPERTS):
        other = logits[jp:jp + 1, :]
        beats = (other > logits) | ((other == logits) & (erow > jp))
        rank = rank + beats.astype(F32)
    chosen = rank < TOP_K
    m = jnp.max(logits, axis=0, keepdims=True)
    e = jnp.where(chosen, jnp.exp(logits - m), 0.0)
    comb = e * (1.0 / jnp.sum(e, axis=0, keepdims=True))

    chosen_b = chosen.astype(BF16)
    carry = carry_s[:, 0:1]
    in_expert = _dot(chosen_b, tri_ref[...]) + carry
    carry_new = carry + jnp.sum(chosen.astype(F32), axis=1, keepdims=True)
    carry_s[...] = jnp.broadcast_to(carry_new, carry_s.shape)
    cnt_ref[...] = jnp.broadcast_to(carry_new, cnt_ref.shape)

    before = _dot(lower_ref[...], chosen_b)
    erow_f = erow.astype(F32)
    fields = {ROUTE_E: erow_f, ROUTE_R: in_expert, ROUTE_W: comb}
    rows = [None] * ROUTE_ROWS
    for k in range(TOP_K):
        pick = chosen & (before == k)
        for base, val in fields.items():
            rows[base + k] = jnp.sum(jnp.where(pick, val, 0.0), axis=0, keepdims=True)
    zero_row = jnp.zeros((1, tm), F32)
    route_ref[...] = jnp.concatenate([zero_row if r is None else r for r in rows], axis=0)


def _merge(x2, onsa, pool_in, qm, km, vm, consts, B, S):
    T = B * S
    tm = TM_MERGE
    n_s = S // tm
    M = km.shape[2]
    halo_per_tile = tm // POOL_HALO
    row = lambda w: pl.BlockSpec((tm, w), lambda i: (i, 0))
    full = lambda a: pl.BlockSpec(a.shape, lambda i: (0,) * a.ndim)
    mem_spec = pl.BlockSpec((1, MEM_HEADS, M, HEAD_DIM), lambda i: (i // n_s, 0, 0, 0))
    return pl.pallas_call(
        functools.partial(_merge_kernel, S=S),
        grid=(T // tm,),
        in_specs=[row(D_MODEL), row(NSA_WIDTH), row(POOL_WIDTH),
                  pl.BlockSpec((POOL_HALO, POOL_WIDTH), lambda i: (jnp.maximum(i * halo_per_tile - 1, 0), 0)),
                  pl.BlockSpec((MEM_HEADS, tm, HEAD_DIM), lambda i: (0, i, 0)),
                  mem_spec, mem_spec] + [full(c) for c in consts],
        out_specs=[row(D_MODEL), row(D_MODEL), pl.BlockSpec((ROUTE_ROWS, tm), lambda i: (0, i)),
                   pl.BlockSpec((N_EXPERTS, LANES), lambda i: (0, 0))],
        out_shape=[jax.ShapeDtypeStruct((T, D_MODEL), F32),
                   jax.ShapeDtypeStruct((T, D_MODEL), F32),
                   jax.ShapeDtypeStruct((ROUTE_ROWS, T), F32),
                   jax.ShapeDtypeStruct((N_EXPERTS, LANES), F32)],
        scratch_shapes=[pltpu.VMEM((tm, MEM_WIDTH), BF16), pltpu.VMEM((N_EXPERTS, LANES), F32)],
        compiler_params=pltpu.CompilerParams(dimension_semantics=("arbitrary",), vmem_limit_bytes=VMEM_LIMIT),
        name="merge",
    )(x2, onsa, pool_in, pool_in, qm, km, vm, *consts)


def _dispatch_kernel(ends_ref, padded_ref, dest_ref, h_ref, xs_ref, zero_s, sem, zsem):
    i = pl.program_id(0)
    tm = h_ref.shape[0]

    @pl.when(i == 0)
    def _clear_pad_tiles():
        zero_s[...] = jnp.zeros_like(zero_s)

        def tile_copy(e):
            start = pl.multiple_of(ends_ref[e] - TM_E, TM_E)
            return pltpu.make_async_copy(zero_s, xs_ref.at[pl.ds(start, TM_E)], zsem)

        for e in range(N_EXPERTS):
            @pl.when(padded_ref[e] > 0)
            def _():
                tile_copy(e).start()
        for e in range(N_EXPERTS):
            @pl.when(padded_ref[e] > 0)
            def _():
                tile_copy(e).wait()

        def tail_copy(j):
            return pltpu.make_async_copy(zero_s, xs_ref.at[pl.ds(pl.multiple_of(j * TM_E, TM_E), TM_E)], zsem)

        def tail_start(j, c):
            tail_copy(j).start()
            return c

        def tail_wait(j, c):
            tail_copy(j).wait()
            return c

        n_used = ends_ref[N_EXPERTS - 1] // TM_E
        lax.fori_loop(n_used, xs_ref.shape[0] // TM_E, tail_start, 0)
        lax.fori_loop(n_used, xs_ref.shape[0] // TM_E, tail_wait, 0)

    def issue(r, c):
        for k in range(TOP_K):
            d = dest_ref[0, 0, k * tm + r]
            pltpu.make_async_copy(h_ref.at[pl.ds(r, 1)], xs_ref.at[pl.ds(d, 1)], sem).start()
        return c

    lax.fori_loop(0, tm, issue, 0)
    for k in range(TOP_K):
        pltpu.make_async_copy(h_ref, xs_ref.at[pl.ds(0, tm)], sem).wait()


def _tile_major(dest, tm):
    T = dest.shape[1]
    return dest.reshape(TOP_K, T // tm, tm).transpose(1, 0, 2).reshape(T // tm, 1, TOP_K * tm)


def _dispatch(dest, ends, padded, hf, P):
    T = hf.shape[0]
    tm = TM_DISPATCH
    dest3 = _tile_major(dest, tm)
    grid_spec = pltpu.PrefetchScalarGridSpec(
        num_scalar_prefetch=2,
        grid=(T // tm,),
        in_specs=[pl.BlockSpec((1, 1, tm * TOP_K), lambda i, en, pd: (i, 0, 0), memory_space=pltpu.SMEM),
                  pl.BlockSpec((tm, D_MODEL), lambda i, en, pd: (i, 0))],
        out_specs=pl.BlockSpec(memory_space=pl.ANY),
        scratch_shapes=[pltpu.VMEM((TM_E, D_MODEL), F32), pltpu.SemaphoreType.DMA, pltpu.SemaphoreType.DMA],
    )
    return pl.pallas_call(
        _dispatch_kernel,
        grid_spec=grid_spec,
        out_shape=jax.ShapeDtypeStruct((P, D_MODEL), F32),
        compiler_params=pltpu.CompilerParams(dimension_semantics=("arbitrary",), vmem_limit_bytes=VMEM_LIMIT),
        name="dispatch",
    )(ends, padded, dest3, hf)


def _moe_kernel(te_ref, nu_ref, xs_ref, wgu_ref, bgu_ref, wd_ref, bd_ref, ys_ref, wgu_s, wd_s):
    j = pl.program_id(0)
    used = j < nu_ref[0]
    new_expert = (j == 0) | (te_ref[j] != te_ref[jnp.maximum(j - 1, 0)])

    @pl.when(used & new_expert)
    def _cast_weights():
        wgu_s[...] = wgu_ref[0].astype(BF16)
        wd_s[...] = wd_ref[0].astype(BF16)

    @pl.when(used)
    def _compute():
        xb = xs_ref[...].astype(BF16)
        gu = _dot(xb, wgu_s[...]) + bgu_ref[0]
        gate = jnp.minimum(gu[:, :D_FF], SWIGLU_LIMIT)
        up = jnp.clip(gu[:, D_FF:], -SWIGLU_LIMIT, SWIGLU_LIMIT)
        act = (up + 1.0) * (gate * _sigmoid(SWIGLU_ALPHA * gate))
        ys_ref[...] = _dot(act.astype(BF16), wd_s[...]) + bd_ref[0]

    @pl.when(j >= nu_ref[0])
    def _unused():
        ys_ref[...] = jnp.zeros_like(ys_ref)


def _moe(tile_expert, n_used, xs, wgu, bgu, wd, bd):
    P = xs.shape[0]
    tm = TM_E
    grid_spec = pltpu.PrefetchScalarGridSpec(
        num_scalar_prefetch=2,
        grid=(P // tm,),
        in_specs=[pl.BlockSpec((tm, D_MODEL), lambda j, te, nu: (j, 0)),
                  pl.BlockSpec((1, D_MODEL, 2 * D_FF), lambda j, te, nu: (te[j], 0, 0)),
                  pl.BlockSpec((1, 1, 2 * D_FF), lambda j, te, nu: (te[j], 0, 0)),
                  pl.BlockSpec((1, D_FF, D_MODEL), lambda j, te, nu: (te[j], 0, 0)),
                  pl.BlockSpec((1, 1, D_MODEL), lambda j, te, nu: (te[j], 0, 0))],
        out_specs=pl.BlockSpec((tm, D_MODEL), lambda j, te, nu: (j, 0)),
        scratch_shapes=[pltpu.VMEM((D_MODEL, 2 * D_FF), BF16), pltpu.VMEM((D_FF, D_MODEL), BF16)],
    )
    return pl.pallas_call(
        _moe_kernel,
        grid_spec=grid_spec,
        out_shape=jax.ShapeDtypeStruct((P, D_MODEL), F32),
        compiler_params=pltpu.CompilerParams(dimension_semantics=("arbitrary",), vmem_limit_bytes=VMEM_LIMIT),
        name="moe",
    )(tile_expert, n_used, xs, wgu, bgu, wd, bd)


def _sc_gather_rows(table, idx):
    n, d = idx.shape[0], table.shape[1]
    per_worker = n // (SC_CORES * SC_SUBCORES)
    mesh = plsc.VectorSubcoreMesh(core_axis_name="c", subcore_axis_name="s")

    @functools.partial(pl.kernel, mesh=mesh, out_type=jax.ShapeDtypeStruct((n, d), table.dtype),
                       scratch_types=[pltpu.VMEM((SC_ROWS,), jnp.int32), pltpu.VMEM((SC_ROWS, d), table.dtype),
                                      pltpu.SemaphoreType.DMA])
    def gather(table_hbm, idx_hbm, out_hbm, idx_v, rows_v, sem):
        base = (lax.axis_index("s") * SC_CORES + lax.axis_index("c")) * per_worker

        @pl.loop(0, per_worker // SC_ROWS)
        def _(j):
            off = pl.multiple_of(base + j * SC_ROWS, SC_ROWS)
            pltpu.sync_copy(idx_hbm.at[pl.ds(off, SC_ROWS)], idx_v)
            pltpu.async_copy(table_hbm.at[idx_v], rows_v, sem).wait()
            pltpu.sync_copy(rows_v, out_hbm.at[pl.ds(off, SC_ROWS)])

    return gather(table, idx)


def _final_kernel(x1_ref, yg_ref, route_ref, g_ref, o_ref):
    tm = x1_ref.shape[0]
    route_t = jnp.concatenate([route_ref[...], jnp.zeros((LANES - ROUTE_ROWS, tm), F32)], axis=0).T
    acc = x1_ref[...]
    for k in range(TOP_K):
        acc = acc + route_t[:, ROUTE_W + k:ROUTE_W + k + 1] * yg_ref[k]
    o_ref[...] = _rms(acc, g_ref[...])


def _final(x1, ys, dest, route, g):
    T = x1.shape[0]
    tm = TM_FINAL
    yg = _sc_gather_rows(ys, dest.reshape(-1)).reshape(TOP_K, T, D_MODEL)
    return pl.pallas_call(
        _final_kernel,
        grid=(T // tm,),
        in_specs=[pl.BlockSpec((tm, D_MODEL), lambda i: (i, 0)),
                  pl.BlockSpec((TOP_K, tm, D_MODEL), lambda i: (0, i, 0)),
                  pl.BlockSpec((ROUTE_ROWS, tm), lambda i: (0, i)),
                  pl.BlockSpec((1, D_MODEL), lambda i: (0, 0))],
        out_specs=pl.BlockSpec((tm, D_MODEL), lambda i: (i, 0)),
        out_shape=jax.ShapeDtypeStruct((T, D_MODEL), F32),
        compiler_params=pltpu.CompilerParams(dimension_semantics=("arbitrary",), vmem_limit_bytes=VMEM_LIMIT),
        name="final",
    )(x1, yg, route, g)


def _rope_tables(S):
    half = HEAD_DIM // 2
    inv = ROPE_THETA ** (-jnp.arange(half, dtype=F32) / half)
    ang = jnp.arange(S, dtype=F32)[:, None] * inv[None, :]
    cos = jnp.tile(jnp.cos(ang), (1, LANES // half))
    sin = jnp.tile(jnp.concatenate([-jnp.sin(ang), jnp.sin(ang)], axis=1), (1, LANES // HEAD_DIM))
    return cos, sin


def _selection_constants(S):
    nc = (S - CMP_BLOCK) // CMP_STRIDE + 1
    n_sel = S // SEL_BLOCK
    j = jnp.arange(n_sel)[:, None]
    i = jnp.arange(N_CMP_PAD)[None, :]
    overlap_t = ((i * CMP_STRIDE <= j * SEL_BLOCK + SEL_BLOCK - 1)
                 & (i * CMP_STRIDE + CMP_BLOCK - 1 >= j * SEL_BLOCK) & (i < nc)).astype(BF16)
    row = jnp.arange(LANES)[:, None]
    key = jnp.arange(S)[None, :]
    drop_bias = jnp.where(key // SEL_BLOCK == row, NEG_INF, 0.0).astype(BF16)[:n_sel]
    drop_bias = drop_bias.T.reshape(S // CKS, CKS, n_sel)
    return overlap_t, drop_bias


def kernel(x, mem, norm_mix, norm_mem, w_in, cmp_pos, cmp_w1, cmp_w2, w_pool, pool_scale, w_mem_kv, w_up_nsa,
           w_up_pool, w_up_mem, w_out, norm_ffn, w_router, b_router, w_gate_up, b_gate_up, w_down, b_down,
           norm_final):
    B, S, D = x.shape
    T = B * S
    assert D == D_MODEL and S % CKS == 0 and S // SEL_BLOCK == 32 and T % TM_DISPATCH == 0
    l = 0
    x2 = x.reshape(T, D)

    w = w_in[l]
    o_gate = NSA_WIDTH + 6 * KV_WIDTH
    n_gate = 3 * NSA_HEADS
    o_pool = o_gate + n_gate
    o_qm = o_pool + POOL_WIDTH
    o_mg = o_qm + MEM_WIDTH
    w_a = jnp.concatenate([w[:, :o_gate], w[:, o_pool:o_mg], w[:, o_gate:o_pool],
                           jnp.zeros((D, GATE_PAD - n_gate), F32)], axis=1).astype(BF16)
    w_mg = w[:, o_mg:].astype(BF16)
    cos_t, sin_t = _rope_tables(S)
    overlap_t, drop_bias = _selection_constants(S)
    w1 = cmp_w1[l].reshape(2, CMP_FLAT, HEAD_DIM).astype(BF16)
    w2 = cmp_w2[l].astype(BF16)
    w2t = jnp.swapaxes(w2, 1, 2)
    pos = jnp.broadcast_to(cmp_pos[l].reshape(2, 1, CMP_FLAT), (2, 8, CMP_FLAT)).astype(BF16)
    wpool_bd = jnp.zeros((POOL_WIDTH, POOL_WIDTH), F32)
    for gi in range(len(POOL_WINDOWS)):
        wpool_bd = wpool_bd.at[gi * POOL_GROUP:(gi + 1) * POOL_GROUP, gi * POOL_GROUP:(gi + 1) * POOL_GROUP].set(w_pool[l, gi])
    wr = w_router[l].T
    wr_hi = wr.astype(BF16)
    wr_lo = (wr - wr_hi.astype(F32)).astype(BF16)
    br = b_router[l].reshape(N_EXPERTS, 1)
    tri = (jnp.arange(TM_MERGE)[:, None] < jnp.arange(TM_MERGE)[None, :]).astype(BF16)
    lower = (jnp.arange(N_EXPERTS)[None, :] < jnp.arange(N_EXPERTS)[:, None]).astype(BF16)

    km, vm = _memkv(mem, norm_mem[l].reshape(1, D), w_mem_kv[l].astype(BF16))
    qc, qr, kvcmp, ksel, vsel, kwin, vwin, pool_in, qm, gates = _inproj(
        x2, norm_mix[l].reshape(1, D), w_a, cos_t, sin_t, S)
    kvc = kvcmp.reshape(B, S // CMP_STRIDE, CMP_STRIDE, 2 * NSA_GROUPS, HEAD_DIM).transpose(0, 3, 1, 2, 4)
    kvc = kvc.reshape(B, 2 * NSA_GROUPS, S // CMP_STRIDE, CMP_HALF)
    to_slabs = lambda v: jnp.swapaxes(v.reshape(NSA_GROUPS, B, S // KV_LANE_CHUNK, KV_LANE_CHUNK, HEAD_DIM), 3, 4)
    o_nsa = _nsa(qc, qr, kvc, ksel, to_slabs(vsel), kwin, to_slabs(vwin), gates, w1, w2, w2t, pos, overlap_t,
                 drop_bias, B, S)
    consts = [norm_mix[l].reshape(1, D), w_mg, wpool_bd.astype(BF16), pool_scale[l].reshape(1, POOL_WIDTH),
              w_up_nsa[l].astype(BF16), w_up_pool[l].astype(BF16), w_up_mem[l].astype(BF16), w_out[l].astype(BF16),
              norm_ffn[l].reshape(1, D), wr_hi, wr_lo, br, tri, lower]
    x1, hf, route, counts = _merge(x2, o_nsa, pool_in, qm, km, vm, consts, B, S)

    counts = counts[:, 0].astype(jnp.int32)
    padded = ((counts + TM_E - 1) // TM_E) * TM_E
    ends = jnp.cumsum(padded)
    starts = ends - padded
    n_tiles = (T * TOP_K) // TM_E + N_EXPERTS
    P = n_tiles * TM_E
    e_k = route[ROUTE_E:ROUTE_E + TOP_K].astype(jnp.int32)
    r_k = route[ROUTE_R:ROUTE_R + TOP_K].astype(jnp.int32)
    group_start = jnp.zeros_like(e_k)
    for e in range(N_EXPERTS):
        group_start = jnp.where(e_k == e, starts[e], group_start)
    dest = group_start + r_k
    tile_start = jnp.arange(n_tiles, dtype=jnp.int32) * TM_E
    tile_expert = jnp.minimum(jnp.sum(tile_start[:, None] >= ends[None, :], axis=1), N_EXPERTS - 1).astype(jnp.int32)
    n_used = (ends[-1] // TM_E).astype(jnp.int32).reshape(1)

    xs = _dispatch(dest, ends, padded, hf, P)
    ys = _moe(tile_expert, n_used, xs, w_gate_up[l], b_gate_up[l].reshape(N_EXPERTS, 1, 2 * D_FF),
              w_down[l], b_down[l].reshape(N_EXPERTS, 1, D_MODEL))
    out = _final(x1, ys, dest, route, norm_final.reshape(1, D))
    return out.reshape(B, S, D)
```

```python
import functools

import jax
import jax.numpy as jnp
from jax import lax
from jax.experimental import pallas as pl
from jax.experimental.pallas import tpu as pltpu
from jax.experimental.pallas import tpu_sc as plsc

F32 = jnp.float32
BF16 = jnp.bfloat16

D_MODEL = 1024
HEAD_DIM = 64
NSA_HEADS = 8
NSA_GROUPS = 2
HEADS_PER_GROUP = NSA_HEADS // NSA_GROUPS
NSA_WIDTH = NSA_HEADS * HEAD_DIM
KV_WIDTH = NSA_GROUPS * HEAD_DIM
CMP_BLOCK = 32
CMP_STRIDE = 16
SEL_BLOCK = 64
SEL_TOPN = 8
FORCE_BONUS = 1000.0
WINDOW = 512
POOL_WINDOWS = (2, 4, 8, 16)
POOL_GROUP = 64
POOL_WIDTH = POOL_GROUP * len(POOL_WINDOWS)
POOL_HALO = 16
MEM_HEADS = 4
MEM_WIDTH = MEM_HEADS * HEAD_DIM
N_EXPERTS = 32
TOP_K = 4
D_FF = 1024
SWIGLU_LIMIT = 7.0
SWIGLU_ALPHA = 1.702
ROPE_THETA = 10000.0
EPS = 1e-5
NEG_INF = -1e30
TINY = 1e-30
QK_SCALE = HEAD_DIM ** -0.5
LOG2_E = 1.4426950408889634
NSA_Q_SCALE = QK_SCALE * LOG2_E

LANES = 128
GATE_PAD = LANES

TM_IN = 512
TQ = 256
CKS = 512
WIN_CHUNKS = (256, 256, 256)
TM_MERGE = 512
TM_E = 512
TM_FINAL = 512
SC_CORES = 2
SC_SUBCORES = 16
SC_WORKERS = SC_CORES * SC_SUBCORES
SC_ROWS = 32
VMEM_LIMIT = 56 * 1024 * 1024


def _rms(x, g):
    return x * lax.rsqrt(jnp.mean(x * x, axis=-1, keepdims=True) + EPS) * g


def _sigmoid(x):
    return 0.5 * jnp.tanh(0.5 * x) + 0.5


def _dot(a, b):
    return jnp.dot(a, b, preferred_element_type=F32)


def _dot_nt(a, b):
    return lax.dot_general(a, b, (((1,), (1,)), ((), ())), preferred_element_type=F32)


def _memkv_kernel(mem_ref, g_ref, w_ref, k_ref, v_ref):
    m = _rms(mem_ref[0], g_ref[...]).astype(BF16)
    kv = _dot(m, w_ref[...])
    for h in range(MEM_HEADS):
        k_ref[0, h] = kv[:, h * HEAD_DIM:(h + 1) * HEAD_DIM].astype(BF16)
        v_ref[0, h] = kv[:, MEM_WIDTH + h * HEAD_DIM:MEM_WIDTH + (h + 1) * HEAD_DIM].astype(BF16)


def _memkv(mem, g, w):
    B, M, D = mem.shape
    return pl.pallas_call(
        _memkv_kernel,
        grid=(B,),
        in_specs=[pl.BlockSpec((1, M, D), lambda b: (b, 0, 0)),
                  pl.BlockSpec((1, D), lambda b: (0, 0)),
                  pl.BlockSpec((D, 2 * MEM_WIDTH), lambda b: (0, 0))],
        out_specs=[pl.BlockSpec((1, MEM_HEADS, M, HEAD_DIM), lambda b: (b, 0, 0, 0)),
                   pl.BlockSpec((1, MEM_HEADS, M, HEAD_DIM), lambda b: (b, 0, 0, 0))],
        out_shape=[jax.ShapeDtypeStruct((B, MEM_HEADS, M, HEAD_DIM), BF16),
                   jax.ShapeDtypeStruct((B, MEM_HEADS, M, HEAD_DIM), BF16)],
        compiler_params=pltpu.CompilerParams(dimension_semantics=("arbitrary",), vmem_limit_bytes=VMEM_LIMIT),
        name="memkv",
    )(mem, g, w)


IN_COLS = NSA_WIDTH + 6 * KV_WIDTH + POOL_WIDTH + MEM_WIDTH + GATE_PAD


def _inproj_kernel(x_ref, g_ref, w_ref, cos_ref, sin_ref,
                   qc_ref, qr_ref, kvc_ref, ksel_ref, vsel_ref, kwin_ref, vwin_ref, pool_ref, qm_ref, gate_ref):
    h = _rms(x_ref[...], g_ref[...]).astype(BF16)
    p = _dot(h, w_ref[...])
    cos = cos_ref[...]
    sin = sin_ref[...]
    lane = lax.broadcasted_iota(jnp.int32, cos.shape, 1)
    first_half = (lane % HEAD_DIM) < (HEAD_DIM // 2)

    def rope(c):
        partner = jnp.where(first_half, pltpu.roll(c, LANES - HEAD_DIM // 2, 1), pltpu.roll(c, HEAD_DIM // 2, 1))
        return c * cos + partner * sin

    def halves(c):
        return c[:, :HEAD_DIM], c[:, HEAD_DIM:]

    for j in range(NSA_WIDTH // LANES):
        c = p[:, j * LANES:(j + 1) * LANES]
        r = rope(c)
        for hh, (cc, rr) in enumerate(zip(halves(c), halves(r))):
            qc_ref[2 * j + hh] = (cc * NSA_Q_SCALE).astype(BF16)
            qr_ref[2 * j + hh] = (rr * NSA_Q_SCALE).astype(BF16)
    o = NSA_WIDTH
    kvc_ref[...] = p[:, o:o + 2 * KV_WIDTH]
    o += 2 * KV_WIDTH
    for ref, rot in ((ksel_ref, True), (vsel_ref, False), (kwin_ref, True), (vwin_ref, False)):
        c = p[:, o:o + KV_WIDTH]
        if rot:
            c = rope(c)
        for g, cc in enumerate(halves(c)):
            ref[g] = cc.astype(BF16)
        o += KV_WIDTH
    pool_ref[...] = p[:, o:o + POOL_WIDTH]
    o += POOL_WIDTH
    for hh in range(MEM_HEADS):
        qm_ref[hh] = (p[:, o + hh * HEAD_DIM:o + (hh + 1) * HEAD_DIM] * QK_SCALE).astype(BF16)
    o += MEM_WIDTH
    gate_ref[...] = _sigmoid(p[:, o:o + GATE_PAD])


def _inproj(x2, g, w_a, cos_t, sin_t, S):
    T = x2.shape[0]
    tm = TM_IN
    n_s = S // tm
    head_spec = lambda n: pl.BlockSpec((n, tm, HEAD_DIM), lambda i: (0, i, 0))
    row_spec = lambda w: pl.BlockSpec((tm, w), lambda i: (i, 0))
    return pl.pallas_call(
        _inproj_kernel,
        grid=(T // tm,),
        in_specs=[row_spec(D_MODEL),
                  pl.BlockSpec((1, D_MODEL), lambda i: (0, 0)),
                  pl.BlockSpec((D_MODEL, IN_COLS), lambda i: (0, 0)),
                  pl.BlockSpec((tm, LANES), lambda i: (i % n_s, 0)),
                  pl.BlockSpec((tm, LANES), lambda i: (i % n_s, 0))],
        out_specs=[head_spec(NSA_HEADS), head_spec(NSA_HEADS), row_spec(2 * KV_WIDTH),
                   head_spec(NSA_GROUPS), head_spec(NSA_GROUPS), head_spec(NSA_GROUPS), head_spec(NSA_GROUPS),
                   row_spec(POOL_WIDTH), head_spec(MEM_HEADS), row_spec(GATE_PAD)],
        out_shape=[jax.ShapeDtypeStruct((NSA_HEADS, T, HEAD_DIM), BF16),
                   jax.ShapeDtypeStruct((NSA_HEADS, T, HEAD_DIM), BF16),
                   jax.ShapeDtypeStruct((T, 2 * KV_WIDTH), F32),
                   jax.ShapeDtypeStruct((NSA_GROUPS, T, HEAD_DIM), BF16),
                   jax.ShapeDtypeStruct((NSA_GROUPS, T, HEAD_DIM), BF16),
                   jax.ShapeDtypeStruct((NSA_GROUPS, T, HEAD_DIM), BF16),
                   jax.ShapeDtypeStruct((NSA_GROUPS, T, HEAD_DIM), BF16),
                   jax.ShapeDtypeStruct((T, POOL_WIDTH), F32),
                   jax.ShapeDtypeStruct((MEM_HEADS, T, HEAD_DIM), BF16),
                   jax.ShapeDtypeStruct((T, GATE_PAD), F32)],
        compiler_params=pltpu.CompilerParams(dimension_semantics=("arbitrary",), vmem_limit_bytes=VMEM_LIMIT),
        name="inproj",
    )(x2, g, w_a, cos_t, sin_t)


N_CMP_PAD = 128
CMP_FLAT = CMP_BLOCK * HEAD_DIM
CMP_HALF = CMP_STRIDE * HEAD_DIM


KV_LANE_CHUNK = LANES


def _nsa_kernel(qc_ref, qr_ref, kvc_ref, ksel_ref, vsel_ref, kwin_ref, vwin_ref, gate_ref,
                w1_ref, w2_ref, w2t_ref, pos_ref, ovl_ref, drop_ref, o_ref, kc_s, vct_s, *, S):
    i = pl.program_id(1)
    tq = TQ
    hpg = HEADS_PER_GROUP
    n_sel = S // SEL_BLOCK
    hq = hpg * tq

    @pl.when(i == 0)
    def _compress():
        for kv in range(2):
            w1 = w1_ref[kv]
            posterm = _dot(pos_ref[kv], w1)[0:1]
            for g in range(NSA_GROUPS):
                a = kvc_ref[0, kv * NSA_GROUPS + g].astype(BF16)
                p1 = _dot(a, w1[:CMP_HALF])
                p2 = _dot(a, w1[CMP_HALF:])
                hid = p1 + pltpu.roll(p2, N_CMP_PAD - 1, 0) + posterm
                hid = (hid * _sigmoid(hid)).astype(BF16)
                if kv == 0:
                    kc_s[g] = _dot(hid, w2_ref[kv]).astype(BF16)
                else:
                    vct_s[g] = _dot_nt(w2t_ref[kv], hid).astype(BF16)

    q0 = i * tq
    t_lane = q0 + lax.broadcasted_iota(jnp.int32, (1, tq), 1)

    def key_pos(start, n):
        return start + lax.broadcasted_iota(jnp.int32, (n, 1), 0)

    cmp_valid = (key_pos(0, N_CMP_PAD) * CMP_STRIDE + CMP_BLOCK - 1) <= t_lane
    jrow = lax.broadcasted_iota(jnp.int32, (n_sel, tq), 0)
    cur = t_lane // SEL_BLOCK
    sel_valid = jrow * SEL_BLOCK <= t_lane
    forced = (jrow == 0) | (jrow == cur) | (jrow == cur - 1)

    cd = q0 // CKS
    causal_bias = jnp.where(key_pos(cd * CKS, CKS) <= t_lane, 0.0, NEG_INF)
    win_chunks = []
    hi = q0 + tq
    for n in WIN_CHUNKS:
        lo = hi - n
        start = pl.multiple_of(jnp.maximum(lo, 0), LANES)
        kp = key_pos(start, n)
        diff = t_lane - kp
        win_chunks.append((start, n, jnp.where((diff >= 0) & (diff < WINDOW) & (kp < hi), 0.0, NEG_INF)))
        hi = lo

    def values_t(ref, g, start, n):
        c0 = start // KV_LANE_CHUNK
        return jnp.concatenate([ref[g, 0, c0 + j] for j in range(n // KV_LANE_CHUNK)], axis=1)

    def attend(jobs):
        scores = [_dot_nt(k, q_all) for q_all, k, _, _ in jobs]
        heads = [slice(hh * tq, (hh + 1) * tq) for hh in range(hpg)]
        maxes = [jnp.concatenate([jnp.max(s_all[:, sl] + bias, axis=0, keepdims=True) for sl in heads], axis=1)
                 for (_, _, _, bias), s_all in zip(jobs, scores)]
        soft = []
        for (_, _, _, bias), s_all, m in zip(jobs, scores, maxes):
            ps = [jnp.exp2(s_all[:, sl] + bias - m[:, sl]) for sl in heads]
            l = jnp.concatenate([jnp.sum(p, axis=0, keepdims=True) for p in ps], axis=1)
            soft.append((l, jnp.concatenate([p.astype(BF16) for p in ps], axis=1)))
        return [(m, l, _dot(v_t, p)) for (_, _, v_t, _), m, (l, p) in zip(jobs, maxes, soft)]

    def merge(a, b):
        m = jnp.maximum(a[0], b[0])
        wa = jnp.exp2(a[0] - m)
        wb = jnp.exp2(b[0] - m)
        return m, wa * a[1] + wb * b[1], wa * a[2] + wb * b[2]

    def select_blocks(g):
        q_cmp = qc_ref[g * hpg:(g + 1) * hpg].reshape(hq, HEAD_DIM)
        s_all = _dot_nt(kc_s[g], q_cmp)
        p_grp = jnp.zeros((N_CMP_PAD, tq), F32)
        ps = []
        for hh in range(hpg):
            sl = slice(hh * tq, (hh + 1) * tq)
            s = jnp.where(cmp_valid, s_all[:, sl], NEG_INF)
            m = jnp.max(s, axis=0, keepdims=True)
            e = jnp.where(cmp_valid, jnp.exp2(s - m), 0.0)
            p = e * (1.0 / jnp.maximum(jnp.sum(e, axis=0, keepdims=True), TINY))
            p_grp = p_grp + p
            ps.append(p.astype(BF16))
        o_cmp = _dot(vct_s[g], jnp.concatenate(ps, axis=1))

        ovl = ovl_ref[...]
        p_hi = p_grp.astype(BF16)
        r1 = p_grp - p_hi.astype(F32)
        p_mid = r1.astype(BF16)
        p_lo = (r1 - p_mid.astype(F32)).astype(BF16)
        score = _dot(ovl, p_hi) + _dot(ovl, p_mid) + _dot(ovl, p_lo)
        score = jnp.where(sel_valid, score + jnp.where(forced, FORCE_BONUS, 0.0), -1.0)
        rank = jnp.zeros((n_sel, tq), F32)
        for jp in range(n_sel):
            other = score[jp:jp + 1, :]
            beats = (other > score) | ((other == score) & (jrow > jp))
            rank = rank + beats.astype(F32)
        return o_cmp, (rank >= SEL_TOPN).astype(BF16)

    groups = range(NSA_GROUPS)
    cmp_out = [select_blocks(g) for g in groups]
    q_rot = [qr_ref[g * hpg:(g + 1) * hpg].reshape(hq, HEAD_DIM) for g in groups]

    def selected_job(g, c, extra_bias):
        k0 = pl.multiple_of(c * CKS, CKS)
        bias = _dot(drop_ref[c], cmp_out[g][1])
        if extra_bias is not None:
            bias = bias + extra_bias
        return q_rot[g], ksel_ref[g, pl.ds(k0, CKS), :], values_t(vsel_ref, g, k0, CKS), bias

    def window_job(g, chunk):
        start, n, bias = chunk
        return q_rot[g], kwin_ref[g, pl.ds(start, n), :], values_t(vwin_ref, g, start, n), bias

    stats = attend([selected_job(g, cd, causal_bias) for g in groups]
                   + [window_job(g, chunk) for chunk in win_chunks for g in groups])
    sel_state = tuple(stats[:NSA_GROUPS])
    win_state = list(stats[NSA_GROUPS:2 * NSA_GROUPS])
    for j in range(1, len(win_chunks)):
        for g in groups:
            win_state[g] = merge(win_state[g], stats[NSA_GROUPS * (1 + j) + g])

    def sel_body(it, states):
        new = attend([selected_job(g, cd - 1 - it, None) for g in groups])
        return tuple(merge(states[g], new[g]) for g in groups)

    sel_state = lax.fori_loop(0, cd, sel_body, sel_state)

    gates_t = gate_ref[...].T
    out_rows = []
    for g in groups:
        o_cmp = cmp_out[g][0]
        o_sel = sel_state[g][2] * (1.0 / sel_state[g][1])
        o_win = win_state[g][2] * (1.0 / win_state[g][1])
        for hh in range(hpg):
            h = g * hpg + hh
            sl = slice(hh * tq, (hh + 1) * tq)
            out_rows.append(gates_t[3 * h:3 * h + 1] * o_cmp[:, sl] + gates_t[3 * h + 1:3 * h + 2] * o_sel[:, sl]
                            + gates_t[3 * h + 2:3 * h + 3] * o_win[:, sl])
    o_ref[...] = jnp.concatenate(out_rows, axis=0).T.astype(BF16)


def _nsa(qc, qr, kvc, ksel, vsel_t, kwin, vwin_t, gates, w1, w2, w2t, pos, ovl, drop_bias, B, S):
    T = B * S
    tq = TQ
    nq = S // tq
    q_spec = pl.BlockSpec((NSA_HEADS, tq, HEAD_DIM), lambda b, i: (0, b * nq + i, 0))
    k_spec = pl.BlockSpec((NSA_GROUPS, S, HEAD_DIM), lambda b, i: (0, b, 0))
    v_spec = pl.BlockSpec((NSA_GROUPS, 1, S // KV_LANE_CHUNK, HEAD_DIM, KV_LANE_CHUNK), lambda b, i: (0, b, 0, 0, 0))
    full = lambda a: pl.BlockSpec(a.shape, lambda b, i: (0,) * a.ndim)
    return pl.pallas_call(
        functools.partial(_nsa_kernel, S=S),
        grid=(B, nq),
        in_specs=[q_spec, q_spec,
                  pl.BlockSpec((1, 2 * NSA_GROUPS, N_CMP_PAD, CMP_HALF), lambda b, i: (b, 0, 0, 0)),
                  k_spec, v_spec, k_spec, v_spec,
                  pl.BlockSpec((tq, GATE_PAD), lambda b, i: (b * nq + i, 0)),
                  full(w1), full(w2), full(w2t), full(pos), full(ovl), full(drop_bias)],
        out_specs=pl.BlockSpec((tq, NSA_WIDTH), lambda b, i: (b * nq + i, 0)),
        out_shape=jax.ShapeDtypeStruct((T, NSA_WIDTH), BF16),
        scratch_shapes=[pltpu.VMEM((NSA_GROUPS, N_CMP_PAD, HEAD_DIM), BF16),
                        pltpu.VMEM((NSA_GROUPS, HEAD_DIM, N_CMP_PAD), BF16)],
        compiler_params=pltpu.CompilerParams(dimension_semantics=("arbitrary", "arbitrary"),
                                             vmem_limit_bytes=VMEM_LIMIT),
        name="nsa",
    )(qc, qr, kvc, ksel, vsel_t, kwin, vwin_t, gates, w1, w2, w2t, pos, ovl, drop_bias)


ROUTE_E, ROUTE_R, ROUTE_W = 0, TOP_K, 2 * TOP_K
ROUTE_ROWS = 16


def _merge_kernel(x_ref, onsa_ref, pool_ref, prev_ref, qm_ref, km_ref, vm_ref,
                  gmix_ref, wmg_ref, wpool_ref, pscale_ref, wun_ref, wup_ref, wum_ref, wout_ref,
                  gffn_ref, wrh_ref, wrl_ref, br_ref, tri_ref, lower_ref,
                  x1_ref, hf_ref, route_ref, cnt_ref, omem_s, carry_s, *, S):
    i = pl.program_id(0)
    tm = TM_MERGE
    n_s = S // tm

    @pl.when(i == 0)
    def _init():
        carry_s[...] = jnp.zeros_like(carry_s)

    x = x_ref[...]
    h = _rms(x, gmix_ref[...]).astype(BF16)
    mg = _sigmoid(_dot(h, wmg_ref[...]))

    u = pool_ref[...]
    seq_tile = i % n_s
    prev = jnp.where(seq_tile == 0, 0.0, prev_ref[...])
    ext = jnp.concatenate([prev, u], axis=0)
    b2 = ext[1:] + ext[:-1]
    b4 = b2[2:] + b2[:-2]
    b8 = b4[4:] + b4[:-4]
    b16 = b8[8:] + b8[:-8]
    sums = (b2[POOL_HALO - 1:POOL_HALO - 1 + tm], b4[POOL_HALO - 3:POOL_HALO - 3 + tm],
            b8[POOL_HALO - 7:POOL_HALO - 7 + tm], b16[POOL_HALO - 15:POOL_HALO - 15 + tm])
    t_seq = seq_tile * tm + lax.broadcasted_iota(jnp.int32, (tm, 1), 0)
    lane_p = lax.broadcasted_iota(jnp.int32, (tm, POOL_WIDTH), 1)
    z = jnp.zeros((tm, POOL_WIDTH), F32)
    for gi, w in enumerate(POOL_WINDOWS):
        cnt = jnp.minimum(t_seq + 1, w).astype(F32)
        z = jnp.where(lane_p // POOL_GROUP == gi, sums[gi] / cnt, z)
    z = z - u
    o_pool = (_dot(z.astype(BF16), wpool_ref[...]) * pscale_ref[...]).astype(BF16)

    for hh in range(MEM_HEADS):
        s = _dot_nt(qm_ref[hh], km_ref[0, hh])
        m = jnp.max(s, axis=-1, keepdims=True)
        e = jnp.exp(s - m)
        p = e / jnp.sum(e, axis=-1, keepdims=True)
        omem_s[:, hh * HEAD_DIM:(hh + 1) * HEAD_DIM] = _dot(p.astype(BF16), vm_ref[0, hh]).astype(BF16)

    merged = (mg[:, :D_MODEL] * _dot(onsa_ref[...], wun_ref[...])
              + mg[:, D_MODEL:2 * D_MODEL] * _dot(o_pool, wup_ref[...])
              + mg[:, 2 * D_MODEL:] * _dot(omem_s[...], wum_ref[...]))
    x1 = x + _dot(merged.astype(BF16), wout_ref[...])
    x1_ref[...] = x1
    hf = _rms(x1, gffn_ref[...])
    hf_ref[...] = hf

    hf_hi = hf.astype(BF16)
    hf_lo = (hf - hf_hi.astype(F32)).astype(BF16)
    logits = (_dot_nt(wrh_ref[...], hf_hi) + _dot_nt(wrl_ref[...], hf_hi) + _dot_nt(wrh_ref[...], hf_lo)
              + br_ref[...])
    erow = lax.broadcasted_iota(jnp.int32, (N_EXPERTS, tm), 0)
    rank = jnp.zeros((N_EXPERTS, tm), F32)
    for jp in range(N_EXPERTS):
        other = logits[jp:jp + 1, :]
        beats = (other > logits) | ((other == logits) & (erow > jp))
        rank = rank + beats.astype(F32)
    chosen = rank < TOP_K
    m = jnp.max(logits, axis=0, keepdims=True)
    e = jnp.where(chosen, jnp.exp(logits - m), 0.0)
    comb = e * (1.0 / jnp.sum(e, axis=0, keepdims=True))

    chosen_b = chosen.astype(BF16)
    carry = carry_s[:, 0:1]
    in_expert = _dot(chosen_b, tri_ref[...]) + carry
    carry_new = carry + jnp.sum(chosen.astype(F32), axis=1, keepdims=True)
    carry_s[...] = jnp.broadcast_to(carry_new, carry_s.shape)
    cnt_ref[...] = jnp.broadcast_to(carry_new, cnt_ref.shape)

    before = _dot(lower_ref[...], chosen_b)
    erow_f = erow.astype(F32)
    fields = {ROUTE_E: erow_f, ROUTE_R: in_expert, ROUTE_W: comb}
    rows = [None] * ROUTE_ROWS
    for k in range(TOP_K):
        pick = chosen & (before == k)
        for base, val in fields.items():
            rows[base + k] = jnp.sum(jnp.where(pick, val, 0.0), axis=0, keepdims=True)
    zero_row = jnp.zeros((1, tm), F32)
    route_ref[...] = jnp.concatenate([zero_row if r is None else r for r in rows], axis=0)


def _merge(x2, onsa, pool_in, qm, km, vm, consts, B, S):
    T = B * S
    tm = TM_MERGE
    n_s = S // tm
    M = km.shape[2]
    halo_per_tile = tm // POOL_HALO
    row = lambda w: pl.BlockSpec((tm, w), lambda i: (i, 0))
    full = lambda a: pl.BlockSpec(a.shape, lambda i: (0,) * a.ndim)
    mem_spec = pl.BlockSpec((1, MEM_HEADS, M, HEAD_DIM), lambda i: (i // n_s, 0, 0, 0))
    return pl.pallas_call(
        functools.partial(_merge_kernel, S=S),
        grid=(T // tm,),
        in_specs=[row(D_MODEL), row(NSA_WIDTH), row(POOL_WIDTH),
                  pl.BlockSpec((POOL_HALO, POOL_WIDTH), lambda i: (jnp.maximum(i * halo_per_tile - 1, 0), 0)),
                  pl.BlockSpec((MEM_HEADS, tm, HEAD_DIM), lambda i: (0, i, 0)),
                  mem_spec, mem_spec] + [full(c) for c in consts],
        out_specs=[row(D_MODEL), row(D_MODEL), pl.BlockSpec((ROUTE_ROWS, tm), lambda i: (0, i)),
                   pl.BlockSpec((N_EXPERTS, LANES), lambda i: (0, 0))],
        out_shape=[jax.ShapeDtypeStruct((T, D_MODEL), F32),
                   jax.ShapeDtypeStruct((T, D_MODEL), F32),
                   jax.ShapeDtypeStruct((ROUTE_ROWS, T), F32),
                   jax.ShapeDtypeStruct((N_EXPERTS, LANES), F32)],
        scratch_shapes=[pltpu.VMEM((tm, MEM_WIDTH), BF16), pltpu.VMEM((N_EXPERTS, LANES), F32)],
        compiler_params=pltpu.CompilerParams(dimension_semantics=("arbitrary",), vmem_limit_bytes=VMEM_LIMIT),
        name="merge",
    )(x2, onsa, pool_in, pool_in, qm, km, vm, *consts)


def _sc_mesh():
    return plsc.VectorSubcoreMesh(core_axis_name="c", subcore_axis_name="s")


def _sc_worker():
    return lax.axis_index("s") * SC_CORES + lax.axis_index("c")


def _sc_dispatch(hf, dest, n_rows):
    T, d = hf.shape
    per_worker = T // SC_WORKERS
    steps = per_worker // SC_ROWS

    @functools.partial(pl.kernel, mesh=_sc_mesh(), out_type=jax.ShapeDtypeStruct((n_rows, d), hf.dtype),
                       scratch_types=[pltpu.VMEM((TOP_K, SC_ROWS), jnp.int32), pltpu.VMEM((SC_ROWS, d), hf.dtype),
                                      pltpu.SemaphoreType.DMA])
    def dispatch(hf_hbm, dest_hbm, xs_hbm, idx_v, rows_v, sem):
        base = _sc_worker() * per_worker

        @pl.loop(0, steps)
        def _(j):
            rows = pl.ds(pl.multiple_of(base + j * SC_ROWS, SC_ROWS), SC_ROWS)
            pltpu.sync_copy(hf_hbm.at[rows], rows_v)
            for k in range(TOP_K):
                pltpu.sync_copy(dest_hbm.at[k, rows], idx_v.at[k])
            copies = [pltpu.make_async_copy(rows_v, xs_hbm.at[idx_v.at[k]], sem) for k in range(TOP_K)]
            for c in copies:
                c.start()
            for c in copies:
                c.wait()

    return dispatch(hf, dest)


def _moe_kernel(te_ref, nu_ref, nv_ref, xs_ref, wgu_ref, bgu_ref, wd_ref, bd_ref, ys_ref, wgu_s, wd_s):
    j = pl.program_id(0)
    used = j < nu_ref[0]
    new_expert = (j == 0) | (te_ref[j] != te_ref[jnp.maximum(j - 1, 0)])

    @pl.when(used & new_expert)
    def _cast_weights():
        wgu_s[...] = wgu_ref[0].astype(BF16)
        wd_s[...] = wd_ref[0].astype(BF16)

    @pl.when(used)
    def _compute():
        live = lax.broadcasted_iota(jnp.int32, (xs_ref.shape[0], 1), 0) < nv_ref[j]
        xb = jnp.where(live, xs_ref[...], 0.0).astype(BF16)
        gu = _dot(xb, wgu_s[...]) + bgu_ref[0]
        gate = jnp.minimum(gu[:, :D_FF], SWIGLU_LIMIT)
        up = jnp.clip(gu[:, D_FF:], -SWIGLU_LIMIT, SWIGLU_LIMIT)
        act = (up + 1.0) * (gate * _sigmoid(SWIGLU_ALPHA * gate))
        ys_ref[...] = _dot(act.astype(BF16), wd_s[...]) + bd_ref[0]

    @pl.when(j >= nu_ref[0])
    def _unused():
        ys_ref[...] = jnp.zeros_like(ys_ref)


def _moe(tile_expert, n_used, n_valid, xs, wgu, bgu, wd, bd):
    P = xs.shape[0]
    tm = TM_E
    grid_spec = pltpu.PrefetchScalarGridSpec(
        num_scalar_prefetch=3,
        grid=(P // tm,),
        in_specs=[pl.BlockSpec((tm, D_MODEL), lambda j, te, nu, nv: (j, 0)),
                  pl.BlockSpec((1, D_MODEL, 2 * D_FF), lambda j, te, nu, nv: (te[j], 0, 0)),
                  pl.BlockSpec((1, 1, 2 * D_FF), lambda j, te, nu, nv: (te[j], 0, 0)),
                  pl.BlockSpec((1, D_FF, D_MODEL), lambda j, te, nu, nv: (te[j], 0, 0)),
                  pl.BlockSpec((1, 1, D_MODEL), lambda j, te, nu, nv: (te[j], 0, 0))],
        out_specs=pl.BlockSpec((tm, D_MODEL), lambda j, te, nu, nv: (j, 0)),
        scratch_shapes=[pltpu.VMEM((D_MODEL, 2 * D_FF), BF16), pltpu.VMEM((D_FF, D_MODEL), BF16)],
    )
    return pl.pallas_call(
        _moe_kernel,
        grid_spec=grid_spec,
        out_shape=jax.ShapeDtypeStruct((P, D_MODEL), F32),
        compiler_params=pltpu.CompilerParams(dimension_semantics=("arbitrary",), vmem_limit_bytes=VMEM_LIMIT),
        name="moe",
    )(tile_expert, n_used, n_valid, xs, wgu, bgu, wd, bd)


def _sc_gather_rows(table, idx):
    n, d = idx.shape[0], table.shape[1]
    per_worker = n // SC_WORKERS
    steps = per_worker // SC_ROWS
    assert steps % 2 == 0

    @functools.partial(pl.kernel, mesh=_sc_mesh(), out_type=jax.ShapeDtypeStruct((n, d), table.dtype),
                       scratch_types=[pltpu.VMEM((2, SC_ROWS), jnp.int32), pltpu.VMEM((2, SC_ROWS, d), table.dtype),
                                      pltpu.SemaphoreType.DMA((2,)), pltpu.SemaphoreType.DMA((2,))])
    def gather(table_hbm, idx_hbm, out_hbm, idx_v, rows_v, gsem, wsem):
        base = _sc_worker() * per_worker

        def rows_at(j):
            return pl.ds(pl.multiple_of(base + j * SC_ROWS, SC_ROWS), SC_ROWS)

        def gather_copy(slot):
            return pltpu.make_async_copy(table_hbm.at[idx_v.at[slot]], rows_v.at[slot], gsem.at[slot])

        def write_copy(j, slot):
            return pltpu.make_async_copy(rows_v.at[slot], out_hbm.at[rows_at(j)], wsem.at[slot])

        def fetch(j, slot):
            pltpu.sync_copy(idx_hbm.at[rows_at(j)], idx_v.at[slot])
            gather_copy(slot).start()

        fetch(0, 0)

        @pl.loop(0, steps, step=2)
        def _(j0):
            for slot in range(2):
                j = j0 + slot
                gather_copy(slot).wait()

                @pl.when(j >= 1)
                def _():
                    write_copy(j - 1, 1 - slot).wait()

                @pl.when(j + 1 < steps)
                def _():
                    fetch(j + 1, 1 - slot)

                write_copy(j, slot).start()

        write_copy(steps - 1, 1).wait()

    return gather(table, idx)


def _final_kernel(x1_ref, yg_ref, route_ref, g_ref, o_ref):
    tm = x1_ref.shape[0]
    route_t = jnp.concatenate([route_ref[...], jnp.zeros((LANES - ROUTE_ROWS, tm), F32)], axis=0).T
    acc = x1_ref[...]
    for k in range(TOP_K):
        acc = acc + route_t[:, ROUTE_W + k:ROUTE_W + k + 1] * yg_ref[k]
    o_ref[...] = _rms(acc, g_ref[...])


def _final(x1, ys, dest, route, g):
    T = x1.shape[0]
    tm = TM_FINAL
    yg = _sc_gather_rows(ys, dest.reshape(-1)).reshape(TOP_K, T, D_MODEL)
    return pl.pallas_call(
        _final_kernel,
        grid=(T // tm,),
        in_specs=[pl.BlockSpec((tm, D_MODEL), lambda i: (i, 0)),
                  pl.BlockSpec((TOP_K, tm, D_MODEL), lambda i: (0, i, 0)),
                  pl.BlockSpec((ROUTE_ROWS, tm), lambda i: (0, i)),
                  pl.BlockSpec((1, D_MODEL), lambda i: (0, 0))],
        out_specs=pl.BlockSpec((tm, D_MODEL), lambda i: (i, 0)),
        out_shape=jax.ShapeDtypeStruct((T, D_MODEL), F32),
        compiler_params=pltpu.CompilerParams(dimension_semantics=("arbitrary",), vmem_limit_bytes=VMEM_LIMIT),
        name="final",
    )(x1, yg, route, g)


def _rope_tables(S):
    half = HEAD_DIM // 2
    inv = ROPE_THETA ** (-jnp.arange(half, dtype=F32) / half)
    ang = jnp.arange(S, dtype=F32)[:, None] * inv[None, :]
    cos = jnp.tile(jnp.cos(ang), (1, LANES // half))
    sin = jnp.tile(jnp.concatenate([-jnp.sin(ang), jnp.sin(ang)], axis=1), (1, LANES // HEAD_DIM))
    return cos, sin


def _selection_constants(S):
    nc = (S - CMP_BLOCK) // CMP_STRIDE + 1
    n_sel = S // SEL_BLOCK
    j = jnp.arange(n_sel)[:, None]
    i = jnp.arange(N_CMP_PAD)[None, :]
    overlap_t = ((i * CMP_STRIDE <= j * SEL_BLOCK + SEL_BLOCK - 1)
                 & (i * CMP_STRIDE + CMP_BLOCK - 1 >= j * SEL_BLOCK) & (i < nc)).astype(BF16)
    row = jnp.arange(LANES)[:, None]
    key = jnp.arange(S)[None, :]
    drop_bias = jnp.where(key // SEL_BLOCK == row, NEG_INF, 0.0).astype(BF16)[:n_sel]
    drop_bias = drop_bias.T.reshape(S // CKS, CKS, n_sel)
    return overlap_t, drop_bias


def kernel(x, mem, norm_mix, norm_mem, w_in, cmp_pos, cmp_w1, cmp_w2, w_pool, pool_scale, w_mem_kv, w_up_nsa,
           w_up_pool, w_up_mem, w_out, norm_ffn, w_router, b_router, w_gate_up, b_gate_up, w_down, b_down,
           norm_final):
    B, S, D = x.shape
    T = B * S
    assert D == D_MODEL and S % CKS == 0 and S // SEL_BLOCK == 32 and T % (SC_WORKERS * SC_ROWS * 2) == 0
    l = 0
    x2 = x.reshape(T, D)

    w = w_in[l]
    o_gate = NSA_WIDTH + 6 * KV_WIDTH
    n_gate = 3 * NSA_HEADS
    o_pool = o_gate + n_gate
    o_qm = o_pool + POOL_WIDTH
    o_mg = o_qm + MEM_WIDTH
    w_a = jnp.concatenate([w[:, :o_gate], w[:, o_pool:o_mg], w[:, o_gate:o_pool],
                           jnp.zeros((D, GATE_PAD - n_gate), F32)], axis=1).astype(BF16)
    w_mg = w[:, o_mg:].astype(BF16)
    cos_t, sin_t = _rope_tables(S)
    overlap_t, drop_bias = _selection_constants(S)
    w1 = cmp_w1[l].reshape(2, CMP_FLAT, HEAD_DIM).astype(BF16)
    w2 = cmp_w2[l].astype(BF16)
    w2t = jnp.swapaxes(w2, 1, 2)
    pos = jnp.broadcast_to(cmp_pos[l].reshape(2, 1, CMP_FLAT), (2, 8, CMP_FLAT)).astype(BF16)
    wpool_bd = jnp.zeros((POOL_WIDTH, POOL_WIDTH), F32)
    for gi in range(len(POOL_WINDOWS)):
        wpool_bd = wpool_bd.at[gi * POOL_GROUP:(gi + 1) * POOL_GROUP, gi * POOL_GROUP:(gi + 1) * POOL_GROUP].set(w_pool[l, gi])
    wr = w_router[l].T
    wr_hi = wr.astype(BF16)
    wr_lo = (wr - wr_hi.astype(F32)).astype(BF16)
    br = b_router[l].reshape(N_EXPERTS, 1)
    tri = (jnp.arange(TM_MERGE)[:, None] < jnp.arange(TM_MERGE)[None, :]).astype(BF16)
    lower = (jnp.arange(N_EXPERTS)[None, :] < jnp.arange(N_EXPERTS)[:, None]).astype(BF16)

    km, vm = _memkv(mem, norm_mem[l].reshape(1, D), w_mem_kv[l].astype(BF16))
    qc, qr, kvcmp, ksel, vsel, kwin, vwin, pool_in, qm, gates = _inproj(
        x2, norm_mix[l].reshape(1, D), w_a, cos_t, sin_t, S)
    kvc = kvcmp.reshape(B, S // CMP_STRIDE, CMP_STRIDE, 2 * NSA_GROUPS, HEAD_DIM).transpose(0, 3, 1, 2, 4)
    kvc = kvc.reshape(B, 2 * NSA_GROUPS, S // CMP_STRIDE, CMP_HALF)
    to_slabs = lambda v: jnp.swapaxes(v.reshape(NSA_GROUPS, B, S // KV_LANE_CHUNK, KV_LANE_CHUNK, HEAD_DIM), 3, 4)
    o_nsa = _nsa(qc, qr, kvc, ksel, to_slabs(vsel), kwin, to_slabs(vwin), gates, w1, w2, w2t, pos, overlap_t,
                 drop_bias, B, S)
    consts = [norm_mix[l].reshape(1, D), w_mg, wpool_bd.astype(BF16), pool_scale[l].reshape(1, POOL_WIDTH),
              w_up_nsa[l].astype(BF16), w_up_pool[l].astype(BF16), w_up_mem[l].astype(BF16), w_out[l].astype(BF16),
              norm_ffn[l].reshape(1, D), wr_hi, wr_lo, br, tri, lower]
    x1, hf, route, counts = _merge(x2, o_nsa, pool_in, qm, km, vm, consts, B, S)

    counts = counts[:, 0].astype(jnp.int32)
    padded = ((counts + TM_E - 1) // TM_E) * TM_E
    ends = jnp.cumsum(padded)
    starts = ends - padded
    n_tiles = (T * TOP_K) // TM_E + N_EXPERTS
    P = n_tiles * TM_E
    e_k = route[ROUTE_E:ROUTE_E + TOP_K].astype(jnp.int32)
    r_k = route[ROUTE_R:ROUTE_R + TOP_K].astype(jnp.int32)
    group_start = jnp.zeros_like(e_k)
    for e in range(N_EXPERTS):
        group_start = jnp.where(e_k == e, starts[e], group_start)
    dest = group_start + r_k
    tile_start = jnp.arange(n_tiles, dtype=jnp.int32) * TM_E
    tile_expert = jnp.minimum(jnp.sum(tile_start[:, None] >= ends[None, :], axis=1), N_EXPERTS - 1).astype(jnp.int32)
    n_used = (ends[-1] // TM_E).astype(jnp.int32).reshape(1)
    n_valid = jnp.clip((starts + counts)[tile_expert] - tile_start, 0, TM_E).astype(jnp.int32)

    xs = _sc_dispatch(hf, dest, P)
    ys = _moe(tile_expert, n_used, n_valid, xs, w_gate_up[l], b_gate_up[l].reshape(N_EXPERTS, 1, 2 * D_FF),
              w_down[l], b_down[l].reshape(N_EXPERTS, 1, D_MODEL))
    out = _final(x1, ys, dest, route, norm_final.reshape(1, D))
    return out.reshape(B, S, D)
```

```python
import functools

import jax
import jax.numpy as jnp
from jax import lax
from jax.experimental import pallas as pl
from jax.experimental.pallas import tpu as pltpu
from jax.experimental.pallas import tpu_sc as plsc

F32 = jnp.float32
BF16 = jnp.bfloat16
U32 = jnp.uint32

D_MODEL = 1024
HEAD_DIM = 64
NSA_HEADS = 8
NSA_GROUPS = 2
HEADS_PER_GROUP = NSA_HEADS // NSA_GROUPS
NSA_WIDTH = NSA_HEADS * HEAD_DIM
KV_WIDTH = NSA_GROUPS * HEAD_DIM
CMP_BLOCK = 32
CMP_STRIDE = 16
SEL_BLOCK = 64
SEL_TOPN = 8
FORCE_BONUS = 1000.0
WINDOW = 512
POOL_WINDOWS = (2, 4, 8, 16)
POOL_GROUP = 64
POOL_WIDTH = POOL_GROUP * len(POOL_WINDOWS)
POOL_HALO = 16
MEM_HEADS = 4
MEM_WIDTH = MEM_HEADS * HEAD_DIM
N_EXPERTS = 32
TOP_K = 4
D_FF = 1024
SWIGLU_LIMIT = 7.0
SWIGLU_ALPHA = 1.702
ROPE_THETA = 10000.0
EPS = 1e-5
NEG_INF = -1e30
TINY = 1e-30
QK_SCALE = HEAD_DIM ** -0.5
LOG2_E = 1.4426950408889634
NSA_Q_SCALE = QK_SCALE * LOG2_E

LANES = 128
GATE_PAD = LANES
PACKED = D_MODEL // 2

TM_IN = 512
TQ = 256
CKS = 512
WIN_CHUNKS = (256, 256, 256)
TM_MERGE = 512
TM_E = 512
TM_FINAL = 512
SC_CORES = 2
SC_SUBCORES = 16
SC_WORKERS = SC_CORES * SC_SUBCORES
SC_ROWS = 64
VMEM_LIMIT = 56 * 1024 * 1024


def _rms(x, g):
    return x * lax.rsqrt(jnp.mean(x * x, axis=-1, keepdims=True) + EPS) * g


def _sigmoid(x):
    return 0.5 * jnp.tanh(0.5 * x) + 0.5


def _dot(a, b):
    return jnp.dot(a, b, preferred_element_type=F32)


def _pack_bf16_pairs(v):
    n = v.shape[1] // 2
    r = v.astype(BF16).astype(F32)
    lo = pltpu.bitcast(r[:, :n], U32) >> 16
    hi = pltpu.bitcast(r[:, n:], U32) & jnp.uint32(0xFFFF0000)
    return lo | hi


def _unpack_bf16_pairs(w):
    lo = pltpu.bitcast(w << 16, F32)
    hi = pltpu.bitcast(w & jnp.uint32(0xFFFF0000), F32)
    return jnp.concatenate([lo, hi], axis=1)


def _dot_nt(a, b):
    return lax.dot_general(a, b, (((1,), (1,)), ((), ())), preferred_element_type=F32)


def _memkv_kernel(mem_ref, g_ref, w_ref, k_ref, v_ref):
    m = _rms(mem_ref[0], g_ref[...]).astype(BF16)
    kv = _dot(m, w_ref[...])
    for h in range(MEM_HEADS):
        k_ref[0, h] = kv[:, h * HEAD_DIM:(h + 1) * HEAD_DIM].astype(BF16)
        v_ref[0, h] = kv[:, MEM_WIDTH + h * HEAD_DIM:MEM_WIDTH + (h + 1) * HEAD_DIM].astype(BF16)


def _memkv(mem, g, w):
    B, M, D = mem.shape
    return pl.pallas_call(
        _memkv_kernel,
        grid=(B,),
        in_specs=[pl.BlockSpec((1, M, D), lambda b: (b, 0, 0)),
                  pl.BlockSpec((1, D), lambda b: (0, 0)),
                  pl.BlockSpec((D, 2 * MEM_WIDTH), lambda b: (0, 0))],
        out_specs=[pl.BlockSpec((1, MEM_HEADS, M, HEAD_DIM), lambda b: (b, 0, 0, 0)),
                   pl.BlockSpec((1, MEM_HEADS, M, HEAD_DIM), lambda b: (b, 0, 0, 0))],
        out_shape=[jax.ShapeDtypeStruct((B, MEM_HEADS, M, HEAD_DIM), BF16),
                   jax.ShapeDtypeStruct((B, MEM_HEADS, M, HEAD_DIM), BF16)],
        compiler_params=pltpu.CompilerParams(dimension_semantics=("arbitrary",), vmem_limit_bytes=VMEM_LIMIT),
        name="memkv",
    )(mem, g, w)


IN_COLS = NSA_WIDTH + 6 * KV_WIDTH + POOL_WIDTH + MEM_WIDTH + GATE_PAD


def _inproj_kernel(x_ref, g_ref, w_ref, cos_ref, sin_ref,
                   qc_ref, qr_ref, kvc_ref, ksel_ref, vsel_ref, kwin_ref, vwin_ref, pool_ref, qm_ref, gate_ref):
    h = _rms(x_ref[...], g_ref[...]).astype(BF16)
    p = _dot(h, w_ref[...])
    cos = cos_ref[...]
    sin = sin_ref[...]
    lane = lax.broadcasted_iota(jnp.int32, cos.shape, 1)
    first_half = (lane % HEAD_DIM) < (HEAD_DIM // 2)

    def rope(c):
        partner = jnp.where(first_half, pltpu.roll(c, LANES - HEAD_DIM // 2, 1), pltpu.roll(c, HEAD_DIM // 2, 1))
        return c * cos + partner * sin

    def halves(c):
        return c[:, :HEAD_DIM], c[:, HEAD_DIM:]

    for j in range(NSA_WIDTH // LANES):
        c = p[:, j * LANES:(j + 1) * LANES]
        r = rope(c)
        for hh, (cc, rr) in enumerate(zip(halves(c), halves(r))):
            qc_ref[2 * j + hh] = (cc * NSA_Q_SCALE).astype(BF16)
            qr_ref[2 * j + hh] = (rr * NSA_Q_SCALE).astype(BF16)
    o = NSA_WIDTH
    kvc_ref[...] = p[:, o:o + 2 * KV_WIDTH]
    o += 2 * KV_WIDTH
    for ref, rot in ((ksel_ref, True), (vsel_ref, False), (kwin_ref, True), (vwin_ref, False)):
        c = p[:, o:o + KV_WIDTH]
        if rot:
            c = rope(c)
        for g, cc in enumerate(halves(c)):
            ref[g] = cc.astype(BF16)
        o += KV_WIDTH
    pool_ref[...] = p[:, o:o + POOL_WIDTH]
    o += POOL_WIDTH
    for hh in range(MEM_HEADS):
        qm_ref[hh] = (p[:, o + hh * HEAD_DIM:o + (hh + 1) * HEAD_DIM] * QK_SCALE).astype(BF16)
    o += MEM_WIDTH
    gate_ref[...] = _sigmoid(p[:, o:o + GATE_PAD])


def _inproj(x2, g, w_a, cos_t, sin_t, S):
    T = x2.shape[0]
    tm = TM_IN
    n_s = S // tm
    head_spec = lambda n: pl.BlockSpec((n, tm, HEAD_DIM), lambda i: (0, i, 0))
    row_spec = lambda w: pl.BlockSpec((tm, w), lambda i: (i, 0))
    return pl.pallas_call(
        _inproj_kernel,
        grid=(T // tm,),
        in_specs=[row_spec(D_MODEL),
                  pl.BlockSpec((1, D_MODEL), lambda i: (0, 0)),
                  pl.BlockSpec((D_MODEL, IN_COLS), lambda i: (0, 0)),
                  pl.BlockSpec((tm, LANES), lambda i: (i % n_s, 0)),
                  pl.BlockSpec((tm, LANES), lambda i: (i % n_s, 0))],
        out_specs=[head_spec(NSA_HEADS), head_spec(NSA_HEADS), row_spec(2 * KV_WIDTH),
                   head_spec(NSA_GROUPS), head_spec(NSA_GROUPS), head_spec(NSA_GROUPS), head_spec(NSA_GROUPS),
                   row_spec(POOL_WIDTH), head_spec(MEM_HEADS), row_spec(GATE_PAD)],
        out_shape=[jax.ShapeDtypeStruct((NSA_HEADS, T, HEAD_DIM), BF16),
                   jax.ShapeDtypeStruct((NSA_HEADS, T, HEAD_DIM), BF16),
                   jax.ShapeDtypeStruct((T, 2 * KV_WIDTH), F32),
                   jax.ShapeDtypeStruct((NSA_GROUPS, T, HEAD_DIM), BF16),
                   jax.ShapeDtypeStruct((NSA_GROUPS, T, HEAD_DIM), BF16),
                   jax.ShapeDtypeStruct((NSA_GROUPS, T, HEAD_DIM), BF16),
                   jax.ShapeDtypeStruct((NSA_GROUPS, T, HEAD_DIM), BF16),
                   jax.ShapeDtypeStruct((T, POOL_WIDTH), F32),
                   jax.ShapeDtypeStruct((MEM_HEADS, T, HEAD_DIM), BF16),
                   jax.ShapeDtypeStruct((T, GATE_PAD), F32)],
        compiler_params=pltpu.CompilerParams(dimension_semantics=("arbitrary",), vmem_limit_bytes=VMEM_LIMIT),
        name="inproj",
    )(x2, g, w_a, cos_t, sin_t)


N_CMP_PAD = 128
CMP_FLAT = CMP_BLOCK * HEAD_DIM
CMP_HALF = CMP_STRIDE * HEAD_DIM


KV_LANE_CHUNK = LANES


def _nsa_kernel(qc_ref, qr_ref, kvc_ref, ksel_ref, vsel_ref, kwin_ref, vwin_ref, gate_ref,
                w1_ref, w2_ref, w2t_ref, pos_ref, ovl_ref, drop_ref, o_ref, kc_s, vct_s, *, S):
    i = pl.program_id(1)
    tq = TQ
    hpg = HEADS_PER_GROUP
    n_sel = S // SEL_BLOCK
    hq = hpg * tq

    @pl.when(i == 0)
    def _compress():
        for kv in range(2):
            w1 = w1_ref[kv]
            posterm = _dot(pos_ref[kv], w1)[0:1]
            for g in range(NSA_GROUPS):
                a = kvc_ref[0, kv * NSA_GROUPS + g].astype(BF16)
                p1 = _dot(a, w1[:CMP_HALF])
                p2 = _dot(a, w1[CMP_HALF:])
                hid = p1 + pltpu.roll(p2, N_CMP_PAD - 1, 0) + posterm
                hid = (hid * _sigmoid(hid)).astype(BF16)
                if kv == 0:
                    kc_s[g] = _dot(hid, w2_ref[kv]).astype(BF16)
                else:
                    vct_s[g] = _dot_nt(w2t_ref[kv], hid).astype(BF16)

    q0 = i * tq
    t_lane = q0 + lax.broadcasted_iota(jnp.int32, (1, tq), 1)

    def key_pos(start, n):
        return start + lax.broadcasted_iota(jnp.int32, (n, 1), 0)

    cmp_valid = (key_pos(0, N_CMP_PAD) * CMP_STRIDE + CMP_BLOCK - 1) <= t_lane
    jrow = lax.broadcasted_iota(jnp.int32, (n_sel, tq), 0)
    cur = t_lane // SEL_BLOCK
    sel_valid = jrow * SEL_BLOCK <= t_lane
    forced = (jrow == 0) | (jrow == cur) | (jrow == cur - 1)

    cd = q0 // CKS
    causal_bias = jnp.where(key_pos(cd * CKS, CKS) <= t_lane, 0.0, NEG_INF)
    win_chunks = []
    hi = q0 + tq
    for n in WIN_CHUNKS:
        lo = hi - n
        start = pl.multiple_of(jnp.maximum(lo, 0), LANES)
        kp = key_pos(start, n)
        diff = t_lane - kp
        win_chunks.append((start, n, jnp.where((diff >= 0) & (diff < WINDOW) & (kp < hi), 0.0, NEG_INF)))
        hi = lo

    def values_t(ref, g, start, n):
        c0 = start // KV_LANE_CHUNK
        return jnp.concatenate([ref[g, 0, c0 + j] for j in range(n // KV_LANE_CHUNK)], axis=1)

    def attend(jobs):
        scores = [_dot_nt(k, q_all) for q_all, k, _, _ in jobs]
        heads = [slice(hh * tq, (hh + 1) * tq) for hh in range(hpg)]
        maxes = [jnp.concatenate([jnp.max(s_all[:, sl] + bias, axis=0, keepdims=True) for sl in heads], axis=1)
                 for (_, _, _, bias), s_all in zip(jobs, scores)]
        soft = []
        for (_, _, _, bias), s_all, m in zip(jobs, scores, maxes):
            ps = [jnp.exp2(s_all[:, sl] + bias - m[:, sl]) for sl in heads]
            l = jnp.concatenate([jnp.sum(p, axis=0, keepdims=True) for p in ps], axis=1)
            soft.append((l, jnp.concatenate([p.astype(BF16) for p in ps], axis=1)))
        return [(m, l, _dot(v_t, p)) for (_, _, v_t, _), m, (l, p) in zip(jobs, maxes, soft)]

    def merge(a, b):
        m = jnp.maximum(a[0], b[0])
        wa = jnp.exp2(a[0] - m)
        wb = jnp.exp2(b[0] - m)
        return m, wa * a[1] + wb * b[1], wa * a[2] + wb * b[2]

    def select_blocks(g):
        q_cmp = qc_ref[g * hpg:(g + 1) * hpg].reshape(hq, HEAD_DIM)
        s_all = _dot_nt(kc_s[g], q_cmp)
        p_grp = jnp.zeros((N_CMP_PAD, tq), F32)
        ps = []
        for hh in range(hpg):
            sl = slice(hh * tq, (hh + 1) * tq)
            s = jnp.where(cmp_valid, s_all[:, sl], NEG_INF)
            m = jnp.max(s, axis=0, keepdims=True)
            e = jnp.where(cmp_valid, jnp.exp2(s - m), 0.0)
            p = e * (1.0 / jnp.maximum(jnp.sum(e, axis=0, keepdims=True), TINY))
            p_grp = p_grp + p
            ps.append(p.astype(BF16))
        o_cmp = _dot(vct_s[g], jnp.concatenate(ps, axis=1))

        ovl = ovl_ref[...]
        p_hi = p_grp.astype(BF16)
        r1 = p_grp - p_hi.astype(F32)
        p_mid = r1.astype(BF16)
        p_lo = (r1 - p_mid.astype(F32)).astype(BF16)
        score = _dot(ovl, p_hi) + _dot(ovl, p_mid) + _dot(ovl, p_lo)
        score = jnp.where(sel_valid, score + jnp.where(forced, FORCE_BONUS, 0.0), -1.0)
        rank = jnp.zeros((n_sel, tq), F32)
        for jp in range(n_sel):
            other = score[jp:jp + 1, :]
            beats = (other > score) | ((other == score) & (jrow > jp))
            rank = rank + beats.astype(F32)
        return o_cmp, (rank >= SEL_TOPN).astype(BF16)

    groups = range(NSA_GROUPS)
    cmp_out = [select_blocks(g) for g in groups]
    q_rot = [qr_ref[g * hpg:(g + 1) * hpg].reshape(hq, HEAD_DIM) for g in groups]

    def selected_job(g, c, extra_bias):
        k0 = pl.multiple_of(c * CKS, CKS)
        bias = _dot(drop_ref[c], cmp_out[g][1])
        if extra_bias is not None:
            bias = bias + extra_bias
        return q_rot[g], ksel_ref[g, pl.ds(k0, CKS), :], values_t(vsel_ref, g, k0, CKS), bias

    def window_job(g, chunk):
        start, n, bias = chunk
        return q_rot[g], kwin_ref[g, pl.ds(start, n), :], values_t(vwin_ref, g, start, n), bias

    stats = attend([selected_job(g, cd, causal_bias) for g in groups]
                   + [window_job(g, chunk) for chunk in win_chunks for g in groups])
    sel_state = tuple(stats[:NSA_GROUPS])
    win_state = list(stats[NSA_GROUPS:2 * NSA_GROUPS])
    for j in range(1, len(win_chunks)):
        for g in groups:
            win_state[g] = merge(win_state[g], stats[NSA_GROUPS * (1 + j) + g])

    def sel_body(it, states):
        new = attend([selected_job(g, cd - 1 - it, None) for g in groups])
        return tuple(merge(states[g], new[g]) for g in groups)

    sel_state = lax.fori_loop(0, cd, sel_body, sel_state)

    gates_t = gate_ref[...].T
    out_rows = []
    for g in groups:
        o_cmp = cmp_out[g][0]
        o_sel = sel_state[g][2] * (1.0 / sel_state[g][1])
        o_win = win_state[g][2] * (1.0 / win_state[g][1])
        for hh in range(hpg):
            h = g * hpg + hh
            sl = slice(hh * tq, (hh + 1) * tq)
            out_rows.append(gates_t[3 * h:3 * h + 1] * o_cmp[:, sl] + gates_t[3 * h + 1:3 * h + 2] * o_sel[:, sl]
                            + gates_t[3 * h + 2:3 * h + 3] * o_win[:, sl])
    o_ref[...] = jnp.concatenate(out_rows, axis=0).T.astype(BF16)


def _nsa(qc, qr, kvc, ksel, vsel_t, kwin, vwin_t, gates, w1, w2, w2t, pos, ovl, drop_bias, B, S):
    T = B * S
    tq = TQ
    nq = S // tq
    q_spec = pl.BlockSpec((NSA_HEADS, tq, HEAD_DIM), lambda b, i: (0, b * nq + i, 0))
    k_spec = pl.BlockSpec((NSA_GROUPS, S, HEAD_DIM), lambda b, i: (0, b, 0))
    v_spec = pl.BlockSpec((NSA_GROUPS, 1, S // KV_LANE_CHUNK, HEAD_DIM, KV_LANE_CHUNK), lambda b, i: (0, b, 0, 0, 0))
    full = lambda a: pl.BlockSpec(a.shape, lambda b, i: (0,) * a.ndim)
    return pl.pallas_call(
        functools.partial(_nsa_kernel, S=S),
        grid=(B, nq),
        in_specs=[q_spec, q_spec,
                  pl.BlockSpec((1, 2 * NSA_GROUPS, N_CMP_PAD, CMP_HALF), lambda b, i: (b, 0, 0, 0)),
                  k_spec, v_spec, k_spec, v_spec,
                  pl.BlockSpec((tq, GATE_PAD), lambda b, i: (b * nq + i, 0)),
                  full(w1), full(w2), full(w2t), full(pos), full(ovl), full(drop_bias)],
        out_specs=pl.BlockSpec((tq, NSA_WIDTH), lambda b, i: (b * nq + i, 0)),
        out_shape=jax.ShapeDtypeStruct((T, NSA_WIDTH), BF16),
        scratch_shapes=[pltpu.VMEM((NSA_GROUPS, N_CMP_PAD, HEAD_DIM), BF16),
                        pltpu.VMEM((NSA_GROUPS, HEAD_DIM, N_CMP_PAD), BF16)],
        compiler_params=pltpu.CompilerParams(dimension_semantics=("arbitrary", "arbitrary"),
                                             vmem_limit_bytes=VMEM_LIMIT),
        name="nsa",
    )(qc, qr, kvc, ksel, vsel_t, kwin, vwin_t, gates, w1, w2, w2t, pos, ovl, drop_bias)


ROUTE_E, ROUTE_R, ROUTE_W = 0, TOP_K, 2 * TOP_K
ROUTE_ROWS = 16


def _merge_kernel(x_ref, onsa_ref, pool_ref, prev_ref, qm_ref, km_ref, vm_ref,
                  gmix_ref, wmg_ref, wpool_ref, pscale_ref, wun_ref, wup_ref, wum_ref, wout_ref,
                  gffn_ref, wrh_ref, wrl_ref, br_ref, tri_ref, lower_ref,
                  x1_ref, hf_ref, route_ref, cnt_ref, omem_s, carry_s, *, S):
    i = pl.program_id(0)
    tm = TM_MERGE
    n_s = S // tm

    @pl.when(i == 0)
    def _init():
        carry_s[...] = jnp.zeros_like(carry_s)

    x = x_ref[...]
    h = _rms(x, gmix_ref[...]).astype(BF16)
    mg = _sigmoid(_dot(h, wmg_ref[...]))

    u = pool_ref[...]
    seq_tile = i % n_s
    prev = jnp.where(seq_tile == 0, 0.0, prev_ref[...])
    ext = jnp.concatenate([prev, u], axis=0)
    b2 = ext[1:] + ext[:-1]
    b4 = b2[2:] + b2[:-2]
    b8 = b4[4:] + b4[:-4]
    b16 = b8[8:] + b8[:-8]
    sums = (b2[POOL_HALO - 1:POOL_HALO - 1 + tm], b4[POOL_HALO - 3:POOL_HALO - 3 + tm],
            b8[POOL_HALO - 7:POOL_HALO - 7 + tm], b16[POOL_HALO - 15:POOL_HALO - 15 + tm])
    t_seq = seq_tile * tm + lax.broadcasted_iota(jnp.int32, (tm, 1), 0)
    lane_p = lax.broadcasted_iota(jnp.int32, (tm, POOL_WIDTH), 1)
    z = jnp.zeros((tm, POOL_WIDTH), F32)
    for gi, w in enumerate(POOL_WINDOWS):
        cnt = jnp.minimum(t_seq + 1, w).astype(F32)
        z = jnp.where(lane_p // POOL_GROUP == gi, sums[gi] / cnt, z)
    z = z - u
    o_pool = (_dot(z.astype(BF16), wpool_ref[...]) * pscale_ref[...]).astype(BF16)

    for hh in range(MEM_HEADS):
        s = _dot_nt(qm_ref[hh], km_ref[0, hh])
        m = jnp.max(s, axis=-1, keepdims=True)
        e = jnp.exp(s - m)
        p = e / jnp.sum(e, axis=-1, keepdims=True)
        omem_s[:, hh * HEAD_DIM:(hh + 1) * HEAD_DIM] = _dot(p.astype(BF16), vm_ref[0, hh]).astype(BF16)

    merged = (mg[:, :D_MODEL] * _dot(onsa_ref[...], wun_ref[...])
              + mg[:, D_MODEL:2 * D_MODEL] * _dot(o_pool, wup_ref[...])
              + mg[:, 2 * D_MODEL:] * _dot(omem_s[...], wum_ref[...]))
    x1 = x + _dot(merged.astype(BF16), wout_ref[...])
    x1_ref[...] = x1
    hf = _rms(x1, gffn_ref[...])
    hf_ref[...] = _pack_bf16_pairs(hf)

    hf_hi = hf.astype(BF16)
    hf_lo = (hf - hf_hi.astype(F32)).astype(BF16)
    logits = (_dot_nt(wrh_ref[...], hf_hi) + _dot_nt(wrl_ref[...], hf_hi) + _dot_nt(wrh_ref[...], hf_lo)
              + br_ref[...])
    erow = lax.broadcasted_iota(jnp.int32, (N_EXPERTS, tm), 0)
    rank = jnp.zeros((N_EXPERTS, tm), F32)
    for jp in range(N_EXPERTS):
        other = logits[jp:jp + 1, :]
        beats = (other > logits) | ((other == logits) & (erow > jp))
        rank = rank + beats.astype(F32)
    chosen = rank < TOP_K
    m = jnp.max(logits, axis=0, keepdims=True)
    e = jnp.where(chosen, jnp.exp(logits - m), 0.0)
    comb = e * (1.0 / jnp.sum(e, axis=0, keepdims=True))

    chosen_b = chosen.astype(BF16)
    carry = carry_s[:, 0:1]
    in_expert = _dot(chosen_b, tri_ref[...]) + carry
    carry_new = carry + jnp.sum(chosen.astype(F32), axis=1, keepdims=True)
    carry_s[...] = jnp.broadcast_to(carry_new, carry_s.shape)
    cnt_ref[...] = jnp.broadcast_to(carry_new, cnt_ref.shape)

    before = _dot(lower_ref[...], chosen_b)
    erow_f = erow.astype(F32)
    fields = {ROUTE_E: erow_f, ROUTE_R: in_expert, ROUTE_W: comb}
    rows = [None] * ROUTE_ROWS
    for k in range(TOP_K):
        pick = chosen & (before == k)
        for base, val in fields.items():
            rows[base + k] = jnp.sum(jnp.where(pick, val, 0.0), axis=0, keepdims=True)
    zero_row = jnp.zeros((1, tm), F32)
    route_ref[...] = jnp.concatenate([zero_row if r is None else r for r in rows], axis=0)


def _merge(x2, onsa, pool_in, qm, km, vm, consts, B, S):
    T = B * S
    tm = TM_MERGE
    n_s = S // tm
    M = km.shape[2]
    halo_per_tile = tm // POOL_HALO
    row = lambda w: pl.BlockSpec((tm, w), lambda i: (i, 0))
    full = lambda a: pl.BlockSpec(a.shape, lambda i: (0,) * a.ndim)
    mem_spec = pl.BlockSpec((1, MEM_HEADS, M, HEAD_DIM), lambda i: (i // n_s, 0, 0, 0))
    return pl.pallas_call(
        functools.partial(_merge_kernel, S=S),
        grid=(T // tm,),
        in_specs=[row(D_MODEL), row(NSA_WIDTH), row(POOL_WIDTH),
                  pl.BlockSpec((POOL_HALO, POOL_WIDTH), lambda i: (jnp.maximum(i * halo_per_tile - 1, 0), 0)),
                  pl.BlockSpec((MEM_HEADS, tm, HEAD_DIM), lambda i: (0, i, 0)),
                  mem_spec, mem_spec] + [full(c) for c in consts],
        out_specs=[row(D_MODEL), row(PACKED), pl.BlockSpec((ROUTE_ROWS, tm), lambda i: (0, i)),
                   pl.BlockSpec((N_EXPERTS, LANES), lambda i: (0, 0))],
        out_shape=[jax.ShapeDtypeStruct((T, D_MODEL), F32),
                   jax.ShapeDtypeStruct((T, PACKED), U32),
                   jax.ShapeDtypeStruct((ROUTE_ROWS, T), F32),
                   jax.ShapeDtypeStruct((N_EXPERTS, LANES), F32)],
        scratch_shapes=[pltpu.VMEM((tm, MEM_WIDTH), BF16), pltpu.VMEM((N_EXPERTS, LANES), F32)],
        compiler_params=pltpu.CompilerParams(dimension_semantics=("arbitrary",), vmem_limit_bytes=VMEM_LIMIT),
        name="merge",
    )(x2, onsa, pool_in, pool_in, qm, km, vm, *consts)


def _sc_mesh():
    return plsc.VectorSubcoreMesh(core_axis_name="c", subcore_axis_name="s")


def _sc_worker():
    return lax.axis_index("s") * SC_CORES + lax.axis_index("c")


def _sc_dispatch(hf, dest, n_rows):
    T, d = hf.shape
    per_worker = T // SC_WORKERS
    steps = per_worker // SC_ROWS

    @functools.partial(pl.kernel, mesh=_sc_mesh(), out_type=jax.ShapeDtypeStruct((n_rows, d), hf.dtype),
                       scratch_types=[pltpu.VMEM((TOP_K, SC_ROWS), jnp.int32), pltpu.VMEM((SC_ROWS, d), hf.dtype),
                                      pltpu.SemaphoreType.DMA])
    def dispatch(hf_hbm, dest_hbm, xs_hbm, idx_v, rows_v, sem):
        base = _sc_worker() * per_worker

        @pl.loop(0, steps)
        def _(j):
            rows = pl.ds(pl.multiple_of(base + j * SC_ROWS, SC_ROWS), SC_ROWS)
            pltpu.sync_copy(hf_hbm.at[rows], rows_v)
            for k in range(TOP_K):
                pltpu.sync_copy(dest_hbm.at[k, rows], idx_v.at[k])
            copies = [pltpu.make_async_copy(rows_v, xs_hbm.at[idx_v.at[k]], sem) for k in range(TOP_K)]
            for c in copies:
                c.start()
            for c in copies:
                c.wait()

    return dispatch(hf, dest)


def _moe_kernel(te_ref, nu_ref, nv_ref, xs_ref, wgu_ref, bgu_ref, wd_ref, bd_ref, ys_ref, wgu_s, wd_s):
    j = pl.program_id(0)
    used = j < nu_ref[0]
    new_expert = (j == 0) | (te_ref[j] != te_ref[jnp.maximum(j - 1, 0)])

    @pl.when(used & new_expert)
    def _cast_weights():
        wgu_s[...] = wgu_ref[0].astype(BF16)
        wd_s[...] = wd_ref[0].astype(BF16)

    @pl.when(used)
    def _compute():
        live = lax.broadcasted_iota(jnp.int32, (xs_ref.shape[0], 1), 0) < nv_ref[j]
        xb = _unpack_bf16_pairs(jnp.where(live, xs_ref[...], jnp.uint32(0))).astype(BF16)
        gu = _dot(xb, wgu_s[...]) + bgu_ref[0]
        gate = jnp.minimum(gu[:, :D_FF], SWIGLU_LIMIT)
        up = jnp.clip(gu[:, D_FF:], -SWIGLU_LIMIT, SWIGLU_LIMIT)
        act = (up + 1.0) * (gate * _sigmoid(SWIGLU_ALPHA * gate))
        ys_ref[...] = _pack_bf16_pairs(_dot(act.astype(BF16), wd_s[...]) + bd_ref[0])

    @pl.when(j >= nu_ref[0])
    def _unused():
        ys_ref[...] = jnp.zeros_like(ys_ref)


def _moe(tile_expert, n_used, n_valid, xs, wgu, bgu, wd, bd):
    P = xs.shape[0]
    tm = TM_E
    grid_spec = pltpu.PrefetchScalarGridSpec(
        num_scalar_prefetch=3,
        grid=(P // tm,),
        in_specs=[pl.BlockSpec((tm, PACKED), lambda j, te, nu, nv: (j, 0)),
                  pl.BlockSpec((1, D_MODEL, 2 * D_FF), lambda j, te, nu, nv: (te[j], 0, 0)),
                  pl.BlockSpec((1, 1, 2 * D_FF), lambda j, te, nu, nv: (te[j], 0, 0)),
                  pl.BlockSpec((1, D_FF, D_MODEL), lambda j, te, nu, nv: (te[j], 0, 0)),
                  pl.BlockSpec((1, 1, D_MODEL), lambda j, te, nu, nv: (te[j], 0, 0))],
        out_specs=pl.BlockSpec((tm, PACKED), lambda j, te, nu, nv: (j, 0)),
        scratch_shapes=[pltpu.VMEM((D_MODEL, 2 * D_FF), BF16), pltpu.VMEM((D_FF, D_MODEL), BF16)],
    )
    return pl.pallas_call(
        _moe_kernel,
        grid_spec=grid_spec,
        out_shape=jax.ShapeDtypeStruct((P, PACKED), U32),
        compiler_params=pltpu.CompilerParams(dimension_semantics=("arbitrary",), vmem_limit_bytes=VMEM_LIMIT),
        name="moe",
    )(tile_expert, n_used, n_valid, xs, wgu, bgu, wd, bd)


def _sc_gather_rows(table, idx):
    n, d = idx.shape[0], table.shape[1]
    per_worker = n // SC_WORKERS
    steps = per_worker // SC_ROWS
    assert steps % 2 == 0

    @functools.partial(pl.kernel, mesh=_sc_mesh(), out_type=jax.ShapeDtypeStruct((n, d), table.dtype),
                       scratch_types=[pltpu.VMEM((2, SC_ROWS), jnp.int32), pltpu.VMEM((2, SC_ROWS, d), table.dtype),
                                      pltpu.SemaphoreType.DMA((2,)), pltpu.SemaphoreType.DMA((2,))])
    def gather(table_hbm, idx_hbm, out_hbm, idx_v, rows_v, gsem, wsem):
        base = _sc_worker() * per_worker

        def rows_at(j):
            return pl.ds(pl.multiple_of(base + j * SC_ROWS, SC_ROWS), SC_ROWS)

        def gather_copy(slot):
            return pltpu.make_async_copy(table_hbm.at[idx_v.at[slot]], rows_v.at[slot], gsem.at[slot])

        def write_copy(j, slot):
            return pltpu.make_async_copy(rows_v.at[slot], out_hbm.at[rows_at(j)], wsem.at[slot])

        def fetch(j, slot):
            pltpu.sync_copy(idx_hbm.at[rows_at(j)], idx_v.at[slot])
            gather_copy(slot).start()

        fetch(0, 0)

        @pl.loop(0, steps, step=2)
        def _(j0):
            for slot in range(2):
                j = j0 + slot
                gather_copy(slot).wait()

                @pl.when(j >= 1)
                def _():
                    write_copy(j - 1, 1 - slot).wait()

                @pl.when(j + 1 < steps)
                def _():
                    fetch(j + 1, 1 - slot)

                write_copy(j, slot).start()

        write_copy(steps - 1, 1).wait()

    return gather(table, idx)


def _final_kernel(x1_ref, yg_ref, route_ref, g_ref, o_ref):
    tm = x1_ref.shape[0]
    route_t = jnp.concatenate([route_ref[...], jnp.zeros((LANES - ROUTE_ROWS, tm), F32)], axis=0).T
    acc = x1_ref[...]
    for k in range(TOP_K):
        acc = acc + route_t[:, ROUTE_W + k:ROUTE_W + k + 1] * _unpack_bf16_pairs(yg_ref[k])
    o_ref[...] = _rms(acc, g_ref[...])


def _final(x1, ys, dest, route, g):
    T = x1.shape[0]
    tm = TM_FINAL
    yg = _sc_gather_rows(ys, dest.reshape(-1)).reshape(TOP_K, T, PACKED)
    return pl.pallas_call(
        _final_kernel,
        grid=(T // tm,),
        in_specs=[pl.BlockSpec((tm, D_MODEL), lambda i: (i, 0)),
                  pl.BlockSpec((TOP_K, tm, PACKED), lambda i: (0, i, 0)),
                  pl.BlockSpec((ROUTE_ROWS, tm), lambda i: (0, i)),
                  pl.BlockSpec((1, D_MODEL), lambda i: (0, 0))],
        out_specs=pl.BlockSpec((tm, D_MODEL), lambda i: (i, 0)),
        out_shape=jax.ShapeDtypeStruct((T, D_MODEL), F32),
        compiler_params=pltpu.CompilerParams(dimension_semantics=("arbitrary",), vmem_limit_bytes=VMEM_LIMIT),
        name="final",
    )(x1, yg, route, g)


def _rope_tables(S):
    half = HEAD_DIM // 2
    inv = ROPE_THETA ** (-jnp.arange(half, dtype=F32) / half)
    ang = jnp.arange(S, dtype=F32)[:, None] * inv[None, :]
    cos = jnp.tile(jnp.cos(ang), (1, LANES // half))
    sin = jnp.tile(jnp.concatenate([-jnp.sin(ang), jnp.sin(ang)], axis=1), (1, LANES // HEAD_DIM))
    return cos, sin


def _selection_constants(S):
    nc = (S - CMP_BLOCK) // CMP_STRIDE + 1
    n_sel = S // SEL_BLOCK
    j = jnp.arange(n_sel)[:, None]
    i = jnp.arange(N_CMP_PAD)[None, :]
    overlap_t = ((i * CMP_STRIDE <= j * SEL_BLOCK + SEL_BLOCK - 1)
                 & (i * CMP_STRIDE + CMP_BLOCK - 1 >= j * SEL_BLOCK) & (i < nc)).astype(BF16)
    row = jnp.arange(LANES)[:, None]
    key = jnp.arange(S)[None, :]
    drop_bias = jnp.where(key // SEL_BLOCK == row, NEG_INF, 0.0).astype(BF16)[:n_sel]
    drop_bias = drop_bias.T.reshape(S // CKS, CKS, n_sel)
    return overlap_t, drop_bias


def kernel(x, mem, norm_mix, norm_mem, w_in, cmp_pos, cmp_w1, cmp_w2, w_pool, pool_scale, w_mem_kv, w_up_nsa,
           w_up_pool, w_up_mem, w_out, norm_ffn, w_router, b_router, w_gate_up, b_gate_up, w_down, b_down,
           norm_final):
    B, S, D = x.shape
    T = B * S
    assert D == D_MODEL and S % CKS == 0 and S // SEL_BLOCK == 32 and T % (SC_WORKERS * SC_ROWS * 2) == 0
    l = 0
    x2 = x.reshape(T, D)

    w = w_in[l]
    o_gate = NSA_WIDTH + 6 * KV_WIDTH
    n_gate = 3 * NSA_HEADS
    o_pool = o_gate + n_gate
    o_qm = o_pool + POOL_WIDTH
    o_mg = o_qm + MEM_WIDTH
    w_a = jnp.concatenate([w[:, :o_gate], w[:, o_pool:o_mg], w[:, o_gate:o_pool],
                           jnp.zeros((D, GATE_PAD - n_gate), F32)], axis=1).astype(BF16)
    w_mg = w[:, o_mg:].astype(BF16)
    cos_t, sin_t = _rope_tables(S)
    overlap_t, drop_bias = _selection_constants(S)
    w1 = cmp_w1[l].reshape(2, CMP_FLAT, HEAD_DIM).astype(BF16)
    w2 = cmp_w2[l].astype(BF16)
    w2t = jnp.swapaxes(w2, 1, 2)
    pos = jnp.broadcast_to(cmp_pos[l].reshape(2, 1, CMP_FLAT), (2, 8, CMP_FLAT)).astype(BF16)
    wpool_bd = jnp.zeros((POOL_WIDTH, POOL_WIDTH), F32)
    for gi in range(len(POOL_WINDOWS)):
        wpool_bd = wpool_bd.at[gi * POOL_GROUP:(gi + 1) * POOL_GROUP, gi * POOL_GROUP:(gi + 1) * POOL_GROUP].set(w_pool[l, gi])
    wr = w_router[l].T
    wr_hi = wr.astype(BF16)
    wr_lo = (wr - wr_hi.astype(F32)).astype(BF16)
    br = b_router[l].reshape(N_EXPERTS, 1)
    tri = (jnp.arange(TM_MERGE)[:, None] < jnp.arange(TM_MERGE)[None, :]).astype(BF16)
    lower = (jnp.arange(N_EXPERTS)[None, :] < jnp.arange(N_EXPERTS)[:, None]).astype(BF16)

    km, vm = _memkv(mem, norm_mem[l].reshape(1, D), w_mem_kv[l].astype(BF16))
    qc, qr, kvcmp, ksel, vsel, kwin, vwin, pool_in, qm, gates = _inproj(
        x2, norm_mix[l].reshape(1, D), w_a, cos_t, sin_t, S)
    kvc = kvcmp.reshape(B, S // CMP_STRIDE, CMP_STRIDE, 2 * NSA_GROUPS, HEAD_DIM).transpose(0, 3, 1, 2, 4)
    kvc = kvc.reshape(B, 2 * NSA_GROUPS, S // CMP_STRIDE, CMP_HALF)
    to_slabs = lambda v: jnp.swapaxes(v.reshape(NSA_GROUPS, B, S // KV_LANE_CHUNK, KV_LANE_CHUNK, HEAD_DIM), 3, 4)
    o_nsa = _nsa(qc, qr, kvc, ksel, to_slabs(vsel), kwin, to_slabs(vwin), gates, w1, w2, w2t, pos, overlap_t,
                 drop_bias, B, S)
    consts = [norm_mix[l].reshape(1, D), w_mg, wpool_bd.astype(BF16), pool_scale[l].reshape(1, POOL_WIDTH),
              w_up_nsa[l].astype(BF16), w_up_pool[l].astype(BF16), w_up_mem[l].astype(BF16), w_out[l].astype(BF16),
              norm_ffn[l].reshape(1, D), wr_hi, wr_lo, br, tri, lower]
    x1, hf, route, counts = _merge(x2, o_nsa, pool_in, qm, km, vm, consts, B, S)

    counts = counts[:, 0].astype(jnp.int32)
    padded = ((counts + TM_E - 1) // TM_E) * TM_E
    ends = jnp.cumsum(padded)
    starts = ends - padded
    n_tiles = (T * TOP_K) // TM_E + N_EXPERTS
    P = n_tiles * TM_E
    e_k = route[ROUTE_E:ROUTE_E + TOP_K].astype(jnp.int32)
    r_k = route[ROUTE_R:ROUTE_R + TOP_K].astype(jnp.int32)
    group_start = jnp.zeros_like(e_k)
    for e in range(N_EXPERTS):
        group_start = jnp.where(e_k == e, starts[e], group_start)
    dest = group_start + r_k
    tile_start = jnp.arange(n_tiles, dtype=jnp.int32) * TM_E
    tile_expert = jnp.minimum(jnp.sum(tile_start[:, None] >= ends[None, :], axis=1), N_EXPERTS - 1).astype(jnp.int32)
    n_used = (ends[-1] // TM_E).astype(jnp.int32).reshape(1)
    n_valid = jnp.clip((starts + counts)[tile_expert] - tile_start, 0, TM_E).astype(jnp.int32)

    xs = _sc_dispatch(hf, dest, P)
    ys = _moe(tile_expert, n_used, n_valid, xs, w_gate_up[l], b_gate_up[l].reshape(N_EXPERTS, 1, 2 * D_FF),
              w_down[l], b_down[l].reshape(N_EXPERTS, 1, D_MODEL))
    out = _final(x1, ys, dest, route, norm_final.reshape(1, D))
    return out.reshape(B, S, D)
```

```python
import functools

import jax
import jax.numpy as jnp
from jax import lax
from jax.experimental import pallas as pl
from jax.experimental.pallas import tpu as pltpu
from jax.experimental.pallas import tpu_sc as plsc

F32 = jnp.float32
BF16 = jnp.bfloat16
U32 = jnp.uint32

D_MODEL = 1024
HEAD_DIM = 64
NSA_HEADS = 8
NSA_GROUPS = 2
HEADS_PER_GROUP = NSA_HEADS // NSA_GROUPS
NSA_WIDTH = NSA_HEADS * HEAD_DIM
KV_WIDTH = NSA_GROUPS * HEAD_DIM
CMP_BLOCK = 32
CMP_STRIDE = 16
SEL_BLOCK = 64
SEL_TOPN = 8
FORCE_BONUS = 1000.0
WINDOW = 512
POOL_WINDOWS = (2, 4, 8, 16)
POOL_GROUP = 64
POOL_WIDTH = POOL_GROUP * len(POOL_WINDOWS)
POOL_HALO = 16
MEM_HEADS = 4
MEM_WIDTH = MEM_HEADS * HEAD_DIM
N_EXPERTS = 32
TOP_K = 4
D_FF = 1024
SWIGLU_LIMIT = 7.0
SWIGLU_ALPHA = 1.702
ROPE_THETA = 10000.0
EPS = 1e-5
NEG_INF = -1e30
TINY = 1e-30
QK_SCALE = HEAD_DIM ** -0.5
LOG2_E = 1.4426950408889634
NSA_Q_SCALE = QK_SCALE * LOG2_E

LANES = 128
GATE_PAD = LANES
PACKED = D_MODEL // 2

TM_IN = 512
TQ = 256
CKS = 512
WIN_CHUNKS = (256, 256, 256)
TM_MERGE = 512
TM_E = 512
TM_FINAL = 512
N_CHUNKS = 2
SC_CORES = 2
SC_SUBCORES = 16
SC_WORKERS = SC_CORES * SC_SUBCORES
SC_ROWS = 64
VMEM_LIMIT = 56 * 1024 * 1024


def _rms(x, g):
    return x * lax.rsqrt(jnp.mean(x * x, axis=-1, keepdims=True) + EPS) * g


def _sigmoid(x):
    return 0.5 * jnp.tanh(0.5 * x) + 0.5


def _dot(a, b):
    return jnp.dot(a, b, preferred_element_type=F32)


def _pack_bf16_pairs(v):
    n = v.shape[1] // 2
    r = v.astype(BF16).astype(F32)
    lo = pltpu.bitcast(r[:, :n], U32) >> 16
    hi = pltpu.bitcast(r[:, n:], U32) & jnp.uint32(0xFFFF0000)
    return lo | hi


def _unpack_bf16_pairs(w):
    lo = pltpu.bitcast(w << 16, F32)
    hi = pltpu.bitcast(w & jnp.uint32(0xFFFF0000), F32)
    return jnp.concatenate([lo, hi], axis=1)


def _dot_nt(a, b):
    return lax.dot_general(a, b, (((1,), (1,)), ((), ())), preferred_element_type=F32)


def _memkv_kernel(mem_ref, g_ref, w_ref, k_ref, v_ref):
    m = _rms(mem_ref[0], g_ref[...]).astype(BF16)
    kv = _dot(m, w_ref[...])
    for h in range(MEM_HEADS):
        k_ref[0, h] = kv[:, h * HEAD_DIM:(h + 1) * HEAD_DIM].astype(BF16)
        v_ref[0, h] = kv[:, MEM_WIDTH + h * HEAD_DIM:MEM_WIDTH + (h + 1) * HEAD_DIM].astype(BF16)


def _memkv(mem, g, w):
    B, M, D = mem.shape
    return pl.pallas_call(
        _memkv_kernel,
        grid=(B,),
        in_specs=[pl.BlockSpec((1, M, D), lambda b: (b, 0, 0)),
                  pl.BlockSpec((1, D), lambda b: (0, 0)),
                  pl.BlockSpec((D, 2 * MEM_WIDTH), lambda b: (0, 0))],
        out_specs=[pl.BlockSpec((1, MEM_HEADS, M, HEAD_DIM), lambda b: (b, 0, 0, 0)),
                   pl.BlockSpec((1, MEM_HEADS, M, HEAD_DIM), lambda b: (b, 0, 0, 0))],
        out_shape=[jax.ShapeDtypeStruct((B, MEM_HEADS, M, HEAD_DIM), BF16),
                   jax.ShapeDtypeStruct((B, MEM_HEADS, M, HEAD_DIM), BF16)],
        compiler_params=pltpu.CompilerParams(dimension_semantics=("arbitrary",), vmem_limit_bytes=VMEM_LIMIT),
        name="memkv",
    )(mem, g, w)


IN_COLS = NSA_WIDTH + 6 * KV_WIDTH + POOL_WIDTH + MEM_WIDTH + GATE_PAD


def _inproj_kernel(x_ref, g_ref, w_ref, cos_ref, sin_ref,
                   qc_ref, qr_ref, kvc_ref, ksel_ref, vsel_ref, kwin_ref, vwin_ref, pool_ref, qm_ref, gate_ref):
    h = _rms(x_ref[...], g_ref[...]).astype(BF16)
    p = _dot(h, w_ref[...])
    cos = cos_ref[...]
    sin = sin_ref[...]
    lane = lax.broadcasted_iota(jnp.int32, cos.shape, 1)
    first_half = (lane % HEAD_DIM) < (HEAD_DIM // 2)

    def rope(c):
        partner = jnp.where(first_half, pltpu.roll(c, LANES - HEAD_DIM // 2, 1), pltpu.roll(c, HEAD_DIM // 2, 1))
        return c * cos + partner * sin

    def halves(c):
        return c[:, :HEAD_DIM], c[:, HEAD_DIM:]

    for j in range(NSA_WIDTH // LANES):
        c = p[:, j * LANES:(j + 1) * LANES]
        r = rope(c)
        for hh, (cc, rr) in enumerate(zip(halves(c), halves(r))):
            qc_ref[2 * j + hh] = (cc * NSA_Q_SCALE).astype(BF16)
            qr_ref[2 * j + hh] = (rr * NSA_Q_SCALE).astype(BF16)
    o = NSA_WIDTH
    kvc_ref[...] = p[:, o:o + 2 * KV_WIDTH]
    o += 2 * KV_WIDTH
    for ref, rot in ((ksel_ref, True), (vsel_ref, False), (kwin_ref, True), (vwin_ref, False)):
        c = p[:, o:o + KV_WIDTH]
        if rot:
            c = rope(c)
        for g, cc in enumerate(halves(c)):
            ref[g] = cc.astype(BF16)
        o += KV_WIDTH
    pool_ref[...] = p[:, o:o + POOL_WIDTH]
    o += POOL_WIDTH
    for hh in range(MEM_HEADS):
        qm_ref[hh] = (p[:, o + hh * HEAD_DIM:o + (hh + 1) * HEAD_DIM] * QK_SCALE).astype(BF16)
    o += MEM_WIDTH
    gate_ref[...] = _sigmoid(p[:, o:o + GATE_PAD])


def _inproj(x2, g, w_a, cos_t, sin_t, S):
    T = x2.shape[0]
    tm = TM_IN
    n_s = S // tm
    head_spec = lambda n: pl.BlockSpec((n, tm, HEAD_DIM), lambda i: (0, i, 0))
    row_spec = lambda w: pl.BlockSpec((tm, w), lambda i: (i, 0))
    return pl.pallas_call(
        _inproj_kernel,
        grid=(T // tm,),
        in_specs=[row_spec(D_MODEL),
                  pl.BlockSpec((1, D_MODEL), lambda i: (0, 0)),
                  pl.BlockSpec((D_MODEL, IN_COLS), lambda i: (0, 0)),
                  pl.BlockSpec((tm, LANES), lambda i: (i % n_s, 0)),
                  pl.BlockSpec((tm, LANES), lambda i: (i % n_s, 0))],
        out_specs=[head_spec(NSA_HEADS), head_spec(NSA_HEADS), row_spec(2 * KV_WIDTH),
                   head_spec(NSA_GROUPS), head_spec(NSA_GROUPS), head_spec(NSA_GROUPS), head_spec(NSA_GROUPS),
                   row_spec(POOL_WIDTH), head_spec(MEM_HEADS), row_spec(GATE_PAD)],
        out_shape=[jax.ShapeDtypeStruct((NSA_HEADS, T, HEAD_DIM), BF16),
                   jax.ShapeDtypeStruct((NSA_HEADS, T, HEAD_DIM), BF16),
                   jax.ShapeDtypeStruct((T, 2 * KV_WIDTH), F32),
                   jax.ShapeDtypeStruct((NSA_GROUPS, T, HEAD_DIM), BF16),
                   jax.ShapeDtypeStruct((NSA_GROUPS, T, HEAD_DIM), BF16),
                   jax.ShapeDtypeStruct((NSA_GROUPS, T, HEAD_DIM), BF16),
                   jax.ShapeDtypeStruct((NSA_GROUPS, T, HEAD_DIM), BF16),
                   jax.ShapeDtypeStruct((T, POOL_WIDTH), F32),
                   jax.ShapeDtypeStruct((MEM_HEADS, T, HEAD_DIM), BF16),
                   jax.ShapeDtypeStruct((T, GATE_PAD), F32)],
        compiler_params=pltpu.CompilerParams(dimension_semantics=("arbitrary",), vmem_limit_bytes=VMEM_LIMIT),
        name="inproj",
    )(x2, g, w_a, cos_t, sin_t)


N_CMP_PAD = 128
CMP_FLAT = CMP_BLOCK * HEAD_DIM
CMP_HALF = CMP_STRIDE * HEAD_DIM


KV_LANE_CHUNK = LANES


def _nsa_kernel(qc_ref, qr_ref, kvc_ref, ksel_ref, vsel_ref, kwin_ref, vwin_ref, gate_ref,
                w1_ref, w2_ref, w2t_ref, pos_ref, ovl_ref, drop_ref, o_ref, kc_s, vct_s, *, S):
    i = pl.program_id(1)
    tq = TQ
    hpg = HEADS_PER_GROUP
    n_sel = S // SEL_BLOCK
    hq = hpg * tq

    @pl.when(i == 0)
    def _compress():
        for kv in range(2):
            w1 = w1_ref[kv]
            posterm = _dot(pos_ref[kv], w1)[0:1]
            for g in range(NSA_GROUPS):
                a = kvc_ref[0, kv * NSA_GROUPS + g].astype(BF16)
                p1 = _dot(a, w1[:CMP_HALF])
                p2 = _dot(a, w1[CMP_HALF:])
                hid = p1 + pltpu.roll(p2, N_CMP_PAD - 1, 0) + posterm
                hid = (hid * _sigmoid(hid)).astype(BF16)
                if kv == 0:
                    kc_s[g] = _dot(hid, w2_ref[kv]).astype(BF16)
                else:
                    vct_s[g] = _dot_nt(w2t_ref[kv], hid).astype(BF16)

    q0 = i * tq
    t_lane = q0 + lax.broadcasted_iota(jnp.int32, (1, tq), 1)

    def key_pos(start, n):
        return start + lax.broadcasted_iota(jnp.int32, (n, 1), 0)

    cmp_valid = (key_pos(0, N_CMP_PAD) * CMP_STRIDE + CMP_BLOCK - 1) <= t_lane
    jrow = lax.broadcasted_iota(jnp.int32, (n_sel, tq), 0)
    cur = t_lane // SEL_BLOCK
    sel_valid = jrow * SEL_BLOCK <= t_lane
    forced = (jrow == 0) | (jrow == cur) | (jrow == cur - 1)

    cd = q0 // CKS
    causal_bias = jnp.where(key_pos(cd * CKS, CKS) <= t_lane, 0.0, NEG_INF)
    win_chunks = []
    hi = q0 + tq
    for n in WIN_CHUNKS:
        lo = hi - n
        start = pl.multiple_of(jnp.maximum(lo, 0), LANES)
        kp = key_pos(start, n)
        diff = t_lane - kp
        win_chunks.append((start, n, jnp.where((diff >= 0) & (diff < WINDOW) & (kp < hi), 0.0, NEG_INF)))
        hi = lo

    def values_t(ref, g, start, n):
        c0 = start // KV_LANE_CHUNK
        return jnp.concatenate([ref[g, 0, c0 + j] for j in range(n // KV_LANE_CHUNK)], axis=1)

    def attend(jobs):
        scores = [_dot_nt(k, q_all) for q_all, k, _, _ in jobs]
        heads = [slice(hh * tq, (hh + 1) * tq) for hh in range(hpg)]
        maxes = [jnp.concatenate([jnp.max(s_all[:, sl] + bias, axis=0, keepdims=True) for sl in heads], axis=1)
                 for (_, _, _, bias), s_all in zip(jobs, scores)]
        soft = []
        for (_, _, _, bias), s_all, m in zip(jobs, scores, maxes):
            ps = [jnp.exp2(s_all[:, sl] + bias - m[:, sl]) for sl in heads]
            l = jnp.concatenate([jnp.sum(p, axis=0, keepdims=True) for p in ps], axis=1)
            soft.append((l, jnp.concatenate([p.astype(BF16) for p in ps], axis=1)))
        return [(m, l, _dot(v_t, p)) for (_, _, v_t, _), m, (l, p) in zip(jobs, maxes, soft)]

    def merge(a, b):
        m = jnp.maximum(a[0], b[0])
        wa = jnp.exp2(a[0] - m)
        wb = jnp.exp2(b[0] - m)
        return m, wa * a[1] + wb * b[1], wa * a[2] + wb * b[2]

    def select_blocks(g):
        q_cmp = qc_ref[g * hpg:(g + 1) * hpg].reshape(hq, HEAD_DIM)
        s_all = _dot_nt(kc_s[g], q_cmp)
        p_grp = jnp.zeros((N_CMP_PAD, tq), F32)
        ps = []
        for hh in range(hpg):
            sl = slice(hh * tq, (hh + 1) * tq)
            s = jnp.where(cmp_valid, s_all[:, sl], NEG_INF)
            m = jnp.max(s, axis=0, keepdims=True)
            e = jnp.where(cmp_valid, jnp.exp2(s - m), 0.0)
            p = e * (1.0 / jnp.maximum(jnp.sum(e, axis=0, keepdims=True), TINY))
            p_grp = p_grp + p
            ps.append(p.astype(BF16))
        o_cmp = _dot(vct_s[g], jnp.concatenate(ps, axis=1))

        ovl = ovl_ref[...]
        p_hi = p_grp.astype(BF16)
        r1 = p_grp - p_hi.astype(F32)
        p_mid = r1.astype(BF16)
        p_lo = (r1 - p_mid.astype(F32)).astype(BF16)
        score = _dot(ovl, p_hi) + _dot(ovl, p_mid) + _dot(ovl, p_lo)
        score = jnp.where(sel_valid, score + jnp.where(forced, FORCE_BONUS, 0.0), -1.0)
        rank = jnp.zeros((n_sel, tq), F32)
        for jp in range(n_sel):
            other = score[jp:jp + 1, :]
            beats = (other > score) | ((other == score) & (jrow > jp))
            rank = rank + beats.astype(F32)
        return o_cmp, (rank >= SEL_TOPN).astype(BF16)

    groups = range(NSA_GROUPS)
    cmp_out = [select_blocks(g) for g in groups]
    q_rot = [qr_ref[g * hpg:(g + 1) * hpg].reshape(hq, HEAD_DIM) for g in groups]

    def selected_job(g, c, extra_bias):
        k0 = pl.multiple_of(c * CKS, CKS)
        bias = _dot(drop_ref[c], cmp_out[g][1])
        if extra_bias is not None:
            bias = bias + extra_bias
        return q_rot[g], ksel_ref[g, pl.ds(k0, CKS), :], values_t(vsel_ref, g, k0, CKS), bias

    def window_job(g, chunk):
        start, n, bias = chunk
        return q_rot[g], kwin_ref[g, pl.ds(start, n), :], values_t(vwin_ref, g, start, n), bias

    stats = attend([selected_job(g, cd, causal_bias) for g in groups]
                   + [window_job(g, chunk) for chunk in win_chunks for g in groups])
    sel_state = tuple(stats[:NSA_GROUPS])
    win_state = list(stats[NSA_GROUPS:2 * NSA_GROUPS])
    for j in range(1, len(win_chunks)):
        for g in groups:
            win_state[g] = merge(win_state[g], stats[NSA_GROUPS * (1 + j) + g])

    def sel_body(it, states):
        new = attend([selected_job(g, cd - 1 - it, None) for g in groups])
        return tuple(merge(states[g], new[g]) for g in groups)

    sel_state = lax.fori_loop(0, cd, sel_body, sel_state)

    gates_t = gate_ref[...].T
    out_rows = []
    for g in groups:
        o_cmp = cmp_out[g][0]
        o_sel = sel_state[g][2] * (1.0 / sel_state[g][1])
        o_win = win_state[g][2] * (1.0 / win_state[g][1])
        for hh in range(hpg):
            h = g * hpg + hh
            sl = slice(hh * tq, (hh + 1) * tq)
            out_rows.append(gates_t[3 * h:3 * h + 1] * o_cmp[:, sl] + gates_t[3 * h + 1:3 * h + 2] * o_sel[:, sl]
                            + gates_t[3 * h + 2:3 * h + 3] * o_win[:, sl])
    o_ref[...] = jnp.concatenate(out_rows, axis=0).T.astype(BF16)


def _nsa(qc, qr, kvc, ksel, vsel_t, kwin, vwin_t, gates, w1, w2, w2t, pos, ovl, drop_bias, B, S):
    T = B * S
    tq = TQ
    nq = S // tq
    q_spec = pl.BlockSpec((NSA_HEADS, tq, HEAD_DIM), lambda b, i: (0, b * nq + i, 0))
    k_spec = pl.BlockSpec((NSA_GROUPS, S, HEAD_DIM), lambda b, i: (0, b, 0))
    v_spec = pl.BlockSpec((NSA_GROUPS, 1, S // KV_LANE_CHUNK, HEAD_DIM, KV_LANE_CHUNK), lambda b, i: (0, b, 0, 0, 0))
    full = lambda a: pl.BlockSpec(a.shape, lambda b, i: (0,) * a.ndim)
    return pl.pallas_call(
        functools.partial(_nsa_kernel, S=S),
        grid=(B, nq),
        in_specs=[q_spec, q_spec,
                  pl.BlockSpec((1, 2 * NSA_GROUPS, N_CMP_PAD, CMP_HALF), lambda b, i: (b, 0, 0, 0)),
                  k_spec, v_spec, k_spec, v_spec,
                  pl.BlockSpec((tq, GATE_PAD), lambda b, i: (b * nq + i, 0)),
                  full(w1), full(w2), full(w2t), full(pos), full(ovl), full(drop_bias)],
        out_specs=pl.BlockSpec((tq, NSA_WIDTH), lambda b, i: (b * nq + i, 0)),
        out_shape=jax.ShapeDtypeStruct((T, NSA_WIDTH), BF16),
        scratch_shapes=[pltpu.VMEM((NSA_GROUPS, N_CMP_PAD, HEAD_DIM), BF16),
                        pltpu.VMEM((NSA_GROUPS, HEAD_DIM, N_CMP_PAD), BF16)],
        compiler_params=pltpu.CompilerParams(dimension_semantics=("arbitrary", "arbitrary"),
                                             vmem_limit_bytes=VMEM_LIMIT),
        name="nsa",
    )(qc, qr, kvc, ksel, vsel_t, kwin, vwin_t, gates, w1, w2, w2t, pos, ovl, drop_bias)


ROUTE_E, ROUTE_R, ROUTE_W = 0, TOP_K, 2 * TOP_K
ROUTE_ROWS = 16


def _merge_kernel(x_ref, onsa_ref, pool_ref, prev_ref, qm_ref, km_ref, vm_ref,
                  gmix_ref, wmg_ref, wpool_ref, pscale_ref, wun_ref, wup_ref, wum_ref, wout_ref,
                  gffn_ref, wrh_ref, wrl_ref, br_ref, tri_ref, lower_ref,
                  x1_ref, hf_ref, route_ref, cnt_ref, omem_s, carry_s, *, S, tile0):
    i = pl.program_id(0) + tile0
    tm = TM_MERGE
    n_s = S // tm

    @pl.when(pl.program_id(0) == 0)
    def _init():
        carry_s[...] = jnp.zeros_like(carry_s)

    x = x_ref[...]
    h = _rms(x, gmix_ref[...]).astype(BF16)
    mg = _sigmoid(_dot(h, wmg_ref[...]))

    u = pool_ref[...]
    seq_tile = i % n_s
    prev = jnp.where(seq_tile == 0, 0.0, prev_ref[...])
    ext = jnp.concatenate([prev, u], axis=0)
    b2 = ext[1:] + ext[:-1]
    b4 = b2[2:] + b2[:-2]
    b8 = b4[4:] + b4[:-4]
    b16 = b8[8:] + b8[:-8]
    sums = (b2[POOL_HALO - 1:POOL_HALO - 1 + tm], b4[POOL_HALO - 3:POOL_HALO - 3 + tm],
            b8[POOL_HALO - 7:POOL_HALO - 7 + tm], b16[POOL_HALO - 15:POOL_HALO - 15 + tm])
    t_seq = seq_tile * tm + lax.broadcasted_iota(jnp.int32, (tm, 1), 0)
    lane_p = lax.broadcasted_iota(jnp.int32, (tm, POOL_WIDTH), 1)
    z = jnp.zeros((tm, POOL_WIDTH), F32)
    for gi, w in enumerate(POOL_WINDOWS):
        cnt = jnp.minimum(t_seq + 1, w).astype(F32)
        z = jnp.where(lane_p // POOL_GROUP == gi, sums[gi] / cnt, z)
    z = z - u
    o_pool = (_dot(z.astype(BF16), wpool_ref[...]) * pscale_ref[...]).astype(BF16)

    for hh in range(MEM_HEADS):
        s = _dot_nt(qm_ref[hh], km_ref[0, hh])
        m = jnp.max(s, axis=-1, keepdims=True)
        e = jnp.exp(s - m)
        p = e / jnp.sum(e, axis=-1, keepdims=True)
        omem_s[:, hh * HEAD_DIM:(hh + 1) * HEAD_DIM] = _dot(p.astype(BF16), vm_ref[0, hh]).astype(BF16)

    merged = (mg[:, :D_MODEL] * _dot(onsa_ref[...], wun_ref[...])
              + mg[:, D_MODEL:2 * D_MODEL] * _dot(o_pool, wup_ref[...])
              + mg[:, 2 * D_MODEL:] * _dot(omem_s[...], wum_ref[...]))
    x1 = x + _dot(merged.astype(BF16), wout_ref[...])
    x1_ref[...] = x1
    hf = _rms(x1, gffn_ref[...])
    hf_ref[...] = _pack_bf16_pairs(hf)

    hf_hi = hf.astype(BF16)
    hf_lo = (hf - hf_hi.astype(F32)).astype(BF16)
    logits = (_dot_nt(wrh_ref[...], hf_hi) + _dot_nt(wrl_ref[...], hf_hi) + _dot_nt(wrh_ref[...], hf_lo)
              + br_ref[...])
    erow = lax.broadcasted_iota(jnp.int32, (N_EXPERTS, tm), 0)
    rank = jnp.zeros((N_EXPERTS, tm), F32)
    for jp in range(N_EXPERTS):
        other = logits[jp:jp + 1, :]
        beats = (other > logits) | ((other == logits) & (erow > jp))
        rank = rank + beats.astype(F32)
    chosen = rank < TOP_K
    m = jnp.max(logits, axis=0, keepdims=True)
    e = jnp.where(chosen, jnp.exp(logits - m), 0.0)
    comb = e * (1.0 / jnp.sum(e, axis=0, keepdims=True))

    chosen_b = chosen.astype(BF16)
    carry = carry_s[:, 0:1]
    in_expert = _dot(chosen_b, tri_ref[...]) + carry
    carry_new = carry + jnp.sum(chosen.astype(F32), axis=1, keepdims=True)
    carry_s[...] = jnp.broadcast_to(carry_new, carry_s.shape)
    cnt_ref[...] = jnp.broadcast_to(carry_new, cnt_ref.shape)

    before = _dot(lower_ref[...], chosen_b)
    erow_f = erow.astype(F32)
    fields = {ROUTE_E: erow_f, ROUTE_R: in_expert, ROUTE_W: comb}
    rows = [None] * ROUTE_ROWS
    for k in range(TOP_K):
        pick = chosen & (before == k)
        for base, val in fields.items():
            rows[base + k] = jnp.sum(jnp.where(pick, val, 0.0), axis=0, keepdims=True)
    zero_row = jnp.zeros((1, tm), F32)
    route_ref[...] = jnp.concatenate([zero_row if r is None else r for r in rows], axis=0)


def _merge(x2, onsa, pool_in, qm, km, vm, consts, S, tile0, n_tiles):
    tm = TM_MERGE
    Tc = n_tiles * tm
    n_s = S // tm
    M = km.shape[2]
    halo_per_tile = tm // POOL_HALO
    row = lambda w: pl.BlockSpec((tm, w), lambda i: (i + tile0, 0))
    out_row = lambda w: pl.BlockSpec((tm, w), lambda i: (i, 0))
    full = lambda a: pl.BlockSpec(a.shape, lambda i: (0,) * a.ndim)
    mem_spec = pl.BlockSpec((1, MEM_HEADS, M, HEAD_DIM), lambda i: ((i + tile0) // n_s, 0, 0, 0))
    return pl.pallas_call(
        functools.partial(_merge_kernel, S=S, tile0=tile0),
        grid=(n_tiles,),
        in_specs=[row(D_MODEL), row(NSA_WIDTH), row(POOL_WIDTH),
                  pl.BlockSpec((POOL_HALO, POOL_WIDTH),
                               lambda i: (jnp.maximum((i + tile0) * halo_per_tile - 1, 0), 0)),
                  pl.BlockSpec((MEM_HEADS, tm, HEAD_DIM), lambda i: (0, i + tile0, 0)),
                  mem_spec, mem_spec] + [full(c) for c in consts],
        out_specs=[out_row(D_MODEL), out_row(PACKED), pl.BlockSpec((ROUTE_ROWS, tm), lambda i: (0, i)),
                   pl.BlockSpec((N_EXPERTS, LANES), lambda i: (0, 0))],
        out_shape=[jax.ShapeDtypeStruct((Tc, D_MODEL), F32),
                   jax.ShapeDtypeStruct((Tc, PACKED), U32),
                   jax.ShapeDtypeStruct((ROUTE_ROWS, Tc), F32),
                   jax.ShapeDtypeStruct((N_EXPERTS, LANES), F32)],
        scratch_shapes=[pltpu.VMEM((tm, MEM_WIDTH), BF16), pltpu.VMEM((N_EXPERTS, LANES), F32)],
        compiler_params=pltpu.CompilerParams(dimension_semantics=("arbitrary",), vmem_limit_bytes=VMEM_LIMIT),
        name="merge",
    )(x2, onsa, pool_in, pool_in, qm, km, vm, *consts)


def _sc_mesh():
    return plsc.VectorSubcoreMesh(core_axis_name="c", subcore_axis_name="s")


def _sc_worker():
    return lax.axis_index("s") * SC_CORES + lax.axis_index("c")


def _sc_dispatch(hf, dest, n_rows):
    T, d = hf.shape
    per_worker = T // SC_WORKERS
    steps = per_worker // SC_ROWS

    @functools.partial(pl.kernel, mesh=_sc_mesh(), out_type=jax.ShapeDtypeStruct((n_rows, d), hf.dtype),
                       scratch_types=[pltpu.VMEM((TOP_K, SC_ROWS), jnp.int32), pltpu.VMEM((SC_ROWS, d), hf.dtype),
                                      pltpu.SemaphoreType.DMA])
    def dispatch(hf_hbm, dest_hbm, xs_hbm, idx_v, rows_v, sem):
        base = _sc_worker() * per_worker

        @pl.loop(0, steps)
        def _(j):
            rows = pl.ds(pl.multiple_of(base + j * SC_ROWS, SC_ROWS), SC_ROWS)
            pltpu.sync_copy(hf_hbm.at[rows], rows_v)
            for k in range(TOP_K):
                pltpu.sync_copy(dest_hbm.at[k, rows], idx_v.at[k])
            copies = [pltpu.make_async_copy(rows_v, xs_hbm.at[idx_v.at[k]], sem) for k in range(TOP_K)]
            for c in copies:
                c.start()
            for c in copies:
                c.wait()

    return dispatch(hf, dest)


def _moe_kernel(te_ref, nu_ref, nv_ref, xs_ref, wgu_ref, bgu_ref, wd_ref, bd_ref, ys_ref, wgu_s, wd_s):
    j = pl.program_id(0)
    used = j < nu_ref[0]
    new_expert = (j == 0) | (te_ref[j] != te_ref[jnp.maximum(j - 1, 0)])

    @pl.when(used & new_expert)
    def _cast_weights():
        wgu_s[...] = wgu_ref[0].astype(BF16)
        wd_s[...] = wd_ref[0].astype(BF16)

    @pl.when(used)
    def _compute():
        live = lax.broadcasted_iota(jnp.int32, (xs_ref.shape[0], 1), 0) < nv_ref[j]
        xb = _unpack_bf16_pairs(jnp.where(live, xs_ref[...], jnp.uint32(0))).astype(BF16)
        gu = _dot(xb, wgu_s[...]) + bgu_ref[0]
        gate = jnp.minimum(gu[:, :D_FF], SWIGLU_LIMIT)
        up = jnp.clip(gu[:, D_FF:], -SWIGLU_LIMIT, SWIGLU_LIMIT)
        act = (up + 1.0) * (gate * _sigmoid(SWIGLU_ALPHA * gate))
        ys_ref[...] = _pack_bf16_pairs(_dot(act.astype(BF16), wd_s[...]) + bd_ref[0])

    @pl.when(j >= nu_ref[0])
    def _unused():
        ys_ref[...] = jnp.zeros_like(ys_ref)


def _moe(tile_expert, n_used, n_valid, xs, wgu, bgu, wd, bd):
    P = xs.shape[0]
    tm = TM_E
    grid_spec = pltpu.PrefetchScalarGridSpec(
        num_scalar_prefetch=3,
        grid=(P // tm,),
        in_specs=[pl.BlockSpec((tm, PACKED), lambda j, te, nu, nv: (j, 0)),
                  pl.BlockSpec((1, D_MODEL, 2 * D_FF), lambda j, te, nu, nv: (te[j], 0, 0)),
                  pl.BlockSpec((1, 1, 2 * D_FF), lambda j, te, nu, nv: (te[j], 0, 0)),
                  pl.BlockSpec((1, D_FF, D_MODEL), lambda j, te, nu, nv: (te[j], 0, 0)),
                  pl.BlockSpec((1, 1, D_MODEL), lambda j, te, nu, nv: (te[j], 0, 0))],
        out_specs=pl.BlockSpec((tm, PACKED), lambda j, te, nu, nv: (j, 0)),
        scratch_shapes=[pltpu.VMEM((D_MODEL, 2 * D_FF), BF16), pltpu.VMEM((D_FF, D_MODEL), BF16)],
    )
    return pl.pallas_call(
        _moe_kernel,
        grid_spec=grid_spec,
        out_shape=jax.ShapeDtypeStruct((P, PACKED), U32),
        compiler_params=pltpu.CompilerParams(dimension_semantics=("arbitrary",), vmem_limit_bytes=VMEM_LIMIT),
        name="moe",
    )(tile_expert, n_used, n_valid, xs, wgu, bgu, wd, bd)


def _sc_gather_rows(table, idx):
    n, d = idx.shape[0], table.shape[1]
    per_worker = n // SC_WORKERS
    steps = per_worker // SC_ROWS
    assert steps % 2 == 0

    @functools.partial(pl.kernel, mesh=_sc_mesh(), out_type=jax.ShapeDtypeStruct((n, d), table.dtype),
                       scratch_types=[pltpu.VMEM((2, SC_ROWS), jnp.int32), pltpu.VMEM((2, SC_ROWS, d), table.dtype),
                                      pltpu.SemaphoreType.DMA((2,)), pltpu.SemaphoreType.DMA((2,))])
    def gather(table_hbm, idx_hbm, out_hbm, idx_v, rows_v, gsem, wsem):
        base = _sc_worker() * per_worker

        def rows_at(j):
            return pl.ds(pl.multiple_of(base + j * SC_ROWS, SC_ROWS), SC_ROWS)

        def gather_copy(slot):
            return pltpu.make_async_copy(table_hbm.at[idx_v.at[slot]], rows_v.at[slot], gsem.at[slot])

        def write_copy(j, slot):
            return pltpu.make_async_copy(rows_v.at[slot], out_hbm.at[rows_at(j)], wsem.at[slot])

        def fetch(j, slot):
            pltpu.sync_copy(idx_hbm.at[rows_at(j)], idx_v.at[slot])
            gather_copy(slot).start()

        fetch(0, 0)

        @pl.loop(0, steps, step=2)
        def _(j0):
            for slot in range(2):
                j = j0 + slot
                gather_copy(slot).wait()

                @pl.when(j >= 1)
                def _():
                    write_copy(j - 1, 1 - slot).wait()

                @pl.when(j + 1 < steps)
                def _():
                    fetch(j + 1, 1 - slot)

                write_copy(j, slot).start()

        write_copy(steps - 1, 1).wait()

    return gather(table, idx)


def _final_kernel(x1_ref, yg_ref, route_ref, g_ref, o_ref):
    tm = x1_ref.shape[0]
    route_t = jnp.concatenate([route_ref[...], jnp.zeros((LANES - ROUTE_ROWS, tm), F32)], axis=0).T
    acc = x1_ref[...]
    for k in range(TOP_K):
        acc = acc + route_t[:, ROUTE_W + k:ROUTE_W + k + 1] * _unpack_bf16_pairs(yg_ref[k])
    o_ref[...] = _rms(acc, g_ref[...])


def _final(x1, yg, route, g, out_prev, tile0, n_total):
    Tc = x1.shape[0]
    tm = TM_FINAL
    in_specs = [pl.BlockSpec((tm, D_MODEL), lambda i: (i, 0)),
                pl.BlockSpec((TOP_K, tm, PACKED), lambda i: (0, i, 0)),
                pl.BlockSpec((ROUTE_ROWS, tm), lambda i: (0, i)),
                pl.BlockSpec((1, D_MODEL), lambda i: (0, 0))]
    args = [x1, yg, route, g]
    kernel_fn, aliases = _final_kernel, {}
    if out_prev is not None:
        in_specs.append(pl.BlockSpec(memory_space=pl.ANY))
        args.append(out_prev)
        kernel_fn = lambda x1_ref, yg_ref, route_ref, g_ref, prev_ref, o_ref: _final_kernel(x1_ref, yg_ref, route_ref,
                                                                                            g_ref, o_ref)
        aliases = {len(args) - 1: 0}
    return pl.pallas_call(
        kernel_fn,
        grid=(Tc // tm,),
        in_specs=in_specs,
        out_specs=pl.BlockSpec((tm, D_MODEL), lambda i: (i + tile0, 0)),
        out_shape=jax.ShapeDtypeStruct((n_total * tm, D_MODEL), F32),
        input_output_aliases=aliases,
        compiler_params=pltpu.CompilerParams(dimension_semantics=("arbitrary",), vmem_limit_bytes=VMEM_LIMIT),
        name="final",
    )(*args)


def _rope_tables(S):
    half = HEAD_DIM // 2
    inv = ROPE_THETA ** (-jnp.arange(half, dtype=F32) / half)
    ang = jnp.arange(S, dtype=F32)[:, None] * inv[None, :]
    cos = jnp.tile(jnp.cos(ang), (1, LANES // half))
    sin = jnp.tile(jnp.concatenate([-jnp.sin(ang), jnp.sin(ang)], axis=1), (1, LANES // HEAD_DIM))
    return cos, sin


def _selection_constants(S):
    nc = (S - CMP_BLOCK) // CMP_STRIDE + 1
    n_sel = S // SEL_BLOCK
    j = jnp.arange(n_sel)[:, None]
    i = jnp.arange(N_CMP_PAD)[None, :]
    overlap_t = ((i * CMP_STRIDE <= j * SEL_BLOCK + SEL_BLOCK - 1)
                 & (i * CMP_STRIDE + CMP_BLOCK - 1 >= j * SEL_BLOCK) & (i < nc)).astype(BF16)
    row = jnp.arange(LANES)[:, None]
    key = jnp.arange(S)[None, :]
    drop_bias = jnp.where(key // SEL_BLOCK == row, NEG_INF, 0.0).astype(BF16)[:n_sel]
    drop_bias = drop_bias.T.reshape(S // CKS, CKS, n_sel)
    return overlap_t, drop_bias


def kernel(x, mem, norm_mix, norm_mem, w_in, cmp_pos, cmp_w1, cmp_w2, w_pool, pool_scale, w_mem_kv, w_up_nsa,
           w_up_pool, w_up_mem, w_out, norm_ffn, w_router, b_router, w_gate_up, b_gate_up, w_down, b_down,
           norm_final):
    B, S, D = x.shape
    T = B * S
    assert D == D_MODEL and S % CKS == 0 and S // SEL_BLOCK == 32 and T % (N_CHUNKS * SC_WORKERS * SC_ROWS * 2) == 0
    l = 0
    x2 = x.reshape(T, D)

    w = w_in[l]
    o_gate = NSA_WIDTH + 6 * KV_WIDTH
    n_gate = 3 * NSA_HEADS
    o_pool = o_gate + n_gate
    o_qm = o_pool + POOL_WIDTH
    o_mg = o_qm + MEM_WIDTH
    w_a = jnp.concatenate([w[:, :o_gate], w[:, o_pool:o_mg], w[:, o_gate:o_pool],
                           jnp.zeros((D, GATE_PAD - n_gate), F32)], axis=1).astype(BF16)
    w_mg = w[:, o_mg:].astype(BF16)
    cos_t, sin_t = _rope_tables(S)
    overlap_t, drop_bias = _selection_constants(S)
    w1 = cmp_w1[l].reshape(2, CMP_FLAT, HEAD_DIM).astype(BF16)
    w2 = cmp_w2[l].astype(BF16)
    w2t = jnp.swapaxes(w2, 1, 2)
    pos = jnp.broadcast_to(cmp_pos[l].reshape(2, 1, CMP_FLAT), (2, 8, CMP_FLAT)).astype(BF16)
    wpool_bd = jnp.zeros((POOL_WIDTH, POOL_WIDTH), F32)
    for gi in range(len(POOL_WINDOWS)):
        wpool_bd = wpool_bd.at[gi * POOL_GROUP:(gi + 1) * POOL_GROUP, gi * POOL_GROUP:(gi + 1) * POOL_GROUP].set(w_pool[l, gi])
    wr = w_router[l].T
    wr_hi = wr.astype(BF16)
    wr_lo = (wr - wr_hi.astype(F32)).astype(BF16)
    br = b_router[l].reshape(N_EXPERTS, 1)
    tri = (jnp.arange(TM_MERGE)[:, None] < jnp.arange(TM_MERGE)[None, :]).astype(BF16)
    lower = (jnp.arange(N_EXPERTS)[None, :] < jnp.arange(N_EXPERTS)[:, None]).astype(BF16)

    km, vm = _memkv(mem, norm_mem[l].reshape(1, D), w_mem_kv[l].astype(BF16))
    qc, qr, kvcmp, ksel, vsel, kwin, vwin, pool_in, qm, gates = _inproj(
        x2, norm_mix[l].reshape(1, D), w_a, cos_t, sin_t, S)
    kvc = kvcmp.reshape(B, S // CMP_STRIDE, CMP_STRIDE, 2 * NSA_GROUPS, HEAD_DIM).transpose(0, 3, 1, 2, 4)
    kvc = kvc.reshape(B, 2 * NSA_GROUPS, S // CMP_STRIDE, CMP_HALF)
    to_slabs = lambda v: jnp.swapaxes(v.reshape(NSA_GROUPS, B, S // KV_LANE_CHUNK, KV_LANE_CHUNK, HEAD_DIM), 3, 4)
    o_nsa = _nsa(qc, qr, kvc, ksel, to_slabs(vsel), kwin, to_slabs(vwin), gates, w1, w2, w2t, pos, overlap_t,
                 drop_bias, B, S)
    consts = [norm_mix[l].reshape(1, D), w_mg, wpool_bd.astype(BF16), pool_scale[l].reshape(1, POOL_WIDTH),
              w_up_nsa[l].astype(BF16), w_up_pool[l].astype(BF16), w_up_mem[l].astype(BF16), w_out[l].astype(BF16),
              norm_ffn[l].reshape(1, D), wr_hi, wr_lo, br, tri, lower]
    Tc = T // N_CHUNKS
    n_tiles = (Tc * TOP_K) // TM_E + N_EXPERTS
    tile_start = jnp.arange(n_tiles, dtype=jnp.int32) * TM_E
    wgu, bgu = w_gate_up[l], b_gate_up[l].reshape(N_EXPERTS, 1, 2 * D_FF)
    wd, bd = w_down[l], b_down[l].reshape(N_EXPERTS, 1, D_MODEL)

    def route_chunk(c):
        x1, hf, route, counts = _merge(x2, o_nsa, pool_in, qm, km, vm, consts, S, c * (Tc // TM_MERGE), Tc // TM_MERGE)
        counts = counts[:, 0].astype(jnp.int32)
        padded = ((counts + TM_E - 1) // TM_E) * TM_E
        ends = jnp.cumsum(padded)
        starts = ends - padded
        e_k = route[ROUTE_E:ROUTE_E + TOP_K].astype(jnp.int32)
        r_k = route[ROUTE_R:ROUTE_R + TOP_K].astype(jnp.int32)
        group_start = jnp.zeros_like(e_k)
        for e in range(N_EXPERTS):
            group_start = jnp.where(e_k == e, starts[e], group_start)
        dest = group_start + r_k
        tile_expert = jnp.minimum(jnp.sum(tile_start[:, None] >= ends[None, :], axis=1), N_EXPERTS - 1).astype(jnp.int32)
        n_used = (ends[-1] // TM_E).astype(jnp.int32).reshape(1)
        n_valid = jnp.clip((starts + counts)[tile_expert] - tile_start, 0, TM_E).astype(jnp.int32)
        xs = _sc_dispatch(hf, dest, n_tiles * TM_E)
        return x1, route, dest, (tile_expert, n_used, n_valid, xs)

    routed = [route_chunk(c) for c in range(N_CHUNKS)]
    expert_out = [_moe(*moe_args, wgu, bgu, wd, bd) for _, _, _, moe_args in routed]
    out = None
    for c, ((x1, route, dest, _), ys) in enumerate(zip(routed, expert_out)):
        yg = _sc_gather_rows(ys, dest.reshape(-1)).reshape(TOP_K, Tc, PACKED)
        out = _final(x1, yg, route, norm_final.reshape(1, D), out, c * (Tc // TM_FINAL), T // TM_FINAL)
    return out.reshape(B, S, D)
```

```python
import functools

import jax
import jax.numpy as jnp
from jax import lax
from jax.experimental import pallas as pl
from jax.experimental.pallas import tpu as pltpu
from jax.experimental.pallas import tpu_sc as plsc

F32 = jnp.float32
BF16 = jnp.bfloat16
U32 = jnp.uint32

D_MODEL = 1024
HEAD_DIM = 64
NSA_HEADS = 8
NSA_GROUPS = 2
HEADS_PER_GROUP = NSA_HEADS // NSA_GROUPS
NSA_WIDTH = NSA_HEADS * HEAD_DIM
KV_WIDTH = NSA_GROUPS * HEAD_DIM
CMP_BLOCK = 32
CMP_STRIDE = 16
SEL_BLOCK = 64
SEL_TOPN = 8
FORCE_BONUS = 1000.0
WINDOW = 512
POOL_WINDOWS = (2, 4, 8, 16)
POOL_GROUP = 64
POOL_WIDTH = POOL_GROUP * len(POOL_WINDOWS)
POOL_HALO = 16
MEM_HEADS = 4
MEM_WIDTH = MEM_HEADS * HEAD_DIM
N_EXPERTS = 32
TOP_K = 4
D_FF = 1024
SWIGLU_LIMIT = 7.0
SWIGLU_ALPHA = 1.702
ROPE_THETA = 10000.0
EPS = 1e-5
NEG_INF = -1e30
TINY = 1e-30
QK_SCALE = HEAD_DIM ** -0.5
LOG2_E = 1.4426950408889634
NSA_Q_SCALE = QK_SCALE * LOG2_E

LANES = 128
GATE_PAD = LANES
PACKED = D_MODEL // 2

TM_IN = 512
TQ = 256
CKS = 512
WIN_CHUNKS = (256, 256, 256)
TM_MERGE = 512
TM_E = 512
TM_FINAL = 512
N_CHUNKS = 2
SC_CORES = 2
SC_SUBCORES = 16
SC_WORKERS = SC_CORES * SC_SUBCORES
SC_ROWS = 64
VMEM_LIMIT = 56 * 1024 * 1024


def _rms(x, g):
    return x * lax.rsqrt(jnp.mean(x * x, axis=-1, keepdims=True) + EPS) * g


def _sigmoid(x):
    return 0.5 * jnp.tanh(0.5 * x) + 0.5


def _dot(a, b):
    return jnp.dot(a, b, preferred_element_type=F32)


def _pack_bf16_pairs(v):
    n = v.shape[1] // 2
    r = v.astype(BF16).astype(F32)
    lo = pltpu.bitcast(r[:, :n], U32) >> 16
    hi = pltpu.bitcast(r[:, n:], U32) & jnp.uint32(0xFFFF0000)
    return lo | hi


def _unpack_bf16_pairs(w):
    lo = pltpu.bitcast(w << 16, F32)
    hi = pltpu.bitcast(w & jnp.uint32(0xFFFF0000), F32)
    return jnp.concatenate([lo, hi], axis=1)


def _dot_nt(a, b):
    return lax.dot_general(a, b, (((1,), (1,)), ((), ())), preferred_element_type=F32)


def _dot_tn(a, b):
    return lax.dot_general(a, b, (((0,), (0,)), ((), ())), preferred_element_type=F32)


def _memkv_kernel(mem_ref, g_ref, w_ref, k_ref, v_ref):
    m = _rms(mem_ref[0], g_ref[...]).astype(BF16)
    kv = _dot(m, w_ref[...])
    for h in range(MEM_HEADS):
        k_ref[0, h] = kv[:, h * HEAD_DIM:(h + 1) * HEAD_DIM].astype(BF16)
        v_ref[0, h] = kv[:, MEM_WIDTH + h * HEAD_DIM:MEM_WIDTH + (h + 1) * HEAD_DIM].astype(BF16)


def _memkv(mem, g, w):
    B, M, D = mem.shape
    return pl.pallas_call(
        _memkv_kernel,
        grid=(B,),
        in_specs=[pl.BlockSpec((1, M, D), lambda b: (b, 0, 0)),
                  pl.BlockSpec((1, D), lambda b: (0, 0)),
                  pl.BlockSpec((D, 2 * MEM_WIDTH), lambda b: (0, 0))],
        out_specs=[pl.BlockSpec((1, MEM_HEADS, M, HEAD_DIM), lambda b: (b, 0, 0, 0)),
                   pl.BlockSpec((1, MEM_HEADS, M, HEAD_DIM), lambda b: (b, 0, 0, 0))],
        out_shape=[jax.ShapeDtypeStruct((B, MEM_HEADS, M, HEAD_DIM), BF16),
                   jax.ShapeDtypeStruct((B, MEM_HEADS, M, HEAD_DIM), BF16)],
        compiler_params=pltpu.CompilerParams(dimension_semantics=("arbitrary",), vmem_limit_bytes=VMEM_LIMIT),
        name="memkv",
    )(mem, g, w)


IN_COLS = NSA_WIDTH + 6 * KV_WIDTH + POOL_WIDTH + MEM_WIDTH + GATE_PAD
CMP_COLS = CMP_STRIDE * 2 * KV_WIDTH


def _inproj_kernel(x_ref, g_ref, w_ref, cos_ref, sin_ref,
                   qc_ref, qr_ref, kvc_ref, ksel_ref, vsel_ref, kwin_ref, vwin_ref, pool_ref, qm_ref, gate_ref,
                   kv_s):
    h = _rms(x_ref[...], g_ref[...]).astype(BF16)
    p = _dot(h, w_ref[...])
    cos = cos_ref[...]
    sin = sin_ref[...]
    lane = lax.broadcasted_iota(jnp.int32, cos.shape, 1)
    first_half = (lane % HEAD_DIM) < (HEAD_DIM // 2)

    def rope(c):
        partner = jnp.where(first_half, pltpu.roll(c, LANES - HEAD_DIM // 2, 1), pltpu.roll(c, HEAD_DIM // 2, 1))
        return c * cos + partner * sin

    def halves(c):
        return c[:, :HEAD_DIM], c[:, HEAD_DIM:]

    for j in range(NSA_WIDTH // LANES):
        c = p[:, j * LANES:(j + 1) * LANES]
        r = rope(c)
        for hh, (cc, rr) in enumerate(zip(halves(c), halves(r))):
            qc_ref[2 * j + hh] = (cc * NSA_Q_SCALE).astype(BF16)
            qr_ref[2 * j + hh] = (rr * NSA_Q_SCALE).astype(BF16)
    o = NSA_WIDTH
    n_rows = kv_s.shape[1] // CMP_STRIDE
    for c in range(2):
        kv_s[c] = p[:, o + c * KV_WIDTH:o + (c + 1) * KV_WIDTH]
    for l_ in range(CMP_STRIDE):
        for c in range(2):
            col = (2 * l_ + c) * KV_WIDTH
            kvc_ref[:, col:col + KV_WIDTH] = kv_s[c, pl.ds(l_, n_rows, stride=CMP_STRIDE), :]
    o += 2 * KV_WIDTH
    for ref, rot in ((ksel_ref, True), (vsel_ref, False), (kwin_ref, True), (vwin_ref, False)):
        c = p[:, o:o + KV_WIDTH]
        if rot:
            c = rope(c)
        for g, cc in enumerate(halves(c)):
            ref[g] = cc.astype(BF16)
        o += KV_WIDTH
    pool_ref[...] = p[:, o:o + POOL_WIDTH]
    o += POOL_WIDTH
    for hh in range(MEM_HEADS):
        qm_ref[hh] = (p[:, o + hh * HEAD_DIM:o + (hh + 1) * HEAD_DIM] * QK_SCALE).astype(BF16)
    o += MEM_WIDTH
    gate_ref[...] = _sigmoid(p[:, o:o + GATE_PAD])


def _inproj(x2, g, w_a, cos_t, sin_t, S):
    T = x2.shape[0]
    tm = TM_IN
    n_s = S // tm
    head_spec = lambda n: pl.BlockSpec((n, tm, HEAD_DIM), lambda i: (0, i, 0))
    row_spec = lambda w: pl.BlockSpec((tm, w), lambda i: (i, 0))
    return pl.pallas_call(
        _inproj_kernel,
        grid=(T // tm,),
        in_specs=[row_spec(D_MODEL),
                  pl.BlockSpec((1, D_MODEL), lambda i: (0, 0)),
                  pl.BlockSpec((D_MODEL, IN_COLS), lambda i: (0, 0)),
                  pl.BlockSpec((tm, LANES), lambda i: (i % n_s, 0)),
                  pl.BlockSpec((tm, LANES), lambda i: (i % n_s, 0))],
        out_specs=[head_spec(NSA_HEADS), head_spec(NSA_HEADS),
                   pl.BlockSpec((tm // CMP_STRIDE, CMP_COLS), lambda i: (i, 0)),
                   head_spec(NSA_GROUPS), head_spec(NSA_GROUPS), head_spec(NSA_GROUPS), head_spec(NSA_GROUPS),
                   row_spec(POOL_WIDTH), head_spec(MEM_HEADS), row_spec(GATE_PAD)],
        out_shape=[jax.ShapeDtypeStruct((NSA_HEADS, T, HEAD_DIM), BF16),
                   jax.ShapeDtypeStruct((NSA_HEADS, T, HEAD_DIM), BF16),
                   jax.ShapeDtypeStruct((T // CMP_STRIDE, CMP_COLS), F32),
                   jax.ShapeDtypeStruct((NSA_GROUPS, T, HEAD_DIM), BF16),
                   jax.ShapeDtypeStruct((NSA_GROUPS, T, HEAD_DIM), BF16),
                   jax.ShapeDtypeStruct((NSA_GROUPS, T, HEAD_DIM), BF16),
                   jax.ShapeDtypeStruct((NSA_GROUPS, T, HEAD_DIM), BF16),
                   jax.ShapeDtypeStruct((T, POOL_WIDTH), F32),
                   jax.ShapeDtypeStruct((MEM_HEADS, T, HEAD_DIM), BF16),
                   jax.ShapeDtypeStruct((T, GATE_PAD), F32)],
        scratch_shapes=[pltpu.VMEM((2, tm, KV_WIDTH), F32)],
        compiler_params=pltpu.CompilerParams(dimension_semantics=("arbitrary",), vmem_limit_bytes=VMEM_LIMIT),
        name="inproj",
    )(x2, g, w_a, cos_t, sin_t)


N_CMP_PAD = 128
CMP_FLAT = CMP_BLOCK * HEAD_DIM
CMP_HALF = CMP_STRIDE * HEAD_DIM


def _nsa_kernel(qc_ref, qr_ref, kvc_ref, ksel_ref, vsel_ref, kwin_ref, vwin_ref, gate_ref,
                w1_ref, w2_ref, w2t_ref, pos_ref, ovl_ref, drop_ref, o_ref, kc_s, vct_s, *, S):
    i = pl.program_id(1)
    tq = TQ
    hpg = HEADS_PER_GROUP
    n_sel = S // SEL_BLOCK
    hq = hpg * tq

    @pl.when(i == 0)
    def _compress():
        a = kvc_ref[...].astype(BF16)
        p1 = _dot(a, w1_ref[0])
        p2 = _dot(a, w1_ref[1])
        posterm = (_dot(pos_ref[0], w1_ref[0]) + _dot(pos_ref[1], w1_ref[1]))[0:1]
        hid = p1 + pltpu.roll(p2, N_CMP_PAD - 1, 0) + posterm
        hid = (hid * _sigmoid(hid)).astype(BF16)
        for g in range(NSA_GROUPS):
            kc_s[g] = _dot(hid[:, g * HEAD_DIM:(g + 1) * HEAD_DIM], w2_ref[0]).astype(BF16)
            vct_s[g] = _dot_nt(w2t_ref[1], hid[:, KV_WIDTH + g * HEAD_DIM:KV_WIDTH + (g + 1) * HEAD_DIM]).astype(BF16)

    q0 = i * tq
    t_lane = q0 + lax.broadcasted_iota(jnp.int32, (1, tq), 1)

    def key_pos(start, n):
        return start + lax.broadcasted_iota(jnp.int32, (n, 1), 0)

    cmp_valid = (key_pos(0, N_CMP_PAD) * CMP_STRIDE + CMP_BLOCK - 1) <= t_lane
    jrow = lax.broadcasted_iota(jnp.int32, (n_sel, tq), 0)
    cur = t_lane // SEL_BLOCK
    sel_valid = jrow * SEL_BLOCK <= t_lane
    forced = (jrow == 0) | (jrow == cur) | (jrow == cur - 1)

    cd = q0 // CKS
    causal_bias = jnp.where(key_pos(cd * CKS, CKS) <= t_lane, 0.0, NEG_INF)
    win_chunks = []
    hi = q0 + tq
    for n in WIN_CHUNKS:
        lo = hi - n
        start = pl.multiple_of(jnp.maximum(lo, 0), LANES)
        kp = key_pos(start, n)
        diff = t_lane - kp
        win_chunks.append((start, n, jnp.where((diff >= 0) & (diff < WINDOW) & (kp < hi), 0.0, NEG_INF)))
        hi = lo

    def attend(jobs):
        scores = [_dot_nt(k, q_all) for q_all, k, _, _ in jobs]
        heads = [slice(hh * tq, (hh + 1) * tq) for hh in range(hpg)]
        maxes = [jnp.concatenate([jnp.max(s_all[:, sl] + bias, axis=0, keepdims=True) for sl in heads], axis=1)
                 for (_, _, _, bias), s_all in zip(jobs, scores)]
        soft = []
        for (_, _, _, bias), s_all, m in zip(jobs, scores, maxes):
            ps = [jnp.exp2(s_all[:, sl] + bias - m[:, sl]) for sl in heads]
            l = jnp.concatenate([jnp.sum(p, axis=0, keepdims=True) for p in ps], axis=1)
            soft.append((l, jnp.concatenate([p.astype(BF16) for p in ps], axis=1)))
        return [(m, l, _dot_tn(v, p)) for (_, _, v, _), m, (l, p) in zip(jobs, maxes, soft)]

    def merge(a, b):
        m = jnp.maximum(a[0], b[0])
        wa = jnp.exp2(a[0] - m)
        wb = jnp.exp2(b[0] - m)
        return m, wa * a[1] + wb * b[1], wa * a[2] + wb * b[2]

    def select_blocks(g):
        q_cmp = qc_ref[g * hpg:(g + 1) * hpg].reshape(hq, HEAD_DIM)
        s_all = _dot_nt(kc_s[g], q_cmp)
        p_grp = jnp.zeros((N_CMP_PAD, tq), F32)
        ps = []
        for hh in range(hpg):
            sl = slice(hh * tq, (hh + 1) * tq)
            s = jnp.where(cmp_valid, s_all[:, sl], NEG_INF)
            m = jnp.max(s, axis=0, keepdims=True)
            e = jnp.where(cmp_valid, jnp.exp2(s - m), 0.0)
            p = e * (1.0 / jnp.maximum(jnp.sum(e, axis=0, keepdims=True), TINY))
            p_grp = p_grp + p
            ps.append(p.astype(BF16))
        o_cmp = _dot(vct_s[g], jnp.concatenate(ps, axis=1))

        ovl = ovl_ref[...]
        p_hi = p_grp.astype(BF16)
        r1 = p_grp - p_hi.astype(F32)
        p_mid = r1.astype(BF16)
        p_lo = (r1 - p_mid.astype(F32)).astype(BF16)
        score = _dot(ovl, p_hi) + _dot(ovl, p_mid) + _dot(ovl, p_lo)
        score = jnp.where(sel_valid, score + jnp.where(forced, FORCE_BONUS, 0.0), -1.0)
        rank = jnp.zeros((n_sel, tq), F32)
        for jp in range(n_sel):
            other = score[jp:jp + 1, :]
            beats = (other > score) | ((other == score) & (jrow > jp))
            rank = rank + beats.astype(F32)
        return o_cmp, (rank >= SEL_TOPN).astype(BF16)

    groups = range(NSA_GROUPS)
    cmp_out = [select_blocks(g) for g in groups]
    q_rot = [qr_ref[g * hpg:(g + 1) * hpg].reshape(hq, HEAD_DIM) for g in groups]

    def selected_job(g, c, extra_bias):
        k0 = pl.multiple_of(c * CKS, CKS)
        bias = _dot(drop_ref[c], cmp_out[g][1])
        if extra_bias is not None:
            bias = bias + extra_bias
        return q_rot[g], ksel_ref[g, pl.ds(k0, CKS), :], vsel_ref[g, pl.ds(k0, CKS), :], bias

    def window_job(g, chunk):
        start, n, bias = chunk
        return q_rot[g], kwin_ref[g, pl.ds(start, n), :], vwin_ref[g, pl.ds(start, n), :], bias

    stats = attend([selected_job(g, cd, causal_bias) for g in groups]
                   + [window_job(g, chunk) for chunk in win_chunks for g in groups])
    sel_state = tuple(stats[:NSA_GROUPS])
    win_state = list(stats[NSA_GROUPS:2 * NSA_GROUPS])
    for j in range(1, len(win_chunks)):
        for g in groups:
            win_state[g] = merge(win_state[g], stats[NSA_GROUPS * (1 + j) + g])

    def sel_body(it, states):
        new = attend([selected_job(g, cd - 1 - it, None) for g in groups])
        return tuple(merge(states[g], new[g]) for g in groups)

    sel_state = lax.fori_loop(0, cd, sel_body, sel_state)

    gates_t = gate_ref[...].T
    out_rows = []
    for g in groups:
        o_cmp = cmp_out[g][0]
        o_sel = sel_state[g][2] * (1.0 / sel_state[g][1])
        o_win = win_state[g][2] * (1.0 / win_state[g][1])
        for hh in range(hpg):
            h = g * hpg + hh
            sl = slice(hh * tq, (hh + 1) * tq)
            out_rows.append(gates_t[3 * h:3 * h + 1] * o_cmp[:, sl] + gates_t[3 * h + 1:3 * h + 2] * o_sel[:, sl]
                            + gates_t[3 * h + 2:3 * h + 3] * o_win[:, sl])
    o_ref[...] = jnp.concatenate(out_rows, axis=0).T.astype(BF16)


def _nsa(qc, qr, kvc, ksel, vsel_t, kwin, vwin_t, gates, w1, w2, w2t, pos, ovl, drop_bias, B, S):
    T = B * S
    tq = TQ
    nq = S // tq
    q_spec = pl.BlockSpec((NSA_HEADS, tq, HEAD_DIM), lambda b, i: (0, b * nq + i, 0))
    k_spec = pl.BlockSpec((NSA_GROUPS, S, HEAD_DIM), lambda b, i: (0, b, 0))
    full = lambda a: pl.BlockSpec(a.shape, lambda b, i: (0,) * a.ndim)
    return pl.pallas_call(
        functools.partial(_nsa_kernel, S=S),
        grid=(B, nq),
        in_specs=[q_spec, q_spec,
                  pl.BlockSpec((N_CMP_PAD, CMP_COLS), lambda b, i: (b, 0)),
                  k_spec, k_spec, k_spec, k_spec,
                  pl.BlockSpec((tq, GATE_PAD), lambda b, i: (b * nq + i, 0)),
                  full(w1), full(w2), full(w2t), full(pos), full(ovl), full(drop_bias)],
        out_specs=pl.BlockSpec((tq, NSA_WIDTH), lambda b, i: (b * nq + i, 0)),
        out_shape=jax.ShapeDtypeStruct((T, NSA_WIDTH), BF16),
        scratch_shapes=[pltpu.VMEM((NSA_GROUPS, N_CMP_PAD, HEAD_DIM), BF16),
                        pltpu.VMEM((NSA_GROUPS, HEAD_DIM, N_CMP_PAD), BF16)],
        compiler_params=pltpu.CompilerParams(dimension_semantics=("arbitrary", "arbitrary"),
                                             vmem_limit_bytes=VMEM_LIMIT),
        name="nsa",
    )(qc, qr, kvc, ksel, vsel_t, kwin, vwin_t, gates, w1, w2, w2t, pos, ovl, drop_bias)


ROUTE_E, ROUTE_R, ROUTE_W = 0, TOP_K, 2 * TOP_K
ROUTE_ROWS = 16


def _merge_kernel(x_ref, onsa_ref, pool_ref, prev_ref, qm_ref, km_ref, vm_ref,
                  gmix_ref, wmg_ref, wpool_ref, pscale_ref, wun_ref, wup_ref, wum_ref, wout_ref,
                  gffn_ref, wrh_ref, wrl_ref, br_ref, tri_ref, lower_ref,
                  x1_ref, hf_ref, route_ref, cnt_ref, omem_s, carry_s, *, S, tile0):
    i = pl.program_id(0) + tile0
    tm = TM_MERGE
    n_s = S // tm

    @pl.when(pl.program_id(0) == 0)
    def _init():
        carry_s[...] = jnp.zeros_like(carry_s)

    x = x_ref[...]
    h = _rms(x, gmix_ref[...]).astype(BF16)
    mg = _sigmoid(_dot(h, wmg_ref[...]))

    u = pool_ref[...]
    seq_tile = i % n_s
    prev = jnp.where(seq_tile == 0, 0.0, prev_ref[...])
    ext = jnp.concatenate([prev, u], axis=0)
    b2 = ext[1:] + ext[:-1]
    b4 = b2[2:] + b2[:-2]
    b8 = b4[4:] + b4[:-4]
    b16 = b8[8:] + b8[:-8]
    sums = (b2[POOL_HALO - 1:POOL_HALO - 1 + tm], b4[POOL_HALO - 3:POOL_HALO - 3 + tm],
            b8[POOL_HALO - 7:POOL_HALO - 7 + tm], b16[POOL_HALO - 15:POOL_HALO - 15 + tm])
    t_seq = seq_tile * tm + lax.broadcasted_iota(jnp.int32, (tm, 1), 0)
    lane_p = lax.broadcasted_iota(jnp.int32, (tm, POOL_WIDTH), 1)
    z = jnp.zeros((tm, POOL_WIDTH), F32)
    for gi, w in enumerate(POOL_WINDOWS):
        cnt = jnp.minimum(t_seq + 1, w).astype(F32)
        z = jnp.where(lane_p // POOL_GROUP == gi, sums[gi] / cnt, z)
    z = z - u
    o_pool = (_dot(z.astype(BF16), wpool_ref[...]) * pscale_ref[...]).astype(BF16)

    for hh in range(MEM_HEADS):
        s = _dot_nt(qm_ref[hh], km_ref[0, hh])
        m = jnp.max(s, axis=-1, keepdims=True)
        e = jnp.exp(s - m)
        p = e / jnp.sum(e, axis=-1, keepdims=True)
        omem_s[:, hh * HEAD_DIM:(hh + 1) * HEAD_DIM] = _dot(p.astype(BF16), vm_ref[0, hh]).astype(BF16)

    merged = (mg[:, :D_MODEL] * _dot(onsa_ref[...], wun_ref[...])
              + mg[:, D_MODEL:2 * D_MODEL] * _dot(o_pool, wup_ref[...])
              + mg[:, 2 * D_MODEL:] * _dot(omem_s[...], wum_ref[...]))
    x1 = x + _dot(merged.astype(BF16), wout_ref[...])
    x1_ref[...] = x1
    hf = _rms(x1, gffn_ref[...])
    hf_ref[...] = _pack_bf16_pairs(hf)

    hf_hi = hf.astype(BF16)
    hf_lo = (hf - hf_hi.astype(F32)).astype(BF16)
    logits = (_dot_nt(wrh_ref[...], hf_hi) + _dot_nt(wrl_ref[...], hf_hi) + _dot_nt(wrh_ref[...], hf_lo)
              + br_ref[...])
    erow = lax.broadcasted_iota(jnp.int32, (N_EXPERTS, tm), 0)
    rank = jnp.zeros((N_EXPERTS, tm), F32)
    for jp in range(N_EXPERTS):
        other = logits[jp:jp + 1, :]
        beats = (other > logits) | ((other == logits) & (erow > jp))
        rank = rank + beats.astype(F32)
    chosen = rank < TOP_K
    m = jnp.max(logits, axis=0, keepdims=True)
    e = jnp.where(chosen, jnp.exp(logits - m), 0.0)
    comb = e * (1.0 / jnp.sum(e, axis=0, keepdims=True))

    chosen_b = chosen.astype(BF16)
    carry = carry_s[:, 0:1]
    in_expert = _dot(chosen_b, tri_ref[...]) + carry
    carry_new = carry + jnp.sum(chosen.astype(F32), axis=1, keepdims=True)
    carry_s[...] = jnp.broadcast_to(carry_new, carry_s.shape)
    cnt_ref[...] = jnp.broadcast_to(carry_new, cnt_ref.shape)

    before = _dot(lower_ref[...], chosen_b)
    erow_f = erow.astype(F32)
    fields = {ROUTE_E: erow_f, ROUTE_R: in_expert, ROUTE_W: comb}
    rows = [None] * ROUTE_ROWS
    for k in range(TOP_K):
        pick = chosen & (before == k)
        for base, val in fields.items():
            rows[base + k] = jnp.sum(jnp.where(pick, val, 0.0), axis=0, keepdims=True)
    zero_row = jnp.zeros((1, tm), F32)
    route_ref[...] = jnp.concatenate([zero_row if r is None else r for r in rows], axis=0)


def _merge(x2, onsa, pool_in, qm, km, vm, consts, S, tile0, n_tiles):
    tm = TM_MERGE
    Tc = n_tiles * tm
    n_s = S // tm
    M = km.shape[2]
    halo_per_tile = tm // POOL_HALO
    row = lambda w: pl.BlockSpec((tm, w), lambda i: (i + tile0, 0))
    out_row = lambda w: pl.BlockSpec((tm, w), lambda i: (i, 0))
    full = lambda a: pl.BlockSpec(a.shape, lambda i: (0,) * a.ndim)
    mem_spec = pl.BlockSpec((1, MEM_HEADS, M, HEAD_DIM), lambda i: ((i + tile0) // n_s, 0, 0, 0))
    return pl.pallas_call(
        functools.partial(_merge_kernel, S=S, tile0=tile0),
        grid=(n_tiles,),
        in_specs=[row(D_MODEL), row(NSA_WIDTH), row(POOL_WIDTH),
                  pl.BlockSpec((POOL_HALO, POOL_WIDTH),
                               lambda i: (jnp.maximum((i + tile0) * halo_per_tile - 1, 0), 0)),
                  pl.BlockSpec((MEM_HEADS, tm, HEAD_DIM), lambda i: (0, i + tile0, 0)),
                  mem_spec, mem_spec] + [full(c) for c in consts],
        out_specs=[out_row(D_MODEL), out_row(PACKED), pl.BlockSpec((ROUTE_ROWS, tm), lambda i: (0, i)),
                   pl.BlockSpec((N_EXPERTS, LANES), lambda i: (0, 0))],
        out_shape=[jax.ShapeDtypeStruct((Tc, D_MODEL), F32),
                   jax.ShapeDtypeStruct((Tc, PACKED), U32),
                   jax.ShapeDtypeStruct((ROUTE_ROWS, Tc), F32),
                   jax.ShapeDtypeStruct((N_EXPERTS, LANES), F32)],
        scratch_shapes=[pltpu.VMEM((tm, MEM_WIDTH), BF16), pltpu.VMEM((N_EXPERTS, LANES), F32)],
        compiler_params=pltpu.CompilerParams(dimension_semantics=("arbitrary",), vmem_limit_bytes=VMEM_LIMIT),
        name="merge",
    )(x2, onsa, pool_in, pool_in, qm, km, vm, *consts)


def _sc_mesh():
    return plsc.VectorSubcoreMesh(core_axis_name="c", subcore_axis_name="s")


def _sc_worker():
    return lax.axis_index("s") * SC_CORES + lax.axis_index("c")


def _sc_dispatch(hf, dest, n_rows):
    T, d = hf.shape
    per_worker = T // SC_WORKERS
    steps = per_worker // SC_ROWS

    @functools.partial(pl.kernel, mesh=_sc_mesh(), out_type=jax.ShapeDtypeStruct((n_rows, d), hf.dtype),
                       scratch_types=[pltpu.VMEM((TOP_K, SC_ROWS), jnp.int32), pltpu.VMEM((SC_ROWS, d), hf.dtype),
                                      pltpu.SemaphoreType.DMA])
    def dispatch(hf_hbm, dest_hbm, xs_hbm, idx_v, rows_v, sem):
        base = _sc_worker() * per_worker

        @pl.loop(0, steps)
        def _(j):
            rows = pl.ds(pl.multiple_of(base + j * SC_ROWS, SC_ROWS), SC_ROWS)
            pltpu.sync_copy(hf_hbm.at[rows], rows_v)
            for k in range(TOP_K):
                pltpu.sync_copy(dest_hbm.at[k, rows], idx_v.at[k])
            copies = [pltpu.make_async_copy(rows_v, xs_hbm.at[idx_v.at[k]], sem) for k in range(TOP_K)]
            for c in copies:
                c.start()
            for c in copies:
                c.wait()

    return dispatch(hf, dest)


def _moe_kernel(te_ref, nu_ref, nv_ref, xs_ref, wgu_ref, bgu_ref, wd_ref, bd_ref, ys_ref, wgu_s, wd_s):
    j = pl.program_id(0)
    used = j < nu_ref[0]
    new_expert = (j == 0) | (te_ref[j] != te_ref[jnp.maximum(j - 1, 0)])

    @pl.when(used & new_expert)
    def _cast_weights():
        wgu_s[...] = wgu_ref[0].astype(BF16)
        wd_s[...] = wd_ref[0].astype(BF16)

    @pl.when(used)
    def _compute():
        live = lax.broadcasted_iota(jnp.int32, (xs_ref.shape[0], 1), 0) < nv_ref[j]
        xb = _unpack_bf16_pairs(jnp.where(live, xs_ref[...], jnp.uint32(0))).astype(BF16)
        gu = _dot(xb, wgu_s[...]) + bgu_ref[0]
        gate = jnp.minimum(gu[:, :D_FF], SWIGLU_LIMIT)
        up = jnp.clip(gu[:, D_FF:], -SWIGLU_LIMIT, SWIGLU_LIMIT)
        act = (up + 1.0) * (gate * _sigmoid(SWIGLU_ALPHA * gate))
        ys_ref[...] = _pack_bf16_pairs(_dot(act.astype(BF16), wd_s[...]) + bd_ref[0])

    @pl.when(j >= nu_ref[0])
    def _unused():
        ys_ref[...] = jnp.zeros_like(ys_ref)


def _moe(tile_expert, n_used, n_valid, xs, wgu, bgu, wd, bd):
    P = xs.shape[0]
    tm = TM_E
    grid_spec = pltpu.PrefetchScalarGridSpec(
        num_scalar_prefetch=3,
        grid=(P // tm,),
        in_specs=[pl.BlockSpec((tm, PACKED), lambda j, te, nu, nv: (j, 0)),
                  pl.BlockSpec((1, D_MODEL, 2 * D_FF), lambda j, te, nu, nv: (te[j], 0, 0)),
                  pl.BlockSpec((1, 1, 2 * D_FF), lambda j, te, nu, nv: (te[j], 0, 0)),
                  pl.BlockSpec((1, D_FF, D_MODEL), lambda j, te, nu, nv: (te[j], 0, 0)),
                  pl.BlockSpec((1, 1, D_MODEL), lambda j, te, nu, nv: (te[j], 0, 0))],
        out_specs=pl.BlockSpec((tm, PACKED), lambda j, te, nu, nv: (j, 0)),
        scratch_shapes=[pltpu.VMEM((D_MODEL, 2 * D_FF), BF16), pltpu.VMEM((D_FF, D_MODEL), BF16)],
    )
    return pl.pallas_call(
        _moe_kernel,
        grid_spec=grid_spec,
        out_shape=jax.ShapeDtypeStruct((P, PACKED), U32),
        compiler_params=pltpu.CompilerParams(dimension_semantics=("arbitrary",), vmem_limit_bytes=VMEM_LIMIT),
        name="moe",
    )(tile_expert, n_used, n_valid, xs, wgu, bgu, wd, bd)


def _sc_gather_rows(table, idx):
    n, d = idx.shape[0], table.shape[1]
    per_worker = n // SC_WORKERS
    steps = per_worker // SC_ROWS
    assert steps % 2 == 0

    @functools.partial(pl.kernel, mesh=_sc_mesh(), out_type=jax.ShapeDtypeStruct((n, d), table.dtype),
                       scratch_types=[pltpu.VMEM((2, SC_ROWS), jnp.int32), pltpu.VMEM((2, SC_ROWS, d), table.dtype),
                                      pltpu.SemaphoreType.DMA((2,)), pltpu.SemaphoreType.DMA((2,))])
    def gather(table_hbm, idx_hbm, out_hbm, idx_v, rows_v, gsem, wsem):
        base = _sc_worker() * per_worker

        def rows_at(j):
            return pl.ds(pl.multiple_of(base + j * SC_ROWS, SC_ROWS), SC_ROWS)

        def gather_copy(slot):
            return pltpu.make_async_copy(table_hbm.at[idx_v.at[slot]], rows_v.at[slot], gsem.at[slot])

        def write_copy(j, slot):
            return pltpu.make_async_copy(rows_v.at[slot], out_hbm.at[rows_at(j)], wsem.at[slot])

        def fetch(j, slot):
            pltpu.sync_copy(idx_hbm.at[rows_at(j)], idx_v.at[slot])
            gather_copy(slot).start()

        fetch(0, 0)

        @pl.loop(0, steps, step=2)
        def _(j0):
            for slot in range(2):
                j = j0 + slot
                gather_copy(slot).wait()

                @pl.when(j >= 1)
                def _():
                    write_copy(j - 1, 1 - slot).wait()

                @pl.when(j + 1 < steps)
                def _():
                    fetch(j + 1, 1 - slot)

                write_copy(j, slot).start()

        write_copy(steps - 1, 1).wait()

    return gather(table, idx)


def _final_kernel(x1_ref, yg_ref, route_ref, g_ref, o_ref):
    tm = x1_ref.shape[0]
    route_t = jnp.concatenate([route_ref[...], jnp.zeros((LANES - ROUTE_ROWS, tm), F32)], axis=0).T
    acc = x1_ref[...]
    for k in range(TOP_K):
        acc = acc + route_t[:, ROUTE_W + k:ROUTE_W + k + 1] * _unpack_bf16_pairs(yg_ref[k])
    o_ref[...] = _rms(acc, g_ref[...])


def _final(x1, yg, route, g, out_prev, tile0, n_total):
    Tc = x1.shape[0]
    tm = TM_FINAL
    in_specs = [pl.BlockSpec((tm, D_MODEL), lambda i: (i, 0)),
                pl.BlockSpec((TOP_K, tm, PACKED), lambda i: (0, i, 0)),
                pl.BlockSpec((ROUTE_ROWS, tm), lambda i: (0, i)),
                pl.BlockSpec((1, D_MODEL), lambda i: (0, 0))]
    args = [x1, yg, route, g]
    kernel_fn, aliases = _final_kernel, {}
    if out_prev is not None:
        in_specs.append(pl.BlockSpec(memory_space=pl.ANY))
        args.append(out_prev)
        kernel_fn = lambda x1_ref, yg_ref, route_ref, g_ref, prev_ref, o_ref: _final_kernel(x1_ref, yg_ref, route_ref,
                                                                                            g_ref, o_ref)
        aliases = {len(args) - 1: 0}
    return pl.pallas_call(
        kernel_fn,
        grid=(Tc // tm,),
        in_specs=in_specs,
        out_specs=pl.BlockSpec((tm, D_MODEL), lambda i: (i + tile0, 0)),
        out_shape=jax.ShapeDtypeStruct((n_total * tm, D_MODEL), F32),
        input_output_aliases=aliases,
        compiler_params=pltpu.CompilerParams(dimension_semantics=("arbitrary",), vmem_limit_bytes=VMEM_LIMIT),
        name="final",
    )(*args)


def _rope_tables(S):
    half = HEAD_DIM // 2
    inv = ROPE_THETA ** (-jnp.arange(half, dtype=F32) / half)
    ang = jnp.arange(S, dtype=F32)[:, None] * inv[None, :]
    cos = jnp.tile(jnp.cos(ang), (1, LANES // half))
    sin = jnp.tile(jnp.concatenate([-jnp.sin(ang), jnp.sin(ang)], axis=1), (1, LANES // HEAD_DIM))
    return cos, sin


def _selection_constants(S):
    nc = (S - CMP_BLOCK) // CMP_STRIDE + 1
    n_sel = S // SEL_BLOCK
    j = jnp.arange(n_sel)[:, None]
    i = jnp.arange(N_CMP_PAD)[None, :]
    overlap_t = ((i * CMP_STRIDE <= j * SEL_BLOCK + SEL_BLOCK - 1)
                 & (i * CMP_STRIDE + CMP_BLOCK - 1 >= j * SEL_BLOCK) & (i < nc)).astype(BF16)
    row = jnp.arange(LANES)[:, None]
    key = jnp.arange(S)[None, :]
    drop_bias = jnp.where(key // SEL_BLOCK == row, NEG_INF, 0.0).astype(BF16)[:n_sel]
    drop_bias = drop_bias.T.reshape(S // CKS, CKS, n_sel)
    return overlap_t, drop_bias


def kernel(x, mem, norm_mix, norm_mem, w_in, cmp_pos, cmp_w1, cmp_w2, w_pool, pool_scale, w_mem_kv, w_up_nsa,
           w_up_pool, w_up_mem, w_out, norm_ffn, w_router, b_router, w_gate_up, b_gate_up, w_down, b_down,
           norm_final):
    B, S, D = x.shape
    T = B * S
    assert D == D_MODEL and S % CKS == 0 and S // SEL_BLOCK == 32 and T % (N_CHUNKS * SC_WORKERS * SC_ROWS * 2) == 0
    l = 0
    x2 = x.reshape(T, D)

    w = w_in[l]
    o_gate = NSA_WIDTH + 6 * KV_WIDTH
    n_gate = 3 * NSA_HEADS
    o_pool = o_gate + n_gate
    o_qm = o_pool + POOL_WIDTH
    o_mg = o_qm + MEM_WIDTH
    w_a = jnp.concatenate([w[:, :o_gate], w[:, o_pool:o_mg], w[:, o_gate:o_pool],
                           jnp.zeros((D, GATE_PAD - n_gate), F32)], axis=1).astype(BF16)
    w_mg = w[:, o_mg:].astype(BF16)
    cos_t, sin_t = _rope_tables(S)
    overlap_t, drop_bias = _selection_constants(S)
    stream_kv = jnp.arange(2 * NSA_GROUPS) // NSA_GROUPS
    w1_half = cmp_w1[l].reshape(2, 2, CMP_STRIDE, HEAD_DIM, HEAD_DIM)[stream_kv]
    w1 = jnp.einsum("shlde,ts->hltdse", w1_half, jnp.eye(2 * NSA_GROUPS, dtype=F32))
    w1 = w1.reshape(2, CMP_COLS, 2 * KV_WIDTH).astype(BF16)
    w2 = cmp_w2[l].astype(BF16)
    w2t = jnp.swapaxes(w2, 1, 2)
    pos_half = cmp_pos[l].reshape(2, 2, CMP_STRIDE, HEAD_DIM)[stream_kv]
    pos = jnp.broadcast_to(pos_half.transpose(1, 2, 0, 3).reshape(2, 1, CMP_COLS), (2, 8, CMP_COLS)).astype(BF16)
    wpool_bd = jnp.zeros((POOL_WIDTH, POOL_WIDTH), F32)
    for gi in range(len(POOL_WINDOWS)):
        wpool_bd = wpool_bd.at[gi * POOL_GROUP:(gi + 1) * POOL_GROUP, gi * POOL_GROUP:(gi + 1) * POOL_GROUP].set(w_pool[l, gi])
    wr = w_router[l].T
    wr_hi = wr.astype(BF16)
    wr_lo = (wr - wr_hi.astype(F32)).astype(BF16)
    br = b_router[l].reshape(N_EXPERTS, 1)
    tri = (jnp.arange(TM_MERGE)[:, None] < jnp.arange(TM_MERGE)[None, :]).astype(BF16)
    lower = (jnp.arange(N_EXPERTS)[None, :] < jnp.arange(N_EXPERTS)[:, None]).astype(BF16)

    km, vm = _memkv(mem, norm_mem[l].reshape(1, D), w_mem_kv[l].astype(BF16))
    qc, qr, kvc, ksel, vsel, kwin, vwin, pool_in, qm, gates = _inproj(
        x2, norm_mix[l].reshape(1, D), w_a, cos_t, sin_t, S)
    o_nsa = _nsa(qc, qr, kvc, ksel, vsel, kwin, vwin, gates, w1, w2, w2t, pos, overlap_t, drop_bias, B, S)
    consts = [norm_mix[l].reshape(1, D), w_mg, wpool_bd.astype(BF16), pool_scale[l].reshape(1, POOL_WIDTH),
              w_up_nsa[l].astype(BF16), w_up_pool[l].astype(BF16), w_up_mem[l].astype(BF16), w_out[l].astype(BF16),
              norm_ffn[l].reshape(1, D), wr_hi, wr_lo, br, tri, lower]
    Tc = T // N_CHUNKS
    n_tiles = (Tc * TOP_K) // TM_E + N_EXPERTS
    tile_start = jnp.arange(n_tiles, dtype=jnp.int32) * TM_E
    wgu, bgu = w_gate_up[l], b_gate_up[l].reshape(N_EXPERTS, 1, 2 * D_FF)
    wd, bd = w_down[l], b_down[l].reshape(N_EXPERTS, 1, D_MODEL)

    def route_chunk(c):
        x1, hf, route, counts = _merge(x2, o_nsa, pool_in, qm, km, vm, consts, S, c * (Tc // TM_MERGE), Tc // TM_MERGE)
        counts = counts[:, 0].astype(jnp.int32)
        padded = ((counts + TM_E - 1) // TM_E) * TM_E
        ends = jnp.cumsum(padded)
        starts = ends - padded
        e_k = route[ROUTE_E:ROUTE_E + TOP_K].astype(jnp.int32)
        r_k = route[ROUTE_R:ROUTE_R + TOP_K].astype(jnp.int32)
        group_start = jnp.zeros_like(e_k)
        for e in range(N_EXPERTS):
            group_start = jnp.where(e_k == e, starts[e], group_start)
        dest = group_start + r_k
        tile_expert = jnp.minimum(jnp.sum(tile_start[:, None] >= ends[None, :], axis=1), N_EXPERTS - 1).astype(jnp.int32)
        n_used = (ends[-1] // TM_E).astype(jnp.int32).reshape(1)
        n_valid = jnp.clip((starts + counts)[tile_expert] - tile_start, 0, TM_E).astype(jnp.int32)
        xs = _sc_dispatch(hf, dest, n_tiles * TM_E)
        return x1, route, dest, (tile_expert, n_used, n_valid, xs)

    routed = [route_chunk(c) for c in range(N_CHUNKS)]
    expert_out = [_moe(*moe_args, wgu, bgu, wd, bd) for _, _, _, moe_args in routed]
    out = None
    for c, ((x1, route, dest, _), ys) in enumerate(zip(routed, expert_out)):
        yg = _sc_gather_rows(ys, dest.reshape(-1)).reshape(TOP_K, Tc, PACKED)
        out = _final(x1, yg, route, norm_final.reshape(1, D), out, c * (Tc // TM_FINAL), T // TM_FINAL)
    return out.reshape(B, S, D)
```

```python
import functools

import jax
import jax.numpy as jnp
from jax import lax
from jax.experimental import pallas as pl
from jax.experimental.pallas import tpu as pltpu
from jax.experimental.pallas import tpu_sc as plsc

F32 = jnp.float32
BF16 = jnp.bfloat16
U32 = jnp.uint32

D_MODEL = 1024
HEAD_DIM = 64
NSA_HEADS = 8
NSA_GROUPS = 2
HEADS_PER_GROUP = NSA_HEADS // NSA_GROUPS
NSA_WIDTH = NSA_HEADS * HEAD_DIM
KV_WIDTH = NSA_GROUPS * HEAD_DIM
CMP_BLOCK = 32
CMP_STRIDE = 16
SEL_BLOCK = 64
SEL_TOPN = 8
FORCE_BONUS = 1000.0
WINDOW = 512
POOL_WINDOWS = (2, 4, 8, 16)
POOL_GROUP = 64
POOL_WIDTH = POOL_GROUP * len(POOL_WINDOWS)
POOL_HALO = 16
MEM_HEADS = 4
MEM_WIDTH = MEM_HEADS * HEAD_DIM
N_EXPERTS = 32
TOP_K = 4
D_FF = 1024
SWIGLU_LIMIT = 7.0
SWIGLU_ALPHA = 1.702
ROPE_THETA = 10000.0
EPS = 1e-5
NEG_INF = -1e30
TINY = 1e-30
QK_SCALE = HEAD_DIM ** -0.5
LOG2_E = 1.4426950408889634
NSA_Q_SCALE = QK_SCALE * LOG2_E

LANES = 128
GATE_PAD = LANES
PACKED = D_MODEL // 2

TM_IN = 1024
TQ = 256
CKS = 512
WIN_CHUNKS = (256, 256, 256)
TM_MERGE = 512
TM_E = 512
TM_FINAL = 512
N_CHUNKS = 2
SC_CORES = 2
SC_SUBCORES = 16
SC_WORKERS = SC_CORES * SC_SUBCORES
SC_ROWS = 64
VMEM_LIMIT = 56 * 1024 * 1024


def _rms(x, g):
    return x * lax.rsqrt(jnp.mean(x * x, axis=-1, keepdims=True) + EPS) * g


def _sigmoid(x):
    return 0.5 * jnp.tanh(0.5 * x) + 0.5


def _dot(a, b):
    return jnp.dot(a, b, preferred_element_type=F32)


def _pack_bf16_pairs(v):
    n = v.shape[1] // 2
    r = v.astype(BF16).astype(F32)
    lo = pltpu.bitcast(r[:, :n], U32) >> 16
    hi = pltpu.bitcast(r[:, n:], U32) & jnp.uint32(0xFFFF0000)
    return lo | hi


def _unpack_bf16_pairs(w):
    lo = pltpu.bitcast(w << 16, F32)
    hi = pltpu.bitcast(w & jnp.uint32(0xFFFF0000), F32)
    return jnp.concatenate([lo, hi], axis=1)


def _dot_nt(a, b):
    return lax.dot_general(a, b, (((1,), (1,)), ((), ())), preferred_element_type=F32)


def _dot_tn(a, b):
    return lax.dot_general(a, b, (((0,), (0,)), ((), ())), preferred_element_type=F32)


def _memkv_kernel(mem_ref, g_ref, w_ref, k_ref, v_ref):
    m = _rms(mem_ref[0], g_ref[...]).astype(BF16)
    kv = _dot(m, w_ref[...])
    for h in range(MEM_HEADS):
        k_ref[0, h] = kv[:, h * HEAD_DIM:(h + 1) * HEAD_DIM].astype(BF16)
        v_ref[0, h] = kv[:, MEM_WIDTH + h * HEAD_DIM:MEM_WIDTH + (h + 1) * HEAD_DIM].astype(BF16)


def _memkv(mem, g, w):
    B, M, D = mem.shape
    return pl.pallas_call(
        _memkv_kernel,
        grid=(B,),
        in_specs=[pl.BlockSpec((1, M, D), lambda b: (b, 0, 0)),
                  pl.BlockSpec((1, D), lambda b: (0, 0)),
                  pl.BlockSpec((D, 2 * MEM_WIDTH), lambda b: (0, 0))],
        out_specs=[pl.BlockSpec((1, MEM_HEADS, M, HEAD_DIM), lambda b: (b, 0, 0, 0)),
                   pl.BlockSpec((1, MEM_HEADS, M, HEAD_DIM), lambda b: (b, 0, 0, 0))],
        out_shape=[jax.ShapeDtypeStruct((B, MEM_HEADS, M, HEAD_DIM), BF16),
                   jax.ShapeDtypeStruct((B, MEM_HEADS, M, HEAD_DIM), BF16)],
        compiler_params=pltpu.CompilerParams(dimension_semantics=("arbitrary",), vmem_limit_bytes=VMEM_LIMIT),
        name="memkv",
    )(mem, g, w)


IN_COLS = NSA_WIDTH + 6 * KV_WIDTH + POOL_WIDTH + MEM_WIDTH + GATE_PAD
CMP_COLS = CMP_STRIDE * 2 * KV_WIDTH


def _inproj_kernel(x_ref, g_ref, w_ref, cos_ref, sin_ref,
                   qc_ref, qr_ref, kvc_ref, ksel_ref, vsel_ref, kwin_ref, vwin_ref, pool_ref, qm_ref, gate_ref,
                   kv_s):
    h = _rms(x_ref[...], g_ref[...]).astype(BF16)
    p = _dot(h, w_ref[...])
    cos = cos_ref[...]
    sin = sin_ref[...]
    lane = lax.broadcasted_iota(jnp.int32, cos.shape, 1)
    first_half = (lane % HEAD_DIM) < (HEAD_DIM // 2)

    def rope(c):
        partner = jnp.where(first_half, pltpu.roll(c, LANES - HEAD_DIM // 2, 1), pltpu.roll(c, HEAD_DIM // 2, 1))
        return c * cos + partner * sin

    def halves(c):
        return c[:, :HEAD_DIM], c[:, HEAD_DIM:]

    for j in range(NSA_WIDTH // LANES):
        c = p[:, j * LANES:(j + 1) * LANES]
        r = rope(c)
        for hh, (cc, rr) in enumerate(zip(halves(c), halves(r))):
            qc_ref[2 * j + hh] = (cc * NSA_Q_SCALE).astype(BF16)
            qr_ref[2 * j + hh] = (rr * NSA_Q_SCALE).astype(BF16)
    o = NSA_WIDTH
    n_rows = kv_s.shape[1] // CMP_STRIDE
    for c in range(2):
        kv_s[c] = p[:, o + c * KV_WIDTH:o + (c + 1) * KV_WIDTH]
    for l_ in range(CMP_STRIDE):
        for c in range(2):
            col = (2 * l_ + c) * KV_WIDTH
            kvc_ref[:, col:col + KV_WIDTH] = kv_s[c, pl.ds(l_, n_rows, stride=CMP_STRIDE), :]
    o += 2 * KV_WIDTH
    for ref, rot in ((ksel_ref, True), (vsel_ref, False), (kwin_ref, True), (vwin_ref, False)):
        c = p[:, o:o + KV_WIDTH]
        if rot:
            c = rope(c)
        for g, cc in enumerate(halves(c)):
            ref[g] = cc.astype(BF16)
        o += KV_WIDTH
    pool_ref[...] = p[:, o:o + POOL_WIDTH]
    o += POOL_WIDTH
    for hh in range(MEM_HEADS):
        qm_ref[hh] = (p[:, o + hh * HEAD_DIM:o + (hh + 1) * HEAD_DIM] * QK_SCALE).astype(BF16)
    o += MEM_WIDTH
    gate_ref[...] = _sigmoid(p[:, o:o + GATE_PAD])


def _inproj(x2, g, w_a, cos_t, sin_t, S):
    T = x2.shape[0]
    tm = TM_IN
    n_s = S // tm
    head_spec = lambda n: pl.BlockSpec((n, tm, HEAD_DIM), lambda i: (0, i, 0))
    row_spec = lambda w: pl.BlockSpec((tm, w), lambda i: (i, 0))
    return pl.pallas_call(
        _inproj_kernel,
        grid=(T // tm,),
        in_specs=[row_spec(D_MODEL),
                  pl.BlockSpec((1, D_MODEL), lambda i: (0, 0)),
                  pl.BlockSpec((D_MODEL, IN_COLS), lambda i: (0, 0)),
                  pl.BlockSpec((tm, LANES), lambda i: (i % n_s, 0)),
                  pl.BlockSpec((tm, LANES), lambda i: (i % n_s, 0))],
        out_specs=[head_spec(NSA_HEADS), head_spec(NSA_HEADS),
                   pl.BlockSpec((tm // CMP_STRIDE, CMP_COLS), lambda i: (i, 0)),
                   head_spec(NSA_GROUPS), head_spec(NSA_GROUPS), head_spec(NSA_GROUPS), head_spec(NSA_GROUPS),
                   row_spec(POOL_WIDTH), head_spec(MEM_HEADS), row_spec(GATE_PAD)],
        out_shape=[jax.ShapeDtypeStruct((NSA_HEADS, T, HEAD_DIM), BF16),
                   jax.ShapeDtypeStruct((NSA_HEADS, T, HEAD_DIM), BF16),
                   jax.ShapeDtypeStruct((T // CMP_STRIDE, CMP_COLS), F32),
                   jax.ShapeDtypeStruct((NSA_GROUPS, T, HEAD_DIM), BF16),
                   jax.ShapeDtypeStruct((NSA_GROUPS, T, HEAD_DIM), BF16),
                   jax.ShapeDtypeStruct((NSA_GROUPS, T, HEAD_DIM), BF16),
                   jax.ShapeDtypeStruct((NSA_GROUPS, T, HEAD_DIM), BF16),
                   jax.ShapeDtypeStruct((T, POOL_WIDTH), F32),
                   jax.ShapeDtypeStruct((MEM_HEADS, T, HEAD_DIM), BF16),
                   jax.ShapeDtypeStruct((T, GATE_PAD), F32)],
        scratch_shapes=[pltpu.VMEM((2, tm, KV_WIDTH), F32)],
        compiler_params=pltpu.CompilerParams(dimension_semantics=("arbitrary",), vmem_limit_bytes=VMEM_LIMIT),
        name="inproj",
    )(x2, g, w_a, cos_t, sin_t)


N_CMP_PAD = 128
CMP_FLAT = CMP_BLOCK * HEAD_DIM
CMP_HALF = CMP_STRIDE * HEAD_DIM


def _nsa_kernel(qc_ref, qr_ref, kvc_ref, ksel_ref, vsel_ref, kwin_ref, vwin_ref, gate_ref,
                w1_ref, w2_ref, w2t_ref, pos_ref, ovl_ref, drop_ref, o_ref, kc_s, vct_s, *, S):
    i = pl.program_id(1)
    tq = TQ
    hpg = HEADS_PER_GROUP
    n_sel = S // SEL_BLOCK
    hq = hpg * tq

    @pl.when(i == 0)
    def _compress():
        a = kvc_ref[...].astype(BF16)
        p1 = _dot(a, w1_ref[0])
        p2 = _dot(a, w1_ref[1])
        posterm = (_dot(pos_ref[0], w1_ref[0]) + _dot(pos_ref[1], w1_ref[1]))[0:1]
        hid = p1 + pltpu.roll(p2, N_CMP_PAD - 1, 0) + posterm
        hid = (hid * _sigmoid(hid)).astype(BF16)
        for g in range(NSA_GROUPS):
            kc_s[g] = _dot(hid[:, g * HEAD_DIM:(g + 1) * HEAD_DIM], w2_ref[0]).astype(BF16)
            vct_s[g] = _dot_nt(w2t_ref[1], hid[:, KV_WIDTH + g * HEAD_DIM:KV_WIDTH + (g + 1) * HEAD_DIM]).astype(BF16)

    q0 = i * tq
    t_lane = q0 + lax.broadcasted_iota(jnp.int32, (1, tq), 1)

    def key_pos(start, n):
        return start + lax.broadcasted_iota(jnp.int32, (n, 1), 0)

    cmp_valid = (key_pos(0, N_CMP_PAD) * CMP_STRIDE + CMP_BLOCK - 1) <= t_lane
    jrow = lax.broadcasted_iota(jnp.int32, (n_sel, tq), 0)
    cur = t_lane // SEL_BLOCK
    sel_valid = jrow * SEL_BLOCK <= t_lane
    forced = (jrow == 0) | (jrow == cur) | (jrow == cur - 1)

    cd = q0 // CKS
    causal_bias = jnp.where(key_pos(cd * CKS, CKS) <= t_lane, 0.0, NEG_INF)
    win_chunks = []
    hi = q0 + tq
    for n in WIN_CHUNKS:
        lo = hi - n
        start = pl.multiple_of(jnp.maximum(lo, 0), LANES)
        kp = key_pos(start, n)
        diff = t_lane - kp
        win_chunks.append((start, n, jnp.where((diff >= 0) & (diff < WINDOW) & (kp < hi), 0.0, NEG_INF)))
        hi = lo

    def attend(jobs):
        scores = [_dot_nt(k, q_all) for q_all, k, _, _ in jobs]
        heads = [slice(hh * tq, (hh + 1) * tq) for hh in range(hpg)]
        maxes = [jnp.concatenate([jnp.max(s_all[:, sl] + bias, axis=0, keepdims=True) for sl in heads], axis=1)
                 for (_, _, _, bias), s_all in zip(jobs, scores)]
        soft = []
        for (_, _, _, bias), s_all, m in zip(jobs, scores, maxes):
            ps = [jnp.exp2(s_all[:, sl] + bias - m[:, sl]) for sl in heads]
            l = jnp.concatenate([jnp.sum(p, axis=0, keepdims=True) for p in ps], axis=1)
            soft.append((l, jnp.concatenate([p.astype(BF16) for p in ps], axis=1)))
        return [(m, l, _dot_tn(v, p)) for (_, _, v, _), m, (l, p) in zip(jobs, maxes, soft)]

    def merge(a, b):
        m = jnp.maximum(a[0], b[0])
        wa = jnp.exp2(a[0] - m)
        wb = jnp.exp2(b[0] - m)
        return m, wa * a[1] + wb * b[1], wa * a[2] + wb * b[2]

    def select_blocks(g):
        q_cmp = qc_ref[g * hpg:(g + 1) * hpg].reshape(hq, HEAD_DIM)
        s_all = _dot_nt(kc_s[g], q_cmp)
        p_grp = jnp.zeros((N_CMP_PAD, tq), F32)
        ps = []
        for hh in range(hpg):
            sl = slice(hh * tq, (hh + 1) * tq)
            s = jnp.where(cmp_valid, s_all[:, sl], NEG_INF)
            m = jnp.max(s, axis=0, keepdims=True)
            e = jnp.where(cmp_valid, jnp.exp2(s - m), 0.0)
            p = e * (1.0 / jnp.maximum(jnp.sum(e, axis=0, keepdims=True), TINY))
            p_grp = p_grp + p
            ps.append(p.astype(BF16))
        o_cmp = _dot(vct_s[g], jnp.concatenate(ps, axis=1))

        ovl = ovl_ref[...]
        p_hi = p_grp.astype(BF16)
        r1 = p_grp - p_hi.astype(F32)
        p_mid = r1.astype(BF16)
        p_lo = (r1 - p_mid.astype(F32)).astype(BF16)
        score = _dot(ovl, p_hi) + _dot(ovl, p_mid) + _dot(ovl, p_lo)
        score = jnp.where(sel_valid, score + jnp.where(forced, FORCE_BONUS, 0.0), -1.0)
        rank = jnp.zeros((n_sel, tq), F32)
        for jp in range(n_sel):
            other = score[jp:jp + 1, :]
            beats = (other > score) | ((other == score) & (jrow > jp))
            rank = rank + beats.astype(F32)
        return o_cmp, (rank >= SEL_TOPN).astype(BF16)

    groups = range(NSA_GROUPS)
    cmp_out = [select_blocks(g) for g in groups]
    q_rot = [qr_ref[g * hpg:(g + 1) * hpg].reshape(hq, HEAD_DIM) for g in groups]

    def selected_job(g, c, extra_bias):
        k0 = pl.multiple_of(c * CKS, CKS)
        bias = _dot(drop_ref[c], cmp_out[g][1])
        if extra_bias is not None:
            bias = bias + extra_bias
        return q_rot[g], ksel_ref[g, pl.ds(k0, CKS), :], vsel_ref[g, pl.ds(k0, CKS), :], bias

    def window_job(g, chunk):
        start, n, bias = chunk
        return q_rot[g], kwin_ref[g, pl.ds(start, n), :], vwin_ref[g, pl.ds(start, n), :], bias

    stats = attend([selected_job(g, cd, causal_bias) for g in groups]
                   + [window_job(g, chunk) for chunk in win_chunks for g in groups])
    sel_state = tuple(stats[:NSA_GROUPS])
    win_state = list(stats[NSA_GROUPS:2 * NSA_GROUPS])
    for j in range(1, len(win_chunks)):
        for g in groups:
            win_state[g] = merge(win_state[g], stats[NSA_GROUPS * (1 + j) + g])

    def sel_body(it, states):
        new = attend([selected_job(g, cd - 1 - it, None) for g in groups])
        return tuple(merge(states[g], new[g]) for g in groups)

    sel_state = lax.fori_loop(0, cd, sel_body, sel_state)

    gates_t = gate_ref[...].T
    out_rows = []
    for g in groups:
        o_cmp = cmp_out[g][0]
        o_sel = sel_state[g][2] * (1.0 / sel_state[g][1])
        o_win = win_state[g][2] * (1.0 / win_state[g][1])
        for hh in range(hpg):
            h = g * hpg + hh
            sl = slice(hh * tq, (hh + 1) * tq)
            out_rows.append(gates_t[3 * h:3 * h + 1] * o_cmp[:, sl] + gates_t[3 * h + 1:3 * h + 2] * o_sel[:, sl]
                            + gates_t[3 * h + 2:3 * h + 3] * o_win[:, sl])
    o_ref[...] = jnp.concatenate(out_rows, axis=0).T.astype(BF16)


def _nsa(qc, qr, kvc, ksel, vsel_t, kwin, vwin_t, gates, w1, w2, w2t, pos, ovl, drop_bias, B, S):
    T = B * S
    tq = TQ
    nq = S // tq
    q_spec = pl.BlockSpec((NSA_HEADS, tq, HEAD_DIM), lambda b, i: (0, b * nq + i, 0))
    k_spec = pl.BlockSpec((NSA_GROUPS, S, HEAD_DIM), lambda b, i: (0, b, 0))
    full = lambda a: pl.BlockSpec(a.shape, lambda b, i: (0,) * a.ndim)
    return pl.pallas_call(
        functools.partial(_nsa_kernel, S=S),
        grid=(B, nq),
        in_specs=[q_spec, q_spec,
                  pl.BlockSpec((N_CMP_PAD, CMP_COLS), lambda b, i: (b, 0)),
                  k_spec, k_spec, k_spec, k_spec,
                  pl.BlockSpec((tq, GATE_PAD), lambda b, i: (b * nq + i, 0)),
                  full(w1), full(w2), full(w2t), full(pos), full(ovl), full(drop_bias)],
        out_specs=pl.BlockSpec((tq, NSA_WIDTH), lambda b, i: (b * nq + i, 0)),
        out_shape=jax.ShapeDtypeStruct((T, NSA_WIDTH), BF16),
        scratch_shapes=[pltpu.VMEM((NSA_GROUPS, N_CMP_PAD, HEAD_DIM), BF16),
                        pltpu.VMEM((NSA_GROUPS, HEAD_DIM, N_CMP_PAD), BF16)],
        compiler_params=pltpu.CompilerParams(dimension_semantics=("arbitrary", "arbitrary"),
                                             vmem_limit_bytes=VMEM_LIMIT),
        name="nsa",
    )(qc, qr, kvc, ksel, vsel_t, kwin, vwin_t, gates, w1, w2, w2t, pos, ovl, drop_bias)


ROUTE_E, ROUTE_R, ROUTE_W = 0, TOP_K, 2 * TOP_K
ROUTE_ROWS = 16


def _merge_kernel(x_ref, onsa_ref, pool_ref, prev_ref, qm_ref, km_ref, vm_ref,
                  gmix_ref, wmg_ref, wpool_ref, pscale_ref, wun_ref, wup_ref, wum_ref, wout_ref,
                  gffn_ref, wrh_ref, wrl_ref, br_ref, tri_ref, lower_ref,
                  x1_ref, hf_ref, route_ref, cnt_ref, omem_s, carry_s, *, S, tile0):
    i = pl.program_id(0) + tile0
    tm = TM_MERGE
    n_s = S // tm

    @pl.when(pl.program_id(0) == 0)
    def _init():
        carry_s[...] = jnp.zeros_like(carry_s)

    x = x_ref[...]
    h = _rms(x, gmix_ref[...]).astype(BF16)

    u = pool_ref[...]
    seq_tile = i % n_s
    prev = jnp.where(seq_tile == 0, 0.0, prev_ref[...])
    ext = jnp.concatenate([prev, u], axis=0)
    b2 = ext[1:] + ext[:-1]
    b4 = b2[2:] + b2[:-2]
    b8 = b4[4:] + b4[:-4]
    b16 = b8[8:] + b8[:-8]
    sums = (b2[POOL_HALO - 1:POOL_HALO - 1 + tm], b4[POOL_HALO - 3:POOL_HALO - 3 + tm],
            b8[POOL_HALO - 7:POOL_HALO - 7 + tm], b16[POOL_HALO - 15:POOL_HALO - 15 + tm])
    t_seq = seq_tile * tm + lax.broadcasted_iota(jnp.int32, (tm, 1), 0)
    lane_p = lax.broadcasted_iota(jnp.int32, (tm, POOL_WIDTH), 1)
    z = jnp.zeros((tm, POOL_WIDTH), F32)
    for gi, w in enumerate(POOL_WINDOWS):
        cnt = jnp.minimum(t_seq + 1, w).astype(F32)
        z = jnp.where(lane_p // POOL_GROUP == gi, sums[gi] / cnt, z)
    z = z - u
    o_pool = (_dot(z.astype(BF16), wpool_ref[...]) * pscale_ref[...]).astype(BF16)

    for hh in range(MEM_HEADS):
        s = _dot_nt(qm_ref[hh], km_ref[0, hh])
        m = jnp.max(s, axis=-1, keepdims=True)
        e = jnp.exp(s - m)
        p = e / jnp.sum(e, axis=-1, keepdims=True)
        omem_s[:, hh * HEAD_DIM:(hh + 1) * HEAD_DIM] = _dot(p.astype(BF16), vm_ref[0, hh]).astype(BF16)

    def gated(branch, w_ref, j):
        return _sigmoid(_dot(h, wmg_ref[:, j * D_MODEL:(j + 1) * D_MODEL])) * _dot(branch, w_ref[...])

    merged = gated(onsa_ref[...], wun_ref, 0) + gated(o_pool, wup_ref, 1) + gated(omem_s[...], wum_ref, 2)
    x1 = x + _dot(merged.astype(BF16), wout_ref[...])
    x1_ref[...] = x1
    hf = _rms(x1, gffn_ref[...])
    hf_ref[...] = _pack_bf16_pairs(hf)

    hf_hi = hf.astype(BF16)
    hf_lo = (hf - hf_hi.astype(F32)).astype(BF16)
    logits = (_dot_nt(wrh_ref[...], hf_hi) + _dot_nt(wrl_ref[...], hf_hi) + _dot_nt(wrh_ref[...], hf_lo)
              + br_ref[...])
    erow = lax.broadcasted_iota(jnp.int32, (N_EXPERTS, tm), 0)
    rank = jnp.zeros((N_EXPERTS, tm), F32)
    for jp in range(N_EXPERTS):
        other = logits[jp:jp + 1, :]
        beats = (other > logits) | ((other == logits) & (erow > jp))
        rank = rank + beats.astype(F32)
    chosen = rank < TOP_K
    m = jnp.max(logits, axis=0, keepdims=True)
    e = jnp.where(chosen, jnp.exp(logits - m), 0.0)
    comb = e * (1.0 / jnp.sum(e, axis=0, keepdims=True))

    chosen_b = chosen.astype(BF16)
    carry = carry_s[:, 0:1]
    in_expert = _dot(chosen_b, tri_ref[...]) + carry
    carry_new = carry + jnp.sum(chosen.astype(F32), axis=1, keepdims=True)
    carry_s[...] = jnp.broadcast_to(carry_new, carry_s.shape)
    cnt_ref[...] = jnp.broadcast_to(carry_new, cnt_ref.shape)

    before = _dot(lower_ref[...], chosen_b)
    erow_f = erow.astype(F32)
    fields = {ROUTE_E: erow_f, ROUTE_R: in_expert, ROUTE_W: comb}
    rows = [None] * ROUTE_ROWS
    for k in range(TOP_K):
        pick = chosen & (before == k)
        for base, val in fields.items():
            rows[base + k] = jnp.sum(jnp.where(pick, val, 0.0), axis=0, keepdims=True)
    zero_row = jnp.zeros((1, tm), F32)
    route_ref[...] = jnp.concatenate([zero_row if r is None else r for r in rows], axis=0)


def _merge(x2, onsa, pool_in, qm, km, vm, consts, S, tile0, n_tiles):
    tm = TM_MERGE
    Tc = n_tiles * tm
    n_s = S // tm
    M = km.shape[2]
    halo_per_tile = tm // POOL_HALO
    row = lambda w: pl.BlockSpec((tm, w), lambda i: (i + tile0, 0))
    out_row = lambda w: pl.BlockSpec((tm, w), lambda i: (i, 0))
    full = lambda a: pl.BlockSpec(a.shape, lambda i: (0,) * a.ndim)
    mem_spec = pl.BlockSpec((1, MEM_HEADS, M, HEAD_DIM), lambda i: ((i + tile0) // n_s, 0, 0, 0))
    return pl.pallas_call(
        functools.partial(_merge_kernel, S=S, tile0=tile0),
        grid=(n_tiles,),
        in_specs=[row(D_MODEL), row(NSA_WIDTH), row(POOL_WIDTH),
                  pl.BlockSpec((POOL_HALO, POOL_WIDTH),
                               lambda i: (jnp.maximum((i + tile0) * halo_per_tile - 1, 0), 0)),
                  pl.BlockSpec((MEM_HEADS, tm, HEAD_DIM), lambda i: (0, i + tile0, 0)),
                  mem_spec, mem_spec] + [full(c) for c in consts],
        out_specs=[out_row(D_MODEL), out_row(PACKED), pl.BlockSpec((ROUTE_ROWS, tm), lambda i: (0, i)),
                   pl.BlockSpec((N_EXPERTS, LANES), lambda i: (0, 0))],
        out_shape=[jax.ShapeDtypeStruct((Tc, D_MODEL), F32),
                   jax.ShapeDtypeStruct((Tc, PACKED), U32),
                   jax.ShapeDtypeStruct((ROUTE_ROWS, Tc), F32),
                   jax.ShapeDtypeStruct((N_EXPERTS, LANES), F32)],
        scratch_shapes=[pltpu.VMEM((tm, MEM_WIDTH), BF16), pltpu.VMEM((N_EXPERTS, LANES), F32)],
        compiler_params=pltpu.CompilerParams(dimension_semantics=("arbitrary",), vmem_limit_bytes=VMEM_LIMIT),
        name="merge",
    )(x2, onsa, pool_in, pool_in, qm, km, vm, *consts)


def _sc_mesh():
    return plsc.VectorSubcoreMesh(core_axis_name="c", subcore_axis_name="s")


def _sc_worker():
    return lax.axis_index("s") * SC_CORES + lax.axis_index("c")


def _sc_dispatch(hf, dest, n_rows):
    T, d = hf.shape
    per_worker = T // SC_WORKERS
    steps = per_worker // SC_ROWS

    @functools.partial(pl.kernel, mesh=_sc_mesh(), out_type=jax.ShapeDtypeStruct((n_rows, d), hf.dtype),
                       scratch_types=[pltpu.VMEM((TOP_K, SC_ROWS), jnp.int32), pltpu.VMEM((SC_ROWS, d), hf.dtype),
                                      pltpu.SemaphoreType.DMA])
    def dispatch(hf_hbm, dest_hbm, xs_hbm, idx_v, rows_v, sem):
        base = _sc_worker() * per_worker

        @pl.loop(0, steps)
        def _(j):
            rows = pl.ds(pl.multiple_of(base + j * SC_ROWS, SC_ROWS), SC_ROWS)
            pltpu.sync_copy(hf_hbm.at[rows], rows_v)
            for k in range(TOP_K):
                pltpu.sync_copy(dest_hbm.at[k, rows], idx_v.at[k])
            copies = [pltpu.make_async_copy(rows_v, xs_hbm.at[idx_v.at[k]], sem) for k in range(TOP_K)]
            for c in copies:
                c.start()
            for c in copies:
                c.wait()

    return dispatch(hf, dest)


def _moe_kernel(te_ref, nu_ref, nv_ref, xs_ref, wgu_ref, bgu_ref, wd_ref, bd_ref, ys_ref, wgu_s, wd_s):
    j = pl.program_id(0)
    used = j < nu_ref[0]
    new_expert = (j == 0) | (te_ref[j] != te_ref[jnp.maximum(j - 1, 0)])

    @pl.when(used & new_expert)
    def _cast_weights():
        wgu_s[...] = wgu_ref[0].astype(BF16)
        wd_s[...] = wd_ref[0].astype(BF16)

    @pl.when(used)
    def _compute():
        live = lax.broadcasted_iota(jnp.int32, (xs_ref.shape[0], 1), 0) < nv_ref[j]
        xb = _unpack_bf16_pairs(jnp.where(live, xs_ref[...], jnp.uint32(0))).astype(BF16)
        gu = _dot(xb, wgu_s[...]) + bgu_ref[0]
        gate = jnp.minimum(gu[:, :D_FF], SWIGLU_LIMIT)
        up = jnp.clip(gu[:, D_FF:], -SWIGLU_LIMIT, SWIGLU_LIMIT)
        act = (up + 1.0) * (gate * _sigmoid(SWIGLU_ALPHA * gate))
        ys_ref[...] = _pack_bf16_pairs(_dot(act.astype(BF16), wd_s[...]) + bd_ref[0])

    @pl.when(j >= nu_ref[0])
    def _unused():
        ys_ref[...] = jnp.zeros_like(ys_ref)


def _moe(tile_expert, n_used, n_valid, xs, wgu, bgu, wd, bd):
    P = xs.shape[0]
    tm = TM_E
    grid_spec = pltpu.PrefetchScalarGridSpec(
        num_scalar_prefetch=3,
        grid=(P // tm,),
        in_specs=[pl.BlockSpec((tm, PACKED), lambda j, te, nu, nv: (j, 0)),
                  pl.BlockSpec((1, D_MODEL, 2 * D_FF), lambda j, te, nu, nv: (te[j], 0, 0)),
                  pl.BlockSpec((1, 1, 2 * D_FF), lambda j, te, nu, nv: (te[j], 0, 0)),
                  pl.BlockSpec((1, D_FF, D_MODEL), lambda j, te, nu, nv: (te[j], 0, 0)),
                  pl.BlockSpec((1, 1, D_MODEL), lambda j, te, nu, nv: (te[j], 0, 0))],
        out_specs=pl.BlockSpec((tm, PACKED), lambda j, te, nu, nv: (j, 0)),
        scratch_shapes=[pltpu.VMEM((D_MODEL, 2 * D_FF), BF16), pltpu.VMEM((D_FF, D_MODEL), BF16)],
    )
    return pl.pallas_call(
        _moe_kernel,
        grid_spec=grid_spec,
        out_shape=jax.ShapeDtypeStruct((P, PACKED), U32),
        compiler_params=pltpu.CompilerParams(dimension_semantics=("arbitrary",), vmem_limit_bytes=VMEM_LIMIT),
        name="moe",
    )(tile_expert, n_used, n_valid, xs, wgu, bgu, wd, bd)


def _sc_gather_rows(table, idx):
    n, d = idx.shape[0], table.shape[1]
    per_worker = n // SC_WORKERS
    steps = per_worker // SC_ROWS
    assert steps % 2 == 0

    @functools.partial(pl.kernel, mesh=_sc_mesh(), out_type=jax.ShapeDtypeStruct((n, d), table.dtype),
                       scratch_types=[pltpu.VMEM((2, SC_ROWS), jnp.int32), pltpu.VMEM((2, SC_ROWS, d), table.dtype),
                                      pltpu.SemaphoreType.DMA((2,)), pltpu.SemaphoreType.DMA((2,))])
    def gather(table_hbm, idx_hbm, out_hbm, idx_v, rows_v, gsem, wsem):
        base = _sc_worker() * per_worker

        def rows_at(j):
            return pl.ds(pl.multiple_of(base + j * SC_ROWS, SC_ROWS), SC_ROWS)

        def gather_copy(slot):
            return pltpu.make_async_copy(table_hbm.at[idx_v.at[slot]], rows_v.at[slot], gsem.at[slot])

        def write_copy(j, slot):
            return pltpu.make_async_copy(rows_v.at[slot], out_hbm.at[rows_at(j)], wsem.at[slot])

        def fetch(j, slot):
            pltpu.sync_copy(idx_hbm.at[rows_at(j)], idx_v.at[slot])
            gather_copy(slot).start()

        fetch(0, 0)

        @pl.loop(0, steps, step=2)
        def _(j0):
            for slot in range(2):
                j = j0 + slot
                gather_copy(slot).wait()

                @pl.when(j >= 1)
                def _():
                    write_copy(j - 1, 1 - slot).wait()

                @pl.when(j + 1 < steps)
                def _():
                    fetch(j + 1, 1 - slot)

                write_copy(j, slot).start()

        write_copy(steps - 1, 1).wait()

    return gather(table, idx)


def _final_kernel(x1_ref, yg_ref, route_ref, g_ref, o_ref):
    tm = x1_ref.shape[0]
    route_t = jnp.concatenate([route_ref[...], jnp.zeros((LANES - ROUTE_ROWS, tm), F32)], axis=0).T
    acc = x1_ref[...]
    for k in range(TOP_K):
        acc = acc + route_t[:, ROUTE_W + k:ROUTE_W + k + 1] * _unpack_bf16_pairs(yg_ref[k])
    o_ref[...] = _rms(acc, g_ref[...])


def _final(x1, yg, route, g, out_prev, tile0, n_total):
    Tc = x1.shape[0]
    tm = TM_FINAL
    in_specs = [pl.BlockSpec((tm, D_MODEL), lambda i: (i, 0)),
                pl.BlockSpec((TOP_K, tm, PACKED), lambda i: (0, i, 0)),
                pl.BlockSpec((ROUTE_ROWS, tm), lambda i: (0, i)),
                pl.BlockSpec((1, D_MODEL), lambda i: (0, 0))]
    args = [x1, yg, route, g]
    kernel_fn, aliases = _final_kernel, {}
    if out_prev is not None:
        in_specs.append(pl.BlockSpec(memory_space=pl.ANY))
        args.append(out_prev)
        kernel_fn = lambda x1_ref, yg_ref, route_ref, g_ref, prev_ref, o_ref: _final_kernel(x1_ref, yg_ref, route_ref,
                                                                                            g_ref, o_ref)
        aliases = {len(args) - 1: 0}
    return pl.pallas_call(
        kernel_fn,
        grid=(Tc // tm,),
        in_specs=in_specs,
        out_specs=pl.BlockSpec((tm, D_MODEL), lambda i: (i + tile0, 0)),
        out_shape=jax.ShapeDtypeStruct((n_total * tm, D_MODEL), F32),
        input_output_aliases=aliases,
        compiler_params=pltpu.CompilerParams(dimension_semantics=("arbitrary",), vmem_limit_bytes=VMEM_LIMIT),
        name="final",
    )(*args)


def _rope_tables(S):
    half = HEAD_DIM // 2
    inv = ROPE_THETA ** (-jnp.arange(half, dtype=F32) / half)
    ang = jnp.arange(S, dtype=F32)[:, None] * inv[None, :]
    cos = jnp.tile(jnp.cos(ang), (1, LANES // half))
    sin = jnp.tile(jnp.concatenate([-jnp.sin(ang), jnp.sin(ang)], axis=1), (1, LANES // HEAD_DIM))
    return cos, sin


def _selection_constants(S):
    nc = (S - CMP_BLOCK) // CMP_STRIDE + 1
    n_sel = S // SEL_BLOCK
    j = jnp.arange(n_sel)[:, None]
    i = jnp.arange(N_CMP_PAD)[None, :]
    overlap_t = ((i * CMP_STRIDE <= j * SEL_BLOCK + SEL_BLOCK - 1)
                 & (i * CMP_STRIDE + CMP_BLOCK - 1 >= j * SEL_BLOCK) & (i < nc)).astype(BF16)
    row = jnp.arange(LANES)[:, None]
    key = jnp.arange(S)[None, :]
    drop_bias = jnp.where(key // SEL_BLOCK == row, NEG_INF, 0.0).astype(BF16)[:n_sel]
    drop_bias = drop_bias.T.reshape(S // CKS, CKS, n_sel)
    return overlap_t, drop_bias


def kernel(x, mem, norm_mix, norm_mem, w_in, cmp_pos, cmp_w1, cmp_w2, w_pool, pool_scale, w_mem_kv, w_up_nsa,
           w_up_pool, w_up_mem, w_out, norm_ffn, w_router, b_router, w_gate_up, b_gate_up, w_down, b_down,
           norm_final):
    B, S, D = x.shape
    T = B * S
    assert D == D_MODEL and S % CKS == 0 and S // SEL_BLOCK == 32 and T % (N_CHUNKS * SC_WORKERS * SC_ROWS * 2) == 0
    l = 0
    x2 = x.reshape(T, D)

    w = w_in[l]
    o_gate = NSA_WIDTH + 6 * KV_WIDTH
    n_gate = 3 * NSA_HEADS
    o_pool = o_gate + n_gate
    o_qm = o_pool + POOL_WIDTH
    o_mg = o_qm + MEM_WIDTH
    w_a = jnp.concatenate([w[:, :o_gate], w[:, o_pool:o_mg], w[:, o_gate:o_pool],
                           jnp.zeros((D, GATE_PAD - n_gate), F32)], axis=1).astype(BF16)
    w_mg = w[:, o_mg:].astype(BF16)
    cos_t, sin_t = _rope_tables(S)
    overlap_t, drop_bias = _selection_constants(S)
    stream_kv = jnp.arange(2 * NSA_GROUPS) // NSA_GROUPS
    w1_half = cmp_w1[l].reshape(2, 2, CMP_STRIDE, HEAD_DIM, HEAD_DIM)[stream_kv]
    w1 = jnp.einsum("shlde,ts->hltdse", w1_half, jnp.eye(2 * NSA_GROUPS, dtype=F32))
    w1 = w1.reshape(2, CMP_COLS, 2 * KV_WIDTH).astype(BF16)
    w2 = cmp_w2[l].astype(BF16)
    w2t = jnp.swapaxes(w2, 1, 2)
    pos_half = cmp_pos[l].reshape(2, 2, CMP_STRIDE, HEAD_DIM)[stream_kv]
    pos = jnp.broadcast_to(pos_half.transpose(1, 2, 0, 3).reshape(2, 1, CMP_COLS), (2, 8, CMP_COLS)).astype(BF16)
    wpool_bd = jnp.zeros((POOL_WIDTH, POOL_WIDTH), F32)
    for gi in range(len(POOL_WINDOWS)):
        wpool_bd = wpool_bd.at[gi * POOL_GROUP:(gi + 1) * POOL_GROUP, gi * POOL_GROUP:(gi + 1) * POOL_GROUP].set(w_pool[l, gi])
    wr = w_router[l].T
    wr_hi = wr.astype(BF16)
    wr_lo = (wr - wr_hi.astype(F32)).astype(BF16)
    br = b_router[l].reshape(N_EXPERTS, 1)
    tri = (jnp.arange(TM_MERGE)[:, None] < jnp.arange(TM_MERGE)[None, :]).astype(BF16)
    lower = (jnp.arange(N_EXPERTS)[None, :] < jnp.arange(N_EXPERTS)[:, None]).astype(BF16)

    km, vm = _memkv(mem, norm_mem[l].reshape(1, D), w_mem_kv[l].astype(BF16))
    qc, qr, kvc, ksel, vsel, kwin, vwin, pool_in, qm, gates = _inproj(
        x2, norm_mix[l].reshape(1, D), w_a, cos_t, sin_t, S)
    o_nsa = _nsa(qc, qr, kvc, ksel, vsel, kwin, vwin, gates, w1, w2, w2t, pos, overlap_t, drop_bias, B, S)
    consts = [norm_mix[l].reshape(1, D), w_mg, wpool_bd.astype(BF16), pool_scale[l].reshape(1, POOL_WIDTH),
              w_up_nsa[l].astype(BF16), w_up_pool[l].astype(BF16), w_up_mem[l].astype(BF16), w_out[l].astype(BF16),
              norm_ffn[l].reshape(1, D), wr_hi, wr_lo, br, tri, lower]
    Tc = T // N_CHUNKS
    n_tiles = (Tc * TOP_K) // TM_E + N_EXPERTS
    tile_start = jnp.arange(n_tiles, dtype=jnp.int32) * TM_E
    wgu, bgu = w_gate_up[l], b_gate_up[l].reshape(N_EXPERTS, 1, 2 * D_FF)
    wd, bd = w_down[l], b_down[l].reshape(N_EXPERTS, 1, D_MODEL)

    def route_chunk(c):
        x1, hf, route, counts = _merge(x2, o_nsa, pool_in, qm, km, vm, consts, S, c * (Tc // TM_MERGE), Tc // TM_MERGE)
        counts = counts[:, 0].astype(jnp.int32)
        padded = ((counts + TM_E - 1) // TM_E) * TM_E
        ends = jnp.cumsum(padded)
        starts = ends - padded
        e_k = route[ROUTE_E:ROUTE_E + TOP_K].astype(jnp.int32)
        r_k = route[ROUTE_R:ROUTE_R + TOP_K].astype(jnp.int32)
        group_start = jnp.zeros_like(e_k)
        for e in range(N_EXPERTS):
            group_start = jnp.where(e_k == e, starts[e], group_start)
        dest = group_start + r_k
        tile_expert = jnp.minimum(jnp.sum(tile_start[:, None] >= ends[None, :], axis=1), N_EXPERTS - 1).astype(jnp.int32)
        n_used = (ends[-1] // TM_E).astype(jnp.int32).reshape(1)
        n_valid = jnp.clip((starts + counts)[tile_expert] - tile_start, 0, TM_E).astype(jnp.int32)
        xs = _sc_dispatch(hf, dest, n_tiles * TM_E)
        return x1, route, dest, (tile_expert, n_used, n_valid, xs)

    routed = [route_chunk(c) for c in range(N_CHUNKS)]
    expert_out = [_moe(*moe_args, wgu, bgu, wd, bd) for _, _, _, moe_args in routed]
    out = None
    for c, ((x1, route, dest, _), ys) in enumerate(zip(routed, expert_out)):
        yg = _sc_gather_rows(ys, dest.reshape(-1)).reshape(TOP_K, Tc, PACKED)
        out = _final(x1, yg, route, norm_final.reshape(1, D), out, c * (Tc // TM_FINAL), T // TM_FINAL)
    return out.reshape(B, S, D)
```

```python
import functools

import jax
import jax.numpy as jnp
from jax import lax
from jax.experimental import pallas as pl
from jax.experimental.pallas import tpu as pltpu
from jax.experimental.pallas import tpu_sc as plsc

F32 = jnp.float32
BF16 = jnp.bfloat16
U32 = jnp.uint32

D_MODEL = 1024
HEAD_DIM = 64
NSA_HEADS = 8
NSA_GROUPS = 2
HEADS_PER_GROUP = NSA_HEADS // NSA_GROUPS
NSA_WIDTH = NSA_HEADS * HEAD_DIM
KV_WIDTH = NSA_GROUPS * HEAD_DIM
CMP_BLOCK = 32
CMP_STRIDE = 16
SEL_BLOCK = 64
SEL_TOPN = 8
FORCE_BONUS = 1000.0
WINDOW = 512
POOL_WINDOWS = (2, 4, 8, 16)
POOL_GROUP = 64
POOL_WIDTH = POOL_GROUP * len(POOL_WINDOWS)
POOL_HALO = 16
MEM_HEADS = 4
MEM_WIDTH = MEM_HEADS * HEAD_DIM
N_EXPERTS = 32
TOP_K = 4
D_FF = 1024
SWIGLU_LIMIT = 7.0
SWIGLU_ALPHA = 1.702
ROPE_THETA = 10000.0
EPS = 1e-5
NEG_INF = -1e30
TINY = 1e-30
QK_SCALE = HEAD_DIM ** -0.5
LOG2_E = 1.4426950408889634
NSA_Q_SCALE = QK_SCALE * LOG2_E

LANES = 128
GATE_PAD = LANES
PACKED = D_MODEL // 2

TM_IN = 1024
TQ = 256
CKS = 512
WIN_CHUNKS = (256, 256, 256)
TM_MERGE = 512
TM_E = 512
TM_FINAL = 512
CHUNK_SHARES = (3, 1)
SC_CORES = 2
SC_SUBCORES = 16
SC_WORKERS = SC_CORES * SC_SUBCORES
SC_ROWS = 64
VMEM_LIMIT = 56 * 1024 * 1024


def _rms(x, g):
    return x * lax.rsqrt(jnp.mean(x * x, axis=-1, keepdims=True) + EPS) * g


def _sigmoid(x):
    return 0.5 * jnp.tanh(0.5 * x) + 0.5


def _dot(a, b):
    return jnp.dot(a, b, preferred_element_type=F32)


def _pack_bf16_pairs(v):
    n = v.shape[1] // 2
    r = v.astype(BF16).astype(F32)
    lo = pltpu.bitcast(r[:, :n], U32) >> 16
    hi = pltpu.bitcast(r[:, n:], U32) & jnp.uint32(0xFFFF0000)
    return lo | hi


def _unpack_bf16_pairs(w):
    lo = pltpu.bitcast(w << 16, F32)
    hi = pltpu.bitcast(w & jnp.uint32(0xFFFF0000), F32)
    return jnp.concatenate([lo, hi], axis=1)


def _dot_nt(a, b):
    return lax.dot_general(a, b, (((1,), (1,)), ((), ())), preferred_element_type=F32)


def _dot_tn(a, b):
    return lax.dot_general(a, b, (((0,), (0,)), ((), ())), preferred_element_type=F32)


def _memkv_kernel(mem_ref, g_ref, w_ref, k_ref, v_ref):
    m = _rms(mem_ref[0], g_ref[...]).astype(BF16)
    kv = _dot(m, w_ref[...])
    for h in range(MEM_HEADS):
        k_ref[0, h] = kv[:, h * HEAD_DIM:(h + 1) * HEAD_DIM].astype(BF16)
        v_ref[0, h] = kv[:, MEM_WIDTH + h * HEAD_DIM:MEM_WIDTH + (h + 1) * HEAD_DIM].astype(BF16)


def _memkv(mem, g, w):
    B, M, D = mem.shape
    return pl.pallas_call(
        _memkv_kernel,
        grid=(B,),
        in_specs=[pl.BlockSpec((1, M, D), lambda b: (b, 0, 0)),
                  pl.BlockSpec((1, D), lambda b: (0, 0)),
                  pl.BlockSpec((D, 2 * MEM_WIDTH), lambda b: (0, 0))],
        out_specs=[pl.BlockSpec((1, MEM_HEADS, M, HEAD_DIM), lambda b: (b, 0, 0, 0)),
                   pl.BlockSpec((1, MEM_HEADS, M, HEAD_DIM), lambda b: (b, 0, 0, 0))],
        out_shape=[jax.ShapeDtypeStruct((B, MEM_HEADS, M, HEAD_DIM), BF16),
                   jax.ShapeDtypeStruct((B, MEM_HEADS, M, HEAD_DIM), BF16)],
        compiler_params=pltpu.CompilerParams(dimension_semantics=("arbitrary",), vmem_limit_bytes=VMEM_LIMIT),
        name="memkv",
    )(mem, g, w)


IN_COLS = NSA_WIDTH + 6 * KV_WIDTH + POOL_WIDTH + MEM_WIDTH + GATE_PAD
CMP_COLS = CMP_STRIDE * 2 * KV_WIDTH


def _inproj_kernel(x_ref, g_ref, w_ref, cos_ref, sin_ref,
                   qc_ref, qr_ref, kvc_ref, ksel_ref, vsel_ref, kwin_ref, vwin_ref, pool_ref, qm_ref, gate_ref,
                   kv_s):
    h = _rms(x_ref[...], g_ref[...]).astype(BF16)
    p = _dot(h, w_ref[...])
    cos = cos_ref[...]
    sin = sin_ref[...]
    lane = lax.broadcasted_iota(jnp.int32, cos.shape, 1)
    first_half = (lane % HEAD_DIM) < (HEAD_DIM // 2)

    def rope(c):
        partner = jnp.where(first_half, pltpu.roll(c, LANES - HEAD_DIM // 2, 1), pltpu.roll(c, HEAD_DIM // 2, 1))
        return c * cos + partner * sin

    def halves(c):
        return c[:, :HEAD_DIM], c[:, HEAD_DIM:]

    for j in range(NSA_WIDTH // LANES):
        c = p[:, j * LANES:(j + 1) * LANES]
        r = rope(c)
        for hh, (cc, rr) in enumerate(zip(halves(c), halves(r))):
            qc_ref[2 * j + hh] = (cc * NSA_Q_SCALE).astype(BF16)
            qr_ref[2 * j + hh] = (rr * NSA_Q_SCALE).astype(BF16)
    o = NSA_WIDTH
    n_rows = kv_s.shape[1] // CMP_STRIDE
    for c in range(2):
        kv_s[c] = p[:, o + c * KV_WIDTH:o + (c + 1) * KV_WIDTH]
    for l_ in range(CMP_STRIDE):
        for c in range(2):
            col = (2 * l_ + c) * KV_WIDTH
            kvc_ref[:, col:col + KV_WIDTH] = kv_s[c, pl.ds(l_, n_rows, stride=CMP_STRIDE), :]
    o += 2 * KV_WIDTH
    for ref, rot in ((ksel_ref, True), (vsel_ref, False), (kwin_ref, True), (vwin_ref, False)):
        c = p[:, o:o + KV_WIDTH]
        if rot:
            c = rope(c)
        for g, cc in enumerate(halves(c)):
            ref[g] = cc.astype(BF16)
        o += KV_WIDTH
    pool_ref[...] = p[:, o:o + POOL_WIDTH]
    o += POOL_WIDTH
    for hh in range(MEM_HEADS):
        qm_ref[hh] = (p[:, o + hh * HEAD_DIM:o + (hh + 1) * HEAD_DIM] * QK_SCALE).astype(BF16)
    o += MEM_WIDTH
    gate_ref[...] = _sigmoid(p[:, o:o + GATE_PAD])


def _inproj(x2, g, w_a, cos_t, sin_t, S):
    T = x2.shape[0]
    tm = TM_IN
    n_s = S // tm
    head_spec = lambda n: pl.BlockSpec((n, tm, HEAD_DIM), lambda i: (0, i, 0))
    row_spec = lambda w: pl.BlockSpec((tm, w), lambda i: (i, 0))
    return pl.pallas_call(
        _inproj_kernel,
        grid=(T // tm,),
        in_specs=[row_spec(D_MODEL),
                  pl.BlockSpec((1, D_MODEL), lambda i: (0, 0)),
                  pl.BlockSpec((D_MODEL, IN_COLS), lambda i: (0, 0)),
                  pl.BlockSpec((tm, LANES), lambda i: (i % n_s, 0)),
                  pl.BlockSpec((tm, LANES), lambda i: (i % n_s, 0))],
        out_specs=[head_spec(NSA_HEADS), head_spec(NSA_HEADS),
                   pl.BlockSpec((tm // CMP_STRIDE, CMP_COLS), lambda i: (i, 0)),
                   head_spec(NSA_GROUPS), head_spec(NSA_GROUPS), head_spec(NSA_GROUPS), head_spec(NSA_GROUPS),
                   row_spec(POOL_WIDTH), head_spec(MEM_HEADS), row_spec(GATE_PAD)],
        out_shape=[jax.ShapeDtypeStruct((NSA_HEADS, T, HEAD_DIM), BF16),
                   jax.ShapeDtypeStruct((NSA_HEADS, T, HEAD_DIM), BF16),
                   jax.ShapeDtypeStruct((T // CMP_STRIDE, CMP_COLS), F32),
                   jax.ShapeDtypeStruct((NSA_GROUPS, T, HEAD_DIM), BF16),
                   jax.ShapeDtypeStruct((NSA_GROUPS, T, HEAD_DIM), BF16),
                   jax.ShapeDtypeStruct((NSA_GROUPS, T, HEAD_DIM), BF16),
                   jax.ShapeDtypeStruct((NSA_GROUPS, T, HEAD_DIM), BF16),
                   jax.ShapeDtypeStruct((T, POOL_WIDTH), F32),
                   jax.ShapeDtypeStruct((MEM_HEADS, T, HEAD_DIM), BF16),
                   jax.ShapeDtypeStruct((T, GATE_PAD), F32)],
        scratch_shapes=[pltpu.VMEM((2, tm, KV_WIDTH), F32)],
        compiler_params=pltpu.CompilerParams(dimension_semantics=("arbitrary",), vmem_limit_bytes=VMEM_LIMIT),
        name="inproj",
    )(x2, g, w_a, cos_t, sin_t)


N_CMP_PAD = 128
CMP_FLAT = CMP_BLOCK * HEAD_DIM
CMP_HALF = CMP_STRIDE * HEAD_DIM


def _nsa_kernel(qc_ref, qr_ref, kvc_ref, ksel_ref, vsel_ref, kwin_ref, vwin_ref, gate_ref,
                w1_ref, w2_ref, w2t_ref, pos_ref, ovl_ref, drop_ref, o_ref, kc_s, vct_s, *, S):
    i = pl.program_id(1)
    tq = TQ
    hpg = HEADS_PER_GROUP
    n_sel = S // SEL_BLOCK
    hq = hpg * tq

    @pl.when(i == 0)
    def _compress():
        a = kvc_ref[...].astype(BF16)
        p1 = _dot(a, w1_ref[0])
        p2 = _dot(a, w1_ref[1])
        posterm = (_dot(pos_ref[0], w1_ref[0]) + _dot(pos_ref[1], w1_ref[1]))[0:1]
        hid = p1 + pltpu.roll(p2, N_CMP_PAD - 1, 0) + posterm
        hid = (hid * _sigmoid(hid)).astype(BF16)
        for g in range(NSA_GROUPS):
            kc_s[g] = _dot(hid[:, g * HEAD_DIM:(g + 1) * HEAD_DIM], w2_ref[0]).astype(BF16)
            vct_s[g] = _dot_nt(w2t_ref[1], hid[:, KV_WIDTH + g * HEAD_DIM:KV_WIDTH + (g + 1) * HEAD_DIM]).astype(BF16)

    q0 = i * tq
    t_lane = q0 + lax.broadcasted_iota(jnp.int32, (1, tq), 1)

    def key_pos(start, n):
        return start + lax.broadcasted_iota(jnp.int32, (n, 1), 0)

    cmp_valid = (key_pos(0, N_CMP_PAD) * CMP_STRIDE + CMP_BLOCK - 1) <= t_lane
    jrow = lax.broadcasted_iota(jnp.int32, (n_sel, tq), 0)
    cur = t_lane // SEL_BLOCK
    sel_valid = jrow * SEL_BLOCK <= t_lane
    forced = (jrow == 0) | (jrow == cur) | (jrow == cur - 1)

    cd = q0 // CKS
    causal_bias = jnp.where(key_pos(cd * CKS, CKS) <= t_lane, 0.0, NEG_INF)
    win_chunks = []
    hi = q0 + tq
    for n in WIN_CHUNKS:
        lo = hi - n
        start = pl.multiple_of(jnp.maximum(lo, 0), LANES)
        kp = key_pos(start, n)
        diff = t_lane - kp
        win_chunks.append((start, n, jnp.where((diff >= 0) & (diff < WINDOW) & (kp < hi), 0.0, NEG_INF)))
        hi = lo

    def attend(jobs):
        scores = [_dot_nt(k, q_all) for q_all, k, _, _ in jobs]
        heads = [slice(hh * tq, (hh + 1) * tq) for hh in range(hpg)]
        maxes = [jnp.concatenate([jnp.max(s_all[:, sl] + bias, axis=0, keepdims=True) for sl in heads], axis=1)
                 for (_, _, _, bias), s_all in zip(jobs, scores)]
        soft = []
        for (_, _, _, bias), s_all, m in zip(jobs, scores, maxes):
            ps = [jnp.exp2(s_all[:, sl] + bias - m[:, sl]) for sl in heads]
            l = jnp.concatenate([jnp.sum(p, axis=0, keepdims=True) for p in ps], axis=1)
            soft.append((l, jnp.concatenate([p.astype(BF16) for p in ps], axis=1)))
        return [(m, l, _dot_tn(v, p)) for (_, _, v, _), m, (l, p) in zip(jobs, maxes, soft)]

    def merge(a, b):
        m = jnp.maximum(a[0], b[0])
        wa = jnp.exp2(a[0] - m)
        wb = jnp.exp2(b[0] - m)
        return m, wa * a[1] + wb * b[1], wa * a[2] + wb * b[2]

    def select_blocks(g):
        q_cmp = qc_ref[g * hpg:(g + 1) * hpg].reshape(hq, HEAD_DIM)
        s_all = _dot_nt(kc_s[g], q_cmp)
        p_grp = jnp.zeros((N_CMP_PAD, tq), F32)
        ps = []
        for hh in range(hpg):
            sl = slice(hh * tq, (hh + 1) * tq)
            s = jnp.where(cmp_valid, s_all[:, sl], NEG_INF)
            m = jnp.max(s, axis=0, keepdims=True)
            e = jnp.where(cmp_valid, jnp.exp2(s - m), 0.0)
            p = e * (1.0 / jnp.maximum(jnp.sum(e, axis=0, keepdims=True), TINY))
            p_grp = p_grp + p
            ps.append(p.astype(BF16))
        o_cmp = _dot(vct_s[g], jnp.concatenate(ps, axis=1))

        ovl = ovl_ref[...]
        p_hi = p_grp.astype(BF16)
        r1 = p_grp - p_hi.astype(F32)
        p_mid = r1.astype(BF16)
        p_lo = (r1 - p_mid.astype(F32)).astype(BF16)
        score = _dot(ovl, p_hi) + _dot(ovl, p_mid) + _dot(ovl, p_lo)
        score = jnp.where(sel_valid, score + jnp.where(forced, FORCE_BONUS, 0.0), -1.0)
        rank = jnp.zeros((n_sel, tq), F32)
        for jp in range(n_sel):
            other = score[jp:jp + 1, :]
            beats = (other > score) | ((other == score) & (jrow > jp))
            rank = rank + beats.astype(F32)
        return o_cmp, (rank >= SEL_TOPN).astype(BF16)

    groups = range(NSA_GROUPS)
    cmp_out = [select_blocks(g) for g in groups]
    q_rot = [qr_ref[g * hpg:(g + 1) * hpg].reshape(hq, HEAD_DIM) for g in groups]

    def selected_job(g, c, extra_bias):
        k0 = pl.multiple_of(c * CKS, CKS)
        bias = _dot(drop_ref[c], cmp_out[g][1])
        if extra_bias is not None:
            bias = bias + extra_bias
        return q_rot[g], ksel_ref[g, pl.ds(k0, CKS), :], vsel_ref[g, pl.ds(k0, CKS), :], bias

    def window_job(g, chunk):
        start, n, bias = chunk
        return q_rot[g], kwin_ref[g, pl.ds(start, n), :], vwin_ref[g, pl.ds(start, n), :], bias

    stats = attend([selected_job(g, cd, causal_bias) for g in groups]
                   + [window_job(g, chunk) for chunk in win_chunks for g in groups])
    sel_state = tuple(stats[:NSA_GROUPS])
    win_state = list(stats[NSA_GROUPS:2 * NSA_GROUPS])
    for j in range(1, len(win_chunks)):
        for g in groups:
            win_state[g] = merge(win_state[g], stats[NSA_GROUPS * (1 + j) + g])

    def sel_body(it, states):
        new = attend([selected_job(g, cd - 1 - it, None) for g in groups])
        return tuple(merge(states[g], new[g]) for g in groups)

    sel_state = lax.fori_loop(0, cd, sel_body, sel_state)

    gates_t = gate_ref[...].T
    out_rows = []
    for g in groups:
        o_cmp = cmp_out[g][0]
        o_sel = sel_state[g][2] * (1.0 / sel_state[g][1])
        o_win = win_state[g][2] * (1.0 / win_state[g][1])
        for hh in range(hpg):
            h = g * hpg + hh
            sl = slice(hh * tq, (hh + 1) * tq)
            out_rows.append(gates_t[3 * h:3 * h + 1] * o_cmp[:, sl] + gates_t[3 * h + 1:3 * h + 2] * o_sel[:, sl]
                            + gates_t[3 * h + 2:3 * h + 3] * o_win[:, sl])
    o_ref[...] = jnp.concatenate(out_rows, axis=0).T.astype(BF16)


def _nsa(qc, qr, kvc, ksel, vsel_t, kwin, vwin_t, gates, w1, w2, w2t, pos, ovl, drop_bias, B, S):
    T = B * S
    tq = TQ
    nq = S // tq
    q_spec = pl.BlockSpec((NSA_HEADS, tq, HEAD_DIM), lambda b, i: (0, b * nq + i, 0))
    k_spec = pl.BlockSpec((NSA_GROUPS, S, HEAD_DIM), lambda b, i: (0, b, 0))
    full = lambda a: pl.BlockSpec(a.shape, lambda b, i: (0,) * a.ndim)
    return pl.pallas_call(
        functools.partial(_nsa_kernel, S=S),
        grid=(B, nq),
        in_specs=[q_spec, q_spec,
                  pl.BlockSpec((N_CMP_PAD, CMP_COLS), lambda b, i: (b, 0)),
                  k_spec, k_spec, k_spec, k_spec,
                  pl.BlockSpec((tq, GATE_PAD), lambda b, i: (b * nq + i, 0)),
                  full(w1), full(w2), full(w2t), full(pos), full(ovl), full(drop_bias)],
        out_specs=pl.BlockSpec((tq, NSA_WIDTH), lambda b, i: (b * nq + i, 0)),
        out_shape=jax.ShapeDtypeStruct((T, NSA_WIDTH), BF16),
        scratch_shapes=[pltpu.VMEM((NSA_GROUPS, N_CMP_PAD, HEAD_DIM), BF16),
                        pltpu.VMEM((NSA_GROUPS, HEAD_DIM, N_CMP_PAD), BF16)],
        compiler_params=pltpu.CompilerParams(dimension_semantics=("arbitrary", "arbitrary"),
                                             vmem_limit_bytes=VMEM_LIMIT),
        name="nsa",
    )(qc, qr, kvc, ksel, vsel_t, kwin, vwin_t, gates, w1, w2, w2t, pos, ovl, drop_bias)


ROUTE_E, ROUTE_R, ROUTE_W = 0, TOP_K, 2 * TOP_K
ROUTE_ROWS = 16


def _merge_kernel(x_ref, onsa_ref, pool_ref, prev_ref, qm_ref, km_ref, vm_ref,
                  gmix_ref, wmg_ref, wpool_ref, pscale_ref, wun_ref, wup_ref, wum_ref, wout_ref,
                  gffn_ref, wrh_ref, wrl_ref, br_ref, tri_ref, lower_ref,
                  x1_ref, hf_ref, route_ref, cnt_ref, omem_s, carry_s, *, S, tile0):
    i = pl.program_id(0) + tile0
    tm = TM_MERGE
    n_s = S // tm

    @pl.when(pl.program_id(0) == 0)
    def _init():
        carry_s[...] = jnp.zeros_like(carry_s)

    x = x_ref[...]
    h = _rms(x, gmix_ref[...]).astype(BF16)

    u = pool_ref[...]
    seq_tile = i % n_s
    prev = jnp.where(seq_tile == 0, 0.0, prev_ref[...])
    ext = jnp.concatenate([prev, u], axis=0)
    b2 = ext[1:] + ext[:-1]
    b4 = b2[2:] + b2[:-2]
    b8 = b4[4:] + b4[:-4]
    b16 = b8[8:] + b8[:-8]
    sums = (b2[POOL_HALO - 1:POOL_HALO - 1 + tm], b4[POOL_HALO - 3:POOL_HALO - 3 + tm],
            b8[POOL_HALO - 7:POOL_HALO - 7 + tm], b16[POOL_HALO - 15:POOL_HALO - 15 + tm])
    t_seq = seq_tile * tm + lax.broadcasted_iota(jnp.int32, (tm, 1), 0)
    lane_p = lax.broadcasted_iota(jnp.int32, (tm, POOL_WIDTH), 1)
    z = jnp.zeros((tm, POOL_WIDTH), F32)
    for gi, w in enumerate(POOL_WINDOWS):
        cnt = jnp.minimum(t_seq + 1, w).astype(F32)
        z = jnp.where(lane_p // POOL_GROUP == gi, sums[gi] / cnt, z)
    z = z - u
    o_pool = (_dot(z.astype(BF16), wpool_ref[...]) * pscale_ref[...]).astype(BF16)

    for hh in range(MEM_HEADS):
        s = _dot_nt(qm_ref[hh], km_ref[0, hh])
        m = jnp.max(s, axis=-1, keepdims=True)
        e = jnp.exp(s - m)
        p = e / jnp.sum(e, axis=-1, keepdims=True)
        omem_s[:, hh * HEAD_DIM:(hh + 1) * HEAD_DIM] = _dot(p.astype(BF16), vm_ref[0, hh]).astype(BF16)

    def gated(branch, w_ref, j):
        return _sigmoid(_dot(h, wmg_ref[:, j * D_MODEL:(j + 1) * D_MODEL])) * _dot(branch, w_ref[...])

    merged = gated(onsa_ref[...], wun_ref, 0) + gated(o_pool, wup_ref, 1) + gated(omem_s[...], wum_ref, 2)
    x1 = x + _dot(merged.astype(BF16), wout_ref[...])
    x1_ref[...] = x1
    hf = _rms(x1, gffn_ref[...])
    hf_ref[...] = _pack_bf16_pairs(hf)

    hf_hi = hf.astype(BF16)
    hf_lo = (hf - hf_hi.astype(F32)).astype(BF16)
    logits = (_dot_nt(wrh_ref[...], hf_hi) + _dot_nt(wrl_ref[...], hf_hi) + _dot_nt(wrh_ref[...], hf_lo)
              + br_ref[...])
    erow = lax.broadcasted_iota(jnp.int32, (N_EXPERTS, tm), 0)
    rank = jnp.zeros((N_EXPERTS, tm), F32)
    for jp in range(N_EXPERTS):
        other = logits[jp:jp + 1, :]
        beats = (other > logits) | ((other == logits) & (erow > jp))
        rank = rank + beats.astype(F32)
    chosen = rank < TOP_K
    m = jnp.max(logits, axis=0, keepdims=True)
    e = jnp.where(chosen, jnp.exp(logits - m), 0.0)
    comb = e * (1.0 / jnp.sum(e, axis=0, keepdims=True))

    chosen_b = chosen.astype(BF16)
    carry = carry_s[:, 0:1]
    in_expert = _dot(chosen_b, tri_ref[...]) + carry
    carry_new = carry + jnp.sum(chosen.astype(F32), axis=1, keepdims=True)
    carry_s[...] = jnp.broadcast_to(carry_new, carry_s.shape)
    cnt_ref[...] = jnp.broadcast_to(carry_new, cnt_ref.shape)

    before = _dot(lower_ref[...], chosen_b)
    erow_f = erow.astype(F32)
    fields = {ROUTE_E: erow_f, ROUTE_R: in_expert, ROUTE_W: comb}
    rows = [None] * ROUTE_ROWS
    for k in range(TOP_K):
        pick = chosen & (before == k)
        for base, val in fields.items():
            rows[base + k] = jnp.sum(jnp.where(pick, val, 0.0), axis=0, keepdims=True)
    zero_row = jnp.zeros((1, tm), F32)
    route_ref[...] = jnp.concatenate([zero_row if r is None else r for r in rows], axis=0)


def _merge(x2, onsa, pool_in, qm, km, vm, consts, S, tile0, n_tiles):
    tm = TM_MERGE
    Tc = n_tiles * tm
    n_s = S // tm
    M = km.shape[2]
    halo_per_tile = tm // POOL_HALO
    row = lambda w: pl.BlockSpec((tm, w), lambda i: (i + tile0, 0))
    out_row = lambda w: pl.BlockSpec((tm, w), lambda i: (i, 0))
    full = lambda a: pl.BlockSpec(a.shape, lambda i: (0,) * a.ndim)
    mem_spec = pl.BlockSpec((1, MEM_HEADS, M, HEAD_DIM), lambda i: ((i + tile0) // n_s, 0, 0, 0))
    return pl.pallas_call(
        functools.partial(_merge_kernel, S=S, tile0=tile0),
        grid=(n_tiles,),
        in_specs=[row(D_MODEL), row(NSA_WIDTH), row(POOL_WIDTH),
                  pl.BlockSpec((POOL_HALO, POOL_WIDTH),
                               lambda i: (jnp.maximum((i + tile0) * halo_per_tile - 1, 0), 0)),
                  pl.BlockSpec((MEM_HEADS, tm, HEAD_DIM), lambda i: (0, i + tile0, 0)),
                  mem_spec, mem_spec] + [full(c) for c in consts],
        out_specs=[out_row(D_MODEL), out_row(PACKED), pl.BlockSpec((ROUTE_ROWS, tm), lambda i: (0, i)),
                   pl.BlockSpec((N_EXPERTS, LANES), lambda i: (0, 0))],
        out_shape=[jax.ShapeDtypeStruct((Tc, D_MODEL), F32),
                   jax.ShapeDtypeStruct((Tc, PACKED), U32),
                   jax.ShapeDtypeStruct((ROUTE_ROWS, Tc), F32),
                   jax.ShapeDtypeStruct((N_EXPERTS, LANES), F32)],
        scratch_shapes=[pltpu.VMEM((tm, MEM_WIDTH), BF16), pltpu.VMEM((N_EXPERTS, LANES), F32)],
        compiler_params=pltpu.CompilerParams(dimension_semantics=("arbitrary",), vmem_limit_bytes=VMEM_LIMIT),
        name="merge",
    )(x2, onsa, pool_in, pool_in, qm, km, vm, *consts)


def _sc_mesh():
    return plsc.VectorSubcoreMesh(core_axis_name="c", subcore_axis_name="s")


def _sc_worker():
    return lax.axis_index("s") * SC_CORES + lax.axis_index("c")


def _sc_dispatch(hf, dest, n_rows):
    T, d = hf.shape
    per_worker = T // SC_WORKERS
    steps = per_worker // SC_ROWS

    @functools.partial(pl.kernel, mesh=_sc_mesh(), out_type=jax.ShapeDtypeStruct((n_rows, d), hf.dtype),
                       scratch_types=[pltpu.VMEM((TOP_K, SC_ROWS), jnp.int32), pltpu.VMEM((SC_ROWS, d), hf.dtype),
                                      pltpu.SemaphoreType.DMA])
    def dispatch(hf_hbm, dest_hbm, xs_hbm, idx_v, rows_v, sem):
        base = _sc_worker() * per_worker

        @pl.loop(0, steps)
        def _(j):
            rows = pl.ds(pl.multiple_of(base + j * SC_ROWS, SC_ROWS), SC_ROWS)
            pltpu.sync_copy(hf_hbm.at[rows], rows_v)
            for k in range(TOP_K):
                pltpu.sync_copy(dest_hbm.at[k, rows], idx_v.at[k])
            copies = [pltpu.make_async_copy(rows_v, xs_hbm.at[idx_v.at[k]], sem) for k in range(TOP_K)]
            for c in copies:
                c.start()
            for c in copies:
                c.wait()

    return dispatch(hf, dest)


def _moe_kernel(te_ref, nu_ref, nv_ref, xs_ref, wgu_ref, bgu_ref, wd_ref, bd_ref, ys_ref, wgu_s, wd_s):
    j = pl.program_id(0)
    used = j < nu_ref[0]
    new_expert = (j == 0) | (te_ref[j] != te_ref[jnp.maximum(j - 1, 0)])

    @pl.when(used & new_expert)
    def _cast_weights():
        wgu_s[...] = wgu_ref[0].astype(BF16)
        wd_s[...] = wd_ref[0].astype(BF16)

    @pl.when(used)
    def _compute():
        live = lax.broadcasted_iota(jnp.int32, (xs_ref.shape[0], 1), 0) < nv_ref[j]
        xb = _unpack_bf16_pairs(jnp.where(live, xs_ref[...], jnp.uint32(0))).astype(BF16)
        gu = _dot(xb, wgu_s[...]) + bgu_ref[0]
        gate = jnp.minimum(gu[:, :D_FF], SWIGLU_LIMIT)
        up = jnp.clip(gu[:, D_FF:], -SWIGLU_LIMIT, SWIGLU_LIMIT)
        act = (up + 1.0) * (gate * _sigmoid(SWIGLU_ALPHA * gate))
        ys_ref[...] = _pack_bf16_pairs(_dot(act.astype(BF16), wd_s[...]) + bd_ref[0])

    @pl.when(j >= nu_ref[0])
    def _unused():
        ys_ref[...] = jnp.zeros_like(ys_ref)


def _moe(tile_expert, n_used, n_valid, xs, wgu, bgu, wd, bd):
    P = xs.shape[0]
    tm = TM_E
    grid_spec = pltpu.PrefetchScalarGridSpec(
        num_scalar_prefetch=3,
        grid=(P // tm,),
        in_specs=[pl.BlockSpec((tm, PACKED), lambda j, te, nu, nv: (j, 0)),
                  pl.BlockSpec((1, D_MODEL, 2 * D_FF), lambda j, te, nu, nv: (te[j], 0, 0)),
                  pl.BlockSpec((1, 1, 2 * D_FF), lambda j, te, nu, nv: (te[j], 0, 0)),
                  pl.BlockSpec((1, D_FF, D_MODEL), lambda j, te, nu, nv: (te[j], 0, 0)),
                  pl.BlockSpec((1, 1, D_MODEL), lambda j, te, nu, nv: (te[j], 0, 0))],
        out_specs=pl.BlockSpec((tm, PACKED), lambda j, te, nu, nv: (j, 0)),
        scratch_shapes=[pltpu.VMEM((D_MODEL, 2 * D_FF), BF16), pltpu.VMEM((D_FF, D_MODEL), BF16)],
    )
    return pl.pallas_call(
        _moe_kernel,
        grid_spec=grid_spec,
        out_shape=jax.ShapeDtypeStruct((P, PACKED), U32),
        compiler_params=pltpu.CompilerParams(dimension_semantics=("arbitrary",), vmem_limit_bytes=VMEM_LIMIT),
        name="moe",
    )(tile_expert, n_used, n_valid, xs, wgu, bgu, wd, bd)


def _sc_gather_rows(table, idx):
    n, d = idx.shape[0], table.shape[1]
    per_worker = n // SC_WORKERS
    steps = per_worker // SC_ROWS
    assert steps % 2 == 0

    @functools.partial(pl.kernel, mesh=_sc_mesh(), out_type=jax.ShapeDtypeStruct((n, d), table.dtype),
                       scratch_types=[pltpu.VMEM((2, SC_ROWS), jnp.int32), pltpu.VMEM((2, SC_ROWS, d), table.dtype),
                                      pltpu.SemaphoreType.DMA((2,)), pltpu.SemaphoreType.DMA((2,))])
    def gather(table_hbm, idx_hbm, out_hbm, idx_v, rows_v, gsem, wsem):
        base = _sc_worker() * per_worker

        def rows_at(j):
            return pl.ds(pl.multiple_of(base + j * SC_ROWS, SC_ROWS), SC_ROWS)

        def gather_copy(slot):
            return pltpu.make_async_copy(table_hbm.at[idx_v.at[slot]], rows_v.at[slot], gsem.at[slot])

        def write_copy(j, slot):
            return pltpu.make_async_copy(rows_v.at[slot], out_hbm.at[rows_at(j)], wsem.at[slot])

        def fetch(j, slot):
            pltpu.sync_copy(idx_hbm.at[rows_at(j)], idx_v.at[slot])
            gather_copy(slot).start()

        fetch(0, 0)

        @pl.loop(0, steps, step=2)
        def _(j0):
            for slot in range(2):
                j = j0 + slot
                gather_copy(slot).wait()

                @pl.when(j >= 1)
                def _():
                    write_copy(j - 1, 1 - slot).wait()

                @pl.when(j + 1 < steps)
                def _():
                    fetch(j + 1, 1 - slot)

                write_copy(j, slot).start()

        write_copy(steps - 1, 1).wait()

    return gather(table, idx)


def _final_kernel(x1_ref, yg_ref, route_ref, g_ref, o_ref):
    tm = x1_ref.shape[0]
    route_t = jnp.concatenate([route_ref[...], jnp.zeros((LANES - ROUTE_ROWS, tm), F32)], axis=0).T
    acc = x1_ref[...]
    for k in range(TOP_K):
        acc = acc + route_t[:, ROUTE_W + k:ROUTE_W + k + 1] * _unpack_bf16_pairs(yg_ref[k])
    o_ref[...] = _rms(acc, g_ref[...])


def _final(x1, yg, route, g, out_prev, tile0, n_total):
    Tc = x1.shape[0]
    tm = TM_FINAL
    in_specs = [pl.BlockSpec((tm, D_MODEL), lambda i: (i, 0)),
                pl.BlockSpec((TOP_K, tm, PACKED), lambda i: (0, i, 0)),
                pl.BlockSpec((ROUTE_ROWS, tm), lambda i: (0, i)),
                pl.BlockSpec((1, D_MODEL), lambda i: (0, 0))]
    args = [x1, yg, route, g]
    kernel_fn, aliases = _final_kernel, {}
    if out_prev is not None:
        in_specs.append(pl.BlockSpec(memory_space=pl.ANY))
        args.append(out_prev)
        kernel_fn = lambda x1_ref, yg_ref, route_ref, g_ref, prev_ref, o_ref: _final_kernel(x1_ref, yg_ref, route_ref,
                                                                                            g_ref, o_ref)
        aliases = {len(args) - 1: 0}
    return pl.pallas_call(
        kernel_fn,
        grid=(Tc // tm,),
        in_specs=in_specs,
        out_specs=pl.BlockSpec((tm, D_MODEL), lambda i: (i + tile0, 0)),
        out_shape=jax.ShapeDtypeStruct((n_total * tm, D_MODEL), F32),
        input_output_aliases=aliases,
        compiler_params=pltpu.CompilerParams(dimension_semantics=("arbitrary",), vmem_limit_bytes=VMEM_LIMIT),
        name="final",
    )(*args)


def _rope_tables(S):
    half = HEAD_DIM // 2
    inv = ROPE_THETA ** (-jnp.arange(half, dtype=F32) / half)
    ang = jnp.arange(S, dtype=F32)[:, None] * inv[None, :]
    cos = jnp.tile(jnp.cos(ang), (1, LANES // half))
    sin = jnp.tile(jnp.concatenate([-jnp.sin(ang), jnp.sin(ang)], axis=1), (1, LANES // HEAD_DIM))
    return cos, sin


def _selection_constants(S):
    nc = (S - CMP_BLOCK) // CMP_STRIDE + 1
    n_sel = S // SEL_BLOCK
    j = jnp.arange(n_sel)[:, None]
    i = jnp.arange(N_CMP_PAD)[None, :]
    overlap_t = ((i * CMP_STRIDE <= j * SEL_BLOCK + SEL_BLOCK - 1)
                 & (i * CMP_STRIDE + CMP_BLOCK - 1 >= j * SEL_BLOCK) & (i < nc)).astype(BF16)
    row = jnp.arange(LANES)[:, None]
    key = jnp.arange(S)[None, :]
    drop_bias = jnp.where(key // SEL_BLOCK == row, NEG_INF, 0.0).astype(BF16)[:n_sel]
    drop_bias = drop_bias.T.reshape(S // CKS, CKS, n_sel)
    return overlap_t, drop_bias


def kernel(x, mem, norm_mix, norm_mem, w_in, cmp_pos, cmp_w1, cmp_w2, w_pool, pool_scale, w_mem_kv, w_up_nsa,
           w_up_pool, w_up_mem, w_out, norm_ffn, w_router, b_router, w_gate_up, b_gate_up, w_down, b_down,
           norm_final):
    B, S, D = x.shape
    T = B * S
    assert D == D_MODEL and S % CKS == 0 and S // SEL_BLOCK == 32 and T % (sum(CHUNK_SHARES) * SC_WORKERS * SC_ROWS * 2) == 0
    l = 0
    x2 = x.reshape(T, D)

    w = w_in[l]
    o_gate = NSA_WIDTH + 6 * KV_WIDTH
    n_gate = 3 * NSA_HEADS
    o_pool = o_gate + n_gate
    o_qm = o_pool + POOL_WIDTH
    o_mg = o_qm + MEM_WIDTH
    w_a = jnp.concatenate([w[:, :o_gate], w[:, o_pool:o_mg], w[:, o_gate:o_pool],
                           jnp.zeros((D, GATE_PAD - n_gate), F32)], axis=1).astype(BF16)
    w_mg = w[:, o_mg:].astype(BF16)
    cos_t, sin_t = _rope_tables(S)
    overlap_t, drop_bias = _selection_constants(S)
    stream_kv = jnp.arange(2 * NSA_GROUPS) // NSA_GROUPS
    w1_half = cmp_w1[l].reshape(2, 2, CMP_STRIDE, HEAD_DIM, HEAD_DIM)[stream_kv]
    w1 = jnp.einsum("shlde,ts->hltdse", w1_half, jnp.eye(2 * NSA_GROUPS, dtype=F32))
    w1 = w1.reshape(2, CMP_COLS, 2 * KV_WIDTH).astype(BF16)
    w2 = cmp_w2[l].astype(BF16)
    w2t = jnp.swapaxes(w2, 1, 2)
    pos_half = cmp_pos[l].reshape(2, 2, CMP_STRIDE, HEAD_DIM)[stream_kv]
    pos = jnp.broadcast_to(pos_half.transpose(1, 2, 0, 3).reshape(2, 1, CMP_COLS), (2, 8, CMP_COLS)).astype(BF16)
    wpool_bd = jnp.zeros((POOL_WIDTH, POOL_WIDTH), F32)
    for gi in range(len(POOL_WINDOWS)):
        wpool_bd = wpool_bd.at[gi * POOL_GROUP:(gi + 1) * POOL_GROUP, gi * POOL_GROUP:(gi + 1) * POOL_GROUP].set(w_pool[l, gi])
    wr = w_router[l].T
    wr_hi = wr.astype(BF16)
    wr_lo = (wr - wr_hi.astype(F32)).astype(BF16)
    br = b_router[l].reshape(N_EXPERTS, 1)
    tri = (jnp.arange(TM_MERGE)[:, None] < jnp.arange(TM_MERGE)[None, :]).astype(BF16)
    lower = (jnp.arange(N_EXPERTS)[None, :] < jnp.arange(N_EXPERTS)[:, None]).astype(BF16)

    km, vm = _memkv(mem, norm_mem[l].reshape(1, D), w_mem_kv[l].astype(BF16))
    qc, qr, kvc, ksel, vsel, kwin, vwin, pool_in, qm, gates = _inproj(
        x2, norm_mix[l].reshape(1, D), w_a, cos_t, sin_t, S)
    o_nsa = _nsa(qc, qr, kvc, ksel, vsel, kwin, vwin, gates, w1, w2, w2t, pos, overlap_t, drop_bias, B, S)
    consts = [norm_mix[l].reshape(1, D), w_mg, wpool_bd.astype(BF16), pool_scale[l].reshape(1, POOL_WIDTH),
              w_up_nsa[l].astype(BF16), w_up_pool[l].astype(BF16), w_up_mem[l].astype(BF16), w_out[l].astype(BF16),
              norm_ffn[l].reshape(1, D), wr_hi, wr_lo, br, tri, lower]
    unit = T // sum(CHUNK_SHARES)
    chunk_tokens = [share * unit for share in CHUNK_SHARES]
    chunk_first = [sum(chunk_tokens[:c]) for c in range(len(chunk_tokens))]
    wgu, bgu = w_gate_up[l], b_gate_up[l].reshape(N_EXPERTS, 1, 2 * D_FF)
    wd, bd = w_down[l], b_down[l].reshape(N_EXPERTS, 1, D_MODEL)

    def route_chunk(c):
        Tc = chunk_tokens[c]
        n_tiles = (Tc * TOP_K) // TM_E + N_EXPERTS
        tile_start = jnp.arange(n_tiles, dtype=jnp.int32) * TM_E
        x1, hf, route, counts = _merge(x2, o_nsa, pool_in, qm, km, vm, consts, S, chunk_first[c] // TM_MERGE,
                                       Tc // TM_MERGE)
        counts = counts[:, 0].astype(jnp.int32)
        padded = ((counts + TM_E - 1) // TM_E) * TM_E
        ends = jnp.cumsum(padded)
        starts = ends - padded
        e_k = route[ROUTE_E:ROUTE_E + TOP_K].astype(jnp.int32)
        r_k = route[ROUTE_R:ROUTE_R + TOP_K].astype(jnp.int32)
        group_start = jnp.zeros_like(e_k)
        for e in range(N_EXPERTS):
            group_start = jnp.where(e_k == e, starts[e], group_start)
        dest = group_start + r_k
        tile_expert = jnp.minimum(jnp.sum(tile_start[:, None] >= ends[None, :], axis=1), N_EXPERTS - 1).astype(jnp.int32)
        n_used = (ends[-1] // TM_E).astype(jnp.int32).reshape(1)
        n_valid = jnp.clip((starts + counts)[tile_expert] - tile_start, 0, TM_E).astype(jnp.int32)
        xs = _sc_dispatch(hf, dest, n_tiles * TM_E)
        return x1, route, dest, (tile_expert, n_used, n_valid, xs)

    routed = [route_chunk(c) for c in range(len(chunk_tokens))]
    expert_out = [_moe(*moe_args, wgu, bgu, wd, bd) for _, _, _, moe_args in routed]
    out = None
    for c, ((x1, route, dest, _), ys) in enumerate(zip(routed, expert_out)):
        yg = _sc_gather_rows(ys, dest.reshape(-1)).reshape(TOP_K, chunk_tokens[c], PACKED)
        out = _final(x1, yg, route, norm_final.reshape(1, D), out, chunk_first[c] // TM_FINAL, T // TM_FINAL)
    return out.reshape(B, S, D)
```

```python
import functools

import jax
import jax.numpy as jnp
from jax import lax
from jax.experimental import pallas as pl
from jax.experimental.pallas import tpu as pltpu
from jax.experimental.pallas import tpu_sc as plsc

F32 = jnp.float32
BF16 = jnp.bfloat16
U32 = jnp.uint32

D_MODEL = 1024
HEAD_DIM = 64
NSA_HEADS = 8
NSA_GROUPS = 2
HEADS_PER_GROUP = NSA_HEADS // NSA_GROUPS
NSA_WIDTH = NSA_HEADS * HEAD_DIM
KV_WIDTH = NSA_GROUPS * HEAD_DIM
CMP_BLOCK = 32
CMP_STRIDE = 16
SEL_BLOCK = 64
SEL_TOPN = 8
FORCE_BONUS = 1000.0
WINDOW = 512
POOL_WINDOWS = (2, 4, 8, 16)
POOL_GROUP = 64
POOL_WIDTH = POOL_GROUP * len(POOL_WINDOWS)
POOL_HALO = 16
MEM_HEADS = 4
MEM_WIDTH = MEM_HEADS * HEAD_DIM
N_EXPERTS = 32
TOP_K = 4
D_FF = 1024
SWIGLU_LIMIT = 7.0
SWIGLU_ALPHA = 1.702
ROPE_THETA = 10000.0
EPS = 1e-5
NEG_INF = -1e30
TINY = 1e-30
QK_SCALE = HEAD_DIM ** -0.5
LOG2_E = 1.4426950408889634
NSA_Q_SCALE = QK_SCALE * LOG2_E

LANES = 128
GATE_PAD = LANES
PACKED = D_MODEL // 2

TM_IN = 1024
TQ = 256
CKS = 512
WIN_CHUNKS = (256, 256, 256)
TM_MERGE = 1024
TM_E = 512
TM_FINAL = 512
CHUNK_SHARES = (3, 1)
SC_CORES = 2
SC_SUBCORES = 16
SC_WORKERS = SC_CORES * SC_SUBCORES
SC_ROWS = 64
VMEM_LIMIT = 56 * 1024 * 1024


def _rms(x, g):
    return x * lax.rsqrt(jnp.mean(x * x, axis=-1, keepdims=True) + EPS) * g


def _sigmoid(x):
    return 0.5 * jnp.tanh(0.5 * x) + 0.5


def _dot(a, b):
    return jnp.dot(a, b, preferred_element_type=F32)


def _pack_bf16_pairs(v):
    n = v.shape[1] // 2
    r = v.astype(BF16).astype(F32)
    lo = pltpu.bitcast(r[:, :n], U32) >> 16
    hi = pltpu.bitcast(r[:, n:], U32) & jnp.uint32(0xFFFF0000)
    return lo | hi


def _unpack_bf16_pairs(w):
    lo = pltpu.bitcast(w << 16, F32)
    hi = pltpu.bitcast(w & jnp.uint32(0xFFFF0000), F32)
    return jnp.concatenate([lo, hi], axis=1)


def _dot_nt(a, b):
    return lax.dot_general(a, b, (((1,), (1,)), ((), ())), preferred_element_type=F32)


def _dot_tn(a, b):
    return lax.dot_general(a, b, (((0,), (0,)), ((), ())), preferred_element_type=F32)


def _memkv_kernel(mem_ref, g_ref, w_ref, k_ref, v_ref):
    m = _rms(mem_ref[0], g_ref[...]).astype(BF16)
    kv = _dot(m, w_ref[...])
    for h in range(MEM_HEADS):
        k_ref[0, h] = kv[:, h * HEAD_DIM:(h + 1) * HEAD_DIM].astype(BF16)
        v_ref[0, h] = kv[:, MEM_WIDTH + h * HEAD_DIM:MEM_WIDTH + (h + 1) * HEAD_DIM].astype(BF16)


def _memkv(mem, g, w):
    B, M, D = mem.shape
    return pl.pallas_call(
        _memkv_kernel,
        grid=(B,),
        in_specs=[pl.BlockSpec((1, M, D), lambda b: (b, 0, 0)),
                  pl.BlockSpec((1, D), lambda b: (0, 0)),
                  pl.BlockSpec((D, 2 * MEM_WIDTH), lambda b: (0, 0))],
        out_specs=[pl.BlockSpec((1, MEM_HEADS, M, HEAD_DIM), lambda b: (b, 0, 0, 0)),
                   pl.BlockSpec((1, MEM_HEADS, M, HEAD_DIM), lambda b: (b, 0, 0, 0))],
        out_shape=[jax.ShapeDtypeStruct((B, MEM_HEADS, M, HEAD_DIM), BF16),
                   jax.ShapeDtypeStruct((B, MEM_HEADS, M, HEAD_DIM), BF16)],
        compiler_params=pltpu.CompilerParams(dimension_semantics=("arbitrary",), vmem_limit_bytes=VMEM_LIMIT),
        name="memkv",
    )(mem, g, w)


IN_COLS = NSA_WIDTH + 6 * KV_WIDTH + POOL_WIDTH + MEM_WIDTH + GATE_PAD
CMP_COLS = CMP_STRIDE * 2 * KV_WIDTH


def _inproj_kernel(x_ref, g_ref, w_ref, cos_ref, sin_ref,
                   qc_ref, qr_ref, kvc_ref, ksel_ref, vsel_ref, kwin_ref, vwin_ref, pool_ref, qm_ref, gate_ref,
                   kv_s):
    h = _rms(x_ref[...], g_ref[...]).astype(BF16)
    p = _dot(h, w_ref[...])
    cos = cos_ref[...]
    sin = sin_ref[...]
    lane = lax.broadcasted_iota(jnp.int32, cos.shape, 1)
    first_half = (lane % HEAD_DIM) < (HEAD_DIM // 2)

    def rope(c):
        partner = jnp.where(first_half, pltpu.roll(c, LANES - HEAD_DIM // 2, 1), pltpu.roll(c, HEAD_DIM // 2, 1))
        return c * cos + partner * sin

    def halves(c):
        return c[:, :HEAD_DIM], c[:, HEAD_DIM:]

    for j in range(NSA_WIDTH // LANES):
        c = p[:, j * LANES:(j + 1) * LANES]
        r = rope(c)
        for hh, (cc, rr) in enumerate(zip(halves(c), halves(r))):
            qc_ref[2 * j + hh] = (cc * NSA_Q_SCALE).astype(BF16)
            qr_ref[2 * j + hh] = (rr * NSA_Q_SCALE).astype(BF16)
    o = NSA_WIDTH
    n_rows = kv_s.shape[1] // CMP_STRIDE
    for c in range(2):
        kv_s[c] = p[:, o + c * KV_WIDTH:o + (c + 1) * KV_WIDTH]
    for l_ in range(CMP_STRIDE):
        for c in range(2):
            col = (2 * l_ + c) * KV_WIDTH
            kvc_ref[:, col:col + KV_WIDTH] = kv_s[c, pl.ds(l_, n_rows, stride=CMP_STRIDE), :]
    o += 2 * KV_WIDTH
    for ref, rot in ((ksel_ref, True), (vsel_ref, False), (kwin_ref, True), (vwin_ref, False)):
        c = p[:, o:o + KV_WIDTH]
        if rot:
            c = rope(c)
        for g, cc in enumerate(halves(c)):
            ref[g] = cc.astype(BF16)
        o += KV_WIDTH
    pool_ref[...] = p[:, o:o + POOL_WIDTH]
    o += POOL_WIDTH
    for hh in range(MEM_HEADS):
        qm_ref[hh] = (p[:, o + hh * HEAD_DIM:o + (hh + 1) * HEAD_DIM] * QK_SCALE).astype(BF16)
    o += MEM_WIDTH
    gate_ref[...] = _sigmoid(p[:, o:o + GATE_PAD])


def _inproj(x2, g, w_a, cos_t, sin_t, S):
    T = x2.shape[0]
    tm = TM_IN
    n_s = S // tm
    head_spec = lambda n: pl.BlockSpec((n, tm, HEAD_DIM), lambda i: (0, i, 0))
    row_spec = lambda w: pl.BlockSpec((tm, w), lambda i: (i, 0))
    return pl.pallas_call(
        _inproj_kernel,
        grid=(T // tm,),
        in_specs=[row_spec(D_MODEL),
                  pl.BlockSpec((1, D_MODEL), lambda i: (0, 0)),
                  pl.BlockSpec((D_MODEL, IN_COLS), lambda i: (0, 0)),
                  pl.BlockSpec((tm, LANES), lambda i: (i % n_s, 0)),
                  pl.BlockSpec((tm, LANES), lambda i: (i % n_s, 0))],
        out_specs=[head_spec(NSA_HEADS), head_spec(NSA_HEADS),
                   pl.BlockSpec((tm // CMP_STRIDE, CMP_COLS), lambda i: (i, 0)),
                   head_spec(NSA_GROUPS), head_spec(NSA_GROUPS), head_spec(NSA_GROUPS), head_spec(NSA_GROUPS),
                   row_spec(POOL_WIDTH), head_spec(MEM_HEADS), row_spec(GATE_PAD)],
        out_shape=[jax.ShapeDtypeStruct((NSA_HEADS, T, HEAD_DIM), BF16),
                   jax.ShapeDtypeStruct((NSA_HEADS, T, HEAD_DIM), BF16),
                   jax.ShapeDtypeStruct((T // CMP_STRIDE, CMP_COLS), F32),
                   jax.ShapeDtypeStruct((NSA_GROUPS, T, HEAD_DIM), BF16),
                   jax.ShapeDtypeStruct((NSA_GROUPS, T, HEAD_DIM), BF16),
                   jax.ShapeDtypeStruct((NSA_GROUPS, T, HEAD_DIM), BF16),
                   jax.ShapeDtypeStruct((NSA_GROUPS, T, HEAD_DIM), BF16),
                   jax.ShapeDtypeStruct((T, POOL_WIDTH), F32),
                   jax.ShapeDtypeStruct((MEM_HEADS, T, HEAD_DIM), BF16),
                   jax.ShapeDtypeStruct((T, GATE_PAD), F32)],
        scratch_shapes=[pltpu.VMEM((2, tm, KV_WIDTH), F32)],
        compiler_params=pltpu.CompilerParams(dimension_semantics=("arbitrary",), vmem_limit_bytes=VMEM_LIMIT),
        name="inproj",
    )(x2, g, w_a, cos_t, sin_t)


N_CMP_PAD = 128


def _nsa_kernel(qc_ref, qr_ref, kvc_ref, ksel_ref, vsel_ref, kwin_ref, vwin_ref, gate_ref,
                w1_ref, w2_ref, w2t_ref, pos_ref, ovl_ref, drop_ref, o_ref, kc_s, vct_s, *, S):
    i = pl.program_id(1)
    tq = TQ
    hpg = HEADS_PER_GROUP
    n_sel = S // SEL_BLOCK
    hq = hpg * tq

    @pl.when(i == 0)
    def _compress():
        a = kvc_ref[...].astype(BF16)
        p1 = _dot(a, w1_ref[0])
        p2 = _dot(a, w1_ref[1])
        posterm = (_dot(pos_ref[0], w1_ref[0]) + _dot(pos_ref[1], w1_ref[1]))[0:1]
        hid = p1 + pltpu.roll(p2, N_CMP_PAD - 1, 0) + posterm
        hid = (hid * _sigmoid(hid)).astype(BF16)
        for g in range(NSA_GROUPS):
            kc_s[g] = _dot(hid[:, g * HEAD_DIM:(g + 1) * HEAD_DIM], w2_ref[0]).astype(BF16)
            vct_s[g] = _dot_nt(w2t_ref[1], hid[:, KV_WIDTH + g * HEAD_DIM:KV_WIDTH + (g + 1) * HEAD_DIM]).astype(BF16)

    q0 = i * tq
    t_lane = q0 + lax.broadcasted_iota(jnp.int32, (1, tq), 1)

    def key_pos(start, n):
        return start + lax.broadcasted_iota(jnp.int32, (n, 1), 0)

    cmp_valid = (key_pos(0, N_CMP_PAD) * CMP_STRIDE + CMP_BLOCK - 1) <= t_lane
    jrow = lax.broadcasted_iota(jnp.int32, (n_sel, tq), 0)
    cur = t_lane // SEL_BLOCK
    sel_valid = jrow * SEL_BLOCK <= t_lane
    forced = (jrow == 0) | (jrow == cur) | (jrow == cur - 1)

    cd = q0 // CKS
    causal_bias = jnp.where(key_pos(cd * CKS, CKS) <= t_lane, 0.0, NEG_INF)
    win_chunks = []
    hi = q0 + tq
    for n in WIN_CHUNKS:
        lo = hi - n
        start = pl.multiple_of(jnp.maximum(lo, 0), LANES)
        kp = key_pos(start, n)
        diff = t_lane - kp
        win_chunks.append((start, n, jnp.where((diff >= 0) & (diff < WINDOW) & (kp < hi), 0.0, NEG_INF)))
        hi = lo

    def attend(jobs):
        scores = [_dot_nt(k, q_all) for q_all, k, _, _ in jobs]
        heads = [slice(hh * tq, (hh + 1) * tq) for hh in range(hpg)]
        maxes = [jnp.concatenate([jnp.max(s_all[:, sl] + bias, axis=0, keepdims=True) for sl in heads], axis=1)
                 for (_, _, _, bias), s_all in zip(jobs, scores)]
        soft = []
        for (_, _, _, bias), s_all, m in zip(jobs, scores, maxes):
            ps = [jnp.exp2(s_all[:, sl] + bias - m[:, sl]) for sl in heads]
            l = jnp.concatenate([jnp.sum(p, axis=0, keepdims=True) for p in ps], axis=1)
            soft.append((l, jnp.concatenate([p.astype(BF16) for p in ps], axis=1)))
        return [(m, l, _dot_tn(v, p)) for (_, _, v, _), m, (l, p) in zip(jobs, maxes, soft)]

    def merge(a, b):
        m = jnp.maximum(a[0], b[0])
        wa = jnp.exp2(a[0] - m)
        wb = jnp.exp2(b[0] - m)
        return m, wa * a[1] + wb * b[1], wa * a[2] + wb * b[2]

    def select_blocks(g):
        q_cmp = qc_ref[g * hpg:(g + 1) * hpg].reshape(hq, HEAD_DIM)
        s_all = _dot_nt(kc_s[g], q_cmp)
        p_grp = jnp.zeros((N_CMP_PAD, tq), F32)
        ps = []
        for hh in range(hpg):
            sl = slice(hh * tq, (hh + 1) * tq)
            s = jnp.where(cmp_valid, s_all[:, sl], NEG_INF)
            m = jnp.max(s, axis=0, keepdims=True)
            e = jnp.where(cmp_valid, jnp.exp2(s - m), 0.0)
            p = e * (1.0 / jnp.maximum(jnp.sum(e, axis=0, keepdims=True), TINY))
            p_grp = p_grp + p
            ps.append(p.astype(BF16))
        o_cmp = _dot(vct_s[g], jnp.concatenate(ps, axis=1))

        ovl = ovl_ref[...]
        p_hi = p_grp.astype(BF16)
        r1 = p_grp - p_hi.astype(F32)
        p_mid = r1.astype(BF16)
        p_lo = (r1 - p_mid.astype(F32)).astype(BF16)
        score = _dot(ovl, p_hi) + _dot(ovl, p_mid) + _dot(ovl, p_lo)
        score = jnp.where(sel_valid, score + jnp.where(forced, FORCE_BONUS, 0.0), -1.0)
        rank = jnp.zeros((n_sel, tq), F32)
        for jp in range(n_sel):
            other = score[jp:jp + 1, :]
            beats = (other > score) | ((other == score) & (jrow > jp))
            rank = rank + beats.astype(F32)
        return o_cmp, (rank >= SEL_TOPN).astype(BF16)

    groups = range(NSA_GROUPS)
    cmp_out = [select_blocks(g) for g in groups]
    q_rot = [qr_ref[g * hpg:(g + 1) * hpg].reshape(hq, HEAD_DIM) for g in groups]

    def selected_job(g, c, extra_bias):
        k0 = pl.multiple_of(c * CKS, CKS)
        bias = _dot(drop_ref[c], cmp_out[g][1])
        if extra_bias is not None:
            bias = bias + extra_bias
        return q_rot[g], ksel_ref[g, pl.ds(k0, CKS), :], vsel_ref[g, pl.ds(k0, CKS), :], bias

    def window_job(g, chunk):
        start, n, bias = chunk
        return q_rot[g], kwin_ref[g, pl.ds(start, n), :], vwin_ref[g, pl.ds(start, n), :], bias

    stats = attend([selected_job(g, cd, causal_bias) for g in groups]
                   + [window_job(g, chunk) for chunk in win_chunks for g in groups])
    sel_state = tuple(stats[:NSA_GROUPS])
    win_state = list(stats[NSA_GROUPS:2 * NSA_GROUPS])
    for j in range(1, len(win_chunks)):
        for g in groups:
            win_state[g] = merge(win_state[g], stats[NSA_GROUPS * (1 + j) + g])

    def sel_body(it, states):
        new = attend([selected_job(g, cd - 1 - it, None) for g in groups])
        return tuple(merge(states[g], new[g]) for g in groups)

    sel_state = lax.fori_loop(0, cd, sel_body, sel_state)

    gates_t = gate_ref[...].T
    out_rows = []
    for g in groups:
        o_cmp = cmp_out[g][0]
        o_sel = sel_state[g][2] * (1.0 / sel_state[g][1])
        o_win = win_state[g][2] * (1.0 / win_state[g][1])
        for hh in range(hpg):
            h = g * hpg + hh
            sl = slice(hh * tq, (hh + 1) * tq)
            out_rows.append(gates_t[3 * h:3 * h + 1] * o_cmp[:, sl] + gates_t[3 * h + 1:3 * h + 2] * o_sel[:, sl]
                            + gates_t[3 * h + 2:3 * h + 3] * o_win[:, sl])
    o_ref[...] = jnp.concatenate(out_rows, axis=0).T.astype(BF16)


def _nsa(qc, qr, kvc, ksel, vsel_t, kwin, vwin_t, gates, w1, w2, w2t, pos, ovl, drop_bias, B, S):
    T = B * S
    tq = TQ
    nq = S // tq
    q_spec = pl.BlockSpec((NSA_HEADS, tq, HEAD_DIM), lambda b, i: (0, b * nq + i, 0))
    k_spec = pl.BlockSpec((NSA_GROUPS, S, HEAD_DIM), lambda b, i: (0, b, 0))
    full = lambda a: pl.BlockSpec(a.shape, lambda b, i: (0,) * a.ndim)
    return pl.pallas_call(
        functools.partial(_nsa_kernel, S=S),
        grid=(B, nq),
        in_specs=[q_spec, q_spec,
                  pl.BlockSpec((N_CMP_PAD, CMP_COLS), lambda b, i: (b, 0)),
                  k_spec, k_spec, k_spec, k_spec,
                  pl.BlockSpec((tq, GATE_PAD), lambda b, i: (b * nq + i, 0)),
                  full(w1), full(w2), full(w2t), full(pos), full(ovl), full(drop_bias)],
        out_specs=pl.BlockSpec((tq, NSA_WIDTH), lambda b, i: (b * nq + i, 0)),
        out_shape=jax.ShapeDtypeStruct((T, NSA_WIDTH), BF16),
        scratch_shapes=[pltpu.VMEM((NSA_GROUPS, N_CMP_PAD, HEAD_DIM), BF16),
                        pltpu.VMEM((NSA_GROUPS, HEAD_DIM, N_CMP_PAD), BF16)],
        compiler_params=pltpu.CompilerParams(dimension_semantics=("arbitrary", "arbitrary"),
                                             vmem_limit_bytes=VMEM_LIMIT),
        name="nsa",
    )(qc, qr, kvc, ksel, vsel_t, kwin, vwin_t, gates, w1, w2, w2t, pos, ovl, drop_bias)


ROUTE_E, ROUTE_R, ROUTE_W = 0, TOP_K, 2 * TOP_K
ROUTE_ROWS = 16


def _merge_kernel(x_ref, onsa_ref, pool_ref, prev_ref, qm_ref, km_ref, vm_ref,
                  gmix_ref, wmg_ref, wpool_ref, pscale_ref, wun_ref, wup_ref, wum_ref, wout_ref,
                  gffn_ref, wrh_ref, wrl_ref, br_ref, tri_ref, lower_ref,
                  x1_ref, hf_ref, route_ref, cnt_ref, omem_s, carry_s, *, S, tile0):
    i = pl.program_id(0) + tile0
    tm = TM_MERGE
    n_s = S // tm

    @pl.when(pl.program_id(0) == 0)
    def _init():
        carry_s[...] = jnp.zeros_like(carry_s)

    x = x_ref[...]
    h = _rms(x, gmix_ref[...]).astype(BF16)

    u = pool_ref[...]
    seq_tile = i % n_s
    prev = jnp.where(seq_tile == 0, 0.0, prev_ref[...])
    ext = jnp.concatenate([prev, u], axis=0)
    b2 = ext[1:] + ext[:-1]
    b4 = b2[2:] + b2[:-2]
    b8 = b4[4:] + b4[:-4]
    b16 = b8[8:] + b8[:-8]
    sums = (b2[POOL_HALO - 1:POOL_HALO - 1 + tm], b4[POOL_HALO - 3:POOL_HALO - 3 + tm],
            b8[POOL_HALO - 7:POOL_HALO - 7 + tm], b16[POOL_HALO - 15:POOL_HALO - 15 + tm])
    t_seq = seq_tile * tm + lax.broadcasted_iota(jnp.int32, (tm, 1), 0)
    lane_p = lax.broadcasted_iota(jnp.int32, (tm, POOL_WIDTH), 1)
    z = jnp.zeros((tm, POOL_WIDTH), F32)
    for gi, w in enumerate(POOL_WINDOWS):
        cnt = jnp.minimum(t_seq + 1, w).astype(F32)
        z = jnp.where(lane_p // POOL_GROUP == gi, sums[gi] / cnt, z)
    z = z - u
    o_pool = (_dot(z.astype(BF16), wpool_ref[...]) * pscale_ref[...]).astype(BF16)

    for hh in range(MEM_HEADS):
        s = _dot_nt(qm_ref[hh], km_ref[0, hh])
        m = jnp.max(s, axis=-1, keepdims=True)
        e = jnp.exp(s - m)
        p = e / jnp.sum(e, axis=-1, keepdims=True)
        omem_s[:, hh * HEAD_DIM:(hh + 1) * HEAD_DIM] = _dot(p.astype(BF16), vm_ref[0, hh]).astype(BF16)

    def gated(branch, w_ref, j):
        return _sigmoid(_dot(h, wmg_ref[:, j * D_MODEL:(j + 1) * D_MODEL])) * _dot(branch, w_ref[...])

    merged = gated(onsa_ref[...], wun_ref, 0) + gated(o_pool, wup_ref, 1) + gated(omem_s[...], wum_ref, 2)
    x1 = x + _dot(merged.astype(BF16), wout_ref[...])
    x1_ref[...] = x1
    hf = _rms(x1, gffn_ref[...])
    hf_ref[...] = _pack_bf16_pairs(hf)

    hf_hi = hf.astype(BF16)
    hf_lo = (hf - hf_hi.astype(F32)).astype(BF16)
    logits = (_dot_nt(wrh_ref[...], hf_hi) + _dot_nt(wrl_ref[...], hf_hi) + _dot_nt(wrh_ref[...], hf_lo)
              + br_ref[...])
    erow = lax.broadcasted_iota(jnp.int32, (N_EXPERTS, tm), 0)
    rank = jnp.zeros((N_EXPERTS, tm), F32)
    for jp in range(N_EXPERTS):
        other = logits[jp:jp + 1, :]
        beats = (other > logits) | ((other == logits) & (erow > jp))
        rank = rank + beats.astype(F32)
    chosen = rank < TOP_K
    m = jnp.max(logits, axis=0, keepdims=True)
    e = jnp.where(chosen, jnp.exp(logits - m), 0.0)
    comb = e * (1.0 / jnp.sum(e, axis=0, keepdims=True))

    chosen_b = chosen.astype(BF16)
    carry = carry_s[:, 0:1]
    in_expert = _dot(chosen_b, tri_ref[...]) + carry
    carry_new = carry + jnp.sum(chosen.astype(F32), axis=1, keepdims=True)
    carry_s[...] = jnp.broadcast_to(carry_new, carry_s.shape)
    cnt_ref[...] = jnp.broadcast_to(carry_new, cnt_ref.shape)

    before = _dot(lower_ref[...], chosen_b)
    erow_f = erow.astype(F32)
    fields = {ROUTE_E: erow_f, ROUTE_R: in_expert, ROUTE_W: comb}
    rows = [None] * ROUTE_ROWS
    for k in range(TOP_K):
        pick = chosen & (before == k)
        for base, val in fields.items():
            rows[base + k] = jnp.sum(jnp.where(pick, val, 0.0), axis=0, keepdims=True)
    zero_row = jnp.zeros((1, tm), F32)
    route_ref[...] = jnp.concatenate([zero_row if r is None else r for r in rows], axis=0)


def _merge(x2, onsa, pool_in, qm, km, vm, consts, S, tile0, n_tiles):
    tm = TM_MERGE
    Tc = n_tiles * tm
    n_s = S // tm
    M = km.shape[2]
    halo_per_tile = tm // POOL_HALO
    row = lambda w: pl.BlockSpec((tm, w), lambda i: (i + tile0, 0))
    out_row = lambda w: pl.BlockSpec((tm, w), lambda i: (i, 0))
    full = lambda a: pl.BlockSpec(a.shape, lambda i: (0,) * a.ndim)
    mem_spec = pl.BlockSpec((1, MEM_HEADS, M, HEAD_DIM), lambda i: ((i + tile0) // n_s, 0, 0, 0))
    return pl.pallas_call(
        functools.partial(_merge_kernel, S=S, tile0=tile0),
        grid=(n_tiles,),
        in_specs=[row(D_MODEL), row(NSA_WIDTH), row(POOL_WIDTH),
                  pl.BlockSpec((POOL_HALO, POOL_WIDTH),
                               lambda i: (jnp.maximum((i + tile0) * halo_per_tile - 1, 0), 0)),
                  pl.BlockSpec((MEM_HEADS, tm, HEAD_DIM), lambda i: (0, i + tile0, 0)),
                  mem_spec, mem_spec] + [full(c) for c in consts],
        out_specs=[out_row(D_MODEL), out_row(PACKED), pl.BlockSpec((ROUTE_ROWS, tm), lambda i: (0, i)),
                   pl.BlockSpec((N_EXPERTS, LANES), lambda i: (0, 0))],
        out_shape=[jax.ShapeDtypeStruct((Tc, D_MODEL), F32),
                   jax.ShapeDtypeStruct((Tc, PACKED), U32),
                   jax.ShapeDtypeStruct((ROUTE_ROWS, Tc), F32),
                   jax.ShapeDtypeStruct((N_EXPERTS, LANES), F32)],
        scratch_shapes=[pltpu.VMEM((tm, MEM_WIDTH), BF16), pltpu.VMEM((N_EXPERTS, LANES), F32)],
        compiler_params=pltpu.CompilerParams(dimension_semantics=("arbitrary",), vmem_limit_bytes=VMEM_LIMIT),
        name="merge",
    )(x2, onsa, pool_in, pool_in, qm, km, vm, *consts)


def _sc_mesh():
    return plsc.VectorSubcoreMesh(core_axis_name="c", subcore_axis_name="s")


def _sc_worker():
    return lax.axis_index("s") * SC_CORES + lax.axis_index("c")


def _sc_dispatch(hf, dest, n_rows):
    T, d = hf.shape
    per_worker = T // SC_WORKERS
    steps = per_worker // SC_ROWS

    @functools.partial(pl.kernel, mesh=_sc_mesh(), out_type=jax.ShapeDtypeStruct((n_rows, d), hf.dtype),
                       scratch_types=[pltpu.VMEM((TOP_K, SC_ROWS), jnp.int32), pltpu.VMEM((SC_ROWS, d), hf.dtype),
                                      pltpu.SemaphoreType.DMA])
    def dispatch(hf_hbm, dest_hbm, xs_hbm, idx_v, rows_v, sem):
        base = _sc_worker() * per_worker

        @pl.loop(0, steps)
        def _(j):
            rows = pl.ds(pl.multiple_of(base + j * SC_ROWS, SC_ROWS), SC_ROWS)
            pltpu.sync_copy(hf_hbm.at[rows], rows_v)
            for k in range(TOP_K):
                pltpu.sync_copy(dest_hbm.at[k, rows], idx_v.at[k])
            copies = [pltpu.make_async_copy(rows_v, xs_hbm.at[idx_v.at[k]], sem) for k in range(TOP_K)]
            for c in copies:
                c.start()
            for c in copies:
                c.wait()

    return dispatch(hf, dest)


def _moe_kernel(te_ref, nu_ref, nv_ref, xs_ref, wgu_ref, bgu_ref, wd_ref, bd_ref, ys_ref, wgu_s, wd_s):
    j = pl.program_id(0)
    used = j < nu_ref[0]
    new_expert = (j == 0) | (te_ref[j] != te_ref[jnp.maximum(j - 1, 0)])

    @pl.when(used & new_expert)
    def _cast_weights():
        wgu_s[...] = wgu_ref[0].astype(BF16)
        wd_s[...] = wd_ref[0].astype(BF16)

    @pl.when(used)
    def _compute():
        live = lax.broadcasted_iota(jnp.int32, (xs_ref.shape[0], 1), 0) < nv_ref[j]
        xb = _unpack_bf16_pairs(jnp.where(live, xs_ref[...], jnp.uint32(0))).astype(BF16)
        gu = _dot(xb, wgu_s[...]) + bgu_ref[0]
        gate = jnp.minimum(gu[:, :D_FF], SWIGLU_LIMIT)
        up = jnp.clip(gu[:, D_FF:], -SWIGLU_LIMIT, SWIGLU_LIMIT)
        act = (up + 1.0) * (gate * _sigmoid(SWIGLU_ALPHA * gate))
        ys_ref[...] = _pack_bf16_pairs(_dot(act.astype(BF16), wd_s[...]) + bd_ref[0])

    @pl.when(j >= nu_ref[0])
    def _unused():
        ys_ref[...] = jnp.zeros_like(ys_ref)


def _moe(tile_expert, n_used, n_valid, xs, wgu, bgu, wd, bd):
    P = xs.shape[0]
    tm = TM_E
    grid_spec = pltpu.PrefetchScalarGridSpec(
        num_scalar_prefetch=3,
        grid=(P // tm,),
        in_specs=[pl.BlockSpec((tm, PACKED), lambda j, te, nu, nv: (j, 0)),
                  pl.BlockSpec((1, D_MODEL, 2 * D_FF), lambda j, te, nu, nv: (te[j], 0, 0)),
                  pl.BlockSpec((1, 1, 2 * D_FF), lambda j, te, nu, nv: (te[j], 0, 0)),
                  pl.BlockSpec((1, D_FF, D_MODEL), lambda j, te, nu, nv: (te[j], 0, 0)),
                  pl.BlockSpec((1, 1, D_MODEL), lambda j, te, nu, nv: (te[j], 0, 0))],
        out_specs=pl.BlockSpec((tm, PACKED), lambda j, te, nu, nv: (j, 0)),
        scratch_shapes=[pltpu.VMEM((D_MODEL, 2 * D_FF), BF16), pltpu.VMEM((D_FF, D_MODEL), BF16)],
    )
    return pl.pallas_call(
        _moe_kernel,
        grid_spec=grid_spec,
        out_shape=jax.ShapeDtypeStruct((P, PACKED), U32),
        compiler_params=pltpu.CompilerParams(dimension_semantics=("arbitrary",), vmem_limit_bytes=VMEM_LIMIT),
        name="moe",
    )(tile_expert, n_used, n_valid, xs, wgu, bgu, wd, bd)


def _sc_gather_rows(table, idx):
    n, d = idx.shape[0], table.shape[1]
    per_worker = n // SC_WORKERS
    steps = per_worker // SC_ROWS
    assert steps % 2 == 0

    @functools.partial(pl.kernel, mesh=_sc_mesh(), out_type=jax.ShapeDtypeStruct((n, d), table.dtype),
                       scratch_types=[pltpu.VMEM((2, SC_ROWS), jnp.int32), pltpu.VMEM((2, SC_ROWS, d), table.dtype),
                                      pltpu.SemaphoreType.DMA((2,)), pltpu.SemaphoreType.DMA((2,))])
    def gather(table_hbm, idx_hbm, out_hbm, idx_v, rows_v, gsem, wsem):
        base = _sc_worker() * per_worker

        def rows_at(j):
            return pl.ds(pl.multiple_of(base + j * SC_ROWS, SC_ROWS), SC_ROWS)

        def gather_copy(slot):
            return pltpu.make_async_copy(table_hbm.at[idx_v.at[slot]], rows_v.at[slot], gsem.at[slot])

        def write_copy(j, slot):
            return pltpu.make_async_copy(rows_v.at[slot], out_hbm.at[rows_at(j)], wsem.at[slot])

        def fetch(j, slot):
            pltpu.sync_copy(idx_hbm.at[rows_at(j)], idx_v.at[slot])
            gather_copy(slot).start()

        fetch(0, 0)

        @pl.loop(0, steps, step=2)
        def _(j0):
            for slot in range(2):
                j = j0 + slot
                gather_copy(slot).wait()

                @pl.when(j >= 1)
                def _():
                    write_copy(j - 1, 1 - slot).wait()

                @pl.when(j + 1 < steps)
                def _():
                    fetch(j + 1, 1 - slot)

                write_copy(j, slot).start()

        write_copy(steps - 1, 1).wait()

    return gather(table, idx)


def _final_kernel(x1_ref, yg_ref, route_ref, g_ref, o_ref):
    tm = x1_ref.shape[0]
    route_t = jnp.concatenate([route_ref[...], jnp.zeros((LANES - ROUTE_ROWS, tm), F32)], axis=0).T
    acc = x1_ref[...]
    for k in range(TOP_K):
        acc = acc + route_t[:, ROUTE_W + k:ROUTE_W + k + 1] * _unpack_bf16_pairs(yg_ref[k])
    o_ref[...] = _rms(acc, g_ref[...])


def _final(x1, yg, route, g, out_prev, tile0, n_total):
    Tc = x1.shape[0]
    tm = TM_FINAL
    in_specs = [pl.BlockSpec((tm, D_MODEL), lambda i: (i, 0)),
                pl.BlockSpec((TOP_K, tm, PACKED), lambda i: (0, i, 0)),
                pl.BlockSpec((ROUTE_ROWS, tm), lambda i: (0, i)),
                pl.BlockSpec((1, D_MODEL), lambda i: (0, 0))]
    args = [x1, yg, route, g]
    kernel_fn, aliases = _final_kernel, {}
    if out_prev is not None:
        in_specs.append(pl.BlockSpec(memory_space=pl.ANY))
        args.append(out_prev)
        kernel_fn = lambda x1_ref, yg_ref, route_ref, g_ref, prev_ref, o_ref: _final_kernel(x1_ref, yg_ref, route_ref,
                                                                                            g_ref, o_ref)
        aliases = {len(args) - 1: 0}
    return pl.pallas_call(
        kernel_fn,
        grid=(Tc // tm,),
        in_specs=in_specs,
        out_specs=pl.BlockSpec((tm, D_MODEL), lambda i: (i + tile0, 0)),
        out_shape=jax.ShapeDtypeStruct((n_total * tm, D_MODEL), F32),
        input_output_aliases=aliases,
        compiler_params=pltpu.CompilerParams(dimension_semantics=("arbitrary",), vmem_limit_bytes=VMEM_LIMIT),
        name="final",
    )(*args)


def _rope_tables(S):
    half = HEAD_DIM // 2
    inv = ROPE_THETA ** (-jnp.arange(half, dtype=F32) / half)
    ang = jnp.arange(S, dtype=F32)[:, None] * inv[None, :]
    cos = jnp.tile(jnp.cos(ang), (1, LANES // half))
    sin = jnp.tile(jnp.concatenate([-jnp.sin(ang), jnp.sin(ang)], axis=1), (1, LANES // HEAD_DIM))
    return cos, sin


def _selection_constants(S):
    nc = (S - CMP_BLOCK) // CMP_STRIDE + 1
    n_sel = S // SEL_BLOCK
    j = jnp.arange(n_sel)[:, None]
    i = jnp.arange(N_CMP_PAD)[None, :]
    overlap_t = ((i * CMP_STRIDE <= j * SEL_BLOCK + SEL_BLOCK - 1)
                 & (i * CMP_STRIDE + CMP_BLOCK - 1 >= j * SEL_BLOCK) & (i < nc)).astype(BF16)
    row = jnp.arange(LANES)[:, None]
    key = jnp.arange(S)[None, :]
    drop_bias = jnp.where(key // SEL_BLOCK == row, NEG_INF, 0.0).astype(BF16)[:n_sel]
    drop_bias = drop_bias.T.reshape(S // CKS, CKS, n_sel)
    return overlap_t, drop_bias


def kernel(x, mem, norm_mix, norm_mem, w_in, cmp_pos, cmp_w1, cmp_w2, w_pool, pool_scale, w_mem_kv, w_up_nsa,
           w_up_pool, w_up_mem, w_out, norm_ffn, w_router, b_router, w_gate_up, b_gate_up, w_down, b_down,
           norm_final):
    B, S, D = x.shape
    T = B * S
    assert D == D_MODEL and S % CKS == 0 and S // SEL_BLOCK == 32 and T % (sum(CHUNK_SHARES) * SC_WORKERS * SC_ROWS * 2) == 0
    l = 0
    x2 = x.reshape(T, D)

    w = w_in[l]
    o_gate = NSA_WIDTH + 6 * KV_WIDTH
    n_gate = 3 * NSA_HEADS
    o_pool = o_gate + n_gate
    o_qm = o_pool + POOL_WIDTH
    o_mg = o_qm + MEM_WIDTH
    w_a = jnp.concatenate([w[:, :o_gate], w[:, o_pool:o_mg], w[:, o_gate:o_pool],
                           jnp.zeros((D, GATE_PAD - n_gate), F32)], axis=1).astype(BF16)
    w_mg = w[:, o_mg:].astype(BF16)
    cos_t, sin_t = _rope_tables(S)
    overlap_t, drop_bias = _selection_constants(S)
    stream_kv = jnp.arange(2 * NSA_GROUPS) // NSA_GROUPS
    w1_half = cmp_w1[l].reshape(2, 2, CMP_STRIDE, HEAD_DIM, HEAD_DIM)[stream_kv]
    w1 = jnp.einsum("shlde,ts->hltdse", w1_half, jnp.eye(2 * NSA_GROUPS, dtype=F32))
    w1 = w1.reshape(2, CMP_COLS, 2 * KV_WIDTH).astype(BF16)
    w2 = cmp_w2[l].astype(BF16)
    w2t = jnp.swapaxes(w2, 1, 2)
    pos_half = cmp_pos[l].reshape(2, 2, CMP_STRIDE, HEAD_DIM)[stream_kv]
    pos = jnp.broadcast_to(pos_half.transpose(1, 2, 0, 3).reshape(2, 1, CMP_COLS), (2, 8, CMP_COLS)).astype(BF16)
    wpool_bd = jnp.zeros((POOL_WIDTH, POOL_WIDTH), F32)
    for gi in range(len(POOL_WINDOWS)):
        wpool_bd = wpool_bd.at[gi * POOL_GROUP:(gi + 1) * POOL_GROUP, gi * POOL_GROUP:(gi + 1) * POOL_GROUP].set(w_pool[l, gi])
    wr = w_router[l].T
    wr_hi = wr.astype(BF16)
    wr_lo = (wr - wr_hi.astype(F32)).astype(BF16)
    br = b_router[l].reshape(N_EXPERTS, 1)
    tri = (jnp.arange(TM_MERGE)[:, None] < jnp.arange(TM_MERGE)[None, :]).astype(BF16)
    lower = (jnp.arange(N_EXPERTS)[None, :] < jnp.arange(N_EXPERTS)[:, None]).astype(BF16)

    km, vm = _memkv(mem, norm_mem[l].reshape(1, D), w_mem_kv[l].astype(BF16))
    qc, qr, kvc, ksel, vsel, kwin, vwin, pool_in, qm, gates = _inproj(
        x2, norm_mix[l].reshape(1, D), w_a, cos_t, sin_t, S)
    o_nsa = _nsa(qc, qr, kvc, ksel, vsel, kwin, vwin, gates, w1, w2, w2t, pos, overlap_t, drop_bias, B, S)
    consts = [norm_mix[l].reshape(1, D), w_mg, wpool_bd.astype(BF16), pool_scale[l].reshape(1, POOL_WIDTH),
              w_up_nsa[l].astype(BF16), w_up_pool[l].astype(BF16), w_up_mem[l].astype(BF16), w_out[l].astype(BF16),
              norm_ffn[l].reshape(1, D), wr_hi, wr_lo, br, tri, lower]
    unit = T // sum(CHUNK_SHARES)
    chunk_tokens = [share * unit for share in CHUNK_SHARES]
    chunk_first = [sum(chunk_tokens[:c]) for c in range(len(chunk_tokens))]
    wgu, bgu = w_gate_up[l], b_gate_up[l].reshape(N_EXPERTS, 1, 2 * D_FF)
    wd, bd = w_down[l], b_down[l].reshape(N_EXPERTS, 1, D_MODEL)

    def route_chunk(c):
        Tc = chunk_tokens[c]
        n_tiles = (Tc * TOP_K) // TM_E + N_EXPERTS
        tile_start = jnp.arange(n_tiles, dtype=jnp.int32) * TM_E
        x1, hf, route, counts = _merge(x2, o_nsa, pool_in, qm, km, vm, consts, S, chunk_first[c] // TM_MERGE,
                                       Tc // TM_MERGE)
        counts = counts[:, 0].astype(jnp.int32)
        padded = ((counts + TM_E - 1) // TM_E) * TM_E
        ends = jnp.cumsum(padded)
        starts = ends - padded
        e_k = route[ROUTE_E:ROUTE_E + TOP_K].astype(jnp.int32)
        r_k = route[ROUTE_R:ROUTE_R + TOP_K].astype(jnp.int32)
        group_start = jnp.zeros_like(e_k)
        for e in range(N_EXPERTS):
            group_start = jnp.where(e_k == e, starts[e], group_start)
        dest = group_start + r_k
        tile_expert = jnp.minimum(jnp.sum(tile_start[:, None] >= ends[None, :], axis=1), N_EXPERTS - 1).astype(jnp.int32)
        n_used = (ends[-1] // TM_E).astype(jnp.int32).reshape(1)
        n_valid = jnp.clip((starts + counts)[tile_expert] - tile_start, 0, TM_E).astype(jnp.int32)
        xs = _sc_dispatch(hf, dest, n_tiles * TM_E)
        return x1, route, dest, (tile_expert, n_used, n_valid, xs)

    routed = [route_chunk(c) for c in range(len(chunk_tokens))]
    expert_out = [_moe(*moe_args, wgu, bgu, wd, bd) for _, _, _, moe_args in routed]
    out = None
    for c, ((x1, route, dest, _), ys) in enumerate(zip(routed, expert_out)):
        yg = _sc_gather_rows(ys, dest.reshape(-1)).reshape(TOP_K, chunk_tokens[c], PACKED)
        out = _final(x1, yg, route, norm_final.reshape(1, D), out, chunk_first[c] // TM_FINAL, T // TM_FINAL)
    return out.reshape(B, S, D)
```

```python
import functools

import jax
import jax.numpy as jnp
from jax import lax
from jax.experimental import pallas as pl
from jax.experimental.pallas import tpu as pltpu
from jax.experimental.pallas import tpu_sc as plsc

F32 = jnp.float32
BF16 = jnp.bfloat16
U32 = jnp.uint32

D_MODEL = 1024
HEAD_DIM = 64
NSA_HEADS = 8
NSA_GROUPS = 2
HEADS_PER_GROUP = NSA_HEADS // NSA_GROUPS
NSA_WIDTH = NSA_HEADS * HEAD_DIM
KV_WIDTH = NSA_GROUPS * HEAD_DIM
CMP_BLOCK = 32
CMP_STRIDE = 16
SEL_BLOCK = 64
SEL_TOPN = 8
FORCE_BONUS = 1000.0
WINDOW = 512
POOL_WINDOWS = (2, 4, 8, 16)
POOL_GROUP = 64
POOL_WIDTH = POOL_GROUP * len(POOL_WINDOWS)
POOL_HALO = 16
MEM_HEADS = 4
MEM_WIDTH = MEM_HEADS * HEAD_DIM
N_EXPERTS = 32
TOP_K = 4
D_FF = 1024
SWIGLU_LIMIT = 7.0
SWIGLU_ALPHA = 1.702
ROPE_THETA = 10000.0
EPS = 1e-5
NEG_INF = -1e30
TINY = 1e-30
QK_SCALE = HEAD_DIM ** -0.5
LOG2_E = 1.4426950408889634
NSA_Q_SCALE = QK_SCALE * LOG2_E

LANES = 128
GATE_PAD = LANES
PACKED = D_MODEL // 2

TM_IN = 1024
TQ = 256
CKS = 512
WIN_CHUNKS = (256, 256, 256)
TM_MERGE = 1024
TM_E = 512
TM_FINAL = 512
CHUNK_SHARES = (3, 1)
SC_CORES = 2
SC_SUBCORES = 16
SC_WORKERS = SC_CORES * SC_SUBCORES
SC_ROWS = 64
VMEM_LIMIT = 56 * 1024 * 1024


def _rms(x, g):
    return x * lax.rsqrt(jnp.mean(x * x, axis=-1, keepdims=True) + EPS) * g


def _sigmoid(x):
    return 0.5 * jnp.tanh(0.5 * x) + 0.5


def _dot(a, b):
    return jnp.dot(a, b, preferred_element_type=F32)


def _pack_bf16_pairs(v):
    n = v.shape[1] // 2
    r = v.astype(BF16).astype(F32)
    lo = pltpu.bitcast(r[:, :n], U32) >> 16
    hi = pltpu.bitcast(r[:, n:], U32) & jnp.uint32(0xFFFF0000)
    return lo | hi


def _unpack_bf16_pairs(w):
    lo = pltpu.bitcast(w << 16, F32)
    hi = pltpu.bitcast(w & jnp.uint32(0xFFFF0000), F32)
    return jnp.concatenate([lo, hi], axis=1)


def _dot_nt(a, b):
    return lax.dot_general(a, b, (((1,), (1,)), ((), ())), preferred_element_type=F32)


def _dot_tn(a, b):
    return lax.dot_general(a, b, (((0,), (0,)), ((), ())), preferred_element_type=F32)


def _memkv_kernel(mem_ref, g_ref, w_ref, k_ref, v_ref):
    m = _rms(mem_ref[0], g_ref[...]).astype(BF16)
    kv = _dot(m, w_ref[...])
    for h in range(MEM_HEADS):
        k_ref[0, h] = kv[:, h * HEAD_DIM:(h + 1) * HEAD_DIM].astype(BF16)
        v_ref[0, h] = kv[:, MEM_WIDTH + h * HEAD_DIM:MEM_WIDTH + (h + 1) * HEAD_DIM].astype(BF16)


def _memkv(mem, g, w):
    B, M, D = mem.shape
    return pl.pallas_call(
        _memkv_kernel,
        grid=(B,),
        in_specs=[pl.BlockSpec((1, M, D), lambda b: (b, 0, 0)),
                  pl.BlockSpec((1, D), lambda b: (0, 0)),
                  pl.BlockSpec((D, 2 * MEM_WIDTH), lambda b: (0, 0))],
        out_specs=[pl.BlockSpec((1, MEM_HEADS, M, HEAD_DIM), lambda b: (b, 0, 0, 0)),
                   pl.BlockSpec((1, MEM_HEADS, M, HEAD_DIM), lambda b: (b, 0, 0, 0))],
        out_shape=[jax.ShapeDtypeStruct((B, MEM_HEADS, M, HEAD_DIM), BF16),
                   jax.ShapeDtypeStruct((B, MEM_HEADS, M, HEAD_DIM), BF16)],
        compiler_params=pltpu.CompilerParams(dimension_semantics=("arbitrary",), vmem_limit_bytes=VMEM_LIMIT),
        name="memkv",
    )(mem, g, w)


IN_COLS = NSA_WIDTH + 6 * KV_WIDTH + POOL_WIDTH + MEM_WIDTH + GATE_PAD
CMP_COLS = CMP_STRIDE * 2 * KV_WIDTH


def _inproj_kernel(x_ref, g_ref, w_ref, cos_ref, sin_ref,
                   qc_ref, qr_ref, kvc_ref, ksel_ref, vsel_ref, kwin_ref, vwin_ref, pool_ref, qm_ref, gate_ref,
                   kv_s):
    h = _rms(x_ref[...], g_ref[...]).astype(BF16)
    p = _dot(h, w_ref[...])
    cos = cos_ref[...]
    sin = sin_ref[...]
    lane = lax.broadcasted_iota(jnp.int32, cos.shape, 1)
    first_half = (lane % HEAD_DIM) < (HEAD_DIM // 2)

    def rope(c):
        partner = jnp.where(first_half, pltpu.roll(c, LANES - HEAD_DIM // 2, 1), pltpu.roll(c, HEAD_DIM // 2, 1))
        return c * cos + partner * sin

    def halves(c):
        return c[:, :HEAD_DIM], c[:, HEAD_DIM:]

    for j in range(NSA_WIDTH // LANES):
        c = p[:, j * LANES:(j + 1) * LANES]
        r = rope(c)
        for hh, (cc, rr) in enumerate(zip(halves(c), halves(r))):
            qc_ref[2 * j + hh] = (cc * NSA_Q_SCALE).astype(BF16)
            qr_ref[2 * j + hh] = (rr * NSA_Q_SCALE).astype(BF16)
    o = NSA_WIDTH
    n_rows = kv_s.shape[1] // CMP_STRIDE
    for c in range(2):
        kv_s[c] = p[:, o + c * KV_WIDTH:o + (c + 1) * KV_WIDTH]
    for l_ in range(CMP_STRIDE):
        for c in range(2):
            col = (2 * l_ + c) * KV_WIDTH
            kvc_ref[:, col:col + KV_WIDTH] = kv_s[c, pl.ds(l_, n_rows, stride=CMP_STRIDE), :]
    o += 2 * KV_WIDTH
    for ref, rot in ((ksel_ref, True), (vsel_ref, False), (kwin_ref, True), (vwin_ref, False)):
        c = p[:, o:o + KV_WIDTH]
        if rot:
            c = rope(c)
        for g, cc in enumerate(halves(c)):
            ref[g] = cc.astype(BF16)
        o += KV_WIDTH
    pool_ref[...] = p[:, o:o + POOL_WIDTH]
    o += POOL_WIDTH
    for hh in range(MEM_HEADS):
        qm_ref[hh] = (p[:, o + hh * HEAD_DIM:o + (hh + 1) * HEAD_DIM] * QK_SCALE).astype(BF16)
    o += MEM_WIDTH
    gate_ref[...] = _sigmoid(p[:, o:o + GATE_PAD])


def _inproj(x2, g, w_a, cos_t, sin_t, S):
    T = x2.shape[0]
    tm = TM_IN
    n_s = S // tm
    head_spec = lambda n: pl.BlockSpec((n, tm, HEAD_DIM), lambda i: (0, i, 0))
    row_spec = lambda w: pl.BlockSpec((tm, w), lambda i: (i, 0))
    return pl.pallas_call(
        _inproj_kernel,
        grid=(T // tm,),
        in_specs=[row_spec(D_MODEL),
                  pl.BlockSpec((1, D_MODEL), lambda i: (0, 0)),
                  pl.BlockSpec((D_MODEL, IN_COLS), lambda i: (0, 0)),
                  pl.BlockSpec((tm, LANES), lambda i: (i % n_s, 0)),
                  pl.BlockSpec((tm, LANES), lambda i: (i % n_s, 0))],
        out_specs=[head_spec(NSA_HEADS), head_spec(NSA_HEADS),
                   pl.BlockSpec((tm // CMP_STRIDE, CMP_COLS), lambda i: (i, 0)),
                   head_spec(NSA_GROUPS), head_spec(NSA_GROUPS), head_spec(NSA_GROUPS), head_spec(NSA_GROUPS),
                   row_spec(POOL_WIDTH), head_spec(MEM_HEADS), row_spec(GATE_PAD)],
        out_shape=[jax.ShapeDtypeStruct((NSA_HEADS, T, HEAD_DIM), BF16),
                   jax.ShapeDtypeStruct((NSA_HEADS, T, HEAD_DIM), BF16),
                   jax.ShapeDtypeStruct((T // CMP_STRIDE, CMP_COLS), F32),
                   jax.ShapeDtypeStruct((NSA_GROUPS, T, HEAD_DIM), BF16),
                   jax.ShapeDtypeStruct((NSA_GROUPS, T, HEAD_DIM), BF16),
                   jax.ShapeDtypeStruct((NSA_GROUPS, T, HEAD_DIM), BF16),
                   jax.ShapeDtypeStruct((NSA_GROUPS, T, HEAD_DIM), BF16),
                   jax.ShapeDtypeStruct((T, POOL_WIDTH), F32),
                   jax.ShapeDtypeStruct((MEM_HEADS, T, HEAD_DIM), BF16),
                   jax.ShapeDtypeStruct((T, GATE_PAD), F32)],
        scratch_shapes=[pltpu.VMEM((2, tm, KV_WIDTH), F32)],
        compiler_params=pltpu.CompilerParams(dimension_semantics=("arbitrary",), vmem_limit_bytes=VMEM_LIMIT),
        name="inproj",
    )(x2, g, w_a, cos_t, sin_t)


N_CMP_PAD = 128


def _nsa_kernel(qc_ref, qr_ref, kvc_ref, ksel_ref, vsel_ref, kwin_ref, vwin_ref, gate_ref,
                w1_ref, w2_ref, w2t_ref, pos_ref, ovl_ref, drop_ref, o_ref, kc_s, vct_s, *, S):
    i = pl.program_id(1)
    tq = TQ
    hpg = HEADS_PER_GROUP
    n_sel = S // SEL_BLOCK
    hq = hpg * tq

    @pl.when(i == 0)
    def _compress():
        a = kvc_ref[...].astype(BF16)
        p1 = _dot(a, w1_ref[0])
        p2 = _dot(a, w1_ref[1])
        posterm = (_dot(pos_ref[0], w1_ref[0]) + _dot(pos_ref[1], w1_ref[1]))[0:1]
        hid = p1 + pltpu.roll(p2, N_CMP_PAD - 1, 0) + posterm
        hid = (hid * _sigmoid(hid)).astype(BF16)
        for g in range(NSA_GROUPS):
            kc_s[g] = _dot(hid[:, g * HEAD_DIM:(g + 1) * HEAD_DIM], w2_ref[0]).astype(BF16)
            vct_s[g] = _dot_nt(w2t_ref[1], hid[:, KV_WIDTH + g * HEAD_DIM:KV_WIDTH + (g + 1) * HEAD_DIM]).astype(BF16)

    q0 = i * tq
    t_lane = q0 + lax.broadcasted_iota(jnp.int32, (1, tq), 1)

    def key_pos(start, n):
        return start + lax.broadcasted_iota(jnp.int32, (n, 1), 0)

    cmp_valid = (key_pos(0, N_CMP_PAD) * CMP_STRIDE + CMP_BLOCK - 1) <= t_lane
    jrow = lax.broadcasted_iota(jnp.int32, (n_sel, tq), 0)
    cur = t_lane // SEL_BLOCK
    sel_valid = jrow * SEL_BLOCK <= t_lane
    forced = (jrow == 0) | (jrow == cur) | (jrow == cur - 1)

    cd = q0 // CKS
    causal_bias = jnp.where(key_pos(cd * CKS, CKS) <= t_lane, 0.0, NEG_INF)
    win_chunks = []
    hi = q0 + tq
    for n in WIN_CHUNKS:
        lo = hi - n
        start = pl.multiple_of(jnp.maximum(lo, 0), LANES)
        kp = key_pos(start, n)
        diff = t_lane - kp
        win_chunks.append((start, n, jnp.where((diff >= 0) & (diff < WINDOW) & (kp < hi), 0.0, NEG_INF)))
        hi = lo

    def attend(jobs):
        scores = [_dot_nt(k, q_all) for q_all, k, _, _ in jobs]
        heads = [slice(hh * tq, (hh + 1) * tq) for hh in range(hpg)]
        maxes = [jnp.concatenate([jnp.max(s_all[:, sl] + bias, axis=0, keepdims=True) for sl in heads], axis=1)
                 for (_, _, _, bias), s_all in zip(jobs, scores)]
        soft = []
        for (_, _, _, bias), s_all, m in zip(jobs, scores, maxes):
            ps = [jnp.exp2(s_all[:, sl] + bias - m[:, sl]) for sl in heads]
            l = jnp.concatenate([jnp.sum(p, axis=0, keepdims=True) for p in ps], axis=1)
            soft.append((l, jnp.concatenate([p.astype(BF16) for p in ps], axis=1)))
        return [(m, l, _dot_tn(v, p)) for (_, _, v, _), m, (l, p) in zip(jobs, maxes, soft)]

    def merge(a, b):
        m = jnp.maximum(a[0], b[0])
        wa = jnp.exp2(a[0] - m)
        wb = jnp.exp2(b[0] - m)
        return m, wa * a[1] + wb * b[1], wa * a[2] + wb * b[2]

    def select_blocks(g):
        q_cmp = qc_ref[g * hpg:(g + 1) * hpg].reshape(hq, HEAD_DIM)
        s_all = _dot_nt(kc_s[g], q_cmp)
        p_grp = jnp.zeros((N_CMP_PAD, tq), F32)
        ps = []
        for hh in range(hpg):
            sl = slice(hh * tq, (hh + 1) * tq)
            s = jnp.where(cmp_valid, s_all[:, sl], NEG_INF)
            m = jnp.max(s, axis=0, keepdims=True)
            e = jnp.where(cmp_valid, jnp.exp2(s - m), 0.0)
            p = e * (1.0 / jnp.maximum(jnp.sum(e, axis=0, keepdims=True), TINY))
            p_grp = p_grp + p
            ps.append(p.astype(BF16))
        o_cmp = _dot(vct_s[g], jnp.concatenate(ps, axis=1))

        ovl = ovl_ref[...]
        p_hi = p_grp.astype(BF16)
        r1 = p_grp - p_hi.astype(F32)
        p_mid = r1.astype(BF16)
        p_lo = (r1 - p_mid.astype(F32)).astype(BF16)
        score = _dot(ovl, p_hi) + _dot(ovl, p_mid) + _dot(ovl, p_lo)
        score = jnp.where(sel_valid, score + jnp.where(forced, FORCE_BONUS, 0.0), -1.0)
        rank = jnp.zeros((n_sel, tq), F32)
        for jp in range(n_sel):
            other = score[jp:jp + 1, :]
            beats = (other > score) | ((other == score) & (jrow > jp))
            rank = rank + beats.astype(F32)
        return o_cmp, (rank >= SEL_TOPN).astype(BF16)

    groups = range(NSA_GROUPS)
    cmp_out = [select_blocks(g) for g in groups]
    q_rot = [qr_ref[g * hpg:(g + 1) * hpg].reshape(hq, HEAD_DIM) for g in groups]

    def selected_job(g, c, extra_bias):
        k0 = pl.multiple_of(c * CKS, CKS)
        bias = _dot(drop_ref[c], cmp_out[g][1])
        if extra_bias is not None:
            bias = bias + extra_bias
        return q_rot[g], ksel_ref[g, pl.ds(k0, CKS), :], vsel_ref[g, pl.ds(k0, CKS), :], bias

    def window_job(g, chunk):
        start, n, bias = chunk
        return q_rot[g], kwin_ref[g, pl.ds(start, n), :], vwin_ref[g, pl.ds(start, n), :], bias

    stats = attend([selected_job(g, cd, causal_bias) for g in groups]
                   + [window_job(g, chunk) for chunk in win_chunks for g in groups])
    sel_state = tuple(stats[:NSA_GROUPS])
    win_state = list(stats[NSA_GROUPS:2 * NSA_GROUPS])
    for j in range(1, len(win_chunks)):
        for g in groups:
            win_state[g] = merge(win_state[g], stats[NSA_GROUPS * (1 + j) + g])

    def sel_body(it, states):
        new = attend([selected_job(g, cd - 1 - it, None) for g in groups])
        return tuple(merge(states[g], new[g]) for g in groups)

    sel_state = lax.fori_loop(0, cd, sel_body, sel_state)

    gates_t = gate_ref[...].T
    out_rows = []
    for g in groups:
        o_cmp = cmp_out[g][0]
        o_sel = sel_state[g][2] * (1.0 / sel_state[g][1])
        o_win = win_state[g][2] * (1.0 / win_state[g][1])
        for hh in range(hpg):
            h = g * hpg + hh
            sl = slice(hh * tq, (hh + 1) * tq)
            out_rows.append(gates_t[3 * h:3 * h + 1] * o_cmp[:, sl] + gates_t[3 * h + 1:3 * h + 2] * o_sel[:, sl]
                            + gates_t[3 * h + 2:3 * h + 3] * o_win[:, sl])
    o_ref[...] = jnp.concatenate(out_rows, axis=0).T.astype(BF16)


def _nsa(qc, qr, kvc, ksel, vsel_t, kwin, vwin_t, gates, w1, w2, w2t, pos, ovl, drop_bias, B, S):
    T = B * S
    tq = TQ
    nq = S // tq
    q_spec = pl.BlockSpec((NSA_HEADS, tq, HEAD_DIM), lambda b, i: (0, b * nq + i, 0))
    k_spec = pl.BlockSpec((NSA_GROUPS, S, HEAD_DIM), lambda b, i: (0, b, 0))
    full = lambda a: pl.BlockSpec(a.shape, lambda b, i: (0,) * a.ndim)
    return pl.pallas_call(
        functools.partial(_nsa_kernel, S=S),
        grid=(B, nq),
        in_specs=[q_spec, q_spec,
                  pl.BlockSpec((N_CMP_PAD, CMP_COLS), lambda b, i: (b, 0)),
                  k_spec, k_spec, k_spec, k_spec,
                  pl.BlockSpec((tq, GATE_PAD), lambda b, i: (b * nq + i, 0)),
                  full(w1), full(w2), full(w2t), full(pos), full(ovl), full(drop_bias)],
        out_specs=pl.BlockSpec((tq, NSA_WIDTH), lambda b, i: (b * nq + i, 0)),
        out_shape=jax.ShapeDtypeStruct((T, NSA_WIDTH), BF16),
        scratch_shapes=[pltpu.VMEM((NSA_GROUPS, N_CMP_PAD, HEAD_DIM), BF16),
                        pltpu.VMEM((NSA_GROUPS, HEAD_DIM, N_CMP_PAD), BF16)],
        compiler_params=pltpu.CompilerParams(dimension_semantics=("arbitrary", "arbitrary"),
                                             vmem_limit_bytes=VMEM_LIMIT),
        name="nsa",
    )(qc, qr, kvc, ksel, vsel_t, kwin, vwin_t, gates, w1, w2, w2t, pos, ovl, drop_bias)


ROUTE_E, ROUTE_R, ROUTE_W = 0, TOP_K, 2 * TOP_K
ROUTE_ROWS = 16


def _merge_kernel(x_ref, onsa_ref, pool_ref, prev_ref, qm_ref, km_ref, vm_ref,
                  gmix_ref, wmg_ref, wpool_ref, pscale_ref, wun_ref, wup_ref, wum_ref, wout_ref,
                  gffn_ref, wrh_ref, wrl_ref, br_ref, tri_ref, lower_ref,
                  x1_ref, hf_ref, route_ref, cnt_ref, omem_s, carry_s, *, S, tile0):
    i = pl.program_id(0) + tile0
    tm = TM_MERGE
    n_s = S // tm

    @pl.when(pl.program_id(0) == 0)
    def _init():
        carry_s[...] = jnp.zeros_like(carry_s)

    x = x_ref[...]
    h = _rms(x, gmix_ref[...]).astype(BF16)

    u = pool_ref[...]
    seq_tile = i % n_s
    prev = jnp.where(seq_tile == 0, 0.0, prev_ref[...])
    ext = jnp.concatenate([prev, u], axis=0)
    b2 = ext[1:] + ext[:-1]
    b4 = b2[2:] + b2[:-2]
    b8 = b4[4:] + b4[:-4]
    b16 = b8[8:] + b8[:-8]
    sums = (b2[POOL_HALO - 1:POOL_HALO - 1 + tm], b4[POOL_HALO - 3:POOL_HALO - 3 + tm],
            b8[POOL_HALO - 7:POOL_HALO - 7 + tm], b16[POOL_HALO - 15:POOL_HALO - 15 + tm])
    t_seq = seq_tile * tm + lax.broadcasted_iota(jnp.int32, (tm, 1), 0)
    lane_p = lax.broadcasted_iota(jnp.int32, (tm, POOL_WIDTH), 1)
    z = jnp.zeros((tm, POOL_WIDTH), F32)
    for gi, w in enumerate(POOL_WINDOWS):
        cnt = jnp.minimum(t_seq + 1, w).astype(F32)
        z = jnp.where(lane_p // POOL_GROUP == gi, sums[gi] / cnt, z)
    z = z - u
    o_pool = (_dot(z.astype(BF16), wpool_ref[...]) * pscale_ref[...]).astype(BF16)

    for hh in range(MEM_HEADS):
        s = _dot_nt(qm_ref[hh], km_ref[0, hh])
        m = jnp.max(s, axis=-1, keepdims=True)
        e = jnp.exp(s - m)
        p = e / jnp.sum(e, axis=-1, keepdims=True)
        omem_s[:, hh * HEAD_DIM:(hh + 1) * HEAD_DIM] = _dot(p.astype(BF16), vm_ref[0, hh]).astype(BF16)

    def gated(branch, w_ref, j):
        return _sigmoid(_dot(h, wmg_ref[:, j * D_MODEL:(j + 1) * D_MODEL])) * _dot(branch, w_ref[...])

    merged = gated(onsa_ref[...], wun_ref, 0) + gated(o_pool, wup_ref, 1) + gated(omem_s[...], wum_ref, 2)
    x1 = x + _dot(merged.astype(BF16), wout_ref[...])
    x1_ref[...] = x1
    hf = _rms(x1, gffn_ref[...])
    hf_ref[...] = _pack_bf16_pairs(hf)

    hf_hi = hf.astype(BF16)
    hf_lo = (hf - hf_hi.astype(F32)).astype(BF16)
    logits = (_dot_nt(wrh_ref[...], hf_hi) + _dot_nt(wrl_ref[...], hf_hi) + _dot_nt(wrh_ref[...], hf_lo)
              + br_ref[...])
    erow = lax.broadcasted_iota(jnp.int32, (N_EXPERTS, tm), 0)
    rank = jnp.zeros((N_EXPERTS, tm), F32)
    for jp in range(N_EXPERTS):
        other = logits[jp:jp + 1, :]
        beats = (other > logits) | ((other == logits) & (erow > jp))
        rank = rank + beats.astype(F32)
    chosen = rank < TOP_K
    m = jnp.max(logits, axis=0, keepdims=True)
    e = jnp.where(chosen, jnp.exp(logits - m), 0.0)
    comb = e * (1.0 / jnp.sum(e, axis=0, keepdims=True))

    chosen_b = chosen.astype(BF16)
    carry = carry_s[:, 0:1]
    in_expert = _dot(chosen_b, tri_ref[...]) + carry
    carry_new = carry + jnp.sum(chosen.astype(F32), axis=1, keepdims=True)
    carry_s[...] = jnp.broadcast_to(carry_new, carry_s.shape)
    cnt_ref[...] = jnp.broadcast_to(carry_new, cnt_ref.shape)

    before = _dot(lower_ref[...], chosen_b)
    erow_f = erow.astype(F32)
    fields = {ROUTE_E: erow_f, ROUTE_R: in_expert, ROUTE_W: comb}
    rows = [None] * ROUTE_ROWS
    for k in range(TOP_K):
        pick = chosen & (before == k)
        for base, val in fields.items():
            rows[base + k] = jnp.sum(jnp.where(pick, val, 0.0), axis=0, keepdims=True)
    zero_row = jnp.zeros((1, tm), F32)
    route_ref[...] = jnp.concatenate([zero_row if r is None else r for r in rows], axis=0)


def _merge(x2, onsa, pool_in, qm, km, vm, consts, S, tile0, n_tiles):
    tm = TM_MERGE
    Tc = n_tiles * tm
    n_s = S // tm
    M = km.shape[2]
    halo_per_tile = tm // POOL_HALO
    row = lambda w: pl.BlockSpec((tm, w), lambda i: (i + tile0, 0))
    out_row = lambda w: pl.BlockSpec((tm, w), lambda i: (i, 0))
    full = lambda a: pl.BlockSpec(a.shape, lambda i: (0,) * a.ndim)
    mem_spec = pl.BlockSpec((1, MEM_HEADS, M, HEAD_DIM), lambda i: ((i + tile0) // n_s, 0, 0, 0))
    return pl.pallas_call(
        functools.partial(_merge_kernel, S=S, tile0=tile0),
        grid=(n_tiles,),
        in_specs=[row(D_MODEL), row(NSA_WIDTH), row(POOL_WIDTH),
                  pl.BlockSpec((POOL_HALO, POOL_WIDTH),
                               lambda i: (jnp.maximum((i + tile0) * halo_per_tile - 1, 0), 0)),
                  pl.BlockSpec((MEM_HEADS, tm, HEAD_DIM), lambda i: (0, i + tile0, 0)),
                  mem_spec, mem_spec] + [full(c) for c in consts],
        out_specs=[out_row(D_MODEL), out_row(PACKED), pl.BlockSpec((ROUTE_ROWS, tm), lambda i: (0, i)),
                   pl.BlockSpec((N_EXPERTS, LANES), lambda i: (0, 0))],
        out_shape=[jax.ShapeDtypeStruct((Tc, D_MODEL), F32),
                   jax.ShapeDtypeStruct((Tc, PACKED), U32),
                   jax.ShapeDtypeStruct((ROUTE_ROWS, Tc), F32),
                   jax.ShapeDtypeStruct((N_EXPERTS, LANES), F32)],
        scratch_shapes=[pltpu.VMEM((tm, MEM_WIDTH), BF16), pltpu.VMEM((N_EXPERTS, LANES), F32)],
        compiler_params=pltpu.CompilerParams(dimension_semantics=("arbitrary",), vmem_limit_bytes=VMEM_LIMIT),
        name="merge",
    )(x2, onsa, pool_in, pool_in, qm, km, vm, *consts)


def _sc_mesh():
    return plsc.VectorSubcoreMesh(core_axis_name="c", subcore_axis_name="s")


def _sc_worker():
    return lax.axis_index("s") * SC_CORES + lax.axis_index("c")


def _sc_dispatch(hf, dest, n_rows):
    T, d = hf.shape
    per_worker = T // SC_WORKERS
    steps = per_worker // SC_ROWS

    @functools.partial(pl.kernel, mesh=_sc_mesh(), out_type=jax.ShapeDtypeStruct((n_rows, d), hf.dtype),
                       scratch_types=[pltpu.VMEM((TOP_K, SC_ROWS), jnp.int32), pltpu.VMEM((SC_ROWS, d), hf.dtype),
                                      pltpu.SemaphoreType.DMA])
    def dispatch(hf_hbm, dest_hbm, xs_hbm, idx_v, rows_v, sem):
        base = _sc_worker() * per_worker

        @pl.loop(0, steps)
        def _(j):
            rows = pl.ds(pl.multiple_of(base + j * SC_ROWS, SC_ROWS), SC_ROWS)
            pltpu.sync_copy(hf_hbm.at[rows], rows_v)
            for k in range(TOP_K):
                pltpu.sync_copy(dest_hbm.at[k, rows], idx_v.at[k])
            copies = [pltpu.make_async_copy(rows_v, xs_hbm.at[idx_v.at[k]], sem) for k in range(TOP_K)]
            for c in copies:
                c.start()
            for c in copies:
                c.wait()

    return dispatch(hf, dest)


def _moe_kernel(te_ref, nu_ref, nv_ref, xs_ref, wgu_ref, bgu_ref, wd_ref, bd_ref, ys_ref, wgu_s, wd_s):
    j = pl.program_id(0)
    used = j < nu_ref[0]
    new_expert = (j == 0) | (te_ref[j] != te_ref[jnp.maximum(j - 1, 0)])

    @pl.when(used & new_expert)
    def _cast_weights():
        wgu_s[...] = wgu_ref[0].astype(BF16)
        wd_s[...] = wd_ref[0].astype(BF16)

    tm = xs_ref.shape[0]
    n_live = nv_ref[j]

    def expert_rows(n):
        live = lax.broadcasted_iota(jnp.int32, (n, 1), 0) < n_live
        xb = _unpack_bf16_pairs(jnp.where(live, xs_ref[0:n, :], jnp.uint32(0))).astype(BF16)
        gu = _dot(xb, wgu_s[...]) + bgu_ref[0]
        gate = jnp.minimum(gu[:, :D_FF], SWIGLU_LIMIT)
        up = jnp.clip(gu[:, D_FF:], -SWIGLU_LIMIT, SWIGLU_LIMIT)
        act = (up + 1.0) * (gate * _sigmoid(SWIGLU_ALPHA * gate))
        ys_ref[0:n, :] = _pack_bf16_pairs(_dot(act.astype(BF16), wd_s[...]) + bd_ref[0])

    @pl.when(used & (n_live > tm // 2))
    def _full_tile():
        expert_rows(tm)

    @pl.when(used & (n_live <= tm // 2))
    def _half_tile():
        expert_rows(tm // 2)
        ys_ref[tm // 2:, :] = jnp.zeros((tm - tm // 2, ys_ref.shape[1]), ys_ref.dtype)

    @pl.when(j >= nu_ref[0])
    def _unused():
        ys_ref[...] = jnp.zeros_like(ys_ref)


def _moe(tile_expert, n_used, n_valid, xs, wgu, bgu, wd, bd):
    P = xs.shape[0]
    tm = TM_E
    grid_spec = pltpu.PrefetchScalarGridSpec(
        num_scalar_prefetch=3,
        grid=(P // tm,),
        in_specs=[pl.BlockSpec((tm, PACKED), lambda j, te, nu, nv: (j, 0)),
                  pl.BlockSpec((1, D_MODEL, 2 * D_FF), lambda j, te, nu, nv: (te[j], 0, 0)),
                  pl.BlockSpec((1, 1, 2 * D_FF), lambda j, te, nu, nv: (te[j], 0, 0)),
                  pl.BlockSpec((1, D_FF, D_MODEL), lambda j, te, nu, nv: (te[j], 0, 0)),
                  pl.BlockSpec((1, 1, D_MODEL), lambda j, te, nu, nv: (te[j], 0, 0))],
        out_specs=pl.BlockSpec((tm, PACKED), lambda j, te, nu, nv: (j, 0)),
        scratch_shapes=[pltpu.VMEM((D_MODEL, 2 * D_FF), BF16), pltpu.VMEM((D_FF, D_MODEL), BF16)],
    )
    return pl.pallas_call(
        _moe_kernel,
        grid_spec=grid_spec,
        out_shape=jax.ShapeDtypeStruct((P, PACKED), U32),
        compiler_params=pltpu.CompilerParams(dimension_semantics=("arbitrary",), vmem_limit_bytes=VMEM_LIMIT),
        name="moe",
    )(tile_expert, n_used, n_valid, xs, wgu, bgu, wd, bd)


def _sc_gather_rows(table, idx):
    n, d = idx.shape[0], table.shape[1]
    per_worker = n // SC_WORKERS
    steps = per_worker // SC_ROWS
    assert steps % 2 == 0

    @functools.partial(pl.kernel, mesh=_sc_mesh(), out_type=jax.ShapeDtypeStruct((n, d), table.dtype),
                       scratch_types=[pltpu.VMEM((2, SC_ROWS), jnp.int32), pltpu.VMEM((2, SC_ROWS, d), table.dtype),
                                      pltpu.SemaphoreType.DMA((2,)), pltpu.SemaphoreType.DMA((2,))])
    def gather(table_hbm, idx_hbm, out_hbm, idx_v, rows_v, gsem, wsem):
        base = _sc_worker() * per_worker

        def rows_at(j):
            return pl.ds(pl.multiple_of(base + j * SC_ROWS, SC_ROWS), SC_ROWS)

        def gather_copy(slot):
            return pltpu.make_async_copy(table_hbm.at[idx_v.at[slot]], rows_v.at[slot], gsem.at[slot])

        def write_copy(j, slot):
            return pltpu.make_async_copy(rows_v.at[slot], out_hbm.at[rows_at(j)], wsem.at[slot])

        def fetch(j, slot):
            pltpu.sync_copy(idx_hbm.at[rows_at(j)], idx_v.at[slot])
            gather_copy(slot).start()

        fetch(0, 0)

        @pl.loop(0, steps, step=2)
        def _(j0):
            for slot in range(2):
                j = j0 + slot
                gather_copy(slot).wait()

                @pl.when(j >= 1)
                def _():
                    write_copy(j - 1, 1 - slot).wait()

                @pl.when(j + 1 < steps)
                def _():
                    fetch(j + 1, 1 - slot)

                write_copy(j, slot).start()

        write_copy(steps - 1, 1).wait()

    return gather(table, idx)


def _final_kernel(x1_ref, yg_ref, route_ref, g_ref, o_ref):
    tm = x1_ref.shape[0]
    route_t = jnp.concatenate([route_ref[...], jnp.zeros((LANES - ROUTE_ROWS, tm), F32)], axis=0).T
    acc = x1_ref[...]
    for k in range(TOP_K):
        acc = acc + route_t[:, ROUTE_W + k:ROUTE_W + k + 1] * _unpack_bf16_pairs(yg_ref[k])
    o_ref[...] = _rms(acc, g_ref[...])


def _final(x1, yg, route, g, out_prev, tile0, n_total):
    Tc = x1.shape[0]
    tm = TM_FINAL
    in_specs = [pl.BlockSpec((tm, D_MODEL), lambda i: (i, 0)),
                pl.BlockSpec((TOP_K, tm, PACKED), lambda i: (0, i, 0)),
                pl.BlockSpec((ROUTE_ROWS, tm), lambda i: (0, i)),
                pl.BlockSpec((1, D_MODEL), lambda i: (0, 0))]
    args = [x1, yg, route, g]
    kernel_fn, aliases = _final_kernel, {}
    if out_prev is not None:
        in_specs.append(pl.BlockSpec(memory_space=pl.ANY))
        args.append(out_prev)
        kernel_fn = lambda x1_ref, yg_ref, route_ref, g_ref, prev_ref, o_ref: _final_kernel(x1_ref, yg_ref, route_ref,
                                                                                            g_ref, o_ref)
        aliases = {len(args) - 1: 0}
    return pl.pallas_call(
        kernel_fn,
        grid=(Tc // tm,),
        in_specs=in_specs,
        out_specs=pl.BlockSpec((tm, D_MODEL), lambda i: (i + tile0, 0)),
        out_shape=jax.ShapeDtypeStruct((n_total * tm, D_MODEL), F32),
        input_output_aliases=aliases,
        compiler_params=pltpu.CompilerParams(dimension_semantics=("arbitrary",), vmem_limit_bytes=VMEM_LIMIT),
        name="final",
    )(*args)


def _rope_tables(S):
    half = HEAD_DIM // 2
    inv = ROPE_THETA ** (-jnp.arange(half, dtype=F32) / half)
    ang = jnp.arange(S, dtype=F32)[:, None] * inv[None, :]
    cos = jnp.tile(jnp.cos(ang), (1, LANES // half))
    sin = jnp.tile(jnp.concatenate([-jnp.sin(ang), jnp.sin(ang)], axis=1), (1, LANES // HEAD_DIM))
    return cos, sin


def _selection_constants(S):
    nc = (S - CMP_BLOCK) // CMP_STRIDE + 1
    n_sel = S // SEL_BLOCK
    j = jnp.arange(n_sel)[:, None]
    i = jnp.arange(N_CMP_PAD)[None, :]
    overlap_t = ((i * CMP_STRIDE <= j * SEL_BLOCK + SEL_BLOCK - 1)
                 & (i * CMP_STRIDE + CMP_BLOCK - 1 >= j * SEL_BLOCK) & (i < nc)).astype(BF16)
    row = jnp.arange(LANES)[:, None]
    key = jnp.arange(S)[None, :]
    drop_bias = jnp.where(key // SEL_BLOCK == row, NEG_INF, 0.0).astype(BF16)[:n_sel]
    drop_bias = drop_bias.T.reshape(S // CKS, CKS, n_sel)
    return overlap_t, drop_bias


def kernel(x, mem, norm_mix, norm_mem, w_in, cmp_pos, cmp_w1, cmp_w2, w_pool, pool_scale, w_mem_kv, w_up_nsa,
           w_up_pool, w_up_mem, w_out, norm_ffn, w_router, b_router, w_gate_up, b_gate_up, w_down, b_down,
           norm_final):
    B, S, D = x.shape
    T = B * S
    assert D == D_MODEL and S % CKS == 0 and S // SEL_BLOCK == 32 and T % (sum(CHUNK_SHARES) * SC_WORKERS * SC_ROWS * 2) == 0
    l = 0
    x2 = x.reshape(T, D)

    w = w_in[l]
    o_gate = NSA_WIDTH + 6 * KV_WIDTH
    n_gate = 3 * NSA_HEADS
    o_pool = o_gate + n_gate
    o_qm = o_pool + POOL_WIDTH
    o_mg = o_qm + MEM_WIDTH
    w_a = jnp.concatenate([w[:, :o_gate], w[:, o_pool:o_mg], w[:, o_gate:o_pool],
                           jnp.zeros((D, GATE_PAD - n_gate), F32)], axis=1).astype(BF16)
    w_mg = w[:, o_mg:].astype(BF16)
    cos_t, sin_t = _rope_tables(S)
    overlap_t, drop_bias = _selection_constants(S)
    stream_kv = jnp.arange(2 * NSA_GROUPS) // NSA_GROUPS
    w1_half = cmp_w1[l].reshape(2, 2, CMP_STRIDE, HEAD_DIM, HEAD_DIM)[stream_kv]
    w1 = jnp.einsum("shlde,ts->hltdse", w1_half, jnp.eye(2 * NSA_GROUPS, dtype=F32))
    w1 = w1.reshape(2, CMP_COLS, 2 * KV_WIDTH).astype(BF16)
    w2 = cmp_w2[l].astype(BF16)
    w2t = jnp.swapaxes(w2, 1, 2)
    pos_half = cmp_pos[l].reshape(2, 2, CMP_STRIDE, HEAD_DIM)[stream_kv]
    pos = jnp.broadcast_to(pos_half.transpose(1, 2, 0, 3).reshape(2, 1, CMP_COLS), (2, 8, CMP_COLS)).astype(BF16)
    wpool_bd = jnp.zeros((POOL_WIDTH, POOL_WIDTH), F32)
    for gi in range(len(POOL_WINDOWS)):
        wpool_bd = wpool_bd.at[gi * POOL_GROUP:(gi + 1) * POOL_GROUP, gi * POOL_GROUP:(gi + 1) * POOL_GROUP].set(w_pool[l, gi])
    wr = w_router[l].T
    wr_hi = wr.astype(BF16)
    wr_lo = (wr - wr_hi.astype(F32)).astype(BF16)
    br = b_router[l].reshape(N_EXPERTS, 1)
    tri = (jnp.arange(TM_MERGE)[:, None] < jnp.arange(TM_MERGE)[None, :]).astype(BF16)
    lower = (jnp.arange(N_EXPERTS)[None, :] < jnp.arange(N_EXPERTS)[:, None]).astype(BF16)

    km, vm = _memkv(mem, norm_mem[l].reshape(1, D), w_mem_kv[l].astype(BF16))
    qc, qr, kvc, ksel, vsel, kwin, vwin, pool_in, qm, gates = _inproj(
        x2, norm_mix[l].reshape(1, D), w_a, cos_t, sin_t, S)
    o_nsa = _nsa(qc, qr, kvc, ksel, vsel, kwin, vwin, gates, w1, w2, w2t, pos, overlap_t, drop_bias, B, S)
    consts = [norm_mix[l].reshape(1, D), w_mg, wpool_bd.astype(BF16), pool_scale[l].reshape(1, POOL_WIDTH),
              w_up_nsa[l].astype(BF16), w_up_pool[l].astype(BF16), w_up_mem[l].astype(BF16), w_out[l].astype(BF16),
              norm_ffn[l].reshape(1, D), wr_hi, wr_lo, br, tri, lower]
    unit = T // sum(CHUNK_SHARES)
    chunk_tokens = [share * unit for share in CHUNK_SHARES]
    chunk_first = [sum(chunk_tokens[:c]) for c in range(len(chunk_tokens))]
    wgu, bgu = w_gate_up[l], b_gate_up[l].reshape(N_EXPERTS, 1, 2 * D_FF)
    wd, bd = w_down[l], b_down[l].reshape(N_EXPERTS, 1, D_MODEL)

    def route_chunk(c):
        Tc = chunk_tokens[c]
        n_tiles = (Tc * TOP_K) // TM_E + N_EXPERTS
        tile_start = jnp.arange(n_tiles, dtype=jnp.int32) * TM_E
        x1, hf, route, counts = _merge(x2, o_nsa, pool_in, qm, km, vm, consts, S, chunk_first[c] // TM_MERGE,
                                       Tc // TM_MERGE)
        counts = counts[:, 0].astype(jnp.int32)
        padded = ((counts + TM_E - 1) // TM_E) * TM_E
        ends = jnp.cumsum(padded)
        starts = ends - padded
        e_k = route[ROUTE_E:ROUTE_E + TOP_K].astype(jnp.int32)
        r_k = route[ROUTE_R:ROUTE_R + TOP_K].astype(jnp.int32)
        group_start = jnp.zeros_like(e_k)
        for e in range(N_EXPERTS):
            group_start = jnp.where(e_k == e, starts[e], group_start)
        dest = group_start + r_k
        tile_expert = jnp.minimum(jnp.sum(tile_start[:, None] >= ends[None, :], axis=1), N_EXPERTS - 1).astype(jnp.int32)
        n_used = (ends[-1] // TM_E).astype(jnp.int32).reshape(1)
        n_valid = jnp.clip((starts + counts)[tile_expert] - tile_start, 0, TM_E).astype(jnp.int32)
        xs = _sc_dispatch(hf, dest, n_tiles * TM_E)
        return x1, route, dest, (tile_expert, n_used, n_valid, xs)

    routed = [route_chunk(c) for c in range(len(chunk_tokens))]
    expert_out = [_moe(*moe_args, wgu, bgu, wd, bd) for _, _, _, moe_args in routed]
    out = None
    for c, ((x1, route, dest, _), ys) in enumerate(zip(routed, expert_out)):
        yg = _sc_gather_rows(ys, dest.reshape(-1)).reshape(TOP_K, chunk_tokens[c], PACKED)
        out = _final(x1, yg, route, norm_final.reshape(1, D), out, chunk_first[c] // TM_FINAL, T // TM_FINAL)
    return out.reshape(B, S, D)
```

```python
import functools

import jax
import jax.numpy as jnp
from jax import lax
from jax.experimental import pallas as pl
from jax.experimental.pallas import tpu as pltpu
from jax.experimental.pallas import tpu_sc as plsc

F32 = jnp.float32
BF16 = jnp.bfloat16
U32 = jnp.uint32

D_MODEL = 1024
HEAD_DIM = 64
NSA_HEADS = 8
NSA_GROUPS = 2
HEADS_PER_GROUP = NSA_HEADS // NSA_GROUPS
NSA_WIDTH = NSA_HEADS * HEAD_DIM
KV_WIDTH = NSA_GROUPS * HEAD_DIM
CMP_BLOCK = 32
CMP_STRIDE = 16
SEL_BLOCK = 64
SEL_TOPN = 8
FORCE_BONUS = 1000.0
WINDOW = 512
POOL_WINDOWS = (2, 4, 8, 16)
POOL_GROUP = 64
POOL_WIDTH = POOL_GROUP * len(POOL_WINDOWS)
POOL_HALO = 16
MEM_HEADS = 4
MEM_WIDTH = MEM_HEADS * HEAD_DIM
N_EXPERTS = 32
TOP_K = 4
D_FF = 1024
SWIGLU_LIMIT = 7.0
SWIGLU_ALPHA = 1.702
ROPE_THETA = 10000.0
EPS = 1e-5
NEG_INF = -1e30
TINY = 1e-30
QK_SCALE = HEAD_DIM ** -0.5
LOG2_E = 1.4426950408889634
NSA_Q_SCALE = QK_SCALE * LOG2_E

LANES = 128
GATE_PAD = LANES
PACKED = D_MODEL // 2

TM_IN = 1024
TQ = 256
CKS = 512
WIN_CHUNKS = (256, 256, 256)
TM_MERGE = 1024
TM_E = 512
TM_FINAL = 512
CHUNK_SHARES = (3, 1)
SC_CORES = 2
SC_SUBCORES = 16
SC_WORKERS = SC_CORES * SC_SUBCORES
SC_ROWS = 64
VMEM_LIMIT = 56 * 1024 * 1024


def _rms(x, g):
    return x * lax.rsqrt(jnp.mean(x * x, axis=-1, keepdims=True) + EPS) * g


def _sigmoid(x):
    return 0.5 * jnp.tanh(0.5 * x) + 0.5


def _dot(a, b):
    return jnp.dot(a, b, preferred_element_type=F32)


def _pack_bf16_pairs(v):
    n = v.shape[1] // 2
    r = v.astype(BF16).astype(F32)
    lo = pltpu.bitcast(r[:, :n], U32) >> 16
    hi = pltpu.bitcast(r[:, n:], U32) & jnp.uint32(0xFFFF0000)
    return lo | hi


def _unpack_bf16_pairs(w):
    lo = pltpu.bitcast(w << 16, F32)
    hi = pltpu.bitcast(w & jnp.uint32(0xFFFF0000), F32)
    return jnp.concatenate([lo, hi], axis=1)


def _dot_nt(a, b):
    return lax.dot_general(a, b, (((1,), (1,)), ((), ())), preferred_element_type=F32)


def _dot_tn(a, b):
    return lax.dot_general(a, b, (((0,), (0,)), ((), ())), preferred_element_type=F32)


def _memkv_kernel(mem_ref, g_ref, w_ref, k_ref, v_ref):
    m = _rms(mem_ref[0], g_ref[...]).astype(BF16)
    kv = _dot(m, w_ref[...])
    for h in range(MEM_HEADS):
        k_ref[0, h] = kv[:, h * HEAD_DIM:(h + 1) * HEAD_DIM].astype(BF16)
        v_ref[0, h] = kv[:, MEM_WIDTH + h * HEAD_DIM:MEM_WIDTH + (h + 1) * HEAD_DIM].astype(BF16)


def _memkv(mem, g, w):
    B, M, D = mem.shape
    return pl.pallas_call(
        _memkv_kernel,
        grid=(B,),
        in_specs=[pl.BlockSpec((1, M, D), lambda b: (b, 0, 0)),
                  pl.BlockSpec((1, D), lambda b: (0, 0)),
                  pl.BlockSpec((D, 2 * MEM_WIDTH), lambda b: (0, 0))],
        out_specs=[pl.BlockSpec((1, MEM_HEADS, M, HEAD_DIM), lambda b: (b, 0, 0, 0)),
                   pl.BlockSpec((1, MEM_HEADS, M, HEAD_DIM), lambda b: (b, 0, 0, 0))],
        out_shape=[jax.ShapeDtypeStruct((B, MEM_HEADS, M, HEAD_DIM), BF16),
                   jax.ShapeDtypeStruct((B, MEM_HEADS, M, HEAD_DIM), BF16)],
        compiler_params=pltpu.CompilerParams(dimension_semantics=("arbitrary",), vmem_limit_bytes=VMEM_LIMIT),
        name="memkv",
    )(mem, g, w)


IN_COLS = NSA_WIDTH + 6 * KV_WIDTH + POOL_WIDTH + MEM_WIDTH + GATE_PAD
CMP_COLS = CMP_STRIDE * 2 * KV_WIDTH


def _inproj_kernel(x_ref, g_ref, w_ref, cos_ref, sin_ref,
                   qc_ref, qr_ref, kvc_ref, ksel_ref, vsel_ref, kwin_ref, vwin_ref, pool_ref, qm_ref, gate_ref,
                   kv_s):
    h = _rms(x_ref[...], g_ref[...]).astype(BF16)
    p = _dot(h, w_ref[...])
    cos = cos_ref[...]
    sin = sin_ref[...]
    lane = lax.broadcasted_iota(jnp.int32, cos.shape, 1)
    first_half = (lane % HEAD_DIM) < (HEAD_DIM // 2)

    def rope(c):
        partner = jnp.where(first_half, pltpu.roll(c, LANES - HEAD_DIM // 2, 1), pltpu.roll(c, HEAD_DIM // 2, 1))
        return c * cos + partner * sin

    def halves(c):
        return c[:, :HEAD_DIM], c[:, HEAD_DIM:]

    for j in range(NSA_WIDTH // LANES):
        c = p[:, j * LANES:(j + 1) * LANES]
        r = rope(c)
        for hh, (cc, rr) in enumerate(zip(halves(c), halves(r))):
            qc_ref[2 * j + hh] = (cc * NSA_Q_SCALE).astype(BF16)
            qr_ref[2 * j + hh] = (rr * NSA_Q_SCALE).astype(BF16)
    o = NSA_WIDTH
    n_rows = kv_s.shape[1] // CMP_STRIDE
    for c in range(2):
        kv_s[c] = p[:, o + c * KV_WIDTH:o + (c + 1) * KV_WIDTH]
    for l_ in range(CMP_STRIDE):
        for c in range(2):
            col = (2 * l_ + c) * KV_WIDTH
            kvc_ref[:, col:col + KV_WIDTH] = kv_s[c, pl.ds(l_, n_rows, stride=CMP_STRIDE), :]
    o += 2 * KV_WIDTH
    for ref, rot in ((ksel_ref, True), (vsel_ref, False), (kwin_ref, True), (vwin_ref, False)):
        c = p[:, o:o + KV_WIDTH]
        if rot:
            c = rope(c)
        for g, cc in enumerate(halves(c)):
            ref[g] = cc.astype(BF16)
        o += KV_WIDTH
    pool_ref[...] = p[:, o:o + POOL_WIDTH]
    o += POOL_WIDTH
    for hh in range(MEM_HEADS):
        qm_ref[hh] = (p[:, o + hh * HEAD_DIM:o + (hh + 1) * HEAD_DIM] * QK_SCALE).astype(BF16)
    o += MEM_WIDTH
    gate_ref[...] = _sigmoid(p[:, o:o + GATE_PAD])


def _inproj(x2, g, w_a, cos_t, sin_t, S):
    T = x2.shape[0]
    tm = TM_IN
    n_s = S // tm
    head_spec = lambda n: pl.BlockSpec((n, tm, HEAD_DIM), lambda i: (0, i, 0))
    row_spec = lambda w: pl.BlockSpec((tm, w), lambda i: (i, 0))
    return pl.pallas_call(
        _inproj_kernel,
        grid=(T // tm,),
        in_specs=[row_spec(D_MODEL),
                  pl.BlockSpec((1, D_MODEL), lambda i: (0, 0)),
                  pl.BlockSpec((D_MODEL, IN_COLS), lambda i: (0, 0)),
                  pl.BlockSpec((tm, LANES), lambda i: (i % n_s, 0)),
                  pl.BlockSpec((tm, LANES), lambda i: (i % n_s, 0))],
        out_specs=[head_spec(NSA_HEADS), head_spec(NSA_HEADS),
                   pl.BlockSpec((tm // CMP_STRIDE, CMP_COLS), lambda i: (i, 0)),
                   head_spec(NSA_GROUPS), head_spec(NSA_GROUPS), head_spec(NSA_GROUPS), head_spec(NSA_GROUPS),
                   row_spec(POOL_WIDTH), head_spec(MEM_HEADS), row_spec(GATE_PAD)],
        out_shape=[jax.ShapeDtypeStruct((NSA_HEADS, T, HEAD_DIM), BF16),
                   jax.ShapeDtypeStruct((NSA_HEADS, T, HEAD_DIM), BF16),
                   jax.ShapeDtypeStruct((T // CMP_STRIDE, CMP_COLS), F32),
                   jax.ShapeDtypeStruct((NSA_GROUPS, T, HEAD_DIM), BF16),
                   jax.ShapeDtypeStruct((NSA_GROUPS, T, HEAD_DIM), BF16),
                   jax.ShapeDtypeStruct((NSA_GROUPS, T, HEAD_DIM), BF16),
                   jax.ShapeDtypeStruct((NSA_GROUPS, T, HEAD_DIM), BF16),
                   jax.ShapeDtypeStruct((T, POOL_WIDTH), F32),
                   jax.ShapeDtypeStruct((MEM_HEADS, T, HEAD_DIM), BF16),
                   jax.ShapeDtypeStruct((T, GATE_PAD), F32)],
        scratch_shapes=[pltpu.VMEM((2, tm, KV_WIDTH), F32)],
        compiler_params=pltpu.CompilerParams(dimension_semantics=("arbitrary",), vmem_limit_bytes=VMEM_LIMIT),
        name="inproj",
    )(x2, g, w_a, cos_t, sin_t)


N_CMP_PAD = 128


def _nsa_kernel(qc_ref, qr_ref, kvc_ref, ksel_ref, vsel_ref, kwin_ref, vwin_ref, gate_ref,
                w1_ref, w2_ref, w2t_ref, pos_ref, ovl_ref, o_ref, kc_s, vct_s, drop_s, *, S):
    i = pl.program_id(1)
    tq = TQ
    hpg = HEADS_PER_GROUP
    n_sel = S // SEL_BLOCK
    hq = hpg * tq

    @pl.when(i == 0)
    def _compress():
        a = kvc_ref[...].astype(BF16)
        p1 = _dot(a, w1_ref[0])
        p2 = _dot(a, w1_ref[1])
        posterm = (_dot(pos_ref[0], w1_ref[0]) + _dot(pos_ref[1], w1_ref[1]))[0:1]
        hid = p1 + pltpu.roll(p2, N_CMP_PAD - 1, 0) + posterm
        hid = (hid * _sigmoid(hid)).astype(BF16)
        for g in range(NSA_GROUPS):
            kc_s[g] = _dot(hid[:, g * HEAD_DIM:(g + 1) * HEAD_DIM], w2_ref[0]).astype(BF16)
            vct_s[g] = _dot_nt(w2t_ref[1], hid[:, KV_WIDTH + g * HEAD_DIM:KV_WIDTH + (g + 1) * HEAD_DIM]).astype(BF16)

    q0 = i * tq
    t_lane = q0 + lax.broadcasted_iota(jnp.int32, (1, tq), 1)

    def key_pos(start, n):
        return start + lax.broadcasted_iota(jnp.int32, (n, 1), 0)

    cmp_valid = (key_pos(0, N_CMP_PAD) * CMP_STRIDE + CMP_BLOCK - 1) <= t_lane
    jrow = lax.broadcasted_iota(jnp.int32, (n_sel, tq), 0)
    cur = t_lane // SEL_BLOCK
    sel_valid = jrow * SEL_BLOCK <= t_lane
    forced = (jrow == 0) | (jrow == cur) | (jrow == cur - 1)

    cd = q0 // CKS
    causal_bias = jnp.where(key_pos(cd * CKS, CKS) <= t_lane, 0.0, NEG_INF)
    win_chunks = []
    hi = q0 + tq
    for n in WIN_CHUNKS:
        lo = hi - n
        start = pl.multiple_of(jnp.maximum(lo, 0), LANES)
        kp = key_pos(start, n)
        diff = t_lane - kp
        win_chunks.append((start, n, jnp.where((diff >= 0) & (diff < WINDOW) & (kp < hi), 0.0, NEG_INF)))
        hi = lo

    def attend(jobs):
        scores = [_dot_nt(k, q_all) for q_all, k, _, _ in jobs]
        heads = [slice(hh * tq, (hh + 1) * tq) for hh in range(hpg)]
        maxes = [jnp.concatenate([jnp.max(s_all[:, sl] + bias, axis=0, keepdims=True) for sl in heads], axis=1)
                 for (_, _, _, bias), s_all in zip(jobs, scores)]
        soft = []
        for (_, _, _, bias), s_all, m in zip(jobs, scores, maxes):
            ps = [jnp.exp2(s_all[:, sl] + bias - m[:, sl]) for sl in heads]
            l = jnp.concatenate([jnp.sum(p, axis=0, keepdims=True) for p in ps], axis=1)
            soft.append((l, jnp.concatenate([p.astype(BF16) for p in ps], axis=1)))
        return [(m, l, _dot_tn(v, p)) for (_, _, v, _), m, (l, p) in zip(jobs, maxes, soft)]

    def merge(a, b):
        m = jnp.maximum(a[0], b[0])
        wa = jnp.exp2(a[0] - m)
        wb = jnp.exp2(b[0] - m)
        return m, wa * a[1] + wb * b[1], wa * a[2] + wb * b[2]

    def select_blocks(g):
        q_cmp = qc_ref[g * hpg:(g + 1) * hpg].reshape(hq, HEAD_DIM)
        s_all = _dot_nt(kc_s[g], q_cmp)
        p_grp = jnp.zeros((N_CMP_PAD, tq), F32)
        ps = []
        for hh in range(hpg):
            sl = slice(hh * tq, (hh + 1) * tq)
            s = jnp.where(cmp_valid, s_all[:, sl], NEG_INF)
            m = jnp.max(s, axis=0, keepdims=True)
            e = jnp.where(cmp_valid, jnp.exp2(s - m), 0.0)
            p = e * (1.0 / jnp.maximum(jnp.sum(e, axis=0, keepdims=True), TINY))
            p_grp = p_grp + p
            ps.append(p.astype(BF16))
        o_cmp = _dot(vct_s[g], jnp.concatenate(ps, axis=1))

        ovl = ovl_ref[...]
        p_hi = p_grp.astype(BF16)
        r1 = p_grp - p_hi.astype(F32)
        p_mid = r1.astype(BF16)
        p_lo = (r1 - p_mid.astype(F32)).astype(BF16)
        score = _dot(ovl, p_hi) + _dot(ovl, p_mid) + _dot(ovl, p_lo)
        score = jnp.where(sel_valid, score + jnp.where(forced, FORCE_BONUS, 0.0), -1.0)
        rank = jnp.zeros((n_sel, tq), F32)
        for jp in range(n_sel):
            other = score[jp:jp + 1, :]
            beats = (other > score) | ((other == score) & (jrow > jp))
            rank = rank + beats.astype(F32)
        drop_s[g] = jnp.where(rank >= SEL_TOPN, NEG_INF, 0.0)
        return o_cmp

    groups = range(NSA_GROUPS)
    cmp_out = [select_blocks(g) for g in groups]
    q_rot = [qr_ref[g * hpg:(g + 1) * hpg].reshape(hq, HEAD_DIM) for g in groups]
    blocks_per_chunk = CKS // SEL_BLOCK

    def selected_job(g, c, extra_bias):
        k0 = pl.multiple_of(c * CKS, CKS)
        rows = drop_s[g, pl.ds(pl.multiple_of(c * blocks_per_chunk, blocks_per_chunk), blocks_per_chunk), :]
        bias = jnp.broadcast_to(rows[:, None, :], (blocks_per_chunk, SEL_BLOCK, tq)).reshape(CKS, tq)
        if extra_bias is not None:
            bias = bias + extra_bias
        return q_rot[g], ksel_ref[g, pl.ds(k0, CKS), :], vsel_ref[g, pl.ds(k0, CKS), :], bias

    def window_job(g, chunk):
        start, n, bias = chunk
        return q_rot[g], kwin_ref[g, pl.ds(start, n), :], vwin_ref[g, pl.ds(start, n), :], bias

    stats = attend([selected_job(g, cd, causal_bias) for g in groups]
                   + [window_job(g, chunk) for chunk in win_chunks for g in groups])
    sel_state = tuple(stats[:NSA_GROUPS])
    win_state = list(stats[NSA_GROUPS:2 * NSA_GROUPS])
    for j in range(1, len(win_chunks)):
        for g in groups:
            win_state[g] = merge(win_state[g], stats[NSA_GROUPS * (1 + j) + g])

    def sel_body(it, states):
        new = attend([selected_job(g, cd - 1 - it, None) for g in groups])
        return tuple(merge(states[g], new[g]) for g in groups)

    sel_state = lax.fori_loop(0, cd, sel_body, sel_state)

    gates_t = gate_ref[...].T
    out_rows = []
    for g in groups:
        o_cmp = cmp_out[g]
        o_sel = sel_state[g][2] * (1.0 / sel_state[g][1])
        o_win = win_state[g][2] * (1.0 / win_state[g][1])
        for hh in range(hpg):
            h = g * hpg + hh
            sl = slice(hh * tq, (hh + 1) * tq)
            out_rows.append(gates_t[3 * h:3 * h + 1] * o_cmp[:, sl] + gates_t[3 * h + 1:3 * h + 2] * o_sel[:, sl]
                            + gates_t[3 * h + 2:3 * h + 3] * o_win[:, sl])
    o_ref[...] = jnp.concatenate(out_rows, axis=0).T.astype(BF16)


def _nsa(qc, qr, kvc, ksel, vsel_t, kwin, vwin_t, gates, w1, w2, w2t, pos, ovl, B, S):
    T = B * S
    tq = TQ
    nq = S // tq
    q_spec = pl.BlockSpec((NSA_HEADS, tq, HEAD_DIM), lambda b, i: (0, b * nq + i, 0))
    k_spec = pl.BlockSpec((NSA_GROUPS, S, HEAD_DIM), lambda b, i: (0, b, 0))
    full = lambda a: pl.BlockSpec(a.shape, lambda b, i: (0,) * a.ndim)
    return pl.pallas_call(
        functools.partial(_nsa_kernel, S=S),
        grid=(B, nq),
        in_specs=[q_spec, q_spec,
                  pl.BlockSpec((N_CMP_PAD, CMP_COLS), lambda b, i: (b, 0)),
                  k_spec, k_spec, k_spec, k_spec,
                  pl.BlockSpec((tq, GATE_PAD), lambda b, i: (b * nq + i, 0)),
                  full(w1), full(w2), full(w2t), full(pos), full(ovl)],
        out_specs=pl.BlockSpec((tq, NSA_WIDTH), lambda b, i: (b * nq + i, 0)),
        out_shape=jax.ShapeDtypeStruct((T, NSA_WIDTH), BF16),
        scratch_shapes=[pltpu.VMEM((NSA_GROUPS, N_CMP_PAD, HEAD_DIM), BF16),
                        pltpu.VMEM((NSA_GROUPS, HEAD_DIM, N_CMP_PAD), BF16),
                        pltpu.VMEM((NSA_GROUPS, S // SEL_BLOCK, tq), F32)],
        compiler_params=pltpu.CompilerParams(dimension_semantics=("arbitrary", "arbitrary"),
                                             vmem_limit_bytes=VMEM_LIMIT),
        name="nsa",
    )(qc, qr, kvc, ksel, vsel_t, kwin, vwin_t, gates, w1, w2, w2t, pos, ovl)


ROUTE_E, ROUTE_R, ROUTE_W = 0, TOP_K, 2 * TOP_K
ROUTE_ROWS = 16


def _merge_kernel(x_ref, onsa_ref, pool_ref, prev_ref, qm_ref, km_ref, vm_ref,
                  gmix_ref, wmg_ref, wpool_ref, pscale_ref, wun_ref, wup_ref, wum_ref, wout_ref,
                  gffn_ref, wrh_ref, wrl_ref, br_ref, tri_ref, lower_ref,
                  x1_ref, hf_ref, route_ref, cnt_ref, omem_s, carry_s, *, S, tile0):
    i = pl.program_id(0) + tile0
    tm = TM_MERGE
    n_s = S // tm

    @pl.when(pl.program_id(0) == 0)
    def _init():
        carry_s[...] = jnp.zeros_like(carry_s)

    x = x_ref[...]
    h = _rms(x, gmix_ref[...]).astype(BF16)

    u = pool_ref[...]
    seq_tile = i % n_s
    prev = jnp.where(seq_tile == 0, 0.0, prev_ref[...])
    ext = jnp.concatenate([prev, u], axis=0)
    b2 = ext[1:] + ext[:-1]
    b4 = b2[2:] + b2[:-2]
    b8 = b4[4:] + b4[:-4]
    b16 = b8[8:] + b8[:-8]
    sums = (b2[POOL_HALO - 1:POOL_HALO - 1 + tm], b4[POOL_HALO - 3:POOL_HALO - 3 + tm],
            b8[POOL_HALO - 7:POOL_HALO - 7 + tm], b16[POOL_HALO - 15:POOL_HALO - 15 + tm])
    t_seq = seq_tile * tm + lax.broadcasted_iota(jnp.int32, (tm, 1), 0)
    lane_p = lax.broadcasted_iota(jnp.int32, (tm, POOL_WIDTH), 1)
    z = jnp.zeros((tm, POOL_WIDTH), F32)
    for gi, w in enumerate(POOL_WINDOWS):
        cnt = jnp.minimum(t_seq + 1, w).astype(F32)
        z = jnp.where(lane_p // POOL_GROUP == gi, sums[gi] / cnt, z)
    z = z - u
    o_pool = (_dot(z.astype(BF16), wpool_ref[...]) * pscale_ref[...]).astype(BF16)

    for hh in range(MEM_HEADS):
        s = _dot_nt(qm_ref[hh], km_ref[0, hh])
        m = jnp.max(s, axis=-1, keepdims=True)
        e = jnp.exp(s - m)
        p = e / jnp.sum(e, axis=-1, keepdims=True)
        omem_s[:, hh * HEAD_DIM:(hh + 1) * HEAD_DIM] = _dot(p.astype(BF16), vm_ref[0, hh]).astype(BF16)

    def gated(branch, w_ref, j):
        return _sigmoid(_dot(h, wmg_ref[:, j * D_MODEL:(j + 1) * D_MODEL])) * _dot(branch, w_ref[...])

    merged = gated(onsa_ref[...], wun_ref, 0) + gated(o_pool, wup_ref, 1) + gated(omem_s[...], wum_ref, 2)
    x1 = x + _dot(merged.astype(BF16), wout_ref[...])
    x1_ref[...] = x1
    hf = _rms(x1, gffn_ref[...])
    hf_ref[...] = _pack_bf16_pairs(hf)

    hf_hi = hf.astype(BF16)
    hf_lo = (hf - hf_hi.astype(F32)).astype(BF16)
    logits = (_dot_nt(wrh_ref[...], hf_hi) + _dot_nt(wrl_ref[...], hf_hi) + _dot_nt(wrh_ref[...], hf_lo)
              + br_ref[...])
    erow = lax.broadcasted_iota(jnp.int32, (N_EXPERTS, tm), 0)
    rank = jnp.zeros((N_EXPERTS, tm), F32)
    for jp in range(N_EXPERTS):
        other = logits[jp:jp + 1, :]
        beats = (other > logits) | ((other == logits) & (erow > jp))
        rank = rank + beats.astype(F32)
    chosen = rank < TOP_K
    m = jnp.max(logits, axis=0, keepdims=True)
    e = jnp.where(chosen, jnp.exp(logits - m), 0.0)
    comb = e * (1.0 / jnp.sum(e, axis=0, keepdims=True))

    chosen_b = chosen.astype(BF16)
    carry = carry_s[:, 0:1]
    in_expert = _dot(chosen_b, tri_ref[...]) + carry
    carry_new = carry + jnp.sum(chosen.astype(F32), axis=1, keepdims=True)
    carry_s[...] = jnp.broadcast_to(carry_new, carry_s.shape)
    cnt_ref[...] = jnp.broadcast_to(carry_new, cnt_ref.shape)

    before = _dot(lower_ref[...], chosen_b)
    erow_f = erow.astype(F32)
    fields = {ROUTE_E: erow_f, ROUTE_R: in_expert, ROUTE_W: comb}
    rows = [None] * ROUTE_ROWS
    for k in range(TOP_K):
        pick = chosen & (before == k)
        for base, val in fields.items():
            rows[base + k] = jnp.sum(jnp.where(pick, val, 0.0), axis=0, keepdims=True)
    zero_row = jnp.zeros((1, tm), F32)
    route_ref[...] = jnp.concatenate([zero_row if r is None else r for r in rows], axis=0)


def _merge(x2, onsa, pool_in, qm, km, vm, consts, S, tile0, n_tiles):
    tm = TM_MERGE
    Tc = n_tiles * tm
    n_s = S // tm
    M = km.shape[2]
    halo_per_tile = tm // POOL_HALO
    row = lambda w: pl.BlockSpec((tm, w), lambda i: (i + tile0, 0))
    out_row = lambda w: pl.BlockSpec((tm, w), lambda i: (i, 0))
    full = lambda a: pl.BlockSpec(a.shape, lambda i: (0,) * a.ndim)
    mem_spec = pl.BlockSpec((1, MEM_HEADS, M, HEAD_DIM), lambda i: ((i + tile0) // n_s, 0, 0, 0))
    return pl.pallas_call(
        functools.partial(_merge_kernel, S=S, tile0=tile0),
        grid=(n_tiles,),
        in_specs=[row(D_MODEL), row(NSA_WIDTH), row(POOL_WIDTH),
                  pl.BlockSpec((POOL_HALO, POOL_WIDTH),
                               lambda i: (jnp.maximum((i + tile0) * halo_per_tile - 1, 0), 0)),
                  pl.BlockSpec((MEM_HEADS, tm, HEAD_DIM), lambda i: (0, i + tile0, 0)),
                  mem_spec, mem_spec] + [full(c) for c in consts],
        out_specs=[out_row(D_MODEL), out_row(PACKED), pl.BlockSpec((ROUTE_ROWS, tm), lambda i: (0, i)),
                   pl.BlockSpec((N_EXPERTS, LANES), lambda i: (0, 0))],
        out_shape=[jax.ShapeDtypeStruct((Tc, D_MODEL), F32),
                   jax.ShapeDtypeStruct((Tc, PACKED), U32),
                   jax.ShapeDtypeStruct((ROUTE_ROWS, Tc), F32),
                   jax.ShapeDtypeStruct((N_EXPERTS, LANES), F32)],
        scratch_shapes=[pltpu.VMEM((tm, MEM_WIDTH), BF16), pltpu.VMEM((N_EXPERTS, LANES), F32)],
        compiler_params=pltpu.CompilerParams(dimension_semantics=("arbitrary",), vmem_limit_bytes=VMEM_LIMIT),
        name="merge",
    )(x2, onsa, pool_in, pool_in, qm, km, vm, *consts)


def _sc_mesh():
    return plsc.VectorSubcoreMesh(core_axis_name="c", subcore_axis_name="s")


def _sc_worker():
    return lax.axis_index("s") * SC_CORES + lax.axis_index("c")


def _sc_dispatch(hf, dest, n_rows):
    T, d = hf.shape
    per_worker = T // SC_WORKERS
    steps = per_worker // SC_ROWS

    @functools.partial(pl.kernel, mesh=_sc_mesh(), out_type=jax.ShapeDtypeStruct((n_rows, d), hf.dtype),
                       scratch_types=[pltpu.VMEM((TOP_K, SC_ROWS), jnp.int32), pltpu.VMEM((SC_ROWS, d), hf.dtype),
                                      pltpu.SemaphoreType.DMA])
    def dispatch(hf_hbm, dest_hbm, xs_hbm, idx_v, rows_v, sem):
        base = _sc_worker() * per_worker

        @pl.loop(0, steps)
        def _(j):
            rows = pl.ds(pl.multiple_of(base + j * SC_ROWS, SC_ROWS), SC_ROWS)
            pltpu.sync_copy(hf_hbm.at[rows], rows_v)
            for k in range(TOP_K):
                pltpu.sync_copy(dest_hbm.at[k, rows], idx_v.at[k])
            copies = [pltpu.make_async_copy(rows_v, xs_hbm.at[idx_v.at[k]], sem) for k in range(TOP_K)]
            for c in copies:
                c.start()
            for c in copies:
                c.wait()

    return dispatch(hf, dest)


def _moe_kernel(te_ref, nu_ref, nv_ref, xs_ref, wgu_ref, bgu_ref, wd_ref, bd_ref, ys_ref, wgu_s, wd_s):
    j = pl.program_id(0)
    used = j < nu_ref[0]
    new_expert = (j == 0) | (te_ref[j] != te_ref[jnp.maximum(j - 1, 0)])

    @pl.when(used & new_expert)
    def _cast_weights():
        wgu_s[...] = wgu_ref[0].astype(BF16)
        wd_s[...] = wd_ref[0].astype(BF16)

    @pl.when(used)
    def _compute():
        live = lax.broadcasted_iota(jnp.int32, (xs_ref.shape[0], 1), 0) < nv_ref[j]
        xb = _unpack_bf16_pairs(jnp.where(live, xs_ref[...], jnp.uint32(0))).astype(BF16)
        gu = _dot(xb, wgu_s[...]) + bgu_ref[0]
        gate = jnp.minimum(gu[:, :D_FF], SWIGLU_LIMIT)
        up = jnp.clip(gu[:, D_FF:], -SWIGLU_LIMIT, SWIGLU_LIMIT)
        act = (up + 1.0) * (gate * _sigmoid(SWIGLU_ALPHA * gate))
        ys_ref[...] = _pack_bf16_pairs(_dot(act.astype(BF16), wd_s[...]) + bd_ref[0])

    @pl.when(j >= nu_ref[0])
    def _unused():
        ys_ref[...] = jnp.zeros_like(ys_ref)


def _moe(tile_expert, n_used, n_valid, xs, wgu, bgu, wd, bd):
    P = xs.shape[0]
    tm = TM_E
    grid_spec = pltpu.PrefetchScalarGridSpec(
        num_scalar_prefetch=3,
        grid=(P // tm,),
        in_specs=[pl.BlockSpec((tm, PACKED), lambda j, te, nu, nv: (j, 0)),
                  pl.BlockSpec((1, D_MODEL, 2 * D_FF), lambda j, te, nu, nv: (te[j], 0, 0)),
                  pl.BlockSpec((1, 1, 2 * D_FF), lambda j, te, nu, nv: (te[j], 0, 0)),
                  pl.BlockSpec((1, D_FF, D_MODEL), lambda j, te, nu, nv: (te[j], 0, 0)),
                  pl.BlockSpec((1, 1, D_MODEL), lambda j, te, nu, nv: (te[j], 0, 0))],
        out_specs=pl.BlockSpec((tm, PACKED), lambda j, te, nu, nv: (j, 0)),
        scratch_shapes=[pltpu.VMEM((D_MODEL, 2 * D_FF), BF16), pltpu.VMEM((D_FF, D_MODEL), BF16)],
    )
    return pl.pallas_call(
        _moe_kernel,
        grid_spec=grid_spec,
        out_shape=jax.ShapeDtypeStruct((P, PACKED), U32),
        compiler_params=pltpu.CompilerParams(dimension_semantics=("arbitrary",), vmem_limit_bytes=VMEM_LIMIT),
        name="moe",
    )(tile_expert, n_used, n_valid, xs, wgu, bgu, wd, bd)


def _sc_gather_rows(table, idx):
    n, d = idx.shape[0], table.shape[1]
    per_worker = n // SC_WORKERS
    steps = per_worker // SC_ROWS
    assert steps % 2 == 0

    @functools.partial(pl.kernel, mesh=_sc_mesh(), out_type=jax.ShapeDtypeStruct((n, d), table.dtype),
                       scratch_types=[pltpu.VMEM((2, SC_ROWS), jnp.int32), pltpu.VMEM((2, SC_ROWS, d), table.dtype),
                                      pltpu.SemaphoreType.DMA((2,)), pltpu.SemaphoreType.DMA((2,))])
    def gather(table_hbm, idx_hbm, out_hbm, idx_v, rows_v, gsem, wsem):
        base = _sc_worker() * per_worker

        def rows_at(j):
            return pl.ds(pl.multiple_of(base + j * SC_ROWS, SC_ROWS), SC_ROWS)

        def gather_copy(slot):
            return pltpu.make_async_copy(table_hbm.at[idx_v.at[slot]], rows_v.at[slot], gsem.at[slot])

        def write_copy(j, slot):
            return pltpu.make_async_copy(rows_v.at[slot], out_hbm.at[rows_at(j)], wsem.at[slot])

        def fetch(j, slot):
            pltpu.sync_copy(idx_hbm.at[rows_at(j)], idx_v.at[slot])
            gather_copy(slot).start()

        fetch(0, 0)

        @pl.loop(0, steps, step=2)
        def _(j0):
            for slot in range(2):
                j = j0 + slot
                gather_copy(slot).wait()

                @pl.when(j >= 1)
                def _():
                    write_copy(j - 1, 1 - slot).wait()

                @pl.when(j + 1 < steps)
                def _():
                    fetch(j + 1, 1 - slot)

                write_copy(j, slot).start()

        write_copy(steps - 1, 1).wait()

    return gather(table, idx)


def _final_kernel(x1_ref, yg_ref, route_ref, g_ref, o_ref):
    tm = x1_ref.shape[0]
    route_t = jnp.concatenate([route_ref[...], jnp.zeros((LANES - ROUTE_ROWS, tm), F32)], axis=0).T
    acc = x1_ref[...]
    for k in range(TOP_K):
        acc = acc + route_t[:, ROUTE_W + k:ROUTE_W + k + 1] * _unpack_bf16_pairs(yg_ref[k])
    o_ref[...] = _rms(acc, g_ref[...])


def _final(x1, yg, route, g, out_prev, tile0, n_total):
    Tc = x1.shape[0]
    tm = TM_FINAL
    in_specs = [pl.BlockSpec((tm, D_MODEL), lambda i: (i, 0)),
                pl.BlockSpec((TOP_K, tm, PACKED), lambda i: (0, i, 0)),
                pl.BlockSpec((ROUTE_ROWS, tm), lambda i: (0, i)),
                pl.BlockSpec((1, D_MODEL), lambda i: (0, 0))]
    args = [x1, yg, route, g]
    kernel_fn, aliases = _final_kernel, {}
    if out_prev is not None:
        in_specs.append(pl.BlockSpec(memory_space=pl.ANY))
        args.append(out_prev)
        kernel_fn = lambda x1_ref, yg_ref, route_ref, g_ref, prev_ref, o_ref: _final_kernel(x1_ref, yg_ref, route_ref,
                                                                                            g_ref, o_ref)
        aliases = {len(args) - 1: 0}
    return pl.pallas_call(
        kernel_fn,
        grid=(Tc // tm,),
        in_specs=in_specs,
        out_specs=pl.BlockSpec((tm, D_MODEL), lambda i: (i + tile0, 0)),
        out_shape=jax.ShapeDtypeStruct((n_total * tm, D_MODEL), F32),
        input_output_aliases=aliases,
        compiler_params=pltpu.CompilerParams(dimension_semantics=("arbitrary",), vmem_limit_bytes=VMEM_LIMIT),
        name="final",
    )(*args)


def _rope_tables(S):
    half = HEAD_DIM // 2
    inv = ROPE_THETA ** (-jnp.arange(half, dtype=F32) / half)
    ang = jnp.arange(S, dtype=F32)[:, None] * inv[None, :]
    cos = jnp.tile(jnp.cos(ang), (1, LANES // half))
    sin = jnp.tile(jnp.concatenate([-jnp.sin(ang), jnp.sin(ang)], axis=1), (1, LANES // HEAD_DIM))
    return cos, sin


def _selection_constants(S):
    nc = (S - CMP_BLOCK) // CMP_STRIDE + 1
    n_sel = S // SEL_BLOCK
    j = jnp.arange(n_sel)[:, None]
    i = jnp.arange(N_CMP_PAD)[None, :]
    overlap_t = ((i * CMP_STRIDE <= j * SEL_BLOCK + SEL_BLOCK - 1)
                 & (i * CMP_STRIDE + CMP_BLOCK - 1 >= j * SEL_BLOCK) & (i < nc)).astype(BF16)
    return overlap_t


def kernel(x, mem, norm_mix, norm_mem, w_in, cmp_pos, cmp_w1, cmp_w2, w_pool, pool_scale, w_mem_kv, w_up_nsa,
           w_up_pool, w_up_mem, w_out, norm_ffn, w_router, b_router, w_gate_up, b_gate_up, w_down, b_down,
           norm_final):
    B, S, D = x.shape
    T = B * S
    assert D == D_MODEL and S % CKS == 0 and S // SEL_BLOCK == 32 and T % (sum(CHUNK_SHARES) * SC_WORKERS * SC_ROWS * 2) == 0
    l = 0
    x2 = x.reshape(T, D)

    w = w_in[l]
    o_gate = NSA_WIDTH + 6 * KV_WIDTH
    n_gate = 3 * NSA_HEADS
    o_pool = o_gate + n_gate
    o_qm = o_pool + POOL_WIDTH
    o_mg = o_qm + MEM_WIDTH
    w_a = jnp.concatenate([w[:, :o_gate], w[:, o_pool:o_mg], w[:, o_gate:o_pool],
                           jnp.zeros((D, GATE_PAD - n_gate), F32)], axis=1).astype(BF16)
    w_mg = w[:, o_mg:].astype(BF16)
    cos_t, sin_t = _rope_tables(S)
    overlap_t = _selection_constants(S)
    stream_kv = jnp.arange(2 * NSA_GROUPS) // NSA_GROUPS
    w1_half = cmp_w1[l].reshape(2, 2, CMP_STRIDE, HEAD_DIM, HEAD_DIM)[stream_kv]
    w1 = jnp.einsum("shlde,ts->hltdse", w1_half, jnp.eye(2 * NSA_GROUPS, dtype=F32))
    w1 = w1.reshape(2, CMP_COLS, 2 * KV_WIDTH).astype(BF16)
    w2 = cmp_w2[l].astype(BF16)
    w2t = jnp.swapaxes(w2, 1, 2)
    pos_half = cmp_pos[l].reshape(2, 2, CMP_STRIDE, HEAD_DIM)[stream_kv]
    pos = jnp.broadcast_to(pos_half.transpose(1, 2, 0, 3).reshape(2, 1, CMP_COLS), (2, 8, CMP_COLS)).astype(BF16)
    wpool_bd = jnp.zeros((POOL_WIDTH, POOL_WIDTH), F32)
    for gi in range(len(POOL_WINDOWS)):
        wpool_bd = wpool_bd.at[gi * POOL_GROUP:(gi + 1) * POOL_GROUP, gi * POOL_GROUP:(gi + 1) * POOL_GROUP].set(w_pool[l, gi])
    wr = w_router[l].T
    wr_hi = wr.astype(BF16)
    wr_lo = (wr - wr_hi.astype(F32)).astype(BF16)
    br = b_router[l].reshape(N_EXPERTS, 1)
    tri = (jnp.arange(TM_MERGE)[:, None] < jnp.arange(TM_MERGE)[None, :]).astype(BF16)
    lower = (jnp.arange(N_EXPERTS)[None, :] < jnp.arange(N_EXPERTS)[:, None]).astype(BF16)

    km, vm = _memkv(mem, norm_mem[l].reshape(1, D), w_mem_kv[l].astype(BF16))
    qc, qr, kvc, ksel, vsel, kwin, vwin, pool_in, qm, gates = _inproj(
        x2, norm_mix[l].reshape(1, D), w_a, cos_t, sin_t, S)
    o_nsa = _nsa(qc, qr, kvc, ksel, vsel, kwin, vwin, gates, w1, w2, w2t, pos, overlap_t, B, S)
    consts = [norm_mix[l].reshape(1, D), w_mg, wpool_bd.astype(BF16), pool_scale[l].reshape(1, POOL_WIDTH),
              w_up_nsa[l].astype(BF16), w_up_pool[l].astype(BF16), w_up_mem[l].astype(BF16), w_out[l].astype(BF16),
              norm_ffn[l].reshape(1, D), wr_hi, wr_lo, br, tri, lower]
    unit = T // sum(CHUNK_SHARES)
    chunk_tokens = [share * unit for share in CHUNK_SHARES]
    chunk_first = [sum(chunk_tokens[:c]) for c in range(len(chunk_tokens))]
    wgu, bgu = w_gate_up[l], b_gate_up[l].reshape(N_EXPERTS, 1, 2 * D_FF)
    wd, bd = w_down[l], b_down[l].reshape(N_EXPERTS, 1, D_MODEL)

    def route_chunk(c):
        Tc = chunk_tokens[c]
        n_tiles = (Tc * TOP_K) // TM_E + N_EXPERTS
        tile_start = jnp.arange(n_tiles, dtype=jnp.int32) * TM_E
        x1, hf, route, counts = _merge(x2, o_nsa, pool_in, qm, km, vm, consts, S, chunk_first[c] // TM_MERGE,
                                       Tc // TM_MERGE)
        counts = counts[:, 0].astype(jnp.int32)
        padded = ((counts + TM_E - 1) // TM_E) * TM_E
        ends = jnp.cumsum(padded)
        starts = ends - padded
        e_k = route[ROUTE_E:ROUTE_E + TOP_K].astype(jnp.int32)
        r_k = route[ROUTE_R:ROUTE_R + TOP_K].astype(jnp.int32)
        group_start = jnp.zeros_like(e_k)
        for e in range(N_EXPERTS):
            group_start = jnp.where(e_k == e, starts[e], group_start)
        dest = group_start + r_k
        tile_expert = jnp.minimum(jnp.sum(tile_start[:, None] >= ends[None, :], axis=1), N_EXPERTS - 1).astype(jnp.int32)
        n_used = (ends[-1] // TM_E).astype(jnp.int32).reshape(1)
        n_valid = jnp.clip((starts + counts)[tile_expert] - tile_start, 0, TM_E).astype(jnp.int32)
        xs = _sc_dispatch(hf, dest, n_tiles * TM_E)
        return x1, route, dest, (tile_expert, n_used, n_valid, xs)

    routed = [route_chunk(c) for c in range(len(chunk_tokens))]
    expert_out = [_moe(*moe_args, wgu, bgu, wd, bd) for _, _, _, moe_args in routed]
    out = None
    for c, ((x1, route, dest, _), ys) in enumerate(zip(routed, expert_out)):
        yg = _sc_gather_rows(ys, dest.reshape(-1)).reshape(TOP_K, chunk_tokens[c], PACKED)
        out = _final(x1, yg, route, norm_final.reshape(1, D), out, chunk_first[c] // TM_FINAL, T // TM_FINAL)
    return out.reshape(B, S, D)
```

```python
import functools

import jax
import jax.numpy as jnp
from jax import lax
from jax.experimental import pallas as pl
from jax.experimental.pallas import tpu as pltpu
from jax.experimental.pallas import tpu_sc as plsc

F32 = jnp.float32
BF16 = jnp.bfloat16
U32 = jnp.uint32

D_MODEL = 1024
HEAD_DIM = 64
NSA_HEADS = 8
NSA_GROUPS = 2
HEADS_PER_GROUP = NSA_HEADS // NSA_GROUPS
NSA_WIDTH = NSA_HEADS * HEAD_DIM
KV_WIDTH = NSA_GROUPS * HEAD_DIM
CMP_BLOCK = 32
CMP_STRIDE = 16
SEL_BLOCK = 64
SEL_TOPN = 8
FORCE_BONUS = 1000.0
WINDOW = 512
POOL_WINDOWS = (2, 4, 8, 16)
POOL_GROUP = 64
POOL_WIDTH = POOL_GROUP * len(POOL_WINDOWS)
POOL_HALO = 16
MEM_HEADS = 4
MEM_WIDTH = MEM_HEADS * HEAD_DIM
N_EXPERTS = 32
TOP_K = 4
D_FF = 1024
SWIGLU_LIMIT = 7.0
SWIGLU_ALPHA = 1.702
ROPE_THETA = 10000.0
EPS = 1e-5
NEG_INF = -1e30
TINY = 1e-30
QK_SCALE = HEAD_DIM ** -0.5
LOG2_E = 1.4426950408889634
NSA_Q_SCALE = QK_SCALE * LOG2_E

LANES = 128
GATE_PAD = LANES
PACKED = D_MODEL // 2

TM_IN = 1024
TQ = 256
CKS = 512
WIN_CHUNKS = (256, 256, 256)
TM_MERGE = 1024
TM_E = 512
TM_FINAL = 512
CHUNK_SHARES = (3, 1)
SC_CORES = 2
SC_SUBCORES = 16
SC_WORKERS = SC_CORES * SC_SUBCORES
SC_ROWS = 64
VMEM_LIMIT = 56 * 1024 * 1024


def _rms(x, g):
    return x * lax.rsqrt(jnp.mean(x * x, axis=-1, keepdims=True) + EPS) * g


def _sigmoid(x):
    return 0.5 * jnp.tanh(0.5 * x) + 0.5


def _dot(a, b):
    return jnp.dot(a, b, preferred_element_type=F32)


def _pack_bf16_pairs(v):
    n = v.shape[1] // 2
    r = v.astype(BF16).astype(F32)
    lo = pltpu.bitcast(r[:, :n], U32) >> 16
    hi = pltpu.bitcast(r[:, n:], U32) & jnp.uint32(0xFFFF0000)
    return lo | hi


def _unpack_bf16_pairs(w):
    lo = pltpu.bitcast(w << 16, F32)
    hi = pltpu.bitcast(w & jnp.uint32(0xFFFF0000), F32)
    return jnp.concatenate([lo, hi], axis=1)


def _dot_nt(a, b):
    return lax.dot_general(a, b, (((1,), (1,)), ((), ())), preferred_element_type=F32)


def _dot_tn(a, b):
    return lax.dot_general(a, b, (((0,), (0,)), ((), ())), preferred_element_type=F32)


def _memkv_kernel(mem_ref, g_ref, w_ref, k_ref, v_ref):
    m = _rms(mem_ref[0], g_ref[...]).astype(BF16)
    kv = _dot(m, w_ref[...])
    for h in range(MEM_HEADS):
        k_ref[0, h] = kv[:, h * HEAD_DIM:(h + 1) * HEAD_DIM].astype(BF16)
        v_ref[0, h] = kv[:, MEM_WIDTH + h * HEAD_DIM:MEM_WIDTH + (h + 1) * HEAD_DIM].astype(BF16)


def _memkv(mem, g, w):
    B, M, D = mem.shape
    return pl.pallas_call(
        _memkv_kernel,
        grid=(B,),
        in_specs=[pl.BlockSpec((1, M, D), lambda b: (b, 0, 0)),
                  pl.BlockSpec((1, D), lambda b: (0, 0)),
                  pl.BlockSpec((D, 2 * MEM_WIDTH), lambda b: (0, 0))],
        out_specs=[pl.BlockSpec((1, MEM_HEADS, M, HEAD_DIM), lambda b: (b, 0, 0, 0)),
                   pl.BlockSpec((1, MEM_HEADS, M, HEAD_DIM), lambda b: (b, 0, 0, 0))],
        out_shape=[jax.ShapeDtypeStruct((B, MEM_HEADS, M, HEAD_DIM), BF16),
                   jax.ShapeDtypeStruct((B, MEM_HEADS, M, HEAD_DIM), BF16)],
        compiler_params=pltpu.CompilerParams(dimension_semantics=("arbitrary",), vmem_limit_bytes=VMEM_LIMIT),
        name="memkv",
    )(mem, g, w)


IN_COLS = NSA_WIDTH + 6 * KV_WIDTH + POOL_WIDTH + MEM_WIDTH + GATE_PAD
CMP_COLS = CMP_STRIDE * 2 * KV_WIDTH


def _inproj_kernel(x_ref, g_ref, w_ref, cos_ref, sin_ref,
                   qc_ref, qr_ref, kvc_ref, ksel_ref, vsel_ref, kwin_ref, vwin_ref, pool_ref, qm_ref, gate_ref,
                   kv_s):
    h = _rms(x_ref[...], g_ref[...]).astype(BF16)
    p = _dot(h, w_ref[...])
    cos = cos_ref[...]
    sin = sin_ref[...]
    lane = lax.broadcasted_iota(jnp.int32, cos.shape, 1)
    first_half = (lane % HEAD_DIM) < (HEAD_DIM // 2)

    def rope(c):
        partner = jnp.where(first_half, pltpu.roll(c, LANES - HEAD_DIM // 2, 1), pltpu.roll(c, HEAD_DIM // 2, 1))
        return c * cos + partner * sin

    def halves(c):
        return c[:, :HEAD_DIM], c[:, HEAD_DIM:]

    for j in range(NSA_WIDTH // LANES):
        c = p[:, j * LANES:(j + 1) * LANES]
        r = rope(c)
        for hh, (cc, rr) in enumerate(zip(halves(c), halves(r))):
            qc_ref[2 * j + hh] = (cc * NSA_Q_SCALE).astype(BF16)
            qr_ref[2 * j + hh] = (rr * NSA_Q_SCALE).astype(BF16)
    o = NSA_WIDTH
    n_rows = kv_s.shape[1] // CMP_STRIDE
    for c in range(2):
        kv_s[c] = p[:, o + c * KV_WIDTH:o + (c + 1) * KV_WIDTH]
    for l_ in range(CMP_STRIDE):
        for c in range(2):
            col = (2 * l_ + c) * KV_WIDTH
            kvc_ref[:, col:col + KV_WIDTH] = kv_s[c, pl.ds(l_, n_rows, stride=CMP_STRIDE), :]
    o += 2 * KV_WIDTH
    for ref, rot in ((ksel_ref, True), (vsel_ref, False), (kwin_ref, True), (vwin_ref, False)):
        c = p[:, o:o + KV_WIDTH]
        if rot:
            c = rope(c)
        for g, cc in enumerate(halves(c)):
            ref[g] = cc.astype(BF16)
        o += KV_WIDTH
    pool_ref[...] = p[:, o:o + POOL_WIDTH]
    o += POOL_WIDTH
    for hh in range(MEM_HEADS):
        qm_ref[hh] = (p[:, o + hh * HEAD_DIM:o + (hh + 1) * HEAD_DIM] * QK_SCALE).astype(BF16)
    o += MEM_WIDTH
    gate_ref[...] = _sigmoid(p[:, o:o + GATE_PAD])


def _inproj(x2, g, w_a, cos_t, sin_t, S):
    T = x2.shape[0]
    tm = TM_IN
    n_s = S // tm
    head_spec = lambda n: pl.BlockSpec((n, tm, HEAD_DIM), lambda i: (0, i, 0))
    row_spec = lambda w: pl.BlockSpec((tm, w), lambda i: (i, 0))
    return pl.pallas_call(
        _inproj_kernel,
        grid=(T // tm,),
        in_specs=[row_spec(D_MODEL),
                  pl.BlockSpec((1, D_MODEL), lambda i: (0, 0)),
                  pl.BlockSpec((D_MODEL, IN_COLS), lambda i: (0, 0)),
                  pl.BlockSpec((tm, LANES), lambda i: (i % n_s, 0)),
                  pl.BlockSpec((tm, LANES), lambda i: (i % n_s, 0))],
        out_specs=[head_spec(NSA_HEADS), head_spec(NSA_HEADS),
                   pl.BlockSpec((tm // CMP_STRIDE, CMP_COLS), lambda i: (i, 0)),
                   head_spec(NSA_GROUPS), head_spec(NSA_GROUPS), head_spec(NSA_GROUPS), head_spec(NSA_GROUPS),
                   row_spec(POOL_WIDTH), head_spec(MEM_HEADS), row_spec(GATE_PAD)],
        out_shape=[jax.ShapeDtypeStruct((NSA_HEADS, T, HEAD_DIM), BF16),
                   jax.ShapeDtypeStruct((NSA_HEADS, T, HEAD_DIM), BF16),
                   jax.ShapeDtypeStruct((T // CMP_STRIDE, CMP_COLS), F32),
                   jax.ShapeDtypeStruct((NSA_GROUPS, T, HEAD_DIM), BF16),
                   jax.ShapeDtypeStruct((NSA_GROUPS, T, HEAD_DIM), BF16),
                   jax.ShapeDtypeStruct((NSA_GROUPS, T, HEAD_DIM), BF16),
                   jax.ShapeDtypeStruct((NSA_GROUPS, T, HEAD_DIM), BF16),
                   jax.ShapeDtypeStruct((T, POOL_WIDTH), F32),
                   jax.ShapeDtypeStruct((MEM_HEADS, T, HEAD_DIM), BF16),
                   jax.ShapeDtypeStruct((T, GATE_PAD), F32)],
        scratch_shapes=[pltpu.VMEM((2, tm, KV_WIDTH), F32)],
        compiler_params=pltpu.CompilerParams(dimension_semantics=("arbitrary",), vmem_limit_bytes=VMEM_LIMIT),
        name="inproj",
    )(x2, g, w_a, cos_t, sin_t)


N_CMP_PAD = 128


def _nsa_kernel(qc_ref, qr_ref, kvc_ref, ksel_ref, vsel_ref, kwin_ref, vwin_ref, gate_ref,
                w1_ref, w2_ref, w2t_ref, pos_ref, ovl_ref, o_ref, kc_s, vct_s, drop_s, *, S):
    i = pl.program_id(1)
    tq = TQ
    hpg = HEADS_PER_GROUP
    n_sel = S // SEL_BLOCK
    hq = hpg * tq

    @pl.when(i == 0)
    def _compress():
        a = kvc_ref[...].astype(BF16)
        p1 = _dot(a, w1_ref[0])
        p2 = _dot(a, w1_ref[1])
        posterm = (_dot(pos_ref[0], w1_ref[0]) + _dot(pos_ref[1], w1_ref[1]))[0:1]
        hid = p1 + pltpu.roll(p2, N_CMP_PAD - 1, 0) + posterm
        hid = (hid * _sigmoid(hid)).astype(BF16)
        for g in range(NSA_GROUPS):
            kc_s[g] = _dot(hid[:, g * HEAD_DIM:(g + 1) * HEAD_DIM], w2_ref[0]).astype(BF16)
            vct_s[g] = _dot_nt(w2t_ref[1], hid[:, KV_WIDTH + g * HEAD_DIM:KV_WIDTH + (g + 1) * HEAD_DIM]).astype(BF16)

    q0 = i * tq
    t_lane = q0 + lax.broadcasted_iota(jnp.int32, (1, tq), 1)

    def key_pos(start, n):
        return start + lax.broadcasted_iota(jnp.int32, (n, 1), 0)

    cmp_valid = (key_pos(0, N_CMP_PAD) * CMP_STRIDE + CMP_BLOCK - 1) <= t_lane
    jrow = lax.broadcasted_iota(jnp.int32, (n_sel, tq), 0)
    cur = t_lane // SEL_BLOCK
    sel_valid = jrow * SEL_BLOCK <= t_lane
    forced = (jrow == 0) | (jrow == cur) | (jrow == cur - 1)

    cd = q0 // CKS
    causal_bias = jnp.where(key_pos(cd * CKS, CKS) <= t_lane, 0.0, NEG_INF)
    win_chunks = []
    hi = q0 + tq
    for n in WIN_CHUNKS:
        lo = hi - n
        start = pl.multiple_of(jnp.maximum(lo, 0), LANES)
        kp = key_pos(start, n)
        diff = t_lane - kp
        win_chunks.append((start, n, jnp.where((diff >= 0) & (diff < WINDOW) & (kp < hi), 0.0, NEG_INF)))
        hi = lo

    def attend(jobs):
        scores = [_dot_nt(k, q_all) for q_all, k, _, _ in jobs]
        heads = [slice(hh * tq, (hh + 1) * tq) for hh in range(hpg)]
        maxes = [jnp.concatenate([jnp.max(s_all[:, sl] + bias, axis=0, keepdims=True) for sl in heads], axis=1)
                 for (_, _, _, bias), s_all in zip(jobs, scores)]
        soft = []
        for (_, _, _, bias), s_all, m in zip(jobs, scores, maxes):
            ps = [jnp.exp2(s_all[:, sl] + bias - m[:, sl]) for sl in heads]
            l = jnp.concatenate([jnp.sum(p, axis=0, keepdims=True) for p in ps], axis=1)
            soft.append((l, jnp.concatenate([p.astype(BF16) for p in ps], axis=1)))
        return [(m, l, _dot_tn(v, p)) for (_, _, v, _), m, (l, p) in zip(jobs, maxes, soft)]

    def merge(a, b):
        m = jnp.maximum(a[0], b[0])
        wa = jnp.exp2(a[0] - m)
        wb = jnp.exp2(b[0] - m)
        return m, wa * a[1] + wb * b[1], wa * a[2] + wb * b[2]

    def select_blocks(g):
        q_cmp = qc_ref[g * hpg:(g + 1) * hpg].reshape(hq, HEAD_DIM)
        s_all = _dot_nt(kc_s[g], q_cmp)
        p_grp = jnp.zeros((N_CMP_PAD, tq), F32)
        ps = []
        for hh in range(hpg):
            sl = slice(hh * tq, (hh + 1) * tq)
            s = jnp.where(cmp_valid, s_all[:, sl], NEG_INF)
            m = jnp.max(s, axis=0, keepdims=True)
            e = jnp.where(cmp_valid, jnp.exp2(s - m), 0.0)
            p = e * (1.0 / jnp.maximum(jnp.sum(e, axis=0, keepdims=True), TINY))
            p_grp = p_grp + p
            ps.append(p.astype(BF16))
        o_cmp = _dot(vct_s[g], jnp.concatenate(ps, axis=1))

        ovl = ovl_ref[...]
        p_hi = p_grp.astype(BF16)
        r1 = p_grp - p_hi.astype(F32)
        p_mid = r1.astype(BF16)
        p_lo = (r1 - p_mid.astype(F32)).astype(BF16)
        score = _dot(ovl, p_hi) + _dot(ovl, p_mid) + _dot(ovl, p_lo)
        score = jnp.where(sel_valid, score + jnp.where(forced, FORCE_BONUS, 0.0), -1.0)
        rank = jnp.zeros((n_sel, tq), F32)
        for jp in range(n_sel):
            other = score[jp:jp + 1, :]
            beats = (other > score) | ((other == score) & (jrow > jp))
            rank = rank + beats.astype(F32)
        drop_s[g] = jnp.where(rank >= SEL_TOPN, NEG_INF, 0.0)
        return o_cmp

    groups = range(NSA_GROUPS)
    cmp_out = [select_blocks(g) for g in groups]
    q_rot = [qr_ref[g * hpg:(g + 1) * hpg].reshape(hq, HEAD_DIM) for g in groups]
    blocks_per_chunk = CKS // SEL_BLOCK

    def selected_job(g, c, extra_bias):
        k0 = pl.multiple_of(c * CKS, CKS)
        rows = drop_s[g, pl.ds(pl.multiple_of(c * blocks_per_chunk, blocks_per_chunk), blocks_per_chunk), :]
        bias = jnp.broadcast_to(rows[:, None, :], (blocks_per_chunk, SEL_BLOCK, tq)).reshape(CKS, tq)
        if extra_bias is not None:
            bias = bias + extra_bias
        return q_rot[g], ksel_ref[g, pl.ds(k0, CKS), :], vsel_ref[g, pl.ds(k0, CKS), :], bias

    def window_job(g, chunk):
        start, n, bias = chunk
        return q_rot[g], kwin_ref[g, pl.ds(start, n), :], vwin_ref[g, pl.ds(start, n), :], bias

    stats = attend([selected_job(g, cd, causal_bias) for g in groups]
                   + [window_job(g, chunk) for chunk in win_chunks for g in groups])
    sel_state = tuple(stats[:NSA_GROUPS])
    win_state = list(stats[NSA_GROUPS:2 * NSA_GROUPS])
    for j in range(1, len(win_chunks)):
        for g in groups:
            win_state[g] = merge(win_state[g], stats[NSA_GROUPS * (1 + j) + g])

    def sel_body(it, states):
        new = attend([selected_job(g, cd - 1 - it, None) for g in groups])
        return tuple(merge(states[g], new[g]) for g in groups)

    sel_state = lax.fori_loop(0, cd, sel_body, sel_state)

    gates_t = gate_ref[...].T
    out_rows = []
    for g in groups:
        o_cmp = cmp_out[g]
        o_sel = sel_state[g][2] * (1.0 / sel_state[g][1])
        o_win = win_state[g][2] * (1.0 / win_state[g][1])
        for hh in range(hpg):
            h = g * hpg + hh
            sl = slice(hh * tq, (hh + 1) * tq)
            out_rows.append(gates_t[3 * h:3 * h + 1] * o_cmp[:, sl] + gates_t[3 * h + 1:3 * h + 2] * o_sel[:, sl]
                            + gates_t[3 * h + 2:3 * h + 3] * o_win[:, sl])
    o_ref[...] = jnp.concatenate(out_rows, axis=0).T.astype(BF16)


def _nsa(qc, qr, kvc, ksel, vsel_t, kwin, vwin_t, gates, w1, w2, w2t, pos, ovl, B, S):
    T = B * S
    tq = TQ
    nq = S // tq
    q_spec = pl.BlockSpec((NSA_HEADS, tq, HEAD_DIM), lambda b, i: (0, b * nq + i, 0))
    k_spec = pl.BlockSpec((NSA_GROUPS, S, HEAD_DIM), lambda b, i: (0, b, 0))
    full = lambda a: pl.BlockSpec(a.shape, lambda b, i: (0,) * a.ndim)
    return pl.pallas_call(
        functools.partial(_nsa_kernel, S=S),
        grid=(B, nq),
        in_specs=[q_spec, q_spec,
                  pl.BlockSpec((N_CMP_PAD, CMP_COLS), lambda b, i: (b, 0)),
                  k_spec, k_spec, k_spec, k_spec,
                  pl.BlockSpec((tq, GATE_PAD), lambda b, i: (b * nq + i, 0)),
                  full(w1), full(w2), full(w2t), full(pos), full(ovl)],
        out_specs=pl.BlockSpec((tq, NSA_WIDTH), lambda b, i: (b * nq + i, 0)),
        out_shape=jax.ShapeDtypeStruct((T, NSA_WIDTH), BF16),
        scratch_shapes=[pltpu.VMEM((NSA_GROUPS, N_CMP_PAD, HEAD_DIM), BF16),
                        pltpu.VMEM((NSA_GROUPS, HEAD_DIM, N_CMP_PAD), BF16),
                        pltpu.VMEM((NSA_GROUPS, S // SEL_BLOCK, tq), F32)],
        compiler_params=pltpu.CompilerParams(dimension_semantics=("arbitrary", "arbitrary"),
                                             vmem_limit_bytes=VMEM_LIMIT),
        name="nsa",
    )(qc, qr, kvc, ksel, vsel_t, kwin, vwin_t, gates, w1, w2, w2t, pos, ovl)


ROUTE_E, ROUTE_R, ROUTE_W = 0, TOP_K, 2 * TOP_K
ROUTE_ROWS = 16


def _merge_kernel(x_ref, onsa_ref, pool_ref, prev_ref, qm_ref, km_ref, vm_ref,
                  gmix_ref, wmg_ref, wpool_ref, pscale_ref, wun_ref, wup_ref, wum_ref, wout_ref,
                  gffn_ref, wrh_ref, wrl_ref, br_ref, tri_ref, lower_ref,
                  x1_ref, hf_ref, route_ref, cnt_ref, omem_s, carry_s, *, S, tile0):
    i = pl.program_id(0) + tile0
    tm = TM_MERGE
    n_s = S // tm

    @pl.when(pl.program_id(0) == 0)
    def _init():
        carry_s[...] = jnp.zeros_like(carry_s)

    x = x_ref[...]
    h = _rms(x, gmix_ref[...]).astype(BF16)

    u = pool_ref[...]
    seq_tile = i % n_s
    prev = jnp.where(seq_tile == 0, 0.0, prev_ref[...])
    ext = jnp.concatenate([prev, u], axis=0)
    b2 = ext[1:] + ext[:-1]
    b4 = b2[2:] + b2[:-2]
    b8 = b4[4:] + b4[:-4]
    b16 = b8[8:] + b8[:-8]
    sums = (b2[POOL_HALO - 1:POOL_HALO - 1 + tm], b4[POOL_HALO - 3:POOL_HALO - 3 + tm],
            b8[POOL_HALO - 7:POOL_HALO - 7 + tm], b16[POOL_HALO - 15:POOL_HALO - 15 + tm])
    t_seq = seq_tile * tm + lax.broadcasted_iota(jnp.int32, (tm, 1), 0)
    lane_p = lax.broadcasted_iota(jnp.int32, (tm, POOL_WIDTH), 1)
    z = jnp.zeros((tm, POOL_WIDTH), F32)
    for gi, w in enumerate(POOL_WINDOWS):
        cnt = jnp.minimum(t_seq + 1, w).astype(F32)
        z = jnp.where(lane_p // POOL_GROUP == gi, sums[gi] / cnt, z)
    z = z - u
    o_pool = (_dot(z.astype(BF16), wpool_ref[...]) * pscale_ref[...]).astype(BF16)

    for hh in range(MEM_HEADS):
        s = _dot_nt(qm_ref[hh], km_ref[0, hh])
        m = jnp.max(s, axis=-1, keepdims=True)
        e = jnp.exp(s - m)
        p = e / jnp.sum(e, axis=-1, keepdims=True)
        omem_s[:, hh * HEAD_DIM:(hh + 1) * HEAD_DIM] = _dot(p.astype(BF16), vm_ref[0, hh]).astype(BF16)

    def gated(branch, w_ref, j):
        return _sigmoid(_dot(h, wmg_ref[:, j * D_MODEL:(j + 1) * D_MODEL])) * _dot(branch, w_ref[...])

    merged = gated(onsa_ref[...], wun_ref, 0) + gated(o_pool, wup_ref, 1) + gated(omem_s[...], wum_ref, 2)
    x1 = x + _dot(merged.astype(BF16), wout_ref[...])
    x1_ref[...] = x1
    hf = _rms(x1, gffn_ref[...])
    hf_ref[...] = _pack_bf16_pairs(hf)

    hf_hi = hf.astype(BF16)
    hf_lo = (hf - hf_hi.astype(F32)).astype(BF16)
    both = _dot_nt(jnp.concatenate([wrh_ref[...], wrl_ref[...]], axis=0), hf_hi)
    logits = both[:N_EXPERTS] + both[N_EXPERTS:] + _dot_nt(wrh_ref[...], hf_lo) + br_ref[...]
    erow = lax.broadcasted_iota(jnp.int32, (N_EXPERTS, tm), 0)
    rank = jnp.zeros((N_EXPERTS, tm), F32)
    for jp in range(N_EXPERTS):
        other = logits[jp:jp + 1, :]
        beats = (other > logits) | ((other == logits) & (erow > jp))
        rank = rank + beats.astype(F32)
    chosen = rank < TOP_K
    m = jnp.max(logits, axis=0, keepdims=True)
    e = jnp.where(chosen, jnp.exp(logits - m), 0.0)
    comb = e * (1.0 / jnp.sum(e, axis=0, keepdims=True))

    chosen_b = chosen.astype(BF16)
    carry = carry_s[:, 0:1]
    in_expert = _dot(chosen_b, tri_ref[...]) + carry
    carry_new = carry + jnp.sum(chosen.astype(F32), axis=1, keepdims=True)
    carry_s[...] = jnp.broadcast_to(carry_new, carry_s.shape)
    cnt_ref[...] = jnp.broadcast_to(carry_new, cnt_ref.shape)

    before = _dot(lower_ref[...], chosen_b)
    erow_f = erow.astype(F32)
    fields = {ROUTE_E: erow_f, ROUTE_R: in_expert, ROUTE_W: comb}
    rows = [None] * ROUTE_ROWS
    for k in range(TOP_K):
        pick = chosen & (before == k)
        for base, val in fields.items():
            rows[base + k] = jnp.sum(jnp.where(pick, val, 0.0), axis=0, keepdims=True)
    zero_row = jnp.zeros((1, tm), F32)
    route_ref[...] = jnp.concatenate([zero_row if r is None else r for r in rows], axis=0)


def _merge(x2, onsa, pool_in, qm, km, vm, consts, S, tile0, n_tiles):
    tm = TM_MERGE
    Tc = n_tiles * tm
    n_s = S // tm
    M = km.shape[2]
    halo_per_tile = tm // POOL_HALO
    row = lambda w: pl.BlockSpec((tm, w), lambda i: (i + tile0, 0))
    out_row = lambda w: pl.BlockSpec((tm, w), lambda i: (i, 0))
    full = lambda a: pl.BlockSpec(a.shape, lambda i: (0,) * a.ndim)
    mem_spec = pl.BlockSpec((1, MEM_HEADS, M, HEAD_DIM), lambda i: ((i + tile0) // n_s, 0, 0, 0))
    return pl.pallas_call(
        functools.partial(_merge_kernel, S=S, tile0=tile0),
        grid=(n_tiles,),
        in_specs=[row(D_MODEL), row(NSA_WIDTH), row(POOL_WIDTH),
                  pl.BlockSpec((POOL_HALO, POOL_WIDTH),
                               lambda i: (jnp.maximum((i + tile0) * halo_per_tile - 1, 0), 0)),
                  pl.BlockSpec((MEM_HEADS, tm, HEAD_DIM), lambda i: (0, i + tile0, 0)),
                  mem_spec, mem_spec] + [full(c) for c in consts],
        out_specs=[out_row(D_MODEL), out_row(PACKED), pl.BlockSpec((ROUTE_ROWS, tm), lambda i: (0, i)),
                   pl.BlockSpec((N_EXPERTS, LANES), lambda i: (0, 0))],
        out_shape=[jax.ShapeDtypeStruct((Tc, D_MODEL), F32),
                   jax.ShapeDtypeStruct((Tc, PACKED), U32),
                   jax.ShapeDtypeStruct((ROUTE_ROWS, Tc), F32),
                   jax.ShapeDtypeStruct((N_EXPERTS, LANES), F32)],
        scratch_shapes=[pltpu.VMEM((tm, MEM_WIDTH), BF16), pltpu.VMEM((N_EXPERTS, LANES), F32)],
        compiler_params=pltpu.CompilerParams(dimension_semantics=("arbitrary",), vmem_limit_bytes=VMEM_LIMIT),
        name="merge",
    )(x2, onsa, pool_in, pool_in, qm, km, vm, *consts)


def _sc_mesh():
    return plsc.VectorSubcoreMesh(core_axis_name="c", subcore_axis_name="s")


def _sc_worker():
    return lax.axis_index("s") * SC_CORES + lax.axis_index("c")


def _sc_dispatch(hf, dest, n_rows):
    T, d = hf.shape
    per_worker = T // SC_WORKERS
    steps = per_worker // SC_ROWS

    @functools.partial(pl.kernel, mesh=_sc_mesh(), out_type=jax.ShapeDtypeStruct((n_rows, d), hf.dtype),
                       scratch_types=[pltpu.VMEM((TOP_K, SC_ROWS), jnp.int32), pltpu.VMEM((SC_ROWS, d), hf.dtype),
                                      pltpu.SemaphoreType.DMA])
    def dispatch(hf_hbm, dest_hbm, xs_hbm, idx_v, rows_v, sem):
        base = _sc_worker() * per_worker

        @pl.loop(0, steps)
        def _(j):
            rows = pl.ds(pl.multiple_of(base + j * SC_ROWS, SC_ROWS), SC_ROWS)
            pltpu.sync_copy(hf_hbm.at[rows], rows_v)
            for k in range(TOP_K):
                pltpu.sync_copy(dest_hbm.at[k, rows], idx_v.at[k])
            copies = [pltpu.make_async_copy(rows_v, xs_hbm.at[idx_v.at[k]], sem) for k in range(TOP_K)]
            for c in copies:
                c.start()
            for c in copies:
                c.wait()

    return dispatch(hf, dest)


def _moe_kernel(te_ref, nu_ref, nv_ref, xs_ref, wgu_ref, bgu_ref, wd_ref, bd_ref, ys_ref, wgu_s, wd_s):
    j = pl.program_id(0)
    used = j < nu_ref[0]
    new_expert = (j == 0) | (te_ref[j] != te_ref[jnp.maximum(j - 1, 0)])

    @pl.when(used & new_expert)
    def _cast_weights():
        wgu_s[...] = wgu_ref[0].astype(BF16)
        wd_s[...] = wd_ref[0].astype(BF16)

    @pl.when(used)
    def _compute():
        live = lax.broadcasted_iota(jnp.int32, (xs_ref.shape[0], 1), 0) < nv_ref[j]
        xb = _unpack_bf16_pairs(jnp.where(live, xs_ref[...], jnp.uint32(0))).astype(BF16)
        gu = _dot(xb, wgu_s[...]) + bgu_ref[0]
        gate = jnp.minimum(gu[:, :D_FF], SWIGLU_LIMIT)
        up = jnp.clip(gu[:, D_FF:], -SWIGLU_LIMIT, SWIGLU_LIMIT)
        act = (up + 1.0) * (gate * _sigmoid(SWIGLU_ALPHA * gate))
        ys_ref[...] = _pack_bf16_pairs(_dot(act.astype(BF16), wd_s[...]) + bd_ref[0])

    @pl.when(j >= nu_ref[0])
    def _unused():
        ys_ref[...] = jnp.zeros_like(ys_ref)


def _moe(tile_expert, n_used, n_valid, xs, wgu, bgu, wd, bd):
    P = xs.shape[0]
    tm = TM_E
    grid_spec = pltpu.PrefetchScalarGridSpec(
        num_scalar_prefetch=3,
        grid=(P // tm,),
        in_specs=[pl.BlockSpec((tm, PACKED), lambda j, te, nu, nv: (j, 0)),
                  pl.BlockSpec((1, D_MODEL, 2 * D_FF), lambda j, te, nu, nv: (te[j], 0, 0)),
                  pl.BlockSpec((1, 1, 2 * D_FF), lambda j, te, nu, nv: (te[j], 0, 0)),
                  pl.BlockSpec((1, D_FF, D_MODEL), lambda j, te, nu, nv: (te[j], 0, 0)),
                  pl.BlockSpec((1, 1, D_MODEL), lambda j, te, nu, nv: (te[j], 0, 0))],
        out_specs=pl.BlockSpec((tm, PACKED), lambda j, te, nu, nv: (j, 0)),
        scratch_shapes=[pltpu.VMEM((D_MODEL, 2 * D_FF), BF16), pltpu.VMEM((D_FF, D_MODEL), BF16)],
    )
    return pl.pallas_call(
        _moe_kernel,
        grid_spec=grid_spec,
        out_shape=jax.ShapeDtypeStruct((P, PACKED), U32),
        compiler_params=pltpu.CompilerParams(dimension_semantics=("arbitrary",), vmem_limit_bytes=VMEM_LIMIT),
        name="moe",
    )(tile_expert, n_used, n_valid, xs, wgu, bgu, wd, bd)


def _sc_gather_rows(table, idx):
    n, d = idx.shape[0], table.shape[1]
    per_worker = n // SC_WORKERS
    steps = per_worker // SC_ROWS
    assert steps % 2 == 0

    @functools.partial(pl.kernel, mesh=_sc_mesh(), out_type=jax.ShapeDtypeStruct((n, d), table.dtype),
                       scratch_types=[pltpu.VMEM((2, SC_ROWS), jnp.int32), pltpu.VMEM((2, SC_ROWS, d), table.dtype),
                                      pltpu.SemaphoreType.DMA((2,)), pltpu.SemaphoreType.DMA((2,))])
    def gather(table_hbm, idx_hbm, out_hbm, idx_v, rows_v, gsem, wsem):
        base = _sc_worker() * per_worker

        def rows_at(j):
            return pl.ds(pl.multiple_of(base + j * SC_ROWS, SC_ROWS), SC_ROWS)

        def gather_copy(slot):
            return pltpu.make_async_copy(table_hbm.at[idx_v.at[slot]], rows_v.at[slot], gsem.at[slot])

        def write_copy(j, slot):
            return pltpu.make_async_copy(rows_v.at[slot], out_hbm.at[rows_at(j)], wsem.at[slot])

        def fetch(j, slot):
            pltpu.sync_copy(idx_hbm.at[rows_at(j)], idx_v.at[slot])
            gather_copy(slot).start()

        fetch(0, 0)

        @pl.loop(0, steps, step=2)
        def _(j0):
            for slot in range(2):
                j = j0 + slot
                gather_copy(slot).wait()

                @pl.when(j >= 1)
                def _():
                    write_copy(j - 1, 1 - slot).wait()

                @pl.when(j + 1 < steps)
                def _():
                    fetch(j + 1, 1 - slot)

                write_copy(j, slot).start()

        write_copy(steps - 1, 1).wait()

    return gather(table, idx)


def _final_kernel(x1_ref, yg_ref, route_ref, g_ref, o_ref):
    tm = x1_ref.shape[0]
    route_t = jnp.concatenate([route_ref[...], jnp.zeros((LANES - ROUTE_ROWS, tm), F32)], axis=0).T
    acc = x1_ref[...]
    for k in range(TOP_K):
        acc = acc + route_t[:, ROUTE_W + k:ROUTE_W + k + 1] * _unpack_bf16_pairs(yg_ref[k])
    o_ref[...] = _rms(acc, g_ref[...])


def _final(x1, yg, route, g, out_prev, tile0, n_total):
    Tc = x1.shape[0]
    tm = TM_FINAL
    in_specs = [pl.BlockSpec((tm, D_MODEL), lambda i: (i, 0)),
                pl.BlockSpec((TOP_K, tm, PACKED), lambda i: (0, i, 0)),
                pl.BlockSpec((ROUTE_ROWS, tm), lambda i: (0, i)),
                pl.BlockSpec((1, D_MODEL), lambda i: (0, 0))]
    args = [x1, yg, route, g]
    kernel_fn, aliases = _final_kernel, {}
    if out_prev is not None:
        in_specs.append(pl.BlockSpec(memory_space=pl.ANY))
        args.append(out_prev)
        kernel_fn = lambda x1_ref, yg_ref, route_ref, g_ref, prev_ref, o_ref: _final_kernel(x1_ref, yg_ref, route_ref,
                                                                                            g_ref, o_ref)
        aliases = {len(args) - 1: 0}
    return pl.pallas_call(
        kernel_fn,
        grid=(Tc // tm,),
        in_specs=in_specs,
        out_specs=pl.BlockSpec((tm, D_MODEL), lambda i: (i + tile0, 0)),
        out_shape=jax.ShapeDtypeStruct((n_total * tm, D_MODEL), F32),
        input_output_aliases=aliases,
        compiler_params=pltpu.CompilerParams(dimension_semantics=("arbitrary",), vmem_limit_bytes=VMEM_LIMIT),
        name="final",
    )(*args)


def _rope_tables(S):
    half = HEAD_DIM // 2
    inv = ROPE_THETA ** (-jnp.arange(half, dtype=F32) / half)
    ang = jnp.arange(S, dtype=F32)[:, None] * inv[None, :]
    cos = jnp.tile(jnp.cos(ang), (1, LANES // half))
    sin = jnp.tile(jnp.concatenate([-jnp.sin(ang), jnp.sin(ang)], axis=1), (1, LANES // HEAD_DIM))
    return cos, sin


def _selection_constants(S):
    nc = (S - CMP_BLOCK) // CMP_STRIDE + 1
    n_sel = S // SEL_BLOCK
    j = jnp.arange(n_sel)[:, None]
    i = jnp.arange(N_CMP_PAD)[None, :]
    overlap_t = ((i * CMP_STRIDE <= j * SEL_BLOCK + SEL_BLOCK - 1)
                 & (i * CMP_STRIDE + CMP_BLOCK - 1 >= j * SEL_BLOCK) & (i < nc)).astype(BF16)
    return overlap_t


def kernel(x, mem, norm_mix, norm_mem, w_in, cmp_pos, cmp_w1, cmp_w2, w_pool, pool_scale, w_mem_kv, w_up_nsa,
           w_up_pool, w_up_mem, w_out, norm_ffn, w_router, b_router, w_gate_up, b_gate_up, w_down, b_down,
           norm_final):
    B, S, D = x.shape
    T = B * S
    assert D == D_MODEL and S % CKS == 0 and S // SEL_BLOCK == 32 and T % (sum(CHUNK_SHARES) * SC_WORKERS * SC_ROWS * 2) == 0
    l = 0
    x2 = x.reshape(T, D)

    w = w_in[l]
    o_gate = NSA_WIDTH + 6 * KV_WIDTH
    n_gate = 3 * NSA_HEADS
    o_pool = o_gate + n_gate
    o_qm = o_pool + POOL_WIDTH
    o_mg = o_qm + MEM_WIDTH
    w_a = jnp.concatenate([w[:, :o_gate], w[:, o_pool:o_mg], w[:, o_gate:o_pool],
                           jnp.zeros((D, GATE_PAD - n_gate), F32)], axis=1).astype(BF16)
    w_mg = w[:, o_mg:].astype(BF16)
    cos_t, sin_t = _rope_tables(S)
    overlap_t = _selection_constants(S)
    stream_kv = jnp.arange(2 * NSA_GROUPS) // NSA_GROUPS
    w1_half = cmp_w1[l].reshape(2, 2, CMP_STRIDE, HEAD_DIM, HEAD_DIM)[stream_kv]
    w1 = jnp.einsum("shlde,ts->hltdse", w1_half, jnp.eye(2 * NSA_GROUPS, dtype=F32))
    w1 = w1.reshape(2, CMP_COLS, 2 * KV_WIDTH).astype(BF16)
    w2 = cmp_w2[l].astype(BF16)
    w2t = jnp.swapaxes(w2, 1, 2)
    pos_half = cmp_pos[l].reshape(2, 2, CMP_STRIDE, HEAD_DIM)[stream_kv]
    pos = jnp.broadcast_to(pos_half.transpose(1, 2, 0, 3).reshape(2, 1, CMP_COLS), (2, 8, CMP_COLS)).astype(BF16)
    wpool_bd = jnp.zeros((POOL_WIDTH, POOL_WIDTH), F32)
    for gi in range(len(POOL_WINDOWS)):
        wpool_bd = wpool_bd.at[gi * POOL_GROUP:(gi + 1) * POOL_GROUP, gi * POOL_GROUP:(gi + 1) * POOL_GROUP].set(w_pool[l, gi])
    wr = w_router[l].T
    wr_hi = wr.astype(BF16)
    wr_lo = (wr - wr_hi.astype(F32)).astype(BF16)
    br = b_router[l].reshape(N_EXPERTS, 1)
    tri = (jnp.arange(TM_MERGE)[:, None] < jnp.arange(TM_MERGE)[None, :]).astype(BF16)
    lower = (jnp.arange(N_EXPERTS)[None, :] < jnp.arange(N_EXPERTS)[:, None]).astype(BF16)

    km, vm = _memkv(mem, norm_mem[l].reshape(1, D), w_mem_kv[l].astype(BF16))
    qc, qr, kvc, ksel, vsel, kwin, vwin, pool_in, qm, gates = _inproj(
        x2, norm_mix[l].reshape(1, D), w_a, cos_t, sin_t, S)
    o_nsa = _nsa(qc, qr, kvc, ksel, vsel, kwin, vwin, gates, w1, w2, w2t, pos, overlap_t, B, S)
    consts = [norm_mix[l].reshape(1, D), w_mg, wpool_bd.astype(BF16), pool_scale[l].reshape(1, POOL_WIDTH),
              w_up_nsa[l].astype(BF16), w_up_pool[l].astype(BF16), w_up_mem[l].astype(BF16), w_out[l].astype(BF16),
              norm_ffn[l].reshape(1, D), wr_hi, wr_lo, br, tri, lower]
    unit = T // sum(CHUNK_SHARES)
    chunk_tokens = [share * unit for share in CHUNK_SHARES]
    chunk_first = [sum(chunk_tokens[:c]) for c in range(len(chunk_tokens))]
    wgu, bgu = w_gate_up[l], b_gate_up[l].reshape(N_EXPERTS, 1, 2 * D_FF)
    wd, bd = w_down[l], b_down[l].reshape(N_EXPERTS, 1, D_MODEL)

    def route_chunk(c):
        Tc = chunk_tokens[c]
        n_tiles = (Tc * TOP_K) // TM_E + N_EXPERTS
        tile_start = jnp.arange(n_tiles, dtype=jnp.int32) * TM_E
        x1, hf, route, counts = _merge(x2, o_nsa, pool_in, qm, km, vm, consts, S, chunk_first[c] // TM_MERGE,
                                       Tc // TM_MERGE)
        counts = counts[:, 0].astype(jnp.int32)
        padded = ((counts + TM_E - 1) // TM_E) * TM_E
        ends = jnp.cumsum(padded)
        starts = ends - padded
        e_k = route[ROUTE_E:ROUTE_E + TOP_K].astype(jnp.int32)
        r_k = route[ROUTE_R:ROUTE_R + TOP_K].astype(jnp.int32)
        group_start = jnp.zeros_like(e_k)
        for e in range(N_EXPERTS):
            group_start = jnp.where(e_k == e, starts[e], group_start)
        dest = group_start + r_k
        tile_expert = jnp.minimum(jnp.sum(tile_start[:, None] >= ends[None, :], axis=1), N_EXPERTS - 1).astype(jnp.int32)
        n_used = (ends[-1] // TM_E).astype(jnp.int32).reshape(1)
        n_valid = jnp.clip((starts + counts)[tile_expert] - tile_start, 0, TM_E).astype(jnp.int32)
        xs = _sc_dispatch(hf, dest, n_tiles * TM_E)
        return x1, route, dest, (tile_expert, n_used, n_valid, xs)

    routed = [route_chunk(c) for c in range(len(chunk_tokens))]
    expert_out = [_moe(*moe_args, wgu, bgu, wd, bd) for _, _, _, moe_args in routed]
    out = None
    for c, ((x1, route, dest, _), ys) in enumerate(zip(routed, expert_out)):
        yg = _sc_gather_rows(ys, dest.reshape(-1)).reshape(TOP_K, chunk_tokens[c], PACKED)
        out = _final(x1, yg, route, norm_final.reshape(1, D), out, chunk_first[c] // TM_FINAL, T // TM_FINAL)
    return out.reshape(B, S, D)
```
